```python
import jax, jax.numpy as jnp
from jax import lax
import numpy as np

D_MODEL = 2048
BATCH = 1
SEQ = 8192
DEPTH = 1

HEAD_DIM = 128
DILATED_GROUPS = ((128, 1), (512, 4), (2048, 16))
HEADS_PER_GROUP = 8
N_HEADS_A = HEADS_PER_GROUP * len(DILATED_GROUPS)
A_OUT_WIDTH = HEADS_PER_GROUP * HEAD_DIM
ATTN_BLOCK = 128
NUM_BUCKETS = 32
MAX_DISTANCE = 2048
NEG_INF = -1e30
RET_HEADS = 8
RET_QK_DIM = 128
RET_V_DIM = 256
RET_V_WIDTH = RET_HEADS * RET_V_DIM
RET_CHUNK = 128
ROPE_BASE = 10000.0
GN_EPS = 1e-5
N_EXPERTS = 64
N_GROUPS = 8
TOPK_GROUPS = 4
TOP_K = 8
EXPERT_DIM = 512
SHARED_DIM = 512
ROUTED_SCALE = 2.5
MOE_BLOCK = 128
RMS_EPS = 1e-6
A_QKV = N_HEADS_A * HEAD_DIM
R_QK = RET_HEADS * RET_QK_DIM
IN_WIDTH = 3 * A_QKV + 2 * R_QK + 2 * RET_V_WIDTH + 2 * D_MODEL

kernel_name = 'hybrid_dilated_retention_moe_block'


def _rmsnorm(x, g):
    xf = x.astype(jnp.float32)
    y = xf * lax.rsqrt(jnp.mean(xf * xf, axis=-1, keepdims=True) + RMS_EPS)
    return (y * g.astype(jnp.float32)).astype(x.dtype)


def _t5_bucket(dist):
    max_exact = NUM_BUCKETS // 2
    safe = np.maximum(dist, 1).astype(np.float32)
    large = max_exact + (np.log(safe / max_exact) / np.log(MAX_DISTANCE / max_exact)
                         * (NUM_BUCKETS - max_exact)).astype(np.int32)
    return np.where(dist < max_exact, dist, np.minimum(large, NUM_BUCKETS - 1)).astype(np.int32)


def _dilated_group(q, k, v, bias_tab, w_steps, dilation):
    B, S, H, E = q.shape
    L = S // dilation
    nb = -(-L // ATTN_BLOCK)
    Lp = nb * ATTN_BLOCK

    def to_res(t):
        t = t.reshape(B, L, dilation, H, E).transpose(0, 2, 3, 1, 4)
        return jnp.pad(t, ((0, 0), (0, 0), (0, 0), (0, Lp - L), (0, 0)))

    def key_blocks(t):
        t = jnp.pad(to_res(t), ((0, 0), (0, 0), (0, 0), (ATTN_BLOCK, 0), (0, 0)))
        t = t.reshape(B, dilation, H, nb + 1, ATTN_BLOCK, E)
        return jnp.concatenate([t[:, :, :, :-1], t[:, :, :, 1:]], axis=4)

    qb = to_res(q).reshape(B, dilation, H, nb, ATTN_BLOCK, E)
    kb, vb = key_blocks(k), key_blocks(v)
    a = np.arange(ATTN_BLOCK)[:, None]
    cc = np.arange(2 * ATTN_BLOCK)[None, :]
    delta = ATTN_BLOCK + a - cc
    band = (delta >= 0) & (delta <= w_steps)
    key_ok = ((np.arange(nb)[:, None] - 1) * ATTN_BLOCK + np.arange(2 * ATTN_BLOCK)[None, :]) >= 0
    mask = band[None] & key_ok[:, None, :]
    bucket = _t5_bucket(np.maximum(delta, 0) * dilation)
    bias = jnp.transpose(bias_tab.astype(jnp.float32)[bucket], (2, 0, 1))
    s = jnp.einsum('bdhnqe,bdhnke->bdhnqk', qb, kb, preferred_element_type=jnp.float32)
    s = jnp.where(mask, s + bias[None, None, :, None], NEG_INF)
    m = jnp.max(s, axis=-1, keepdims=True)
    p = jnp.exp(s - m)
    l = jnp.sum(p, axis=-1, keepdims=True)
    o = jnp.einsum('bdhnqk,bdhnke->bdhnqe', p.astype(v.dtype), vb,
                   preferred_element_type=jnp.float32) / l
    lse = (m + jnp.log(l))[..., 0]

    def from_res(t):
        rest = t.shape[5:]
        t = t.reshape((B, dilation, H, Lp) + rest)[:, :, :, :L]
        perm = (0, 3, 1, 2) + tuple(range(4, t.ndim))
        return t.transpose(perm).reshape((B, S, H) + rest)

    return from_res(o), from_res(lse)


def _rotary(t):
    S, E = t.shape[1], t.shape[3]
    inv = ROPE_BASE ** (-jnp.arange(0, E, 2, dtype=jnp.float32) / E)
    ang = jnp.arange(S, dtype=jnp.float32)[:, None] * inv[None, :]
    cos, sin = jnp.cos(ang)[None, :, None, :], jnp.sin(ang)[None, :, None, :]
    tf = t.astype(jnp.float32)
    t1, t2 = tf[..., :E // 2], tf[..., E // 2:]
    return jnp.concatenate([t1 * cos - t2 * sin, t1 * sin + t2 * cos], axis=-1).astype(t.dtype)


def _retention(q, k, v):
    B, S, H, DK = q.shape
    DV = v.shape[-1]
    C = RET_CHUNK
    N = S // C
    log_g = jnp.log1p(-jnp.exp2(-5.0 - jnp.arange(H, dtype=jnp.float32)))

    def chunk(t):
        return t.astype(jnp.float32).reshape(B, N, C, H, t.shape[-1]).transpose(0, 3, 1, 2, 4)

    qc, kc, vc = chunk(q), chunk(k), chunk(v)
    idx = jnp.arange(C, dtype=jnp.float32)
    diff = idx[:, None] - idx[None, :]
    dmat = jnp.where(diff >= 0, jnp.exp(log_g[:, None, None] * jnp.maximum(diff, 0.0)), 0.0)
    inner = jnp.einsum('bhnqd,bhnkd->bhnqk', qc, kc) * dmat[None, :, None]
    inner = jnp.einsum('bhnqk,bhnke->bhnqe', inner, vc)
    zeta = jnp.exp(log_g[:, None] * (C - 1 - idx))
    upd = jnp.einsum('bhnkd,bhnke->bhnde', kc, vc * zeta[None, :, None, :, None])
    g_chunk = jnp.exp(log_g * C)[None, :, None, None]

    def step(state, u):
        return g_chunk * state + u, state

    _, prev = lax.scan(step, jnp.zeros((B, H, DK, DV), jnp.float32), jnp.moveaxis(upd, 2, 0))
    prev = jnp.moveaxis(prev, 0, 2)
    xi = jnp.exp(log_g[:, None] * (idx + 1.0))
    cross = jnp.einsum('bhnqd,bhnde->bhnqe', qc, prev) * xi[None, :, None, :, None]
    return (inner + cross).transpose(0, 2, 3, 1, 4).reshape(B, S, H, DV)


def _moe(h, router_w, router_bias, w_gate_e, w_up_e, w_down_e):
    T, D = h.shape
    E = N_EXPERTS
    scores = jax.nn.sigmoid(jnp.dot(h.astype(jnp.float32), router_w.astype(jnp.float32)))
    sel = scores + router_bias.astype(jnp.float32)
    grp_score = lax.top_k(sel.reshape(T, N_GROUPS, E // N_GROUPS), 2)[0].sum(-1)
    _, grp_idx = lax.top_k(grp_score, TOPK_GROUPS)
    grp_mask = jax.nn.one_hot(grp_idx, N_GROUPS, dtype=jnp.float32).sum(1)
    exp_mask = jnp.repeat(grp_mask, E // N_GROUPS, axis=1) > 0
    _, top_idx = lax.top_k(jnp.where(exp_mask, sel, -jnp.inf), TOP_K)
    top_w = jnp.take_along_axis(scores, top_idx, axis=1)
    top_w = top_w / jnp.sum(top_w, axis=-1, keepdims=True) * ROUTED_SCALE
    TK = T * TOP_K
    flat_e = top_idx.reshape(TK)
    flat_w = top_w.reshape(TK)
    order = jnp.argsort(flat_e)
    sorted_e = flat_e[order]
    counts = jnp.bincount(flat_e, length=E)
    padded = (counts + MOE_BLOCK - 1) // MOE_BLOCK * MOE_BLOCK
    pad_end = jnp.cumsum(padded)
    pad_start = pad_end - padded
    start = jnp.cumsum(counts) - counts
    dest = pad_start[sorted_e] + jnp.arange(TK, dtype=jnp.int32) - start[sorted_e]
    n_blocks = (TK + E * (MOE_BLOCK - 1) + MOE_BLOCK - 1) // MOE_BLOCK
    P = n_blocks * MOE_BLOCK
    tok_buf = jnp.full((P,), T, jnp.int32).at[dest].set((order // TOP_K).astype(jnp.int32))
    w_buf = jnp.zeros((P,), jnp.float32).at[dest].set(flat_w[order])
    block_e = jnp.minimum(jnp.searchsorted(pad_end, jnp.arange(n_blocks) * MOE_BLOCK, side='right'), E - 1)
    h_pad = jnp.concatenate([h, jnp.zeros((1, D), h.dtype)], axis=0)

    def body(out, blk):
        tok, wt, e = blk
        xb = h_pad[tok]
        y = (jax.nn.silu(xb @ w_gate_e[e]) * (xb @ w_up_e[e])) @ w_down_e[e]
        return out.at[tok].add(y.astype(jnp.float32) * wt[:, None]), None

    out, _ = lax.scan(body, jnp.zeros((T + 1, D), jnp.float32),
                      (tok_buf.reshape(n_blocks, MOE_BLOCK), w_buf.reshape(n_blocks, MOE_BLOCK), block_e))
    return out[:T].astype(h.dtype)


def setup_inputs(seed: int = 0) -> dict:
    key = jax.random.key(seed)
    ks = jax.random.split(key, 22)
    f32 = jnp.float32

    def nrm(k, shape, scale):
        return jax.random.normal(k, shape, f32) * scale

    L = DEPTH
    return {
        'x': nrm(ks[0], (BATCH, SEQ, D_MODEL), 1.0),
        'c': nrm(ks[1], (BATCH, D_MODEL), 1.0),
        'rel_bias': nrm(ks[2], (NUM_BUCKETS, N_HEADS_A), 0.5),
        'w_ada': nrm(ks[3], (L, D_MODEL, 6 * D_MODEL), 0.5 * D_MODEL ** -0.5),
        'b_ada': nrm(ks[4], (L, 6 * D_MODEL), 0.02),
        'ln1_g': 1.0 + nrm(ks[5], (L, D_MODEL), 0.02),
        'w_in': nrm(ks[6], (L, D_MODEL, IN_WIDTH), D_MODEL ** -0.5),
        'q_norm_g': 1.0 + nrm(ks[7], (L, HEAD_DIM), 0.02),
        'k_norm_g': 1.0 + nrm(ks[8], (L, HEAD_DIM), 0.02),
        'ret_gn_g': 1.0 + nrm(ks[9], (L, RET_V_WIDTH), 0.02),
        'p_a': nrm(ks[10], (L, A_OUT_WIDTH, D_MODEL), A_OUT_WIDTH ** -0.5),
        'p_b': nrm(ks[11], (L, RET_V_WIDTH, D_MODEL), RET_V_WIDTH ** -0.5),
        'w_o': nrm(ks[12], (L, D_MODEL, D_MODEL), D_MODEL ** -0.5),
        'ln2_g': 1.0 + nrm(ks[13], (L, D_MODEL), 0.02),
        'router_w': nrm(ks[14], (L, D_MODEL, N_EXPERTS), D_MODEL ** -0.5),
        'router_bias': nrm(ks[15], (L, N_EXPERTS), 0.01),
        'w_gate_e': nrm(ks[16], (L, N_EXPERTS, D_MODEL, EXPERT_DIM), D_MODEL ** -0.5),
        'w_up_e': nrm(ks[17], (L, N_EXPERTS, D_MODEL, EXPERT_DIM), D_MODEL ** -0.5),
        'w_down_e': nrm(ks[18], (L, N_EXPERTS, EXPERT_DIM, D_MODEL), EXPERT_DIM ** -0.5),
        'w_gate_s': nrm(ks[19], (L, D_MODEL, SHARED_DIM), D_MODEL ** -0.5),
        'w_up_s': nrm(ks[20], (L, D_MODEL, SHARED_DIM), D_MODEL ** -0.5),
        'w_down_s': nrm(ks[21], (L, SHARED_DIM, D_MODEL), SHARED_DIM ** -0.5),
    }


def reference(x, c, rel_bias, w_ada, b_ada, ln1_g, w_in, q_norm_g, k_norm_g, ret_gn_g,
              p_a, p_b, w_o, ln2_g, router_w, router_bias, w_gate_e, w_up_e, w_down_e,
              w_gate_s, w_up_s, w_down_s):
    B, S, D = x.shape
    sizes = [A_QKV, A_QKV, A_QKV, R_QK, R_QK, RET_V_WIDTH, RET_V_WIDTH, D_MODEL, D_MODEL]
    cuts = [int(v) for v in np.cumsum(sizes)[:-1]]
    for l in range(DEPTH):
        mod = jax.nn.silu(c) @ w_ada[l] + b_ada[l]
        sh1, sc1, g1, sh2, sc2, g2 = jnp.split(mod[:, None, :], 6, axis=-1)
        h = _rmsnorm(x, ln1_g[l]) * (1.0 + sc1) + sh1
        proj = h @ w_in[l]
        qa, ka, va, qr, kr, vr, gr, gate_a, gate_b = jnp.split(proj, cuts, axis=-1)
        qa = _rmsnorm(qa.reshape(B, S, N_HEADS_A, HEAD_DIM), q_norm_g[l]) * (HEAD_DIM ** -0.5)
        ka = _rmsnorm(ka.reshape(B, S, N_HEADS_A, HEAD_DIM), k_norm_g[l])
        va = va.reshape(B, S, N_HEADS_A, HEAD_DIM)
        outs, lses = [], []
        for gi, (win, dil) in enumerate(DILATED_GROUPS):
            hs = slice(gi * HEADS_PER_GROUP, (gi + 1) * HEADS_PER_GROUP)
            o, lse = _dilated_group(qa[:, :, hs], ka[:, :, hs], va[:, :, hs], rel_bias[:, hs], win // dil, dil)
            outs.append(o)
            lses.append(lse)
        alpha = jax.nn.softmax(jnp.stack(lses, axis=0), axis=0)
        y_a = jnp.sum(alpha[..., None] * jnp.stack(outs, axis=0), axis=0)
        y_a = y_a.reshape(B, S, A_OUT_WIDTH).astype(x.dtype)
        qr = _rotary(qr.reshape(B, S, RET_HEADS, RET_QK_DIM))
        kr = _rotary(kr.reshape(B, S, RET_HEADS, RET_QK_DIM)) * (RET_QK_DIM ** -0.5)
        ret = _retention(qr, kr, vr.reshape(B, S, RET_HEADS, RET_V_DIM))
        mu = jnp.mean(ret, axis=-1, keepdims=True)
        var = jnp.mean(jnp.square(ret - mu), axis=-1, keepdims=True)
        ret = ((ret - mu) * lax.rsqrt(var + GN_EPS)).reshape(B, S, RET_V_WIDTH) * ret_gn_g[l].astype(jnp.float32)
        y_b = (ret * jax.nn.silu(gr.astype(jnp.float32))).astype(x.dtype)
        merged = jax.nn.sigmoid(gate_a) * (y_a @ p_a[l]) + jax.nn.sigmoid(gate_b) * (y_b @ p_b[l])
        x = x + g1 * (merged @ w_o[l])
        h2 = _rmsnorm(x, ln2_g[l]) * (1.0 + sc2) + sh2
        hf = h2.reshape(B * S, D)
        routed = _moe(hf, router_w[l], router_bias[l], w_gate_e[l], w_up_e[l], w_down_e[l])
        shared = (jax.nn.silu(hf @ w_gate_s[l]) * (hf @ w_up_s[l])) @ w_down_s[l]
        x = x + g2 * (routed + shared).reshape(B, S, D)
    return x
```

```python
import functools

import numpy as np
import jax
import jax.numpy as jnp
from jax import lax
from jax.experimental import pallas as pl
from jax.experimental.pallas import tpu as pltpu

F32 = jnp.float32
BF16 = jnp.bfloat16
U32 = jnp.uint32
I32 = jnp.int32

HEAD_DIM = 128
DILATED_GROUPS = ((128, 1), (512, 4), (2048, 16))
HEADS_PER_GROUP = 8
N_HEADS_A = HEADS_PER_GROUP * len(DILATED_GROUPS)
A_GROUP_WIDTH = HEADS_PER_GROUP * HEAD_DIM
ATTN_BLOCK = 128
NUM_BUCKETS = 32
MAX_DISTANCE = 2048
NEG_INF = -1e30
RET_HEADS = 8
RET_QK_DIM = 128
RET_V_DIM = 256
RET_CHUNK = 128
ROPE_BASE = 10000.0
GN_EPS = 1e-5
N_EXPERTS = 64
N_GROUPS = 8
TOPK_GROUPS = 4
TOP_K = 8
ROUTED_SCALE = 2.5
RMS_EPS = 1e-6

LANE = 128
COLBLK = 1024
VMEM_LIMIT = 56 * 1024 * 1024
EXPERT_ROWS = 256


def _params(sem, vmem=VMEM_LIMIT):
    return pltpu.CompilerParams(dimension_semantics=sem, vmem_limit_bytes=vmem)


def _silu(v):
    return v * jax.nn.sigmoid(v)


def _ada_kernel(c_ref, w_ref, b_ref, o_ref):
    sc = _silu(c_ref[...])
    o_ref[...] = jnp.sum(w_ref[...] * sc, axis=0, keepdims=True) + b_ref[...]


def _ada(c, w, b, tn=512):
    d, n = w.shape
    return pl.pallas_call(
        _ada_kernel,
        grid=(n // tn,),
        in_specs=[pl.BlockSpec((d, 1), lambda j: (0, 0)),
                  pl.BlockSpec((d, tn), lambda j: (0, j)),
                  pl.BlockSpec((1, tn), lambda j: (0, j))],
        out_specs=pl.BlockSpec((1, tn), lambda j: (0, j)),
        out_shape=jax.ShapeDtypeStruct((1, n), F32),
        compiler_params=_params(("parallel",)),
        name="ada",
    )(c.reshape(d, 1), w, b.reshape(1, n))


def _norm1_kernel(x_ref, g_ref, sc_ref, sh_ref, o_ref):
    x = x_ref[...]
    inv = lax.rsqrt(jnp.mean(x * x, axis=-1, keepdims=True) + RMS_EPS)
    o_ref[...] = ((x * inv * g_ref[...]) * (1.0 + sc_ref[...]) + sh_ref[...]).astype(o_ref.dtype)


def _norm1(x, g, mod, tm=512):
    t, d_model = x.shape
    vec = lambda k: pl.BlockSpec((1, d_model), lambda i, k=k: (0, k))
    return pl.pallas_call(
        _norm1_kernel,
        grid=(t // tm,),
        in_specs=[pl.BlockSpec((tm, d_model), lambda i: (i, 0)),
                  pl.BlockSpec((1, d_model), lambda i: (0, 0)),
                  vec(1), vec(0)],
        out_specs=pl.BlockSpec((tm, d_model), lambda i: (i, 0)),
        out_shape=jax.ShapeDtypeStruct((t, d_model), BF16),
        compiler_params=_params(("parallel",)),
        name="norm1",
    )(x, g.reshape(1, d_model), mod, mod)


def _to_residue_major(a, d):
    t, w = a.shape
    return a.reshape(t // d, d, w).transpose(1, 0, 2).reshape(t, w)


def _from_residue_major(a, d):
    t, w = a.shape
    return a.reshape(d, t // d, w).transpose(1, 0, 2).reshape(t, w)


EPI_QNORM, EPI_KNORM, EPI_PLAIN, EPI_ROT_Q, EPI_ROT_K, EPI_SILU, EPI_SIGMOID = range(7)


def _inproj_kernel(colblk_ref, order_ref, epi_ref, h0_ref, h1_ref, h2_ref, w_ref, qg_ref, kg_ref,
                   cos_ref, sin_ref, o_ref, acc_ref):
    del colblk_ref
    j = pl.program_id(1)
    epi = epi_ref[j]
    order = order_ref[j]
    for which, h_ref in enumerate((h0_ref, h1_ref, h2_ref)):
        @pl.when(order == which)
        def _(h_ref=h_ref):
            acc_ref[...] = jnp.dot(h_ref[...], w_ref[...], preferred_element_type=F32)
    nh = acc_ref.shape[1] // HEAD_DIM

    def head_norm(gain, scale):
        for hh in range(nh):
            a = acc_ref[:, hh * HEAD_DIM:(hh + 1) * HEAD_DIM]
            inv = lax.rsqrt(jnp.mean(a * a, axis=-1, keepdims=True) + RMS_EPS)
            o_ref[:, hh * HEAD_DIM:(hh + 1) * HEAD_DIM] = (
                (a * inv * gain) * scale).astype(o_ref.dtype)

    def rotary(scale):
        cos = cos_ref[...]
        sin = sin_ref[...]
        for hh in range(nh):
            a = acc_ref[:, hh * HEAD_DIM:(hh + 1) * HEAD_DIM]
            rot = pltpu.roll(a, HEAD_DIM // 2, 1)
            o_ref[:, hh * HEAD_DIM:(hh + 1) * HEAD_DIM] = (
                (a * cos + rot * sin) * scale).astype(o_ref.dtype)

    @pl.when(epi == EPI_QNORM)
    def _():
        head_norm(qg_ref[...], HEAD_DIM ** -0.5)

    @pl.when(epi == EPI_KNORM)
    def _():
        head_norm(kg_ref[...], 1.0)

    @pl.when(epi == EPI_PLAIN)
    def _():
        o_ref[...] = acc_ref[...].astype(o_ref.dtype)

    @pl.when(epi == EPI_ROT_Q)
    def _():
        rotary(1.0)

    @pl.when(epi == EPI_ROT_K)
    def _():
        rotary(RET_QK_DIM ** -0.5)

    @pl.when(epi == EPI_SILU)
    def _():
        o_ref[...] = _silu(acc_ref[...]).astype(o_ref.dtype)

    @pl.when(epi == EPI_SIGMOID)
    def _():
        o_ref[...] = jax.nn.sigmoid(acc_ref[...]).astype(o_ref.dtype)


def _inproj_plan(d_model):
    a_blocks = N_HEADS_A * HEAD_DIM // COLBLK
    groups = len(DILATED_GROUPS)
    per_group = a_blocks // groups
    rq = RET_HEADS * RET_QK_DIM // COLBLK
    rv = RET_HEADS * RET_V_DIM // COLBLK
    gd = d_model // COLBLK
    seg_epi = ([EPI_QNORM] * a_blocks + [EPI_KNORM] * a_blocks + [EPI_PLAIN] * a_blocks
               + [EPI_ROT_Q] * rq + [EPI_ROT_K] * rq + [EPI_PLAIN] * rv + [EPI_SILU] * rv
               + [EPI_SIGMOID] * (2 * gd))
    order_of = [0] * len(seg_epi)
    for seg in range(3):
        for blk in range(a_blocks):
            order_of[seg * a_blocks + blk] = blk // per_group
    steps = sorted(range(len(seg_epi)), key=lambda cb: (order_of[cb], cb))
    colblk = np.array(steps, np.int32)
    order = np.array([order_of[cb] for cb in steps], np.int32)
    epi = np.array([seg_epi[cb] for cb in steps], np.int32)
    return colblk, order, epi


def _inproj(h_orders, w_bf, qg, kg, cos_tab, sin_tab, tm=1024):
    t, d_model = h_orders[0].shape
    n = w_bf.shape[1]
    colblk, order, epi = _inproj_plan(d_model)
    assert len(colblk) * COLBLK == n
    row = lambda width: pl.BlockSpec((tm, width), lambda i, j, cb, od, ep: (i, 0))
    one = lambda width: pl.BlockSpec((1, width), lambda i, j, cb, od, ep: (0, 0))
    grid_spec = pltpu.PrefetchScalarGridSpec(
        num_scalar_prefetch=3,
        grid=(t // tm, len(colblk)),
        in_specs=[
            row(d_model), row(d_model), row(d_model),
            pl.BlockSpec((d_model, COLBLK), lambda i, j, cb, od, ep: (0, cb[j])),
            one(HEAD_DIM), one(HEAD_DIM), row(HEAD_DIM), row(HEAD_DIM),
        ],
        out_specs=pl.BlockSpec((tm, COLBLK), lambda i, j, cb, od, ep: (i, cb[j])),
        scratch_shapes=[pltpu.VMEM((tm, COLBLK), F32)],
    )
    return pl.pallas_call(
        _inproj_kernel,
        grid_spec=grid_spec,
        out_shape=jax.ShapeDtypeStruct((t, n), BF16),
        compiler_params=_params(("parallel", "arbitrary")),
        name="inproj",
    )(jnp.asarray(colblk), jnp.asarray(order), jnp.asarray(epi),
      *h_orders, w_bf, qg.reshape(1, HEAD_DIM), kg.reshape(1, HEAD_DIM), cos_tab, sin_tab)


def _rotary_tables(t):
    half = RET_QK_DIM // 2
    inv = ROPE_BASE ** (-np.arange(0, RET_QK_DIM, 2, dtype=np.float64) / RET_QK_DIM)
    ang = np.arange(t, dtype=np.float64)[:, None] * inv[None, :]
    cos, sin = np.cos(ang), np.sin(ang)
    del half
    cos_tab = np.concatenate([cos, cos], axis=1).astype(np.float32)
    sin_tab = np.concatenate([-sin, sin], axis=1).astype(np.float32)
    return jnp.asarray(cos_tab), jnp.asarray(sin_tab)


def _t5_bucket(dist):
    max_exact = NUM_BUCKETS // 2
    safe = np.maximum(dist, 1).astype(np.float32)
    large = max_exact + (np.log(safe / max_exact) / np.log(MAX_DISTANCE / max_exact)
                         * (NUM_BUCKETS - max_exact)).astype(np.int32)
    return np.where(dist < max_exact, dist, np.minimum(large, NUM_BUCKETS - 1)).astype(np.int32)


def _attn_kernel(head0, w_steps, blocks_per_res, tab_ref, bucket_ref, q_ref, kp_ref, kc_ref,
                 vp_ref, vc_ref, o_ref, lse_ref, bias_ref):
    m_idx = pl.program_id(0)
    blk = ATTN_BLOCK

    @pl.when(m_idx == 0)
    def _():
        bucket = bucket_ref[...]
        for hh in range(HEADS_PER_GROUP):
            bias = jnp.zeros(bucket.shape, F32)
            for b in range(NUM_BUCKETS):
                bias = jnp.where(bucket == b, tab_ref[b, head0 + hh], bias)
            bias_ref[hh] = bias

    a = lax.broadcasted_iota(I32, (blk, blk), 0)
    cc = lax.broadcasted_iota(I32, (blk, blk), 1)
    has_prev = (m_idx % blocks_per_res) > 0
    ok_prev = ((blk + a - cc) <= w_steps) & has_prev
    ok_cur = ((a - cc) >= 0) & ((a - cc) <= w_steps)
    nt = (((1,), (1,)), ((), ()))
    lses = []
    for hh in range(HEADS_PER_GROUP):
        sl = slice(hh * HEAD_DIM, (hh + 1) * HEAD_DIM)
        q = q_ref[:, sl]
        s_p = lax.dot_general(q, kp_ref[:, sl], nt, preferred_element_type=F32)
        s_c = lax.dot_general(q, kc_ref[:, sl], nt, preferred_element_type=F32)
        s_p = jnp.where(ok_prev, s_p + bias_ref[hh, :, :blk], NEG_INF)
        s_c = jnp.where(ok_cur, s_c + bias_ref[hh, :, blk:], NEG_INF)
        mx = jnp.maximum(jnp.max(s_p, axis=-1, keepdims=True),
                         jnp.max(s_c, axis=-1, keepdims=True))
        p_p = jnp.exp(s_p - mx)
        p_c = jnp.exp(s_c - mx)
        den = jnp.sum(p_p, axis=-1, keepdims=True) + jnp.sum(p_c, axis=-1, keepdims=True)
        acc = (jnp.dot(p_p.astype(BF16), vp_ref[:, sl], preferred_element_type=F32)
               + jnp.dot(p_c.astype(BF16), vc_ref[:, sl], preferred_element_type=F32))
        o_ref[:, sl] = acc / den
        lses.append(mx + jnp.log(den))
    lse_ref[...] = jnp.concatenate(lses, axis=-1)


def _attn_group(proj, rel_bias, gi, window, dilation, qcol, kcol, vcol):
    t = proj.shape[0]
    blk = ATTN_BLOCK
    w_steps = window // dilation
    blocks_per_res = t // dilation // blk
    nblk = t // blk
    a = np.arange(blk)[:, None]
    cc = np.arange(2 * blk)[None, :]
    bucket = _t5_bucket(np.maximum(blk + a - cc, 0) * dilation)

    def prev_map(m):
        return jnp.where(m % blocks_per_res > 0, m - 1, m)

    kern = functools.partial(_attn_kernel, gi * HEADS_PER_GROUP, w_steps, blocks_per_res)
    width = A_GROUP_WIDTH
    return pl.pallas_call(
        kern,
        grid=(nblk,),
        in_specs=[
            pl.BlockSpec(memory_space=pltpu.SMEM),
            pl.BlockSpec((blk, 2 * blk), lambda m: (0, 0)),
            pl.BlockSpec((blk, width), lambda m: (m, qcol)),
            pl.BlockSpec((blk, width), lambda m: (prev_map(m), kcol)),
            pl.BlockSpec((blk, width), lambda m: (m, kcol)),
            pl.BlockSpec((blk, width), lambda m: (prev_map(m), vcol)),
            pl.BlockSpec((blk, width), lambda m: (m, vcol)),
        ],
        out_specs=[pl.BlockSpec((blk, width), lambda m: (m, 0)),
                   pl.BlockSpec((blk, HEADS_PER_GROUP), lambda m: (m, 0))],
        out_shape=[jax.ShapeDtypeStruct((t, width), F32),
                   jax.ShapeDtypeStruct((t, HEADS_PER_GROUP), F32)],
        scratch_shapes=[pltpu.VMEM((HEADS_PER_GROUP, blk, 2 * blk), F32)],
        compiler_params=_params(("arbitrary",)),
        name=f"attn_d{dilation}",
    )(rel_bias, jnp.asarray(bucket), proj, proj, proj, proj, proj)


def _retention_kernel(q_ref, k_ref, v0_ref, v1_ref, g0_ref, g1_ref, dmat_ref, zeta_ref, xi_ref,
                      gch_ref, gn_ref, o_ref, state_ref):
    @pl.when(pl.program_id(0) == 0)
    def _():
        state_ref[...] = jnp.zeros_like(state_ref)

    nt = (((1,), (1,)), ((), ()))
    tn = (((0,), (0,)), ((), ()))
    per_half = RET_HEADS // 2
    for hh in range(RET_HEADS):
        qs = slice(hh * RET_QK_DIM, (hh + 1) * RET_QK_DIM)
        vs = slice(hh * RET_V_DIM, (hh + 1) * RET_V_DIM)
        hs = slice((hh % per_half) * RET_V_DIM, (hh % per_half + 1) * RET_V_DIM)
        v_ref, g_ref = (v0_ref, g0_ref) if hh < per_half else (v1_ref, g1_ref)
        q = q_ref[:, qs]
        k = k_ref[:, qs]
        v = v_ref[:, hs]
        state = state_ref[hh]
        s = lax.dot_general(q, k, nt, preferred_element_type=F32) * dmat_ref[hh]
        inner = jnp.dot(s.astype(BF16), v, preferred_element_type=F32)
        cross = jnp.dot(q, state.astype(BF16), preferred_element_type=F32) * xi_ref[hh]
        vz = (v.astype(F32) * zeta_ref[hh]).astype(BF16)
        upd = lax.dot_general(k, vz, tn, preferred_element_type=F32)
        state_ref[hh] = gch_ref[hh] * state + upd
        ret = inner + cross
        mu = jnp.mean(ret, axis=-1, keepdims=True)
        cen = ret - mu
        var = jnp.mean(cen * cen, axis=-1, keepdims=True)
        y = cen * lax.rsqrt(var + GN_EPS) * gn_ref[:, vs]
        o_ref[:, vs] = (y * g_ref[:, hs].astype(F32)).astype(o_ref.dtype)


def _retention_tables():
    c = RET_CHUNK
    hh = np.arange(RET_HEADS, dtype=np.float64)
    log_g = np.log1p(-np.exp2(-5.0 - hh))
    idx = np.arange(c, dtype=np.float64)
    diff = idx[:, None] - idx[None, :]
    dmat = np.where(diff >= 0, np.exp(log_g[:, None, None] * np.maximum(diff, 0.0)), 0.0)
    zeta = np.exp(log_g[:, None] * (c - 1 - idx))[:, :, None]
    xi = np.exp(log_g[:, None] * (idx + 1.0))[:, :, None]
    gch = np.exp(log_g * c)
    f = lambda v: jnp.asarray(v.astype(np.float32))
    return f(dmat), f(zeta), f(xi), f(gch)


def _retention(proj, gn_g, qcol, kcol, vcol, gcol):
    t = proj.shape[0]
    c = RET_CHUNK
    qw = RET_HEADS * RET_QK_DIM
    vw = RET_HEADS * RET_V_DIM
    dmat, zeta, xi, gch = _retention_tables()
    full3 = lambda shp: pl.BlockSpec(shp, lambda n: (0, 0, 0))
    return pl.pallas_call(
        _retention_kernel,
        grid=(t // c,),
        in_specs=[
            pl.BlockSpec((c, qw), lambda n: (n, qcol)),
            pl.BlockSpec((c, qw), lambda n: (n, kcol)),
            pl.BlockSpec((c, vw // 2), lambda n: (n, vcol)),
            pl.BlockSpec((c, vw // 2), lambda n: (n, vcol + 1)),
            pl.BlockSpec((c, vw // 2), lambda n: (n, gcol)),
            pl.BlockSpec((c, vw // 2), lambda n: (n, gcol + 1)),
            full3((RET_HEADS, c, c)),
            full3((RET_HEADS, c, 1)),
            full3((RET_HEADS, c, 1)),
            pl.BlockSpec(memory_space=pltpu.SMEM),
            pl.BlockSpec((1, vw), lambda n: (0, 0)),
        ],
        out_specs=pl.BlockSpec((c, vw), lambda n: (n, 0)),
        out_shape=jax.ShapeDtypeStruct((t, vw), BF16),
        scratch_shapes=[pltpu.VMEM((RET_HEADS, RET_QK_DIM, RET_V_DIM), F32)],
        compiler_params=_params(("arbitrary",)),
        name="retention",
    )(proj, proj, proj, proj, proj, proj, dmat, zeta, xi, gch, gn_g.reshape(1, vw))


def _merge_kernel(o1_ref, l1_ref, o2_ref, l2_ref, o3_ref, l3_ref, yb_ref, ga_ref, gb_ref,
                  pa_ref, pb_ref, out_ref, ya_ref):
    @pl.when(pl.program_id(1) == 0)
    def _():
        l1 = l1_ref[...]
        l2 = l2_ref[...]
        l3 = l3_ref[...]
        mx = jnp.maximum(jnp.maximum(l1, l2), l3)
        e1 = jnp.exp(l1 - mx)
        e2 = jnp.exp(l2 - mx)
        e3 = jnp.exp(l3 - mx)
        den = e1 + e2 + e3
        a1, a2, a3 = e1 / den, e2 / den, e3 / den
        for hh in range(HEADS_PER_GROUP):
            sl = slice(hh * HEAD_DIM, (hh + 1) * HEAD_DIM)
            ya = (a1[:, hh:hh + 1] * o1_ref[:, sl] + a2[:, hh:hh + 1] * o2_ref[:, sl]
                  + a3[:, hh:hh + 1] * o3_ref[:, sl])
            ya_ref[:, sl] = ya.astype(ya_ref.dtype)

    za = jnp.dot(ya_ref[...], pa_ref[...], preferred_element_type=F32)
    zb = jnp.dot(yb_ref[...], pb_ref[...], preferred_element_type=F32)
    out_ref[...] = (ga_ref[...].astype(F32) * za + gb_ref[...].astype(F32) * zb).astype(out_ref.dtype)


def _merge(o1, l1, o2, l2, o3, l3, yb, proj, ga_col, gb_col, pa, pb, tm=512, tn=1024):
    t = o1.shape[0]
    wa = o1.shape[1]
    wb = yb.shape[1]
    n = pa.shape[1]
    hg = HEADS_PER_GROUP
    ratio = tn // COLBLK
    o_spec = lambda: pl.BlockSpec((tm, wa), lambda i, j: (i, 0))
    l_spec = lambda: pl.BlockSpec((tm, hg), lambda i, j: (i, 0))
    return pl.pallas_call(
        _merge_kernel,
        grid=(t // tm, n // tn),
        in_specs=[
            o_spec(), l_spec(), o_spec(), l_spec(), o_spec(), l_spec(),
            pl.BlockSpec((tm, wb), lambda i, j: (i, 0)),
            pl.BlockSpec((tm, tn), lambda i, j: (i, ga_col // ratio + j)),
            pl.BlockSpec((tm, tn), lambda i, j: (i, gb_col // ratio + j)),
            pl.BlockSpec((wa, tn), lambda i, j: (0, j)),
            pl.BlockSpec((wb, tn), lambda i, j: (0, j)),
        ],
        out_specs=pl.BlockSpec((tm, tn), lambda i, j: (i, j)),
        out_shape=jax.ShapeDtypeStruct((t, n), BF16),
        scratch_shapes=[pltpu.VMEM((tm, wa), BF16)],
        compiler_params=_params(("parallel", "arbitrary")),
        name="merge",
    )(o1, l1, o2, l2, o3, l3, yb, proj, proj, pa, pb)


def _oproj_kernel(x_ref, m_ref, w_ref, g_ref, o_ref):
    z = jnp.dot(m_ref[...], w_ref[...], preferred_element_type=F32)
    o_ref[...] = x_ref[...] + g_ref[...] * z


def _oproj(x, merged, w_bf, mod, gate_blk, tm=512, tn=1024):
    t, d_model = x.shape
    k = merged.shape[1]
    per = d_model // tn
    return pl.pallas_call(
        _oproj_kernel,
        grid=(t // tm, d_model // tn),
        in_specs=[
            pl.BlockSpec((tm, tn), lambda i, j: (i, j)),
            pl.BlockSpec((tm, k), lambda i, j: (i, 0)),
            pl.BlockSpec((k, tn), lambda i, j: (0, j)),
            pl.BlockSpec((1, tn), lambda i, j: (0, gate_blk * per + j)),
        ],
        out_specs=pl.BlockSpec((tm, tn), lambda i, j: (i, j)),
        out_shape=jax.ShapeDtypeStruct((t, d_model), F32),
        compiler_params=_params(("parallel", "arbitrary")),
        name="oproj",
    )(x, merged, w_bf, mod)


def _pack_pair(lo, hi):
    lo_b = pltpu.bitcast(lo.astype(BF16).astype(F32), U32)
    hi_b = pltpu.bitcast(hi.astype(BF16).astype(F32), U32)
    return (lo_b >> 16) | (hi_b & jnp.uint32(0xFFFF0000))


def _unpack_pair(w):
    lo = pltpu.bitcast(w << 16, F32)
    hi = pltpu.bitcast(w & jnp.uint32(0xFFFF0000), F32)
    return lo, hi


def _route_kernel(x_ref, g_ref, sc_ref, sh_ref, wt_ref, rb_ref, h_ref, hp_ref, idx_ref, rank_ref,
                  wgt_ref, cnt_ref):
    @pl.when(pl.program_id(0) == 0)
    def _():
        cnt_ref[...] = jnp.zeros_like(cnt_ref)

    x = x_ref[...]
    tm, d_model = x.shape
    inv = lax.rsqrt(jnp.mean(x * x, axis=-1, keepdims=True) + RMS_EPS)
    h = (x * inv * g_ref[...]) * (1.0 + sc_ref[...]) + sh_ref[...]
    h_ref[...] = h.astype(h_ref.dtype)
    half = d_model // 2
    hp_ref[...] = _pack_pair(h[:, :half], h[:, half:])

    ne = N_EXPERTS
    per = ne // N_GROUPS
    logits = lax.dot_general(wt_ref[...], h, (((1,), (1,)), ((), ())),
                             precision=lax.Precision.HIGHEST,
                             preferred_element_type=F32)
    scores = jax.nn.sigmoid(logits)
    sel = scores + rb_ref[...]
    eidx = lax.broadcasted_iota(I32, (ne, tm), 0).astype(F32)
    minus_inf = -jnp.inf

    sel3 = sel.reshape(N_GROUPS, per, tm)
    sub = lax.broadcasted_iota(I32, (N_GROUPS, per, tm), 1).astype(F32)
    m1 = jnp.max(sel3, axis=1, keepdims=True)
    first = jnp.min(jnp.where(sel3 == m1, sub, float(per)), axis=1, keepdims=True)
    m2 = jnp.max(jnp.where(sub == first, minus_inf, sel3), axis=1, keepdims=True)
    grp = (m1 + m2).reshape(N_GROUPS, tm)

    gidx = lax.broadcasted_iota(I32, (N_GROUPS, tm), 0).astype(F32)
    gmask = jnp.zeros((N_GROUPS, tm), F32)
    work = grp
    for _ in range(TOPK_GROUPS):
        mx = jnp.max(work, axis=0, keepdims=True)
        pick = jnp.min(jnp.where(work == mx, gidx, float(N_GROUPS)), axis=0, keepdims=True)
        hit = gidx == pick
        gmask = jnp.where(hit, 1.0, gmask)
        work = jnp.where(hit, minus_inf, work)
    emask = jnp.broadcast_to(gmask.reshape(N_GROUPS, 1, tm), (N_GROUPS, per, tm)).reshape(ne, tm)

    work = jnp.where(emask > 0.0, sel, minus_inf)
    onehot = jnp.zeros((ne, tm), F32)
    idx_rows, w_rows = [], []
    for _ in range(TOP_K):
        mx = jnp.max(work, axis=0, keepdims=True)
        pick = jnp.min(jnp.where(work == mx, eidx, float(ne)), axis=0, keepdims=True)
        hit = eidx == pick
        onehot = jnp.where(hit, 1.0, onehot)
        work = jnp.where(hit, minus_inf, work)
        idx_rows.append(pick)
        w_rows.append(jnp.sum(jnp.where(hit, scores, 0.0), axis=0, keepdims=True))
    w_all = jnp.concatenate(w_rows, axis=0)
    wgt_ref[...] = w_all / jnp.sum(w_all, axis=0, keepdims=True) * ROUTED_SCALE
    idx_ref[...] = jnp.concatenate(idx_rows, axis=0).astype(I32)

    ra = lax.broadcasted_iota(I32, (tm, tm), 0)
    rb = lax.broadcasted_iota(I32, (tm, tm), 1)
    tri = jnp.where(ra <= rb, 1.0, 0.0).astype(BF16)
    incl = jnp.dot(onehot.astype(BF16), tri, preferred_element_type=F32)
    before = incl - onehot + cnt_ref[...]
    rank_rows = [jnp.sum(jnp.where(eidx == idx_rows[kk], before, 0.0), axis=0, keepdims=True)
                 for kk in range(TOP_K)]
    rank_ref[...] = jnp.concatenate(rank_rows, axis=0).astype(I32)
    cnt_ref[...] = cnt_ref[...] + jnp.sum(onehot, axis=1, keepdims=True)


def _route(x1, g, mod, sc_blk, sh_blk, router_w, router_bias, tm=256):
    t, d_model = x1.shape
    ne = N_EXPERTS
    vec = lambda k: pl.BlockSpec((1, d_model), lambda i, k=k: (0, k))
    tok = lambda: pl.BlockSpec((TOP_K, tm), lambda i: (0, i))
    return pl.pallas_call(
        _route_kernel,
        grid=(t // tm,),
        in_specs=[pl.BlockSpec((tm, d_model), lambda i: (i, 0)),
                  pl.BlockSpec((1, d_model), lambda i: (0, 0)),
                  vec(sc_blk), vec(sh_blk),
                  pl.BlockSpec((ne, d_model), lambda i: (0, 0)),
                  pl.BlockSpec((ne, 1), lambda i: (0, 0))],
        out_specs=[pl.BlockSpec((tm, d_model), lambda i: (i, 0)),
                   pl.BlockSpec((tm, d_model // 2), lambda i: (i, 0)),
                   tok(), tok(), tok(),
                   pl.BlockSpec((ne, 1), lambda i: (0, 0))],
        out_shape=[jax.ShapeDtypeStruct((t, d_model), BF16),
                   jax.ShapeDtypeStruct((t, d_model // 2), U32),
                   jax.ShapeDtypeStruct((TOP_K, t), I32),
                   jax.ShapeDtypeStruct((TOP_K, t), I32),
                   jax.ShapeDtypeStruct((TOP_K, t), F32),
                   jax.ShapeDtypeStruct((ne, 1), F32)],
        compiler_params=_params(("arbitrary",)),
        name="route",
    )(x1, g.reshape(1, d_model), mod, mod, router_w.T, router_bias.reshape(ne, 1))


def _dispatch_kernel(pos_ref, hp_ref, zeros_ref, xs_ref, sem):
    del zeros_ref
    tm = hp_ref.shape[0]

    def body(tt, carry):
        for kk in range(TOP_K):
            dst = pos_ref[0, 0, tt * TOP_K + kk]
            pltpu.make_async_copy(hp_ref.at[pl.ds(tt, 1)], xs_ref.at[pl.ds(dst, 1)], sem).start()
        return carry

    lax.fori_loop(0, tm, body, 0)
    pltpu.make_async_copy(xs_ref.at[pl.ds(0, tm * TOP_K)], xs_ref.at[pl.ds(0, tm * TOP_K)], sem).wait()


def _dispatch(hp, pos, rows, tm=256):
    t, width = hp.shape
    pos3 = pos.reshape(t // tm, 1, tm * TOP_K)
    zeros = jnp.zeros((rows, width), U32)
    return pl.pallas_call(
        _dispatch_kernel,
        grid=(t // tm,),
        in_specs=[pl.BlockSpec((1, 1, tm * TOP_K), lambda i: (i, 0, 0), memory_space=pltpu.SMEM),
                  pl.BlockSpec((tm, width), lambda i: (i, 0)),
                  pl.BlockSpec(memory_space=pl.ANY)],
        out_specs=pl.BlockSpec(memory_space=pl.ANY),
        out_shape=jax.ShapeDtypeStruct((rows, width), U32),
        scratch_shapes=[pltpu.SemaphoreType.DMA(())],
        input_output_aliases={2: 0},
        compiler_params=_params(("arbitrary",)),
        name="dispatch",
    )(pos3, hp, zeros)


def _experts_kernel(be_ref, nv_ref, x_ref, wg_ref, wu_ref, wd_ref, y_ref, wg_bf, wu_bf, wd_bf):
    b = pl.program_id(0)
    prev = be_ref[jnp.maximum(b - 1, 0)]
    changed = (b == 0) | (be_ref[b] != prev)

    @pl.when(changed)
    def _():
        wg_bf[...] = wg_ref[...].astype(BF16)
        wu_bf[...] = wu_ref[...].astype(BF16)
        wd_bf[...] = wd_ref[...].astype(BF16)

    @pl.when(b < nv_ref[0])
    def _():
        lo, hi = _unpack_pair(x_ref[...])
        half = lo.shape[1]
        lo = lo.astype(BF16)
        hi = hi.astype(BF16)
        gate = (jnp.dot(lo, wg_bf[:half, :], preferred_element_type=F32)
                + jnp.dot(hi, wg_bf[half:, :], preferred_element_type=F32))
        up = (jnp.dot(lo, wu_bf[:half, :], preferred_element_type=F32)
              + jnp.dot(hi, wu_bf[half:, :], preferred_element_type=F32))
        act = (_silu(gate) * up).astype(BF16)
        y = jnp.dot(act, wd_bf[...], preferred_element_type=F32)
        y_ref[...] = _pack_pair(y[:, :half], y[:, half:])

    @pl.when(b >= nv_ref[0])
    def _():
        y_ref[...] = jnp.zeros_like(y_ref)


def _experts(xs, block_e, n_valid, wg, wu, wd, bm=EXPERT_ROWS):
    rows, width = xs.shape
    ne, d_model, de = wg.shape
    nb = rows // bm
    row_map = lambda b, be, nv: (jnp.minimum(b, nv[0] - 1), 0)
    grid_spec = pltpu.PrefetchScalarGridSpec(
        num_scalar_prefetch=2,
        grid=(nb,),
        in_specs=[
            pl.BlockSpec((bm, width), row_map),
            pl.BlockSpec((None, d_model, de), lambda b, be, nv: (be[b], 0, 0)),
            pl.BlockSpec((None, d_model, de), lambda b, be, nv: (be[b], 0, 0)),
            pl.BlockSpec((None, de, d_model), lambda b, be, nv: (be[b], 0, 0)),
        ],
        out_specs=pl.BlockSpec((bm, width), lambda b, be, nv: (b, 0)),
        scratch_shapes=[pltpu.VMEM((d_model, de), BF16),
                        pltpu.VMEM((d_model, de), BF16),
                        pltpu.VMEM((de, d_model), BF16)],
    )
    return pl.pallas_call(
        _experts_kernel,
        grid_spec=grid_spec,
        out_shape=jax.ShapeDtypeStruct((rows, width), U32),
        compiler_params=_params(("arbitrary",)),
        name="experts",
    )(block_e, n_valid, xs, wg, wu, wd)


def _combine_kernel(pos_ref, x_ref, h_ref, wt_ref, g_ref, sg_ref, su_ref, sd_ref, ys_ref, o_ref,
                    buf_ref, sem):
    tm = x_ref.shape[0]

    def body(tt, carry):
        for kk in range(TOP_K):
            src = pos_ref[0, 0, tt * TOP_K + kk]
            pltpu.make_async_copy(ys_ref.at[pl.ds(src, 1)], buf_ref.at[kk, pl.ds(tt, 1)], sem).start()
        return carry

    lax.fori_loop(0, tm, body, 0)
    h = h_ref[...]
    act = (_silu(jnp.dot(h, sg_ref[...], preferred_element_type=F32))
           * jnp.dot(h, su_ref[...], preferred_element_type=F32)).astype(BF16)
    shared = jnp.dot(act, sd_ref[...], preferred_element_type=F32)
    for kk in range(TOP_K):
        pltpu.make_async_copy(ys_ref.at[pl.ds(0, tm)], buf_ref.at[kk], sem).wait()
    half = buf_ref.shape[2]
    wt = wt_ref[...]
    lo_acc = jnp.zeros((tm, half), F32)
    hi_acc = jnp.zeros((tm, half), F32)
    for kk in range(TOP_K):
        lo, hi = _unpack_pair(buf_ref[kk])
        wk = wt[:, kk:kk + 1]
        lo_acc = lo_acc + wk * lo
        hi_acc = hi_acc + wk * hi
    g = g_ref[...]
    o_ref[:, :half] = x_ref[:, :half] + g[:, :half] * (lo_acc + shared[:, :half])
    o_ref[:, half:] = x_ref[:, half:] + g[:, half:] * (hi_acc + shared[:, half:])


def _combine(x1, h2, pos, wts, mod, gate_blk, sg, su, sd, ys, tm=256):
    t, d_model = x1.shape
    ds_ = sg.shape[1]
    width = ys.shape[1]
    pos3 = pos.reshape(t // tm, 1, tm * TOP_K)
    return pl.pallas_call(
        _combine_kernel,
        grid=(t // tm,),
        in_specs=[pl.BlockSpec((1, 1, tm * TOP_K), lambda i: (i, 0, 0), memory_space=pltpu.SMEM),
                  pl.BlockSpec((tm, d_model), lambda i: (i, 0)),
                  pl.BlockSpec((tm, d_model), lambda i: (i, 0)),
                  pl.BlockSpec((tm, TOP_K), lambda i: (i, 0)),
                  pl.BlockSpec((1, d_model), lambda i: (0, gate_blk)),
                  pl.BlockSpec((d_model, ds_), lambda i: (0, 0)),
                  pl.BlockSpec((d_model, ds_), lambda i: (0, 0)),
                  pl.BlockSpec((ds_, d_model), lambda i: (0, 0)),
                  pl.BlockSpec(memory_space=pl.ANY)],
        out_specs=pl.BlockSpec((tm, d_model), lambda i: (i, 0)),
        out_shape=jax.ShapeDtypeStruct((t, d_model), F32),
        scratch_shapes=[pltpu.VMEM((TOP_K, tm, width), U32),
                        pltpu.SemaphoreType.DMA(())],
        compiler_params=_params(("arbitrary",)),
        name="combine",
    )(pos3, x1, h2, wts, mod, sg, su, sd, ys)


def _layout(counts, idx_t, rank_t, bm, n_blocks):
    counts = counts.reshape(-1).astype(I32)
    padded = (counts + bm - 1) // bm * bm
    pad_end = jnp.cumsum(padded)
    pad_start = pad_end - padded
    pos = (pad_start[idx_t] + rank_t).T.reshape(-1)
    n_valid = (pad_end[-1] // bm).astype(I32).reshape(1)
    block_e = jnp.minimum(
        jnp.searchsorted(pad_end, jnp.arange(n_blocks, dtype=I32) * bm, side="right"),
        N_EXPERTS - 1).astype(I32)
    return pos, block_e, n_valid


def _layer(x, c, rel_bias, w_ada, b_ada, ln1_g, w_in, q_norm_g, k_norm_g, ret_gn_g, p_a, p_b, w_o,
           ln2_g, router_w, router_bias, w_gate_e, w_up_e, w_down_e, w_gate_s, w_up_s, w_down_s):
    t, d_model = x.shape
    dils = tuple(d for _, d in DILATED_GROUPS)

    mod = _ada(c.reshape(d_model), w_ada, b_ada)
    h = _norm1(x, ln1_g, mod)
    h_orders = (h, _to_residue_major(h, dils[1]), _to_residue_major(h, dils[2]))
    cos_tab, sin_tab = _rotary_tables(t)
    proj = _inproj(h_orders, w_in.astype(BF16), q_norm_g, k_norm_g, cos_tab, sin_tab)

    a_blocks = N_HEADS_A * HEAD_DIM // COLBLK
    attn = []
    for gi, (win, dil) in enumerate(DILATED_GROUPS):
        attn.append(_attn_group(proj, rel_bias, gi, win, dil,
                                gi, a_blocks + gi, 2 * a_blocks + gi))
    base = 3 * a_blocks
    rq = RET_HEADS * RET_QK_DIM // COLBLK
    vw_blk = RET_HEADS * RET_V_DIM // COLBLK
    qcol = base
    kcol = base + rq
    vcol_blk = base + 2 * rq
    gcol_blk = vcol_blk + vw_blk
    ga_blk = gcol_blk + vw_blk
    gb_blk = ga_blk + d_model // COLBLK
    y_b = _retention(proj, ret_gn_g, qcol, kcol, vcol_blk, gcol_blk)
    (o1, l1), (o2, l2), (o3, l3) = attn
    o2, l2 = _from_residue_major(o2, dils[1]), _from_residue_major(l2, dils[1])
    o3, l3 = _from_residue_major(o3, dils[2]), _from_residue_major(l3, dils[2])
    merged = _merge(o1, l1, o2, l2, o3, l3, y_b, proj, ga_blk, gb_blk,
                    p_a.astype(BF16), p_b.astype(BF16))
    x1 = _oproj(x, merged, w_o.astype(BF16), mod, 2)

    h2, h2p, idx_t, rank_t, wgt_t, counts = _route(x1, ln2_g, mod, 4, 3, router_w, router_bias)
    bm = EXPERT_ROWS
    n_blocks = (t * TOP_K + N_EXPERTS * (bm - 1) + bm - 1) // bm
    pos, block_e, n_valid = _layout(counts, idx_t, rank_t, bm, n_blocks)
    xs = _dispatch(h2p, pos, n_blocks * bm)
    ys = _experts(xs, block_e, n_valid, w_gate_e, w_up_e, w_down_e)
    return _combine(x1, h2, pos, wgt_t.T, mod, 5, w_gate_s.astype(BF16), w_up_s.astype(BF16),
                    w_down_s.astype(BF16), ys)


def kernel(x, c, rel_bias, w_ada, b_ada, ln1_g, w_in, q_norm_g, k_norm_g, ret_gn_g, p_a, p_b, w_o,
           ln2_g, router_w, router_bias, w_gate_e, w_up_e, w_down_e, w_gate_s, w_up_s, w_down_s):
    b, s, d_model = x.shape
    depth = w_ada.shape[0]
    outs = []
    for bi in range(b):
        xb = x[bi]
        for l in range(depth):
            xb = _layer(xb, c[bi], rel_bias, w_ada[l], b_ada[l], ln1_g[l], w_in[l], q_norm_g[l],
                        k_norm_g[l], ret_gn_g[l], p_a[l], p_b[l], w_o[l], ln2_g[l], router_w[l],
                        router_bias[l], w_gate_e[l], w_up_e[l], w_down_e[l], w_gate_s[l],
                        w_up_s[l], w_down_s[l])
        outs.append(xb)
    return jnp.stack(outs, axis=0)
```

```python
import functools

import numpy as np
import jax
import jax.numpy as jnp
from jax import lax
from jax.experimental import pallas as pl
from jax.experimental.pallas import tpu as pltpu

F32 = jnp.float32
BF16 = jnp.bfloat16
U32 = jnp.uint32
I32 = jnp.int32

HEAD_DIM = 128
DILATED_GROUPS = ((128, 1), (512, 4), (2048, 16))
HEADS_PER_GROUP = 8
N_HEADS_A = HEADS_PER_GROUP * len(DILATED_GROUPS)
A_GROUP_WIDTH = HEADS_PER_GROUP * HEAD_DIM
ATTN_BLOCK = 128
NUM_BUCKETS = 32
MAX_DISTANCE = 2048
NEG_INF = -1e30
RET_HEADS = 8
RET_QK_DIM = 128
RET_V_DIM = 256
RET_CHUNK = 128
ROPE_BASE = 10000.0
GN_EPS = 1e-5
N_EXPERTS = 64
N_GROUPS = 8
TOPK_GROUPS = 4
TOP_K = 8
ROUTED_SCALE = 2.5
RMS_EPS = 1e-6

LANE = 128
COLBLK = 1024
VMEM_LIMIT = 56 * 1024 * 1024
EXPERT_ROWS = 256


def _params(sem, vmem=VMEM_LIMIT):
    return pltpu.CompilerParams(dimension_semantics=sem, vmem_limit_bytes=vmem)


def _silu(v):
    return v * jax.nn.sigmoid(v)


def _ada_kernel(c_ref, w_ref, b_ref, o_ref):
    sc = _silu(c_ref[...])
    o_ref[...] = jnp.sum(w_ref[...] * sc, axis=0, keepdims=True) + b_ref[...]


def _ada(c, w, b, tn=512):
    d, n = w.shape
    return pl.pallas_call(
        _ada_kernel,
        grid=(n // tn,),
        in_specs=[pl.BlockSpec((d, 1), lambda j: (0, 0)),
                  pl.BlockSpec((d, tn), lambda j: (0, j)),
                  pl.BlockSpec((1, tn), lambda j: (0, j))],
        out_specs=pl.BlockSpec((1, tn), lambda j: (0, j)),
        out_shape=jax.ShapeDtypeStruct((1, n), F32),
        compiler_params=_params(("parallel",)),
        name="ada",
    )(c.reshape(d, 1), w, b.reshape(1, n))


def _norm1_kernel(x_ref, g_ref, sc_ref, sh_ref, o_ref):
    x = x_ref[...]
    inv = lax.rsqrt(jnp.mean(x * x, axis=-1, keepdims=True) + RMS_EPS)
    o_ref[...] = ((x * inv * g_ref[...]) * (1.0 + sc_ref[...]) + sh_ref[...]).astype(o_ref.dtype)


def _norm1(x, g, mod, tm=512):
    t, d_model = x.shape
    vec = lambda k: pl.BlockSpec((1, d_model), lambda i, k=k: (0, k))
    return pl.pallas_call(
        _norm1_kernel,
        grid=(t // tm,),
        in_specs=[pl.BlockSpec((tm, d_model), lambda i: (i, 0)),
                  pl.BlockSpec((1, d_model), lambda i: (0, 0)),
                  vec(1), vec(0)],
        out_specs=pl.BlockSpec((tm, d_model), lambda i: (i, 0)),
        out_shape=jax.ShapeDtypeStruct((t, d_model), BF16),
        compiler_params=_params(("parallel",)),
        name="norm1",
    )(x, g.reshape(1, d_model), mod, mod)


def _to_residue_major(a, d):
    t, w = a.shape
    return a.reshape(t // d, d, w).transpose(1, 0, 2).reshape(t, w)


def _from_residue_major(a, d):
    t, w = a.shape
    return a.reshape(d, t // d, w).transpose(1, 0, 2).reshape(t, w)


EPI_QNORM, EPI_KNORM, EPI_PLAIN, EPI_ROT_Q, EPI_ROT_K, EPI_SILU, EPI_SIGMOID = range(7)


def _inproj_kernel(colblk_ref, order_ref, epi_ref, h0_ref, h1_ref, h2_ref, w_ref, qg_ref, kg_ref,
                   cos_ref, sin_ref, o_ref, acc_ref):
    del colblk_ref
    j = pl.program_id(1)
    epi = epi_ref[j]
    order = order_ref[j]
    for which, h_ref in enumerate((h0_ref, h1_ref, h2_ref)):
        @pl.when(order == which)
        def _(h_ref=h_ref):
            acc_ref[...] = jnp.dot(h_ref[...], w_ref[...], preferred_element_type=F32)
    nh = acc_ref.shape[1] // HEAD_DIM

    def head_norm(gain, scale):
        for hh in range(nh):
            a = acc_ref[:, hh * HEAD_DIM:(hh + 1) * HEAD_DIM]
            inv = lax.rsqrt(jnp.mean(a * a, axis=-1, keepdims=True) + RMS_EPS)
            o_ref[:, hh * HEAD_DIM:(hh + 1) * HEAD_DIM] = (
                (a * inv * gain) * scale).astype(o_ref.dtype)

    def rotary(scale):
        cos = cos_ref[...]
        sin = sin_ref[...]
        for hh in range(nh):
            a = acc_ref[:, hh * HEAD_DIM:(hh + 1) * HEAD_DIM]
            rot = pltpu.roll(a, HEAD_DIM // 2, 1)
            o_ref[:, hh * HEAD_DIM:(hh + 1) * HEAD_DIM] = (
                (a * cos + rot * sin) * scale).astype(o_ref.dtype)

    @pl.when(epi == EPI_QNORM)
    def _():
        head_norm(qg_ref[...], HEAD_DIM ** -0.5)

    @pl.when(epi == EPI_KNORM)
    def _():
        head_norm(kg_ref[...], 1.0)

    @pl.when(epi == EPI_PLAIN)
    def _():
        o_ref[...] = acc_ref[...].astype(o_ref.dtype)

    @pl.when(epi == EPI_ROT_Q)
    def _():
        rotary(1.0)

    @pl.when(epi == EPI_ROT_K)
    def _():
        rotary(RET_QK_DIM ** -0.5)

    @pl.when(epi == EPI_SILU)
    def _():
        o_ref[...] = _silu(acc_ref[...]).astype(o_ref.dtype)

    @pl.when(epi == EPI_SIGMOID)
    def _():
        o_ref[...] = jax.nn.sigmoid(acc_ref[...]).astype(o_ref.dtype)


def _inproj_plan(d_model):
    a_blocks = N_HEADS_A * HEAD_DIM // COLBLK
    groups = len(DILATED_GROUPS)
    per_group = a_blocks // groups
    rq = RET_HEADS * RET_QK_DIM // COLBLK
    rv = RET_HEADS * RET_V_DIM // COLBLK
    gd = d_model // COLBLK
    seg_epi = ([EPI_QNORM] * a_blocks + [EPI_KNORM] * a_blocks + [EPI_PLAIN] * a_blocks
               + [EPI_ROT_Q] * rq + [EPI_ROT_K] * rq + [EPI_PLAIN] * rv + [EPI_SILU] * rv
               + [EPI_SIGMOID] * (2 * gd))
    order_of = [0] * len(seg_epi)
    for seg in range(3):
        for blk in range(a_blocks):
            order_of[seg * a_blocks + blk] = blk // per_group
    steps = sorted(range(len(seg_epi)), key=lambda cb: (order_of[cb], cb))
    colblk = np.array(steps, np.int32)
    order = np.array([order_of[cb] for cb in steps], np.int32)
    epi = np.array([seg_epi[cb] for cb in steps], np.int32)
    return colblk, order, epi


def _inproj(h_orders, w_bf, qg, kg, cos_tab, sin_tab, tm=1024):
    t, d_model = h_orders[0].shape
    n = w_bf.shape[1]
    colblk, order, epi = _inproj_plan(d_model)
    assert len(colblk) * COLBLK == n
    row = lambda width: pl.BlockSpec((tm, width), lambda i, j, cb, od, ep: (i, 0))
    one = lambda width: pl.BlockSpec((1, width), lambda i, j, cb, od, ep: (0, 0))
    grid_spec = pltpu.PrefetchScalarGridSpec(
        num_scalar_prefetch=3,
        grid=(t // tm, len(colblk)),
        in_specs=[
            row(d_model), row(d_model), row(d_model),
            pl.BlockSpec((d_model, COLBLK), lambda i, j, cb, od, ep: (0, cb[j])),
            one(HEAD_DIM), one(HEAD_DIM), row(HEAD_DIM), row(HEAD_DIM),
        ],
        out_specs=pl.BlockSpec((tm, COLBLK), lambda i, j, cb, od, ep: (i, cb[j])),
        scratch_shapes=[pltpu.VMEM((tm, COLBLK), F32)],
    )
    return pl.pallas_call(
        _inproj_kernel,
        grid_spec=grid_spec,
        out_shape=jax.ShapeDtypeStruct((t, n), BF16),
        compiler_params=_params(("parallel", "arbitrary")),
        name="inproj",
    )(jnp.asarray(colblk), jnp.asarray(order), jnp.asarray(epi),
      *h_orders, w_bf, qg.reshape(1, HEAD_DIM), kg.reshape(1, HEAD_DIM), cos_tab, sin_tab)


def _rotary_tables(t):
    half = RET_QK_DIM // 2
    inv = ROPE_BASE ** (-np.arange(0, RET_QK_DIM, 2, dtype=np.float64) / RET_QK_DIM)
    ang = np.arange(t, dtype=np.float64)[:, None] * inv[None, :]
    cos, sin = np.cos(ang), np.sin(ang)
    del half
    cos_tab = np.concatenate([cos, cos], axis=1).astype(np.float32)
    sin_tab = np.concatenate([-sin, sin], axis=1).astype(np.float32)
    return jnp.asarray(cos_tab), jnp.asarray(sin_tab)


def _t5_bucket(dist):
    max_exact = NUM_BUCKETS // 2
    safe = np.maximum(dist, 1).astype(np.float32)
    large = max_exact + (np.log(safe / max_exact) / np.log(MAX_DISTANCE / max_exact)
                         * (NUM_BUCKETS - max_exact)).astype(np.int32)
    return np.where(dist < max_exact, dist, np.minimum(large, NUM_BUCKETS - 1)).astype(np.int32)


def _attn_kernel(head0, w_steps, blocks_per_res, tab_ref, bucket_ref, q_ref, kp_ref, kc_ref,
                 vp_ref, vc_ref, o_ref, lse_ref, bias_ref):
    m_idx = pl.program_id(0)
    blk = ATTN_BLOCK

    @pl.when(m_idx == 0)
    def _():
        bucket = bucket_ref[...]
        for hh in range(HEADS_PER_GROUP):
            bias = jnp.zeros(bucket.shape, F32)
            for b in range(NUM_BUCKETS):
                bias = jnp.where(bucket == b, tab_ref[b, head0 + hh], bias)
            bias_ref[hh] = bias

    a = lax.broadcasted_iota(I32, (blk, blk), 0)
    cc = lax.broadcasted_iota(I32, (blk, blk), 1)
    has_prev = (m_idx % blocks_per_res) > 0
    ok_prev = ((blk + a - cc) <= w_steps) & has_prev
    ok_cur = ((a - cc) >= 0) & ((a - cc) <= w_steps)
    nt = (((1,), (1,)), ((), ()))
    lses = []
    for hh in range(HEADS_PER_GROUP):
        sl = slice(hh * HEAD_DIM, (hh + 1) * HEAD_DIM)
        q = q_ref[:, sl]
        s_p = lax.dot_general(q, kp_ref[:, sl], nt, preferred_element_type=F32)
        s_c = lax.dot_general(q, kc_ref[:, sl], nt, preferred_element_type=F32)
        s_p = jnp.where(ok_prev, s_p + bias_ref[hh, :, :blk], NEG_INF)
        s_c = jnp.where(ok_cur, s_c + bias_ref[hh, :, blk:], NEG_INF)
        mx = jnp.maximum(jnp.max(s_p, axis=-1, keepdims=True),
                         jnp.max(s_c, axis=-1, keepdims=True))
        p_p = jnp.exp(s_p - mx)
        p_c = jnp.exp(s_c - mx)
        den = jnp.sum(p_p, axis=-1, keepdims=True) + jnp.sum(p_c, axis=-1, keepdims=True)
        acc = (jnp.dot(p_p.astype(BF16), vp_ref[:, sl], preferred_element_type=F32)
               + jnp.dot(p_c.astype(BF16), vc_ref[:, sl], preferred_element_type=F32))
        o_ref[:, sl] = acc / den
        lses.append(mx + jnp.log(den))
    lse_ref[...] = jnp.concatenate(lses, axis=-1)


def _attn_group(proj, rel_bias, gi, window, dilation, qcol, kcol, vcol):
    t = proj.shape[0]
    blk = ATTN_BLOCK
    w_steps = window // dilation
    blocks_per_res = t // dilation // blk
    nblk = t // blk
    a = np.arange(blk)[:, None]
    cc = np.arange(2 * blk)[None, :]
    bucket = _t5_bucket(np.maximum(blk + a - cc, 0) * dilation)

    def prev_map(m):
        return jnp.where(m % blocks_per_res > 0, m - 1, m)

    kern = functools.partial(_attn_kernel, gi * HEADS_PER_GROUP, w_steps, blocks_per_res)
    width = A_GROUP_WIDTH
    return pl.pallas_call(
        kern,
        grid=(nblk,),
        in_specs=[
            pl.BlockSpec(memory_space=pltpu.SMEM),
            pl.BlockSpec((blk, 2 * blk), lambda m: (0, 0)),
            pl.BlockSpec((blk, width), lambda m: (m, qcol)),
            pl.BlockSpec((blk, width), lambda m: (prev_map(m), kcol)),
            pl.BlockSpec((blk, width), lambda m: (m, kcol)),
            pl.BlockSpec((blk, width), lambda m: (prev_map(m), vcol)),
            pl.BlockSpec((blk, width), lambda m: (m, vcol)),
        ],
        out_specs=[pl.BlockSpec((blk, width), lambda m: (m, 0)),
                   pl.BlockSpec((blk, HEADS_PER_GROUP), lambda m: (m, 0))],
        out_shape=[jax.ShapeDtypeStruct((t, width), F32),
                   jax.ShapeDtypeStruct((t, HEADS_PER_GROUP), F32)],
        scratch_shapes=[pltpu.VMEM((HEADS_PER_GROUP, blk, 2 * blk), F32)],
        compiler_params=_params(("arbitrary",)),
        name=f"attn_d{dilation}",
    )(rel_bias, jnp.asarray(bucket), proj, proj, proj, proj, proj)


def _retention_kernel(q_ref, k_ref, v0_ref, v1_ref, g0_ref, g1_ref, dmat_ref, zeta_ref, xi_ref,
                      gch_ref, gn_ref, o_ref, state_ref):
    @pl.when(pl.program_id(0) == 0)
    def _():
        state_ref[...] = jnp.zeros_like(state_ref)

    nt = (((1,), (1,)), ((), ()))
    tn = (((0,), (0,)), ((), ()))
    per_half = RET_HEADS // 2
    for hh in range(RET_HEADS):
        qs = slice(hh * RET_QK_DIM, (hh + 1) * RET_QK_DIM)
        vs = slice(hh * RET_V_DIM, (hh + 1) * RET_V_DIM)
        hs = slice((hh % per_half) * RET_V_DIM, (hh % per_half + 1) * RET_V_DIM)
        v_ref, g_ref = (v0_ref, g0_ref) if hh < per_half else (v1_ref, g1_ref)
        q = q_ref[:, qs]
        k = k_ref[:, qs]
        v = v_ref[:, hs]
        state = state_ref[hh]
        s = lax.dot_general(q, k, nt, preferred_element_type=F32) * dmat_ref[hh]
        inner = jnp.dot(s.astype(BF16), v, preferred_element_type=F32)
        cross = jnp.dot(q, state.astype(BF16), preferred_element_type=F32) * xi_ref[hh]
        vz = (v.astype(F32) * zeta_ref[hh]).astype(BF16)
        upd = lax.dot_general(k, vz, tn, preferred_element_type=F32)
        state_ref[hh] = gch_ref[hh] * state + upd
        ret = inner + cross
        mu = jnp.mean(ret, axis=-1, keepdims=True)
        cen = ret - mu
        var = jnp.mean(cen * cen, axis=-1, keepdims=True)
        y = cen * lax.rsqrt(var + GN_EPS) * gn_ref[:, vs]
        o_ref[:, vs] = (y * g_ref[:, hs].astype(F32)).astype(o_ref.dtype)


def _retention_tables():
    c = RET_CHUNK
    hh = np.arange(RET_HEADS, dtype=np.float64)
    log_g = np.log1p(-np.exp2(-5.0 - hh))
    idx = np.arange(c, dtype=np.float64)
    diff = idx[:, None] - idx[None, :]
    dmat = np.where(diff >= 0, np.exp(log_g[:, None, None] * np.maximum(diff, 0.0)), 0.0)
    zeta = np.exp(log_g[:, None] * (c - 1 - idx))[:, :, None]
    xi = np.exp(log_g[:, None] * (idx + 1.0))[:, :, None]
    gch = np.exp(log_g * c)
    f = lambda v: jnp.asarray(v.astype(np.float32))
    return f(dmat), f(zeta), f(xi), f(gch)


def _retention(proj, gn_g, qcol, kcol, vcol, gcol):
    t = proj.shape[0]
    c = RET_CHUNK
    qw = RET_HEADS * RET_QK_DIM
    vw = RET_HEADS * RET_V_DIM
    dmat, zeta, xi, gch = _retention_tables()
    full3 = lambda shp: pl.BlockSpec(shp, lambda n: (0, 0, 0))
    return pl.pallas_call(
        _retention_kernel,
        grid=(t // c,),
        in_specs=[
            pl.BlockSpec((c, qw), lambda n: (n, qcol)),
            pl.BlockSpec((c, qw), lambda n: (n, kcol)),
            pl.BlockSpec((c, vw // 2), lambda n: (n, vcol)),
            pl.BlockSpec((c, vw // 2), lambda n: (n, vcol + 1)),
            pl.BlockSpec((c, vw // 2), lambda n: (n, gcol)),
            pl.BlockSpec((c, vw // 2), lambda n: (n, gcol + 1)),
            full3((RET_HEADS, c, c)),
            full3((RET_HEADS, c, 1)),
            full3((RET_HEADS, c, 1)),
            pl.BlockSpec(memory_space=pltpu.SMEM),
            pl.BlockSpec((1, vw), lambda n: (0, 0)),
        ],
        out_specs=pl.BlockSpec((c, vw), lambda n: (n, 0)),
        out_shape=jax.ShapeDtypeStruct((t, vw), BF16),
        scratch_shapes=[pltpu.VMEM((RET_HEADS, RET_QK_DIM, RET_V_DIM), F32)],
        compiler_params=_params(("arbitrary",)),
        name="retention",
    )(proj, proj, proj, proj, proj, proj, dmat, zeta, xi, gch, gn_g.reshape(1, vw))


def _merge_kernel(o1_ref, l1_ref, o2_ref, l2_ref, o3_ref, l3_ref, yb_ref, ga_ref, gb_ref,
                  pa_ref, pb_ref, out_ref, ya_ref):
    @pl.when(pl.program_id(1) == 0)
    def _():
        l1 = l1_ref[...]
        l2 = l2_ref[...]
        l3 = l3_ref[...]
        mx = jnp.maximum(jnp.maximum(l1, l2), l3)
        e1 = jnp.exp(l1 - mx)
        e2 = jnp.exp(l2 - mx)
        e3 = jnp.exp(l3 - mx)
        den = e1 + e2 + e3
        a1, a2, a3 = e1 / den, e2 / den, e3 / den
        for hh in range(HEADS_PER_GROUP):
            sl = slice(hh * HEAD_DIM, (hh + 1) * HEAD_DIM)
            ya = (a1[:, hh:hh + 1] * o1_ref[:, sl] + a2[:, hh:hh + 1] * o2_ref[:, sl]
                  + a3[:, hh:hh + 1] * o3_ref[:, sl])
            ya_ref[:, sl] = ya.astype(ya_ref.dtype)

    za = jnp.dot(ya_ref[...], pa_ref[...], preferred_element_type=F32)
    zb = jnp.dot(yb_ref[...], pb_ref[...], preferred_element_type=F32)
    out_ref[...] = (ga_ref[...].astype(F32) * za + gb_ref[...].astype(F32) * zb).astype(out_ref.dtype)


def _merge(o1, l1, o2, l2, o3, l3, yb, proj, ga_col, gb_col, pa, pb, tm=512, tn=1024):
    t = o1.shape[0]
    wa = o1.shape[1]
    wb = yb.shape[1]
    n = pa.shape[1]
    hg = HEADS_PER_GROUP
    ratio = tn // COLBLK
    o_spec = lambda: pl.BlockSpec((tm, wa), lambda i, j: (i, 0))
    l_spec = lambda: pl.BlockSpec((tm, hg), lambda i, j: (i, 0))
    return pl.pallas_call(
        _merge_kernel,
        grid=(t // tm, n // tn),
        in_specs=[
            o_spec(), l_spec(), o_spec(), l_spec(), o_spec(), l_spec(),
            pl.BlockSpec((tm, wb), lambda i, j: (i, 0)),
            pl.BlockSpec((tm, tn), lambda i, j: (i, ga_col // ratio + j)),
            pl.BlockSpec((tm, tn), lambda i, j: (i, gb_col // ratio + j)),
            pl.BlockSpec((wa, tn), lambda i, j: (0, j)),
            pl.BlockSpec((wb, tn), lambda i, j: (0, j)),
        ],
        out_specs=pl.BlockSpec((tm, tn), lambda i, j: (i, j)),
        out_shape=jax.ShapeDtypeStruct((t, n), BF16),
        scratch_shapes=[pltpu.VMEM((tm, wa), BF16)],
        compiler_params=_params(("parallel", "arbitrary")),
        name="merge",
    )(o1, l1, o2, l2, o3, l3, yb, proj, proj, pa, pb)


def _oproj_kernel(x_ref, m_ref, w_ref, g_ref, o_ref):
    z = jnp.dot(m_ref[...], w_ref[...], preferred_element_type=F32)
    o_ref[...] = x_ref[...] + g_ref[...] * z


def _oproj(x, merged, w_bf, mod, gate_blk, tm=512, tn=1024):
    t, d_model = x.shape
    k = merged.shape[1]
    per = d_model // tn
    return pl.pallas_call(
        _oproj_kernel,
        grid=(t // tm, d_model // tn),
        in_specs=[
            pl.BlockSpec((tm, tn), lambda i, j: (i, j)),
            pl.BlockSpec((tm, k), lambda i, j: (i, 0)),
            pl.BlockSpec((k, tn), lambda i, j: (0, j)),
            pl.BlockSpec((1, tn), lambda i, j: (0, gate_blk * per + j)),
        ],
        out_specs=pl.BlockSpec((tm, tn), lambda i, j: (i, j)),
        out_shape=jax.ShapeDtypeStruct((t, d_model), F32),
        compiler_params=_params(("parallel", "arbitrary")),
        name="oproj",
    )(x, merged, w_bf, mod)


def _pack_pair(lo, hi):
    lo_b = pltpu.bitcast(lo.astype(BF16).astype(F32), U32)
    hi_b = pltpu.bitcast(hi.astype(BF16).astype(F32), U32)
    return (lo_b >> 16) | (hi_b & jnp.uint32(0xFFFF0000))


def _unpack_pair(w):
    lo = pltpu.bitcast(w << 16, F32)
    hi = pltpu.bitcast(w & jnp.uint32(0xFFFF0000), F32)
    return lo, hi


def _route_kernel(x_ref, g_ref, sc_ref, sh_ref, wt_ref, rb_ref, h_ref, hp_ref, idx_ref, rank_ref,
                  wgt_ref, cnt_ref):
    @pl.when(pl.program_id(0) == 0)
    def _():
        cnt_ref[...] = jnp.zeros_like(cnt_ref)

    x = x_ref[...]
    tm, d_model = x.shape
    inv = lax.rsqrt(jnp.mean(x * x, axis=-1, keepdims=True) + RMS_EPS)
    h = (x * inv * g_ref[...]) * (1.0 + sc_ref[...]) + sh_ref[...]
    h_ref[...] = h.astype(h_ref.dtype)
    half = d_model // 2
    hp_ref[...] = _pack_pair(h[:, :half], h[:, half:])

    ne = N_EXPERTS
    per = ne // N_GROUPS
    logits = lax.dot_general(wt_ref[...], h, (((1,), (1,)), ((), ())),
                             precision=lax.Precision.HIGHEST,
                             preferred_element_type=F32)
    scores = jax.nn.sigmoid(logits)
    sel = scores + rb_ref[...]
    eidx = lax.broadcasted_iota(I32, (ne, tm), 0).astype(F32)
    minus_inf = -jnp.inf

    sel3 = sel.reshape(N_GROUPS, per, tm)
    sub = lax.broadcasted_iota(I32, (N_GROUPS, per, tm), 1).astype(F32)
    m1 = jnp.max(sel3, axis=1, keepdims=True)
    first = jnp.min(jnp.where(sel3 == m1, sub, float(per)), axis=1, keepdims=True)
    m2 = jnp.max(jnp.where(sub == first, minus_inf, sel3), axis=1, keepdims=True)
    grp = (m1 + m2).reshape(N_GROUPS, tm)

    gidx = lax.broadcasted_iota(I32, (N_GROUPS, tm), 0).astype(F32)
    gmask = jnp.zeros((N_GROUPS, tm), F32)
    work = grp
    for _ in range(TOPK_GROUPS):
        mx = jnp.max(work, axis=0, keepdims=True)
        pick = jnp.min(jnp.where(work == mx, gidx, float(N_GROUPS)), axis=0, keepdims=True)
        hit = gidx == pick
        gmask = jnp.where(hit, 1.0, gmask)
        work = jnp.where(hit, minus_inf, work)
    emask = jnp.broadcast_to(gmask.reshape(N_GROUPS, 1, tm), (N_GROUPS, per, tm)).reshape(ne, tm)

    work = jnp.where(emask > 0.0, sel, minus_inf)
    onehot = jnp.zeros((ne, tm), F32)
    idx_rows, w_rows = [], []
    for _ in range(TOP_K):
        mx = jnp.max(work, axis=0, keepdims=True)
        pick = jnp.min(jnp.where(work == mx, eidx, float(ne)), axis=0, keepdims=True)
        hit = eidx == pick
        onehot = jnp.where(hit, 1.0, onehot)
        work = jnp.where(hit, minus_inf, work)
        idx_rows.append(pick)
        w_rows.append(jnp.sum(jnp.where(hit, scores, 0.0), axis=0, keepdims=True))
    w_all = jnp.concatenate(w_rows, axis=0)
    wgt_ref[...] = w_all / jnp.sum(w_all, axis=0, keepdims=True) * ROUTED_SCALE
    idx_ref[...] = jnp.concatenate(idx_rows, axis=0).astype(I32)

    ra = lax.broadcasted_iota(I32, (tm, tm), 0)
    rb = lax.broadcasted_iota(I32, (tm, tm), 1)
    tri = jnp.where(ra <= rb, 1.0, 0.0).astype(BF16)
    incl = jnp.dot(onehot.astype(BF16), tri, preferred_element_type=F32)
    before = incl - onehot + cnt_ref[...]
    rank_rows = [jnp.sum(jnp.where(eidx == idx_rows[kk], before, 0.0), axis=0, keepdims=True)
                 for kk in range(TOP_K)]
    rank_ref[...] = jnp.concatenate(rank_rows, axis=0).astype(I32)
    cnt_ref[...] = cnt_ref[...] + jnp.sum(onehot, axis=1, keepdims=True)


def _route(x1, g, mod, sc_blk, sh_blk, router_w, router_bias, tm=256):
    t, d_model = x1.shape
    ne = N_EXPERTS
    vec = lambda k: pl.BlockSpec((1, d_model), lambda i, k=k: (0, k))
    tok = lambda: pl.BlockSpec((TOP_K, tm), lambda i: (0, i))
    return pl.pallas_call(
        _route_kernel,
        grid=(t // tm,),
        in_specs=[pl.BlockSpec((tm, d_model), lambda i: (i, 0)),
                  pl.BlockSpec((1, d_model), lambda i: (0, 0)),
                  vec(sc_blk), vec(sh_blk),
                  pl.BlockSpec((ne, d_model), lambda i: (0, 0)),
                  pl.BlockSpec((ne, 1), lambda i: (0, 0))],
        out_specs=[pl.BlockSpec((tm, d_model), lambda i: (i, 0)),
                   pl.BlockSpec((tm, d_model // 2), lambda i: (i, 0)),
                   tok(), tok(), tok(),
                   pl.BlockSpec((ne, 1), lambda i: (0, 0))],
        out_shape=[jax.ShapeDtypeStruct((t, d_model), BF16),
                   jax.ShapeDtypeStruct((t, d_model // 2), U32),
                   jax.ShapeDtypeStruct((TOP_K, t), I32),
                   jax.ShapeDtypeStruct((TOP_K, t), I32),
                   jax.ShapeDtypeStruct((TOP_K, t), F32),
                   jax.ShapeDtypeStruct((ne, 1), F32)],
        compiler_params=_params(("arbitrary",)),
        name="route",
    )(x1, g.reshape(1, d_model), mod, mod, router_w.T, router_bias.reshape(ne, 1))


SUBLANES = 8


def _pad_chunks(bm):
    sizes, s = [], bm // 2
    while s >= SUBLANES:
        sizes.append(s)
        s //= 2
    return sizes


def _dispatch_kernel(bm, pos_ref, fill_start_ref, fill_len_ref, nv_ref, hp_ref, xs_ref, sem, pad_sem):
    tm = hp_ref.shape[0]

    @pl.when(pl.program_id(0) == 0)
    def _():
        def pad_copies(action):
            def per_expert(e, carry):
                start = fill_start_ref[e]
                n = fill_len_ref[e]
                head = (-start) & (SUBLANES - 1)
                for r in range(SUBLANES - 1):
                    @pl.when(r < head)
                    def _(r=r):
                        action(pltpu.make_async_copy(hp_ref.at[pl.ds(0, 1)],
                                                     xs_ref.at[pl.ds(start + r, 1)], pad_sem))

                start = start + head
                n = n - head
                for size in _pad_chunks(bm):
                    take = (n & size) != 0

                    @pl.when(take)
                    def _(start=start, size=size):
                        dst = pl.multiple_of(start, SUBLANES)
                        action(pltpu.make_async_copy(hp_ref.at[pl.ds(0, size)],
                                                     xs_ref.at[pl.ds(dst, size)], pad_sem))

                    start = start + jnp.where(take, size, 0)
                return carry

            lax.fori_loop(0, N_EXPERTS, per_expert, 0)

            def unused_block(b, carry):
                dst = pl.multiple_of(b * bm, bm)
                action(pltpu.make_async_copy(hp_ref.at[pl.ds(0, bm)],
                                             xs_ref.at[pl.ds(dst, bm)], pad_sem))
                return carry

            lax.fori_loop(nv_ref[0], xs_ref.shape[0] // bm, unused_block, 0)

        pad_copies(lambda cp: cp.start())
        pad_copies(lambda cp: cp.wait())

    def body(tt, carry):
        for kk in range(TOP_K):
            dst = pos_ref[0, 0, kk * tm + tt]
            pltpu.make_async_copy(hp_ref.at[pl.ds(tt, 1)], xs_ref.at[pl.ds(dst, 1)], sem).start()
        return carry

    lax.fori_loop(0, tm, body, 0)
    pltpu.make_async_copy(xs_ref.at[pl.ds(0, tm * TOP_K)], xs_ref.at[pl.ds(0, tm * TOP_K)], sem).wait()


def _tile_major(a_t, tm):
    k, t = a_t.shape
    return a_t.reshape(k, t // tm, tm).transpose(1, 0, 2).reshape(t // tm, 1, k * tm)


def _dispatch(hp, pos_t, fill_start, fill_len, n_valid, rows, bm, tm=256):
    t, width = hp.shape
    assert tm >= bm
    smem = lambda: pl.BlockSpec(memory_space=pltpu.SMEM)
    return pl.pallas_call(
        functools.partial(_dispatch_kernel, bm),
        grid=(t // tm,),
        in_specs=[pl.BlockSpec((1, 1, tm * TOP_K), lambda i: (i, 0, 0), memory_space=pltpu.SMEM),
                  smem(), smem(), smem(),
                  pl.BlockSpec((tm, width), lambda i: (i, 0))],
        out_specs=pl.BlockSpec(memory_space=pl.ANY),
        out_shape=jax.ShapeDtypeStruct((rows, width), U32),
        scratch_shapes=[pltpu.SemaphoreType.DMA(()), pltpu.SemaphoreType.DMA(())],
        compiler_params=_params(("arbitrary",)),
        name="dispatch",
    )(_tile_major(pos_t, tm), fill_start, fill_len, n_valid, hp)


def _experts_kernel(be_ref, nv_ref, x_ref, wg_ref, wu_ref, wd_ref, y_ref, wg_bf, wu_bf, wd_bf):
    b = pl.program_id(0)
    prev = be_ref[jnp.maximum(b - 1, 0)]
    changed = (b == 0) | (be_ref[b] != prev)

    @pl.when(changed)
    def _():
        wg_bf[...] = wg_ref[...].astype(BF16)
        wu_bf[...] = wu_ref[...].astype(BF16)
        wd_bf[...] = wd_ref[...].astype(BF16)

    @pl.when(b < nv_ref[0])
    def _():
        lo, hi = _unpack_pair(x_ref[...])
        half = lo.shape[1]
        lo = lo.astype(BF16)
        hi = hi.astype(BF16)
        gate = (jnp.dot(lo, wg_bf[:half, :], preferred_element_type=F32)
                + jnp.dot(hi, wg_bf[half:, :], preferred_element_type=F32))
        up = (jnp.dot(lo, wu_bf[:half, :], preferred_element_type=F32)
              + jnp.dot(hi, wu_bf[half:, :], preferred_element_type=F32))
        act = (_silu(gate) * up).astype(BF16)
        y = jnp.dot(act, wd_bf[...], preferred_element_type=F32)
        y_ref[...] = _pack_pair(y[:, :half], y[:, half:])

    @pl.when(b >= nv_ref[0])
    def _():
        y_ref[...] = jnp.zeros_like(y_ref)


def _experts(xs, block_e, n_valid, wg, wu, wd, bm=EXPERT_ROWS):
    rows, width = xs.shape
    ne, d_model, de = wg.shape
    nb = rows // bm
    row_map = lambda b, be, nv: (jnp.minimum(b, nv[0] - 1), 0)
    grid_spec = pltpu.PrefetchScalarGridSpec(
        num_scalar_prefetch=2,
        grid=(nb,),
        in_specs=[
            pl.BlockSpec((bm, width), row_map),
            pl.BlockSpec((None, d_model, de), lambda b, be, nv: (be[b], 0, 0)),
            pl.BlockSpec((None, d_model, de), lambda b, be, nv: (be[b], 0, 0)),
            pl.BlockSpec((None, de, d_model), lambda b, be, nv: (be[b], 0, 0)),
        ],
        out_specs=pl.BlockSpec((bm, width), lambda b, be, nv: (b, 0)),
        scratch_shapes=[pltpu.VMEM((d_model, de), BF16),
                        pltpu.VMEM((d_model, de), BF16),
                        pltpu.VMEM((de, d_model), BF16)],
    )
    return pl.pallas_call(
        _experts_kernel,
        grid_spec=grid_spec,
        out_shape=jax.ShapeDtypeStruct((rows, width), U32),
        compiler_params=_params(("arbitrary",)),
        name="experts",
    )(block_e, n_valid, xs, wg, wu, wd)


def _combine_kernel(pos_ref, x_ref, h_ref, wt_ref, g_ref, sg_ref, su_ref, sd_ref, ys_ref, o_ref,
                    buf_ref, sem):
    tm = x_ref.shape[0]

    def body(tt, carry):
        for kk in range(TOP_K):
            src = pos_ref[0, 0, kk * tm + tt]
            pltpu.make_async_copy(ys_ref.at[pl.ds(src, 1)], buf_ref.at[kk, pl.ds(tt, 1)], sem).start()
        return carry

    lax.fori_loop(0, tm, body, 0)
    h = h_ref[...]
    act = (_silu(jnp.dot(h, sg_ref[...], preferred_element_type=F32))
           * jnp.dot(h, su_ref[...], preferred_element_type=F32)).astype(BF16)
    shared = jnp.dot(act, sd_ref[...], preferred_element_type=F32)
    for kk in range(TOP_K):
        pltpu.make_async_copy(ys_ref.at[pl.ds(0, tm)], buf_ref.at[kk], sem).wait()
    half = buf_ref.shape[2]
    wt = wt_ref[...]
    lo_acc = jnp.zeros((tm, half), F32)
    hi_acc = jnp.zeros((tm, half), F32)
    for kk in range(TOP_K):
        lo, hi = _unpack_pair(buf_ref[kk])
        wk = wt[:, kk:kk + 1]
        lo_acc = lo_acc + wk * lo
        hi_acc = hi_acc + wk * hi
    g = g_ref[...]
    o_ref[:, :half] = x_ref[:, :half] + g[:, :half] * (lo_acc + shared[:, :half])
    o_ref[:, half:] = x_ref[:, half:] + g[:, half:] * (hi_acc + shared[:, half:])


def _combine(x1, h2, pos_t, wts, mod, gate_blk, sg, su, sd, ys, tm=256):
    t, d_model = x1.shape
    ds_ = sg.shape[1]
    width = ys.shape[1]
    pos3 = _tile_major(pos_t, tm)
    return pl.pallas_call(
        _combine_kernel,
        grid=(t // tm,),
        in_specs=[pl.BlockSpec((1, 1, tm * TOP_K), lambda i: (i, 0, 0), memory_space=pltpu.SMEM),
                  pl.BlockSpec((tm, d_model), lambda i: (i, 0)),
                  pl.BlockSpec((tm, d_model), lambda i: (i, 0)),
                  pl.BlockSpec((tm, TOP_K), lambda i: (i, 0)),
                  pl.BlockSpec((1, d_model), lambda i: (0, gate_blk)),
                  pl.BlockSpec((d_model, ds_), lambda i: (0, 0)),
                  pl.BlockSpec((d_model, ds_), lambda i: (0, 0)),
                  pl.BlockSpec((ds_, d_model), lambda i: (0, 0)),
                  pl.BlockSpec(memory_space=pl.ANY)],
        out_specs=pl.BlockSpec((tm, d_model), lambda i: (i, 0)),
        out_shape=jax.ShapeDtypeStruct((t, d_model), F32),
        scratch_shapes=[pltpu.VMEM((TOP_K, tm, width), U32),
                        pltpu.SemaphoreType.DMA(())],
        compiler_params=_params(("arbitrary",)),
        name="combine",
    )(pos3, x1, h2, wts, mod, sg, su, sd, ys)


def _layout_kernel(bm, cnt_ref, idx_ref, rank_ref, pos_ref, be_ref, nv_ref, fs_ref, fl_ref):
    shift = bm.bit_length() - 1
    pos_ref[...] = rank_ref[...]

    def per_expert(e, carry):
        start, blk = carry
        cnt = cnt_ref[e]
        nblk = (cnt + (bm - 1)) >> shift
        pos_ref[...] = pos_ref[...] + jnp.where(idx_ref[...] == e, start, 0)

        def mark(b, c):
            be_ref[blk + b] = e
            return c

        lax.fori_loop(0, nblk, mark, 0)
        fs_ref[e] = start + cnt
        fl_ref[e] = (nblk << shift) - cnt
        return start + (nblk << shift), blk + nblk

    _, n_valid = lax.fori_loop(0, N_EXPERTS, per_expert, (jnp.int32(0), jnp.int32(0)))
    nv_ref[0] = n_valid

    def tail(b, c):
        be_ref[b] = N_EXPERTS - 1
        return c

    lax.fori_loop(n_valid, be_ref.shape[0], tail, 0)


def _layout(counts, idx_t, rank_t, bm, n_blocks):
    assert bm & (bm - 1) == 0
    k, t = idx_t.shape
    smem = lambda: pl.BlockSpec(memory_space=pltpu.SMEM)
    full = lambda: pl.BlockSpec((k, t), lambda: (0, 0))
    return pl.pallas_call(
        functools.partial(_layout_kernel, bm),
        in_specs=[smem(), full(), full()],
        out_specs=[full(), smem(), smem(), smem(), smem()],
        out_shape=[jax.ShapeDtypeStruct((k, t), I32),
                   jax.ShapeDtypeStruct((n_blocks,), I32),
                   jax.ShapeDtypeStruct((1,), I32),
                   jax.ShapeDtypeStruct((N_EXPERTS,), I32),
                   jax.ShapeDtypeStruct((N_EXPERTS,), I32)],
        name="layout",
    )(counts.reshape(-1).astype(I32), idx_t, rank_t)


def _layer(x, c, rel_bias, w_ada, b_ada, ln1_g, w_in, q_norm_g, k_norm_g, ret_gn_g, p_a, p_b, w_o,
           ln2_g, router_w, router_bias, w_gate_e, w_up_e, w_down_e, w_gate_s, w_up_s, w_down_s):
    t, d_model = x.shape
    dils = tuple(d for _, d in DILATED_GROUPS)

    mod = _ada(c.reshape(d_model), w_ada, b_ada)
    h = _norm1(x, ln1_g, mod)
    h_orders = (h, _to_residue_major(h, dils[1]), _to_residue_major(h, dils[2]))
    cos_tab, sin_tab = _rotary_tables(t)
    proj = _inproj(h_orders, w_in.astype(BF16), q_norm_g, k_norm_g, cos_tab, sin_tab)

    a_blocks = N_HEADS_A * HEAD_DIM // COLBLK
    attn = []
    for gi, (win, dil) in enumerate(DILATED_GROUPS):
        attn.append(_attn_group(proj, rel_bias, gi, win, dil,
                                gi, a_blocks + gi, 2 * a_blocks + gi))
    base = 3 * a_blocks
    rq = RET_HEADS * RET_QK_DIM // COLBLK
    vw_blk = RET_HEADS * RET_V_DIM // COLBLK
    qcol = base
    kcol = base + rq
    vcol_blk = base + 2 * rq
    gcol_blk = vcol_blk + vw_blk
    ga_blk = gcol_blk + vw_blk
    gb_blk = ga_blk + d_model // COLBLK
    y_b = _retention(proj, ret_gn_g, qcol, kcol, vcol_blk, gcol_blk)
    (o1, l1), (o2, l2), (o3, l3) = attn
    o2, l2 = _from_residue_major(o2, dils[1]), _from_residue_major(l2, dils[1])
    o3, l3 = _from_residue_major(o3, dils[2]), _from_residue_major(l3, dils[2])
    merged = _merge(o1, l1, o2, l2, o3, l3, y_b, proj, ga_blk, gb_blk,
                    p_a.astype(BF16), p_b.astype(BF16))
    x1 = _oproj(x, merged, w_o.astype(BF16), mod, 2)

    h2, h2p, idx_t, rank_t, wgt_t, counts = _route(x1, ln2_g, mod, 4, 3, router_w, router_bias)
    bm = EXPERT_ROWS
    n_blocks = (t * TOP_K + N_EXPERTS * (bm - 1) + bm - 1) // bm
    pos_t, block_e, n_valid, fill_start, fill_len = _layout(counts, idx_t, rank_t, bm, n_blocks)
    xs = _dispatch(h2p, pos_t, fill_start, fill_len, n_valid, n_blocks * bm, bm)
    ys = _experts(xs, block_e, n_valid, w_gate_e, w_up_e, w_down_e)
    return _combine(x1, h2, pos_t, wgt_t.T, mod, 5, w_gate_s.astype(BF16), w_up_s.astype(BF16),
                    w_down_s.astype(BF16), ys)


def kernel(x, c, rel_bias, w_ada, b_ada, ln1_g, w_in, q_norm_g, k_norm_g, ret_gn_g, p_a, p_b, w_o,
           ln2_g, router_w, router_bias, w_gate_e, w_up_e, w_down_e, w_gate_s, w_up_s, w_down_s):
    b, s, d_model = x.shape
    depth = w_ada.shape[0]
    outs = []
    for bi in range(b):
        xb = x[bi]
        for l in range(depth):
            xb = _layer(xb, c[bi], rel_bias, w_ada[l], b_ada[l], ln1_g[l], w_in[l], q_norm_g[l],
                        k_norm_g[l], ret_gn_g[l], p_a[l], p_b[l], w_o[l], ln2_g[l], router_w[l],
                        router_bias[l], w_gate_e[l], w_up_e[l], w_down_e[l], w_gate_s[l],
                        w_up_s[l], w_down_s[l])
        outs.append(xb)
    return jnp.stack(outs, axis=0)
```

```python
import functools

import numpy as np
import jax
import jax.numpy as jnp
from jax import lax
from jax.experimental import pallas as pl
from jax.experimental.pallas import tpu as pltpu

F32 = jnp.float32
BF16 = jnp.bfloat16
U32 = jnp.uint32
I32 = jnp.int32

HEAD_DIM = 128
DILATED_GROUPS = ((128, 1), (512, 4), (2048, 16))
HEADS_PER_GROUP = 8
N_HEADS_A = HEADS_PER_GROUP * len(DILATED_GROUPS)
A_GROUP_WIDTH = HEADS_PER_GROUP * HEAD_DIM
ATTN_BLOCK = 128
NUM_BUCKETS = 32
MAX_DISTANCE = 2048
NEG_INF = -1e30
RET_HEADS = 8
RET_QK_DIM = 128
RET_V_DIM = 256
RET_CHUNK = 128
ROPE_BASE = 10000.0
GN_EPS = 1e-5
N_EXPERTS = 64
N_GROUPS = 8
TOPK_GROUPS = 4
TOP_K = 8
ROUTED_SCALE = 2.5
RMS_EPS = 1e-6

LANE = 128
COLBLK = 1024
VMEM_LIMIT = 56 * 1024 * 1024
EXPERT_ROWS = 256


def _params(sem, vmem=VMEM_LIMIT):
    return pltpu.CompilerParams(dimension_semantics=sem, vmem_limit_bytes=vmem)


def _sigmoid(v):
    return 0.5 * jnp.tanh(0.5 * v) + 0.5


def _silu(v):
    return v * _sigmoid(v)


def _ada_kernel(c_ref, w_ref, b_ref, o_ref):
    sc = _silu(c_ref[...])
    o_ref[...] = jnp.sum(w_ref[...] * sc, axis=0, keepdims=True) + b_ref[...]


def _ada(c, w, b, tn=512):
    d, n = w.shape
    return pl.pallas_call(
        _ada_kernel,
        grid=(n // tn,),
        in_specs=[pl.BlockSpec((d, 1), lambda j: (0, 0)),
                  pl.BlockSpec((d, tn), lambda j: (0, j)),
                  pl.BlockSpec((1, tn), lambda j: (0, j))],
        out_specs=pl.BlockSpec((1, tn), lambda j: (0, j)),
        out_shape=jax.ShapeDtypeStruct((1, n), F32),
        compiler_params=_params(("parallel",)),
        name="ada",
    )(c.reshape(d, 1), w, b.reshape(1, n))


def _norm1_kernel(x_ref, g_ref, sc_ref, sh_ref, o_ref):
    x = x_ref[...]
    inv = lax.rsqrt(jnp.mean(x * x, axis=-1, keepdims=True) + RMS_EPS)
    o_ref[...] = ((x * inv * g_ref[...]) * (1.0 + sc_ref[...]) + sh_ref[...]).astype(o_ref.dtype)


def _norm1(x, g, mod, tm=512):
    t, d_model = x.shape
    vec = lambda k: pl.BlockSpec((1, d_model), lambda i, k=k: (0, k))
    return pl.pallas_call(
        _norm1_kernel,
        grid=(t // tm,),
        in_specs=[pl.BlockSpec((tm, d_model), lambda i: (i, 0)),
                  pl.BlockSpec((1, d_model), lambda i: (0, 0)),
                  vec(1), vec(0)],
        out_specs=pl.BlockSpec((tm, d_model), lambda i: (i, 0)),
        out_shape=jax.ShapeDtypeStruct((t, d_model), BF16),
        compiler_params=_params(("parallel",)),
        name="norm1",
    )(x, g.reshape(1, d_model), mod, mod)


def _to_residue_major(a, d):
    t, w = a.shape
    return a.reshape(t // d, d, w).transpose(1, 0, 2).reshape(t, w)


def _from_residue_major(a, d):
    t, w = a.shape
    return a.reshape(d, t // d, w).transpose(1, 0, 2).reshape(t, w)


EPI_QNORM, EPI_KNORM, EPI_PLAIN, EPI_ROT_Q, EPI_ROT_K, EPI_SILU, EPI_SIGMOID = range(7)
INPROJ_ROW_CHUNK = 256


def _inproj_kernel(epis_present, colblk_ref, epi_ref, h_ref, w_ref, qg_ref, kg_ref, cos_ref, sin_ref,
                   o_ref):
    del colblk_ref
    epi = epi_ref[pl.program_id(1)]
    tm = h_ref.shape[0]
    nh = o_ref.shape[1] // HEAD_DIM

    def head_norm(gain, scale):
        def fn(acc, rows):
            for hh in range(nh):
                sl = slice(hh * HEAD_DIM, (hh + 1) * HEAD_DIM)
                a = acc[:, sl]
                inv = lax.rsqrt(jnp.mean(a * a, axis=-1, keepdims=True) + RMS_EPS)
                o_ref[rows, sl] = ((a * inv * gain) * scale).astype(o_ref.dtype)
        return fn

    def rotary(scale):
        def fn(acc, rows):
            cos = cos_ref[rows, :]
            sin = sin_ref[rows, :]
            for hh in range(nh):
                sl = slice(hh * HEAD_DIM, (hh + 1) * HEAD_DIM)
                a = acc[:, sl]
                rot = pltpu.roll(a, HEAD_DIM // 2, 1)
                o_ref[rows, sl] = ((a * cos + rot * sin) * scale).astype(o_ref.dtype)
        return fn

    def elementwise(f):
        def fn(acc, rows):
            o_ref[rows, :] = f(acc).astype(o_ref.dtype)
        return fn

    epilogues = {
        EPI_QNORM: lambda: head_norm(qg_ref[...], HEAD_DIM ** -0.5),
        EPI_KNORM: lambda: head_norm(kg_ref[...], 1.0),
        EPI_PLAIN: lambda: elementwise(lambda a: a),
        EPI_ROT_Q: lambda: rotary(1.0),
        EPI_ROT_K: lambda: rotary(RET_QK_DIM ** -0.5),
        EPI_SILU: lambda: elementwise(_silu),
        EPI_SIGMOID: lambda: elementwise(_sigmoid),
    }
    for code in epis_present:
        @pl.when(epi == code)
        def _(code=code):
            fn = epilogues[code]()
            for c in range(tm // INPROJ_ROW_CHUNK):
                rows = slice(c * INPROJ_ROW_CHUNK, (c + 1) * INPROJ_ROW_CHUNK)
                acc = jnp.dot(h_ref[rows, :], w_ref[...], preferred_element_type=F32)
                fn(acc, rows)


def _inproj_plan(d_model):
    a_blocks = N_HEADS_A * HEAD_DIM // COLBLK
    groups = len(DILATED_GROUPS)
    per_group = a_blocks // groups
    rq = RET_HEADS * RET_QK_DIM // COLBLK
    rv = RET_HEADS * RET_V_DIM // COLBLK
    gd = d_model // COLBLK
    seg_epi = ([EPI_QNORM] * a_blocks + [EPI_KNORM] * a_blocks + [EPI_PLAIN] * a_blocks
               + [EPI_ROT_Q] * rq + [EPI_ROT_K] * rq + [EPI_PLAIN] * rv + [EPI_SILU] * rv
               + [EPI_SIGMOID] * (2 * gd))
    order_of = [0] * len(seg_epi)
    for seg in range(3):
        for blk in range(a_blocks):
            order_of[seg * a_blocks + blk] = blk // per_group
    plans = []
    for order in range(groups):
        cols = [cb for cb in range(len(seg_epi)) if order_of[cb] == order]
        plans.append((cols, [seg_epi[cb] for cb in cols]))
    return plans


def _inproj(h, w_bf, cols, epis, qg, kg, cos_tab, sin_tab, name, tm=1024):
    t, d_model = h.shape
    row = lambda width: pl.BlockSpec((tm, width), lambda i, j, cb, ep: (i, 0))
    one = lambda width: pl.BlockSpec((1, width), lambda i, j, cb, ep: (0, 0))
    grid_spec = pltpu.PrefetchScalarGridSpec(
        num_scalar_prefetch=2,
        grid=(t // tm, len(cols)),
        in_specs=[
            row(d_model),
            pl.BlockSpec((d_model, COLBLK), lambda i, j, cb, ep: (0, cb[j])),
            one(HEAD_DIM), one(HEAD_DIM), row(HEAD_DIM), row(HEAD_DIM),
        ],
        out_specs=pl.BlockSpec((tm, COLBLK), lambda i, j, cb, ep: (i, j)),
    )
    return pl.pallas_call(
        functools.partial(_inproj_kernel, tuple(sorted(set(epis)))),
        grid_spec=grid_spec,
        out_shape=jax.ShapeDtypeStruct((t, len(cols) * COLBLK), BF16),
        compiler_params=_params(("parallel", "arbitrary")),
        name=name,
    )(jnp.asarray(np.array(cols, np.int32)), jnp.asarray(np.array(epis, np.int32)),
      h, w_bf, qg.reshape(1, HEAD_DIM), kg.reshape(1, HEAD_DIM), cos_tab, sin_tab)


def _rotary_tables(t):
    half = RET_QK_DIM // 2
    inv = ROPE_BASE ** (-np.arange(0, RET_QK_DIM, 2, dtype=np.float64) / RET_QK_DIM)
    ang = np.arange(t, dtype=np.float64)[:, None] * inv[None, :]
    cos, sin = np.cos(ang), np.sin(ang)
    del half
    cos_tab = np.concatenate([cos, cos], axis=1).astype(np.float32)
    sin_tab = np.concatenate([-sin, sin], axis=1).astype(np.float32)
    return jnp.asarray(cos_tab), jnp.asarray(sin_tab)


def _t5_bucket(dist):
    max_exact = NUM_BUCKETS // 2
    safe = np.maximum(dist, 1).astype(np.float32)
    large = max_exact + (np.log(safe / max_exact) / np.log(MAX_DISTANCE / max_exact)
                         * (NUM_BUCKETS - max_exact)).astype(np.int32)
    return np.where(dist < max_exact, dist, np.minimum(large, NUM_BUCKETS - 1)).astype(np.int32)


def _attn_kernel(head0, w_steps, blocks_per_res, tab_ref, bucket_ref, q_ref, kp_ref, kc_ref,
                 vp_ref, vc_ref, o_ref, lse_ref, bias_ref):
    m_idx = pl.program_id(0)
    blk = ATTN_BLOCK

    @pl.when(m_idx == 0)
    def _():
        bucket = bucket_ref[...]
        for hh in range(HEADS_PER_GROUP):
            bias = jnp.zeros(bucket.shape, F32)
            for b in range(NUM_BUCKETS):
                bias = jnp.where(bucket == b, tab_ref[b, head0 + hh], bias)
            bias_ref[hh] = bias

    a = lax.broadcasted_iota(I32, (blk, blk), 0)
    cc = lax.broadcasted_iota(I32, (blk, blk), 1)
    has_prev = (m_idx % blocks_per_res) > 0
    ok_prev = ((blk + a - cc) <= w_steps) & has_prev
    ok_cur = ((a - cc) >= 0) & ((a - cc) <= w_steps)
    nt = (((1,), (1,)), ((), ()))
    lses = []
    for hh in range(HEADS_PER_GROUP):
        sl = slice(hh * HEAD_DIM, (hh + 1) * HEAD_DIM)
        q = q_ref[:, sl]
        s_p = lax.dot_general(q, kp_ref[:, sl], nt, preferred_element_type=F32)
        s_c = lax.dot_general(q, kc_ref[:, sl], nt, preferred_element_type=F32)
        s_p = jnp.where(ok_prev, s_p + bias_ref[hh, :, :blk], NEG_INF)
        s_c = jnp.where(ok_cur, s_c + bias_ref[hh, :, blk:], NEG_INF)
        mx = jnp.maximum(jnp.max(s_p, axis=-1, keepdims=True),
                         jnp.max(s_c, axis=-1, keepdims=True))
        p_p = jnp.exp(s_p - mx)
        p_c = jnp.exp(s_c - mx)
        den = jnp.sum(p_p, axis=-1, keepdims=True) + jnp.sum(p_c, axis=-1, keepdims=True)
        acc = (jnp.dot(p_p.astype(BF16), vp_ref[:, sl], preferred_element_type=F32)
               + jnp.dot(p_c.astype(BF16), vc_ref[:, sl], preferred_element_type=F32))
        o_ref[:, sl] = acc / den
        lses.append(mx + jnp.log(den))
    lse_ref[...] = jnp.concatenate(lses, axis=-1)


def _attn_group(proj, rel_bias, gi, window, dilation, qcol, kcol, vcol):
    t = proj.shape[0]
    blk = ATTN_BLOCK
    w_steps = window // dilation
    blocks_per_res = t // dilation // blk
    nblk = t // blk
    a = np.arange(blk)[:, None]
    cc = np.arange(2 * blk)[None, :]
    bucket = _t5_bucket(np.maximum(blk + a - cc, 0) * dilation)

    def prev_map(m):
        return jnp.where(m % blocks_per_res > 0, m - 1, m)

    kern = functools.partial(_attn_kernel, gi * HEADS_PER_GROUP, w_steps, blocks_per_res)
    width = A_GROUP_WIDTH
    return pl.pallas_call(
        kern,
        grid=(nblk,),
        in_specs=[
            pl.BlockSpec(memory_space=pltpu.SMEM),
            pl.BlockSpec((blk, 2 * blk), lambda m: (0, 0)),
            pl.BlockSpec((blk, width), lambda m: (m, qcol)),
            pl.BlockSpec((blk, width), lambda m: (prev_map(m), kcol)),
            pl.BlockSpec((blk, width), lambda m: (m, kcol)),
            pl.BlockSpec((blk, width), lambda m: (prev_map(m), vcol)),
            pl.BlockSpec((blk, width), lambda m: (m, vcol)),
        ],
        out_specs=[pl.BlockSpec((blk, width), lambda m: (m, 0)),
                   pl.BlockSpec((blk, HEADS_PER_GROUP), lambda m: (m, 0))],
        out_shape=[jax.ShapeDtypeStruct((t, width), F32),
                   jax.ShapeDtypeStruct((t, HEADS_PER_GROUP), F32)],
        scratch_shapes=[pltpu.VMEM((HEADS_PER_GROUP, blk, 2 * blk), F32)],
        compiler_params=_params(("arbitrary",)),
        name=f"attn_d{dilation}",
    )(rel_bias, jnp.asarray(bucket), proj, proj, proj, proj, proj)


def _retention_kernel(q_ref, k_ref, v0_ref, v1_ref, g0_ref, g1_ref, dmat_ref, zeta_ref, xi_ref,
                      gch_ref, gn_ref, o_ref, state_ref):
    @pl.when(pl.program_id(0) == 0)
    def _():
        state_ref[...] = jnp.zeros_like(state_ref)

    nt = (((1,), (1,)), ((), ()))
    tn = (((0,), (0,)), ((), ()))
    per_half = RET_HEADS // 2
    for hh in range(RET_HEADS):
        qs = slice(hh * RET_QK_DIM, (hh + 1) * RET_QK_DIM)
        vs = slice(hh * RET_V_DIM, (hh + 1) * RET_V_DIM)
        hs = slice((hh % per_half) * RET_V_DIM, (hh % per_half + 1) * RET_V_DIM)
        v_ref, g_ref = (v0_ref, g0_ref) if hh < per_half else (v1_ref, g1_ref)
        q = q_ref[:, qs]
        k = k_ref[:, qs]
        v = v_ref[:, hs]
        state = state_ref[hh]
        s = lax.dot_general(q, k, nt, preferred_element_type=F32) * dmat_ref[hh]
        inner = jnp.dot(s.astype(BF16), v, preferred_element_type=F32)
        cross = jnp.dot(q, state.astype(BF16), preferred_element_type=F32) * xi_ref[hh]
        vz = (v.astype(F32) * zeta_ref[hh]).astype(BF16)
        upd = lax.dot_general(k, vz, tn, preferred_element_type=F32)
        state_ref[hh] = gch_ref[hh] * state + upd
        ret = inner + cross
        mu = jnp.mean(ret, axis=-1, keepdims=True)
        cen = ret - mu
        var = jnp.mean(cen * cen, axis=-1, keepdims=True)
        y = cen * lax.rsqrt(var + GN_EPS) * gn_ref[:, vs]
        o_ref[:, vs] = (y * g_ref[:, hs].astype(F32)).astype(o_ref.dtype)


def _retention_tables():
    c = RET_CHUNK
    hh = np.arange(RET_HEADS, dtype=np.float64)
    log_g = np.log1p(-np.exp2(-5.0 - hh))
    idx = np.arange(c, dtype=np.float64)
    diff = idx[:, None] - idx[None, :]
    dmat = np.where(diff >= 0, np.exp(log_g[:, None, None] * np.maximum(diff, 0.0)), 0.0)
    zeta = np.exp(log_g[:, None] * (c - 1 - idx))[:, :, None]
    xi = np.exp(log_g[:, None] * (idx + 1.0))[:, :, None]
    gch = np.exp(log_g * c)
    f = lambda v: jnp.asarray(v.astype(np.float32))
    return f(dmat), f(zeta), f(xi), f(gch)


def _retention(proj, gn_g, qcol, kcol, vcol, gcol):
    t = proj.shape[0]
    c = RET_CHUNK
    qw = RET_HEADS * RET_QK_DIM
    vw = RET_HEADS * RET_V_DIM
    dmat, zeta, xi, gch = _retention_tables()
    full3 = lambda shp: pl.BlockSpec(shp, lambda n: (0, 0, 0))
    return pl.pallas_call(
        _retention_kernel,
        grid=(t // c,),
        in_specs=[
            pl.BlockSpec((c, qw), lambda n: (n, qcol)),
            pl.BlockSpec((c, qw), lambda n: (n, kcol)),
            pl.BlockSpec((c, vw // 2), lambda n: (n, vcol)),
            pl.BlockSpec((c, vw // 2), lambda n: (n, vcol + 1)),
            pl.BlockSpec((c, vw // 2), lambda n: (n, gcol)),
            pl.BlockSpec((c, vw // 2), lambda n: (n, gcol + 1)),
            full3((RET_HEADS, c, c)),
            full3((RET_HEADS, c, 1)),
            full3((RET_HEADS, c, 1)),
            pl.BlockSpec(memory_space=pltpu.SMEM),
            pl.BlockSpec((1, vw), lambda n: (0, 0)),
        ],
        out_specs=pl.BlockSpec((c, vw), lambda n: (n, 0)),
        out_shape=jax.ShapeDtypeStruct((t, vw), BF16),
        scratch_shapes=[pltpu.VMEM((RET_HEADS, RET_QK_DIM, RET_V_DIM), F32)],
        compiler_params=_params(("arbitrary",)),
        name="retention",
    )(proj, proj, proj, proj, proj, proj, dmat, zeta, xi, gch, gn_g.reshape(1, vw))


def _merge_kernel(o1_ref, l1_ref, o2_ref, l2_ref, o3_ref, l3_ref, yb_ref, ga_ref, gb_ref,
                  pa_ref, pb_ref, out_ref, ya_ref):
    @pl.when(pl.program_id(1) == 0)
    def _():
        l1 = l1_ref[...]
        l2 = l2_ref[...]
        l3 = l3_ref[...]
        mx = jnp.maximum(jnp.maximum(l1, l2), l3)
        e1 = jnp.exp(l1 - mx)
        e2 = jnp.exp(l2 - mx)
        e3 = jnp.exp(l3 - mx)
        den = e1 + e2 + e3
        a1, a2, a3 = e1 / den, e2 / den, e3 / den
        for hh in range(HEADS_PER_GROUP):
            sl = slice(hh * HEAD_DIM, (hh + 1) * HEAD_DIM)
            ya = (a1[:, hh:hh + 1] * o1_ref[:, sl] + a2[:, hh:hh + 1] * o2_ref[:, sl]
                  + a3[:, hh:hh + 1] * o3_ref[:, sl])
            ya_ref[:, sl] = ya.astype(ya_ref.dtype)

    za = jnp.dot(ya_ref[...], pa_ref[...], preferred_element_type=F32)
    zb = jnp.dot(yb_ref[...], pb_ref[...], preferred_element_type=F32)
    out_ref[...] = (ga_ref[...].astype(F32) * za + gb_ref[...].astype(F32) * zb).astype(out_ref.dtype)


def _merge(o1, l1, o2, l2, o3, l3, yb, proj, ga_col, gb_col, pa, pb, tm=512, tn=1024):
    t = o1.shape[0]
    wa = o1.shape[1]
    wb = yb.shape[1]
    n = pa.shape[1]
    hg = HEADS_PER_GROUP
    ratio = tn // COLBLK
    o_spec = lambda: pl.BlockSpec((tm, wa), lambda i, j: (i, 0))
    l_spec = lambda: pl.BlockSpec((tm, hg), lambda i, j: (i, 0))
    return pl.pallas_call(
        _merge_kernel,
        grid=(t // tm, n // tn),
        in_specs=[
            o_spec(), l_spec(), o_spec(), l_spec(), o_spec(), l_spec(),
            pl.BlockSpec((tm, wb), lambda i, j: (i, 0)),
            pl.BlockSpec((tm, tn), lambda i, j: (i, ga_col // ratio + j)),
            pl.BlockSpec((tm, tn), lambda i, j: (i, gb_col // ratio + j)),
            pl.BlockSpec((wa, tn), lambda i, j: (0, j)),
            pl.BlockSpec((wb, tn), lambda i, j: (0, j)),
        ],
        out_specs=pl.BlockSpec((tm, tn), lambda i, j: (i, j)),
        out_shape=jax.ShapeDtypeStruct((t, n), BF16),
        scratch_shapes=[pltpu.VMEM((tm, wa), BF16)],
        compiler_params=_params(("parallel", "arbitrary")),
        name="merge",
    )(o1, l1, o2, l2, o3, l3, yb, proj, proj, pa, pb)


def _oproj_kernel(x_ref, m_ref, w_ref, g_ref, o_ref):
    z = jnp.dot(m_ref[...], w_ref[...], preferred_element_type=F32)
    o_ref[...] = x_ref[...] + g_ref[...] * z


def _oproj(x, merged, w_bf, mod, gate_blk, tm=512, tn=1024):
    t, d_model = x.shape
    k = merged.shape[1]
    per = d_model // tn
    return pl.pallas_call(
        _oproj_kernel,
        grid=(t // tm, d_model // tn),
        in_specs=[
            pl.BlockSpec((tm, tn), lambda i, j: (i, j)),
            pl.BlockSpec((tm, k), lambda i, j: (i, 0)),
            pl.BlockSpec((k, tn), lambda i, j: (0, j)),
            pl.BlockSpec((1, tn), lambda i, j: (0, gate_blk * per + j)),
        ],
        out_specs=pl.BlockSpec((tm, tn), lambda i, j: (i, j)),
        out_shape=jax.ShapeDtypeStruct((t, d_model), F32),
        compiler_params=_params(("parallel", "arbitrary")),
        name="oproj",
    )(x, merged, w_bf, mod)


def _pack_pair(lo, hi):
    lo_b = pltpu.bitcast(lo.astype(BF16).astype(F32), U32)
    hi_b = pltpu.bitcast(hi.astype(BF16).astype(F32), U32)
    return (lo_b >> 16) | (hi_b & jnp.uint32(0xFFFF0000))


def _unpack_pair(w):
    lo = pltpu.bitcast(w << 16, F32)
    hi = pltpu.bitcast(w & jnp.uint32(0xFFFF0000), F32)
    return lo, hi


def _route_kernel(x_ref, g_ref, sc_ref, sh_ref, wt_ref, rb_ref, h_ref, hp_ref, idx_ref, rank_ref,
                  wgt_ref, cnt_ref):
    @pl.when(pl.program_id(0) == 0)
    def _():
        cnt_ref[...] = jnp.zeros_like(cnt_ref)

    x = x_ref[...]
    tm, d_model = x.shape
    inv = lax.rsqrt(jnp.mean(x * x, axis=-1, keepdims=True) + RMS_EPS)
    h = (x * inv * g_ref[...]) * (1.0 + sc_ref[...]) + sh_ref[...]
    h_ref[...] = h.astype(h_ref.dtype)
    half = d_model // 2
    hp_ref[...] = _pack_pair(h[:, :half], h[:, half:])

    ne = N_EXPERTS
    per = ne // N_GROUPS
    logits = lax.dot_general(wt_ref[...], h, (((1,), (1,)), ((), ())),
                             precision=lax.Precision.HIGHEST,
                             preferred_element_type=F32)
    scores = jax.nn.sigmoid(logits)
    sel = scores + rb_ref[...]
    eidx = lax.broadcasted_iota(I32, (ne, tm), 0).astype(F32)
    minus_inf = -jnp.inf

    sel3 = sel.reshape(N_GROUPS, per, tm)
    sub = lax.broadcasted_iota(I32, (N_GROUPS, per, tm), 1).astype(F32)
    m1 = jnp.max(sel3, axis=1, keepdims=True)
    first = jnp.min(jnp.where(sel3 == m1, sub, float(per)), axis=1, keepdims=True)
    m2 = jnp.max(jnp.where(sub == first, minus_inf, sel3), axis=1, keepdims=True)
    grp = (m1 + m2).reshape(N_GROUPS, tm)

    gidx = lax.broadcasted_iota(I32, (N_GROUPS, tm), 0).astype(F32)
    gmask = jnp.zeros((N_GROUPS, tm), F32)
    work = grp
    for _ in range(TOPK_GROUPS):
        mx = jnp.max(work, axis=0, keepdims=True)
        pick = jnp.min(jnp.where(work == mx, gidx, float(N_GROUPS)), axis=0, keepdims=True)
        hit = gidx == pick
        gmask = jnp.where(hit, 1.0, gmask)
        work = jnp.where(hit, minus_inf, work)
    emask = jnp.broadcast_to(gmask.reshape(N_GROUPS, 1, tm), (N_GROUPS, per, tm)).reshape(ne, tm)

    work = jnp.where(emask > 0.0, sel, minus_inf)
    onehot = jnp.zeros((ne, tm), F32)
    idx_rows, w_rows = [], []
    for _ in range(TOP_K):
        mx = jnp.max(work, axis=0, keepdims=True)
        pick = jnp.min(jnp.where(work == mx, eidx, float(ne)), axis=0, keepdims=True)
        hit = eidx == pick
        onehot = jnp.where(hit, 1.0, onehot)
        work = jnp.where(hit, minus_inf, work)
        idx_rows.append(pick)
        w_rows.append(jnp.sum(jnp.where(hit, scores, 0.0), axis=0, keepdims=True))
    w_all = jnp.concatenate(w_rows, axis=0)
    wgt_ref[...] = w_all / jnp.sum(w_all, axis=0, keepdims=True) * ROUTED_SCALE
    idx_ref[...] = jnp.concatenate(idx_rows, axis=0).astype(I32)

    ra = lax.broadcasted_iota(I32, (tm, tm), 0)
    rb = lax.broadcasted_iota(I32, (tm, tm), 1)
    tri = jnp.where(ra <= rb, 1.0, 0.0).astype(BF16)
    incl = jnp.dot(onehot.astype(BF16), tri, preferred_element_type=F32)
    before = incl - onehot + cnt_ref[...]
    rank_rows = [jnp.sum(jnp.where(eidx == idx_rows[kk], before, 0.0), axis=0, keepdims=True)
                 for kk in range(TOP_K)]
    rank_ref[...] = jnp.concatenate(rank_rows, axis=0).astype(I32)
    cnt_ref[...] = cnt_ref[...] + jnp.sum(onehot, axis=1, keepdims=True)


def _route(x1, g, mod, sc_blk, sh_blk, router_w, router_bias, tm=256):
    t, d_model = x1.shape
    ne = N_EXPERTS
    vec = lambda k: pl.BlockSpec((1, d_model), lambda i, k=k: (0, k))
    tok = lambda: pl.BlockSpec((TOP_K, tm), lambda i: (0, i))
    return pl.pallas_call(
        _route_kernel,
        grid=(t // tm,),
        in_specs=[pl.BlockSpec((tm, d_model), lambda i: (i, 0)),
                  pl.BlockSpec((1, d_model), lambda i: (0, 0)),
                  vec(sc_blk), vec(sh_blk),
                  pl.BlockSpec((ne, d_model), lambda i: (0, 0)),
                  pl.BlockSpec((ne, 1), lambda i: (0, 0))],
        out_specs=[pl.BlockSpec((tm, d_model), lambda i: (i, 0)),
                   pl.BlockSpec((tm, d_model // 2), lambda i: (i, 0)),
                   tok(), tok(), tok(),
                   pl.BlockSpec((ne, 1), lambda i: (0, 0))],
        out_shape=[jax.ShapeDtypeStruct((t, d_model), BF16),
                   jax.ShapeDtypeStruct((t, d_model // 2), U32),
                   jax.ShapeDtypeStruct((TOP_K, t), I32),
                   jax.ShapeDtypeStruct((TOP_K, t), I32),
                   jax.ShapeDtypeStruct((TOP_K, t), F32),
                   jax.ShapeDtypeStruct((ne, 1), F32)],
        compiler_params=_params(("arbitrary",)),
        name="route",
    )(x1, g.reshape(1, d_model), mod, mod, router_w.T, router_bias.reshape(ne, 1))


SUBLANES = 8


def _pad_chunks(bm):
    sizes, s = [], bm // 2
    while s >= SUBLANES:
        sizes.append(s)
        s //= 2
    return sizes


def _dispatch_kernel(bm, pos_ref, fill_start_ref, fill_len_ref, nv_ref, hp_ref, xs_ref, sem, pad_sem):
    tm = hp_ref.shape[0]

    @pl.when(pl.program_id(0) == 0)
    def _():
        def pad_copies(action):
            def per_expert(e, carry):
                start = fill_start_ref[e]
                n = fill_len_ref[e]
                head = (-start) & (SUBLANES - 1)
                for r in range(SUBLANES - 1):
                    @pl.when(r < head)
                    def _(r=r):
                        action(pltpu.make_async_copy(hp_ref.at[pl.ds(0, 1)],
                                                     xs_ref.at[pl.ds(start + r, 1)], pad_sem))

                start = start + head
                n = n - head
                for size in _pad_chunks(bm):
                    take = (n & size) != 0

                    @pl.when(take)
                    def _(start=start, size=size):
                        dst = pl.multiple_of(start, SUBLANES)
                        action(pltpu.make_async_copy(hp_ref.at[pl.ds(0, size)],
                                                     xs_ref.at[pl.ds(dst, size)], pad_sem))

                    start = start + jnp.where(take, size, 0)
                return carry

            lax.fori_loop(0, N_EXPERTS, per_expert, 0)

            def unused_block(b, carry):
                dst = pl.multiple_of(b * bm, bm)
                action(pltpu.make_async_copy(hp_ref.at[pl.ds(0, bm)],
                                             xs_ref.at[pl.ds(dst, bm)], pad_sem))
                return carry

            lax.fori_loop(nv_ref[0], xs_ref.shape[0] // bm, unused_block, 0)

        pad_copies(lambda cp: cp.start())
        pad_copies(lambda cp: cp.wait())

    def body(tt, carry):
        for kk in range(TOP_K):
            dst = pos_ref[0, 0, kk * tm + tt]
            pltpu.make_async_copy(hp_ref.at[pl.ds(tt, 1)], xs_ref.at[pl.ds(dst, 1)], sem).start()
        return carry

    lax.fori_loop(0, tm, body, 0)
    pltpu.make_async_copy(xs_ref.at[pl.ds(0, tm * TOP_K)], xs_ref.at[pl.ds(0, tm * TOP_K)], sem).wait()


def _tile_major(a_t, tm):
    k, t = a_t.shape
    return a_t.reshape(k, t // tm, tm).transpose(1, 0, 2).reshape(t // tm, 1, k * tm)


def _dispatch(hp, pos_t, fill_start, fill_len, n_valid, rows, bm, tm=256):
    t, width = hp.shape
    assert tm >= bm
    smem = lambda: pl.BlockSpec(memory_space=pltpu.SMEM)
    return pl.pallas_call(
        functools.partial(_dispatch_kernel, bm),
        grid=(t // tm,),
        in_specs=[pl.BlockSpec((1, 1, tm * TOP_K), lambda i: (i, 0, 0), memory_space=pltpu.SMEM),
                  smem(), smem(), smem(),
                  pl.BlockSpec((tm, width), lambda i: (i, 0))],
        out_specs=pl.BlockSpec(memory_space=pl.ANY),
        out_shape=jax.ShapeDtypeStruct((rows, width), U32),
        scratch_shapes=[pltpu.SemaphoreType.DMA(()), pltpu.SemaphoreType.DMA(())],
        compiler_params=_params(("arbitrary",)),
        name="dispatch",
    )(_tile_major(pos_t, tm), fill_start, fill_len, n_valid, hp)


def _experts_kernel(seg_ref, sege_ref, nv_ref, x_ref, wg_hbm, wu_hbm, wd_hbm, y_ref,
                    wg_f32, wu_f32, wd_f32, wg_bf, wu_bf, wd_bf, sems):
    b = pl.program_id(0)
    seg = seg_ref[b]
    slot = seg % 2
    first = (b == 0) | (seg_ref[jnp.maximum(b - 1, 0)] != seg)

    def weight_copies(which_seg, which_slot):
        e = sege_ref[which_seg]
        return [pltpu.make_async_copy(src.at[e], dst.at[which_slot], sems.at[which_slot])
                for src, dst in ((wg_hbm, wg_f32), (wu_hbm, wu_f32), (wd_hbm, wd_f32))]

    @pl.when(b == 0)
    def _():
        for cp in weight_copies(0, 0):
            cp.start()

    @pl.when(first)
    def _():
        for cp in weight_copies(seg, slot):
            cp.wait()
        wg_bf[...] = wg_f32[slot].astype(BF16)
        wu_bf[...] = wu_f32[slot].astype(BF16)
        wd_bf[...] = wd_f32[slot].astype(BF16)

        @pl.when(seg + 1 < nv_ref[1])
        def _():
            for cp in weight_copies(seg + 1, 1 - slot):
                cp.start()

    @pl.when(b < nv_ref[0])
    def _():
        lo, hi = _unpack_pair(x_ref[...])
        half = lo.shape[1]
        lo = lo.astype(BF16)
        hi = hi.astype(BF16)
        gate = (jnp.dot(lo, wg_bf[:half, :], preferred_element_type=F32)
                + jnp.dot(hi, wg_bf[half:, :], preferred_element_type=F32))
        up = (jnp.dot(lo, wu_bf[:half, :], preferred_element_type=F32)
              + jnp.dot(hi, wu_bf[half:, :], preferred_element_type=F32))
        act = (_silu(gate) * up).astype(BF16)
        y = jnp.dot(act, wd_bf[...], preferred_element_type=F32)
        y_ref[...] = _pack_pair(y[:, :half], y[:, half:])

    @pl.when(b >= nv_ref[0])
    def _():
        y_ref[...] = jnp.zeros_like(y_ref)


def _experts(xs, seg_of, seg_e, n_valid, wg, wu, wd, bm=EXPERT_ROWS):
    rows, width = xs.shape
    _, d_model, de = wg.shape
    nb = rows // bm
    row_map = lambda b, sg, se, nv: (jnp.minimum(b, nv[0] - 1), 0)
    hbm = lambda: pl.BlockSpec(memory_space=pl.ANY)
    grid_spec = pltpu.PrefetchScalarGridSpec(
        num_scalar_prefetch=3,
        grid=(nb,),
        in_specs=[pl.BlockSpec((bm, width), row_map), hbm(), hbm(), hbm()],
        out_specs=pl.BlockSpec((bm, width), lambda b, sg, se, nv: (b, 0)),
        scratch_shapes=[pltpu.VMEM((2, d_model, de), F32),
                        pltpu.VMEM((2, d_model, de), F32),
                        pltpu.VMEM((2, de, d_model), F32),
                        pltpu.VMEM((d_model, de), BF16),
                        pltpu.VMEM((d_model, de), BF16),
                        pltpu.VMEM((de, d_model), BF16),
                        pltpu.SemaphoreType.DMA((2,))],
    )
    return pl.pallas_call(
        _experts_kernel,
        grid_spec=grid_spec,
        out_shape=jax.ShapeDtypeStruct((rows, width), U32),
        compiler_params=_params(("arbitrary",)),
        name="experts",
    )(seg_of, seg_e, n_valid, xs, wg, wu, wd)


def _combine_kernel(pos_ref, x_ref, h_ref, wt_ref, g_ref, sg_ref, su_ref, sd_ref, ys_ref, o_ref,
                    buf_ref, sem):
    tm = x_ref.shape[0]

    def body(tt, carry):
        for kk in range(TOP_K):
            src = pos_ref[0, 0, kk * tm + tt]
            pltpu.make_async_copy(ys_ref.at[pl.ds(src, 1)], buf_ref.at[kk, pl.ds(tt, 1)], sem).start()
        return carry

    lax.fori_loop(0, tm, body, 0)
    h = h_ref[...]
    act = (_silu(jnp.dot(h, sg_ref[...], preferred_element_type=F32))
           * jnp.dot(h, su_ref[...], preferred_element_type=F32)).astype(BF16)
    shared = jnp.dot(act, sd_ref[...], preferred_element_type=F32)
    for kk in range(TOP_K):
        pltpu.make_async_copy(ys_ref.at[pl.ds(0, tm)], buf_ref.at[kk], sem).wait()
    half = buf_ref.shape[2]
    wt = wt_ref[...]
    lo_acc = jnp.zeros((tm, half), F32)
    hi_acc = jnp.zeros((tm, half), F32)
    for kk in range(TOP_K):
        lo, hi = _unpack_pair(buf_ref[kk])
        wk = wt[:, kk:kk + 1]
        lo_acc = lo_acc + wk * lo
        hi_acc = hi_acc + wk * hi
    g = g_ref[...]
    o_ref[:, :half] = x_ref[:, :half] + g[:, :half] * (lo_acc + shared[:, :half])
    o_ref[:, half:] = x_ref[:, half:] + g[:, half:] * (hi_acc + shared[:, half:])


def _combine(x1, h2, pos_t, wts, mod, gate_blk, sg, su, sd, ys, tm=256):
    t, d_model = x1.shape
    ds_ = sg.shape[1]
    width = ys.shape[1]
    pos3 = _tile_major(pos_t, tm)
    return pl.pallas_call(
        _combine_kernel,
        grid=(t // tm,),
        in_specs=[pl.BlockSpec((1, 1, tm * TOP_K), lambda i: (i, 0, 0), memory_space=pltpu.SMEM),
                  pl.BlockSpec((tm, d_model), lambda i: (i, 0)),
                  pl.BlockSpec((tm, d_model), lambda i: (i, 0)),
                  pl.BlockSpec((tm, TOP_K), lambda i: (i, 0)),
                  pl.BlockSpec((1, d_model), lambda i: (0, gate_blk)),
                  pl.BlockSpec((d_model, ds_), lambda i: (0, 0)),
                  pl.BlockSpec((d_model, ds_), lambda i: (0, 0)),
                  pl.BlockSpec((ds_, d_model), lambda i: (0, 0)),
                  pl.BlockSpec(memory_space=pl.ANY)],
        out_specs=pl.BlockSpec((tm, d_model), lambda i: (i, 0)),
        out_shape=jax.ShapeDtypeStruct((t, d_model), F32),
        scratch_shapes=[pltpu.VMEM((TOP_K, tm, width), U32),
                        pltpu.SemaphoreType.DMA(())],
        compiler_params=_params(("arbitrary",)),
        name="combine",
    )(pos3, x1, h2, wts, mod, sg, su, sd, ys)


def _layout_kernel(bm, cnt_ref, idx_ref, rank_ref, pos_ref, seg_ref, sege_ref, nv_ref, fs_ref, fl_ref):
    shift = bm.bit_length() - 1
    pos_ref[...] = rank_ref[...]

    def per_expert(e, carry):
        start, blk, seg = carry
        cnt = cnt_ref[e]
        nblk = (cnt + (bm - 1)) >> shift
        pos_ref[...] = pos_ref[...] + jnp.where(idx_ref[...] == e, start, 0)

        def mark(b, c):
            seg_ref[blk + b] = seg
            return c

        lax.fori_loop(0, nblk, mark, 0)

        @pl.when(nblk > 0)
        def _():
            sege_ref[seg] = e

        fs_ref[e] = start + cnt
        fl_ref[e] = (nblk << shift) - cnt
        return start + (nblk << shift), blk + nblk, seg + jnp.where(nblk > 0, 1, 0)

    zero = jnp.int32(0)
    _, n_valid, n_seg = lax.fori_loop(0, N_EXPERTS, per_expert, (zero, zero, zero))
    nv_ref[0] = n_valid
    nv_ref[1] = n_seg

    def tail_blocks(b, c):
        seg_ref[b] = n_seg - 1
        return c

    lax.fori_loop(n_valid, seg_ref.shape[0], tail_blocks, 0)

    def tail_segs(s, c):
        sege_ref[s] = N_EXPERTS - 1
        return c

    lax.fori_loop(n_seg, N_EXPERTS, tail_segs, 0)


def _layout(counts, idx_t, rank_t, bm, n_blocks):
    assert bm & (bm - 1) == 0
    k, t = idx_t.shape
    smem = lambda: pl.BlockSpec(memory_space=pltpu.SMEM)
    full = lambda: pl.BlockSpec((k, t), lambda: (0, 0))
    return pl.pallas_call(
        functools.partial(_layout_kernel, bm),
        in_specs=[smem(), full(), full()],
        out_specs=[full(), smem(), smem(), smem(), smem(), smem()],
        out_shape=[jax.ShapeDtypeStruct((k, t), I32),
                   jax.ShapeDtypeStruct((n_blocks,), I32),
                   jax.ShapeDtypeStruct((N_EXPERTS,), I32),
                   jax.ShapeDtypeStruct((2,), I32),
                   jax.ShapeDtypeStruct((N_EXPERTS,), I32),
                   jax.ShapeDtypeStruct((N_EXPERTS,), I32)],
        name="layout",
    )(counts.reshape(-1).astype(I32), idx_t, rank_t)


def _layer(x, c, rel_bias, w_ada, b_ada, ln1_g, w_in, q_norm_g, k_norm_g, ret_gn_g, p_a, p_b, w_o,
           ln2_g, router_w, router_bias, w_gate_e, w_up_e, w_down_e, w_gate_s, w_up_s, w_down_s):
    t, d_model = x.shape
    dils = tuple(d for _, d in DILATED_GROUPS)

    mod = _ada(c.reshape(d_model), w_ada, b_ada)
    h = _norm1(x, ln1_g, mod)
    cos_tab, sin_tab = _rotary_tables(t)
    w_bf = w_in.astype(BF16)
    projs = []
    for order, (cols, epis) in enumerate(_inproj_plan(d_model)):
        h_in = h if order == 0 else _to_residue_major(h, dils[order])
        projs.append(_inproj(h_in, w_bf, cols, epis, q_norm_g, k_norm_g, cos_tab, sin_tab,
                             f"inproj_d{dils[order]}"))
    proj = projs[0]

    attn = [_attn_group(projs[gi], rel_bias, gi, win, dil, 0, 1, 2)
            for gi, (win, dil) in enumerate(DILATED_GROUPS)]
    base = 3
    rq = RET_HEADS * RET_QK_DIM // COLBLK
    vw_blk = RET_HEADS * RET_V_DIM // COLBLK
    qcol = base
    kcol = base + rq
    vcol_blk = base + 2 * rq
    gcol_blk = vcol_blk + vw_blk
    ga_blk = gcol_blk + vw_blk
    gb_blk = ga_blk + d_model // COLBLK
    y_b = _retention(proj, ret_gn_g, qcol, kcol, vcol_blk, gcol_blk)
    (o1, l1), (o2, l2), (o3, l3) = attn
    o2, l2 = _from_residue_major(o2, dils[1]), _from_residue_major(l2, dils[1])
    o3, l3 = _from_residue_major(o3, dils[2]), _from_residue_major(l3, dils[2])
    merged = _merge(o1, l1, o2, l2, o3, l3, y_b, proj, ga_blk, gb_blk,
                    p_a.astype(BF16), p_b.astype(BF16))
    x1 = _oproj(x, merged, w_o.astype(BF16), mod, 2)

    h2, h2p, idx_t, rank_t, wgt_t, counts = _route(x1, ln2_g, mod, 4, 3, router_w, router_bias)
    bm = EXPERT_ROWS
    n_blocks = (t * TOP_K + N_EXPERTS * (bm - 1) + bm - 1) // bm
    pos_t, seg_of, seg_e, n_valid, fill_start, fill_len = _layout(counts, idx_t, rank_t, bm, n_blocks)
    xs = _dispatch(h2p, pos_t, fill_start, fill_len, n_valid, n_blocks * bm, bm)
    ys = _experts(xs, seg_of, seg_e, n_valid, w_gate_e, w_up_e, w_down_e)
    return _combine(x1, h2, pos_t, wgt_t.T, mod, 5, w_gate_s.astype(BF16), w_up_s.astype(BF16),
                    w_down_s.astype(BF16), ys)


def kernel(x, c, rel_bias, w_ada, b_ada, ln1_g, w_in, q_norm_g, k_norm_g, ret_gn_g, p_a, p_b, w_o,
           ln2_g, router_w, router_bias, w_gate_e, w_up_e, w_down_e, w_gate_s, w_up_s, w_down_s):
    b, s, d_model = x.shape
    depth = w_ada.shape[0]
    outs = []
    for bi in range(b):
        xb = x[bi]
        for l in range(depth):
            xb = _layer(xb, c[bi], rel_bias, w_ada[l], b_ada[l], ln1_g[l], w_in[l], q_norm_g[l],
                        k_norm_g[l], ret_gn_g[l], p_a[l], p_b[l], w_o[l], ln2_g[l], router_w[l],
                        router_bias[l], w_gate_e[l], w_up_e[l], w_down_e[l], w_gate_s[l],
                        w_up_s[l], w_down_s[l])
        outs.append(xb)
    return jnp.stack(outs, axis=0)
```

```python
import functools

import numpy as np
import jax
import jax.numpy as jnp
from jax import lax
from jax.experimental import pallas as pl
from jax.experimental.pallas import tpu as pltpu

F32 = jnp.float32
BF16 = jnp.bfloat16
U32 = jnp.uint32
I32 = jnp.int32

HEAD_DIM = 128
DILATED_GROUPS = ((128, 1), (512, 4), (2048, 16))
HEADS_PER_GROUP = 8
N_HEADS_A = HEADS_PER_GROUP * len(DILATED_GROUPS)
A_GROUP_WIDTH = HEADS_PER_GROUP * HEAD_DIM
ATTN_BLOCK = 128
NUM_BUCKETS = 32
MAX_DISTANCE = 2048
NEG_INF = -1e30
RET_HEADS = 8
RET_QK_DIM = 128
RET_V_DIM = 256
RET_CHUNK = 128
ROPE_BASE = 10000.0
GN_EPS = 1e-5
N_EXPERTS = 64
N_GROUPS = 8
TOPK_GROUPS = 4
TOP_K = 8
ROUTED_SCALE = 2.5
RMS_EPS = 1e-6

LANE = 128
COLBLK = 1024
VMEM_LIMIT = 56 * 1024 * 1024
EXPERT_ROWS = 256


def _params(sem, vmem=VMEM_LIMIT):
    return pltpu.CompilerParams(dimension_semantics=sem, vmem_limit_bytes=vmem)


def _sigmoid(v):
    return 0.5 * jnp.tanh(0.5 * v) + 0.5


def _silu(v):
    return v * _sigmoid(v)


def _ada_kernel(c_ref, w_ref, b_ref, o_ref):
    sc = _silu(c_ref[...])
    o_ref[...] = jnp.sum(w_ref[...] * sc, axis=0, keepdims=True) + b_ref[...]


def _ada(c, w, b, tn=512):
    d, n = w.shape
    return pl.pallas_call(
        _ada_kernel,
        grid=(n // tn,),
        in_specs=[pl.BlockSpec((d, 1), lambda j: (0, 0)),
                  pl.BlockSpec((d, tn), lambda j: (0, j)),
                  pl.BlockSpec((1, tn), lambda j: (0, j))],
        out_specs=pl.BlockSpec((1, tn), lambda j: (0, j)),
        out_shape=jax.ShapeDtypeStruct((1, n), F32),
        compiler_params=_params(("parallel",)),
        name="ada",
    )(c.reshape(d, 1), w, b.reshape(1, n))


def _norm1_kernel(x_ref, g_ref, sc_ref, sh_ref, o_ref):
    x = x_ref[...]
    inv = lax.rsqrt(jnp.mean(x * x, axis=-1, keepdims=True) + RMS_EPS)
    o_ref[...] = ((x * inv * g_ref[...]) * (1.0 + sc_ref[...]) + sh_ref[...]).astype(o_ref.dtype)


def _norm1(x, g, mod, tm=512):
    t, d_model = x.shape
    vec = lambda k: pl.BlockSpec((1, d_model), lambda i, k=k: (0, k))
    return pl.pallas_call(
        _norm1_kernel,
        grid=(t // tm,),
        in_specs=[pl.BlockSpec((tm, d_model), lambda i: (i, 0)),
                  pl.BlockSpec((1, d_model), lambda i: (0, 0)),
                  vec(1), vec(0)],
        out_specs=pl.BlockSpec((tm, d_model), lambda i: (i, 0)),
        out_shape=jax.ShapeDtypeStruct((t, d_model), BF16),
        compiler_params=_params(("parallel",)),
        name="norm1",
    )(x, g.reshape(1, d_model), mod, mod)


def _to_residue_major(a, d):
    t, w = a.shape
    return a.reshape(t // d, d, w).transpose(1, 0, 2).reshape(t, w)


def _from_residue_major(a, d):
    t, w = a.shape
    return a.reshape(d, t // d, w).transpose(1, 0, 2).reshape(t, w)


EPI_QNORM, EPI_KNORM, EPI_PLAIN, EPI_ROT_Q, EPI_ROT_K, EPI_SILU, EPI_SIGMOID = range(7)
INPROJ_ROW_CHUNK = 256


def _inproj_kernel(epis_present, colblk_ref, epi_ref, h_ref, w_ref, qg_ref, kg_ref, cos_ref, sin_ref,
                   o_ref):
    del colblk_ref
    epi = epi_ref[pl.program_id(1)]
    tm = h_ref.shape[0]
    nh = o_ref.shape[1] // HEAD_DIM

    def head_norm(gain, scale):
        def fn(acc, rows):
            for hh in range(nh):
                sl = slice(hh * HEAD_DIM, (hh + 1) * HEAD_DIM)
                a = acc[:, sl]
                inv = lax.rsqrt(jnp.mean(a * a, axis=-1, keepdims=True) + RMS_EPS)
                o_ref[rows, sl] = ((a * inv * gain) * scale).astype(o_ref.dtype)
        return fn

    def rotary(scale):
        def fn(acc, rows):
            cos = cos_ref[rows, :]
            sin = sin_ref[rows, :]
            for hh in range(nh):
                sl = slice(hh * HEAD_DIM, (hh + 1) * HEAD_DIM)
                a = acc[:, sl]
                rot = pltpu.roll(a, HEAD_DIM // 2, 1)
                o_ref[rows, sl] = ((a * cos + rot * sin) * scale).astype(o_ref.dtype)
        return fn

    def elementwise(f):
        def fn(acc, rows):
            o_ref[rows, :] = f(acc).astype(o_ref.dtype)
        return fn

    epilogues = {
        EPI_QNORM: lambda: head_norm(qg_ref[...], HEAD_DIM ** -0.5),
        EPI_KNORM: lambda: head_norm(kg_ref[...], 1.0),
        EPI_PLAIN: lambda: elementwise(lambda a: a),
        EPI_ROT_Q: lambda: rotary(1.0),
        EPI_ROT_K: lambda: rotary(RET_QK_DIM ** -0.5),
        EPI_SILU: lambda: elementwise(_silu),
        EPI_SIGMOID: lambda: elementwise(_sigmoid),
    }
    for code in epis_present:
        @pl.when(epi == code)
        def _(code=code):
            fn = epilogues[code]()
            for c in range(tm // INPROJ_ROW_CHUNK):
                rows = slice(c * INPROJ_ROW_CHUNK, (c + 1) * INPROJ_ROW_CHUNK)
                acc = jnp.dot(h_ref[rows, :], w_ref[...], preferred_element_type=F32)
                fn(acc, rows)


def _inproj_plan(d_model):
    a_blocks = N_HEADS_A * HEAD_DIM // COLBLK
    groups = len(DILATED_GROUPS)
    per_group = a_blocks // groups
    rq = RET_HEADS * RET_QK_DIM // COLBLK
    rv = RET_HEADS * RET_V_DIM // COLBLK
    gd = d_model // COLBLK
    seg_epi = ([EPI_QNORM] * a_blocks + [EPI_KNORM] * a_blocks + [EPI_PLAIN] * a_blocks
               + [EPI_ROT_Q] * rq + [EPI_ROT_K] * rq + [EPI_PLAIN] * rv + [EPI_SILU] * rv
               + [EPI_SIGMOID] * (2 * gd))
    order_of = [0] * len(seg_epi)
    for seg in range(3):
        for blk in range(a_blocks):
            order_of[seg * a_blocks + blk] = blk // per_group
    plans = []
    for order in range(groups):
        cols = [cb for cb in range(len(seg_epi)) if order_of[cb] == order]
        plans.append((cols, [seg_epi[cb] for cb in cols]))
    return plans


def _inproj(h, w_bf, cols, epis, qg, kg, cos_tab, sin_tab, name, tm=1024):
    t, d_model = h.shape
    row = lambda width: pl.BlockSpec((tm, width), lambda i, j, cb, ep: (i, 0))
    one = lambda width: pl.BlockSpec((1, width), lambda i, j, cb, ep: (0, 0))
    grid_spec = pltpu.PrefetchScalarGridSpec(
        num_scalar_prefetch=2,
        grid=(t // tm, len(cols)),
        in_specs=[
            row(d_model),
            pl.BlockSpec((d_model, COLBLK), lambda i, j, cb, ep: (0, cb[j])),
            one(HEAD_DIM), one(HEAD_DIM), row(HEAD_DIM), row(HEAD_DIM),
        ],
        out_specs=pl.BlockSpec((tm, COLBLK), lambda i, j, cb, ep: (i, j)),
    )
    return pl.pallas_call(
        functools.partial(_inproj_kernel, tuple(sorted(set(epis)))),
        grid_spec=grid_spec,
        out_shape=jax.ShapeDtypeStruct((t, len(cols) * COLBLK), BF16),
        compiler_params=_params(("parallel", "arbitrary")),
        name=name,
    )(jnp.asarray(np.array(cols, np.int32)), jnp.asarray(np.array(epis, np.int32)),
      h, w_bf, qg.reshape(1, HEAD_DIM), kg.reshape(1, HEAD_DIM), cos_tab, sin_tab)


def _rotary_tables(t):
    half = RET_QK_DIM // 2
    inv = ROPE_BASE ** (-np.arange(0, RET_QK_DIM, 2, dtype=np.float64) / RET_QK_DIM)
    ang = np.arange(t, dtype=np.float64)[:, None] * inv[None, :]
    cos, sin = np.cos(ang), np.sin(ang)
    del half
    cos_tab = np.concatenate([cos, cos], axis=1).astype(np.float32)
    sin_tab = np.concatenate([-sin, sin], axis=1).astype(np.float32)
    return jnp.asarray(cos_tab), jnp.asarray(sin_tab)


def _t5_bucket(dist):
    max_exact = NUM_BUCKETS // 2
    safe = np.maximum(dist, 1).astype(np.float32)
    large = max_exact + (np.log(safe / max_exact) / np.log(MAX_DISTANCE / max_exact)
                         * (NUM_BUCKETS - max_exact)).astype(np.int32)
    return np.where(dist < max_exact, dist, np.minimum(large, NUM_BUCKETS - 1)).astype(np.int32)


def _attn_kernel(head0, w_steps, blocks_per_res, tab_ref, bucket_ref, q_ref, kp_ref, kc_ref,
                 vp_ref, vc_ref, o_ref, lse_ref, bias_ref):
    m_idx = pl.program_id(0)
    blk = ATTN_BLOCK

    @pl.when(m_idx == 0)
    def _():
        bucket = bucket_ref[...]
        for hh in range(HEADS_PER_GROUP):
            bias = jnp.zeros(bucket.shape, F32)
            for b in range(NUM_BUCKETS):
                bias = jnp.where(bucket == b, tab_ref[b, head0 + hh], bias)
            bias_ref[hh] = bias

    a = lax.broadcasted_iota(I32, (blk, blk), 0)
    cc = lax.broadcasted_iota(I32, (blk, blk), 1)
    has_prev = (m_idx % blocks_per_res) > 0
    ok_prev = ((blk + a - cc) <= w_steps) & has_prev
    ok_cur = ((a - cc) >= 0) & ((a - cc) <= w_steps)
    nt = (((1,), (1,)), ((), ()))
    lses = []
    for hh in range(HEADS_PER_GROUP):
        sl = slice(hh * HEAD_DIM, (hh + 1) * HEAD_DIM)
        q = q_ref[:, sl]
        s_p = lax.dot_general(q, kp_ref[:, sl], nt, preferred_element_type=F32)
        s_c = lax.dot_general(q, kc_ref[:, sl], nt, preferred_element_type=F32)
        s_p = jnp.where(ok_prev, s_p + bias_ref[hh, :, :blk], NEG_INF)
        s_c = jnp.where(ok_cur, s_c + bias_ref[hh, :, blk:], NEG_INF)
        mx = jnp.maximum(jnp.max(s_p, axis=-1, keepdims=True),
                         jnp.max(s_c, axis=-1, keepdims=True))
        p_p = jnp.exp(s_p - mx)
        p_c = jnp.exp(s_c - mx)
        den = jnp.sum(p_p, axis=-1, keepdims=True) + jnp.sum(p_c, axis=-1, keepdims=True)
        acc = (jnp.dot(p_p.astype(BF16), vp_ref[:, sl], preferred_element_type=F32)
               + jnp.dot(p_c.astype(BF16), vc_ref[:, sl], preferred_element_type=F32))
        o_ref[:, sl] = acc / den
        lses.append(mx + jnp.log(den))
    lse_ref[...] = jnp.concatenate(lses, axis=-1)


def _attn_group(proj, rel_bias, gi, window, dilation, qcol, kcol, vcol):
    t = proj.shape[0]
    blk = ATTN_BLOCK
    w_steps = window // dilation
    blocks_per_res = t // dilation // blk
    nblk = t // blk
    a = np.arange(blk)[:, None]
    cc = np.arange(2 * blk)[None, :]
    bucket = _t5_bucket(np.maximum(blk + a - cc, 0) * dilation)

    def prev_map(m):
        return jnp.where(m % blocks_per_res > 0, m - 1, m)

    kern = functools.partial(_attn_kernel, gi * HEADS_PER_GROUP, w_steps, blocks_per_res)
    width = A_GROUP_WIDTH
    return pl.pallas_call(
        kern,
        grid=(nblk,),
        in_specs=[
            pl.BlockSpec(memory_space=pltpu.SMEM),
            pl.BlockSpec((blk, 2 * blk), lambda m: (0, 0)),
            pl.BlockSpec((blk, width), lambda m: (m, qcol)),
            pl.BlockSpec((blk, width), lambda m: (prev_map(m), kcol)),
            pl.BlockSpec((blk, width), lambda m: (m, kcol)),
            pl.BlockSpec((blk, width), lambda m: (prev_map(m), vcol)),
            pl.BlockSpec((blk, width), lambda m: (m, vcol)),
        ],
        out_specs=[pl.BlockSpec((blk, width), lambda m: (m, 0)),
                   pl.BlockSpec((blk, HEADS_PER_GROUP), lambda m: (m, 0))],
        out_shape=[jax.ShapeDtypeStruct((t, width), F32),
                   jax.ShapeDtypeStruct((t, HEADS_PER_GROUP), F32)],
        scratch_shapes=[pltpu.VMEM((HEADS_PER_GROUP, blk, 2 * blk), F32)],
        compiler_params=_params(("arbitrary",)),
        name=f"attn_d{dilation}",
    )(rel_bias, jnp.asarray(bucket), proj, proj, proj, proj, proj)


def _retention_kernel(q_ref, k_ref, v0_ref, v1_ref, g0_ref, g1_ref, dmat_ref, zeta_ref, xi_ref,
                      gch_ref, gn_ref, o_ref, state_ref):
    @pl.when(pl.program_id(0) == 0)
    def _():
        state_ref[...] = jnp.zeros_like(state_ref)

    nt = (((1,), (1,)), ((), ()))
    tn = (((0,), (0,)), ((), ()))
    per_half = RET_HEADS // 2
    for hh in range(RET_HEADS):
        qs = slice(hh * RET_QK_DIM, (hh + 1) * RET_QK_DIM)
        vs = slice(hh * RET_V_DIM, (hh + 1) * RET_V_DIM)
        hs = slice((hh % per_half) * RET_V_DIM, (hh % per_half + 1) * RET_V_DIM)
        v_ref, g_ref = (v0_ref, g0_ref) if hh < per_half else (v1_ref, g1_ref)
        q = q_ref[:, qs]
        k = k_ref[:, qs]
        v = v_ref[:, hs]
        state = state_ref[hh]
        s = lax.dot_general(q, k, nt, preferred_element_type=F32) * dmat_ref[hh]
        inner = jnp.dot(s.astype(BF16), v, preferred_element_type=F32)
        cross = jnp.dot(q, state.astype(BF16), preferred_element_type=F32) * xi_ref[hh]
        vz = (v.astype(F32) * zeta_ref[hh]).astype(BF16)
        upd = lax.dot_general(k, vz, tn, preferred_element_type=F32)
        state_ref[hh] = gch_ref[hh] * state + upd
        ret = inner + cross
        mu = jnp.mean(ret, axis=-1, keepdims=True)
        cen = ret - mu
        var = jnp.mean(cen * cen, axis=-1, keepdims=True)
        y = cen * lax.rsqrt(var + GN_EPS) * gn_ref[:, vs]
        o_ref[:, vs] = (y * g_ref[:, hs].astype(F32)).astype(o_ref.dtype)


def _retention_tables():
    c = RET_CHUNK
    hh = np.arange(RET_HEADS, dtype=np.float64)
    log_g = np.log1p(-np.exp2(-5.0 - hh))
    idx = np.arange(c, dtype=np.float64)
    diff = idx[:, None] - idx[None, :]
    dmat = np.where(diff >= 0, np.exp(log_g[:, None, None] * np.maximum(diff, 0.0)), 0.0)
    zeta = np.exp(log_g[:, None] * (c - 1 - idx))[:, :, None]
    xi = np.exp(log_g[:, None] * (idx + 1.0))[:, :, None]
    gch = np.exp(log_g * c)
    f = lambda v: jnp.asarray(v.astype(np.float32))
    return f(dmat), f(zeta), f(xi), f(gch)


def _retention(proj, gn_g, qcol, kcol, vcol, gcol):
    t = proj.shape[0]
    c = RET_CHUNK
    qw = RET_HEADS * RET_QK_DIM
    vw = RET_HEADS * RET_V_DIM
    dmat, zeta, xi, gch = _retention_tables()
    full3 = lambda shp: pl.BlockSpec(shp, lambda n: (0, 0, 0))
    return pl.pallas_call(
        _retention_kernel,
        grid=(t // c,),
        in_specs=[
            pl.BlockSpec((c, qw), lambda n: (n, qcol)),
            pl.BlockSpec((c, qw), lambda n: (n, kcol)),
            pl.BlockSpec((c, vw // 2), lambda n: (n, vcol)),
            pl.BlockSpec((c, vw // 2), lambda n: (n, vcol + 1)),
            pl.BlockSpec((c, vw // 2), lambda n: (n, gcol)),
            pl.BlockSpec((c, vw // 2), lambda n: (n, gcol + 1)),
            full3((RET_HEADS, c, c)),
            full3((RET_HEADS, c, 1)),
            full3((RET_HEADS, c, 1)),
            pl.BlockSpec(memory_space=pltpu.SMEM),
            pl.BlockSpec((1, vw), lambda n: (0, 0)),
        ],
        out_specs=pl.BlockSpec((c, vw), lambda n: (n, 0)),
        out_shape=jax.ShapeDtypeStruct((t, vw), BF16),
        scratch_shapes=[pltpu.VMEM((RET_HEADS, RET_QK_DIM, RET_V_DIM), F32)],
        compiler_params=_params(("arbitrary",)),
        name="retention",
    )(proj, proj, proj, proj, proj, proj, dmat, zeta, xi, gch, gn_g.reshape(1, vw))


def _merge_kernel(o1_ref, l1_ref, o2_ref, l2_ref, o3_ref, l3_ref, yb_ref, ga_ref, gb_ref,
                  pa_ref, pb_ref, out_ref, ya_ref):
    @pl.when(pl.program_id(1) == 0)
    def _():
        l1 = l1_ref[...]
        l2 = l2_ref[...]
        l3 = l3_ref[...]
        mx = jnp.maximum(jnp.maximum(l1, l2), l3)
        e1 = jnp.exp(l1 - mx)
        e2 = jnp.exp(l2 - mx)
        e3 = jnp.exp(l3 - mx)
        den = e1 + e2 + e3
        a1, a2, a3 = e1 / den, e2 / den, e3 / den
        for hh in range(HEADS_PER_GROUP):
            sl = slice(hh * HEAD_DIM, (hh + 1) * HEAD_DIM)
            ya = (a1[:, hh:hh + 1] * o1_ref[:, sl] + a2[:, hh:hh + 1] * o2_ref[:, sl]
                  + a3[:, hh:hh + 1] * o3_ref[:, sl])
            ya_ref[:, sl] = ya.astype(ya_ref.dtype)

    za = jnp.dot(ya_ref[...], pa_ref[...], preferred_element_type=F32)
    zb = jnp.dot(yb_ref[...], pb_ref[...], preferred_element_type=F32)
    out_ref[...] = (ga_ref[...].astype(F32) * za + gb_ref[...].astype(F32) * zb).astype(out_ref.dtype)


def _merge(o1, l1, o2, l2, o3, l3, yb, proj, ga_col, gb_col, pa, pb, tm=512, tn=1024):
    t = o1.shape[0]
    wa = o1.shape[1]
    wb = yb.shape[1]
    n = pa.shape[1]
    hg = HEADS_PER_GROUP
    ratio = tn // COLBLK
    o_spec = lambda: pl.BlockSpec((tm, wa), lambda i, j: (i, 0))
    l_spec = lambda: pl.BlockSpec((tm, hg), lambda i, j: (i, 0))
    return pl.pallas_call(
        _merge_kernel,
        grid=(t // tm, n // tn),
        in_specs=[
            o_spec(), l_spec(), o_spec(), l_spec(), o_spec(), l_spec(),
            pl.BlockSpec((tm, wb), lambda i, j: (i, 0)),
            pl.BlockSpec((tm, tn), lambda i, j: (i, ga_col // ratio + j)),
            pl.BlockSpec((tm, tn), lambda i, j: (i, gb_col // ratio + j)),
            pl.BlockSpec((wa, tn), lambda i, j: (0, j)),
            pl.BlockSpec((wb, tn), lambda i, j: (0, j)),
        ],
        out_specs=pl.BlockSpec((tm, tn), lambda i, j: (i, j)),
        out_shape=jax.ShapeDtypeStruct((t, n), BF16),
        scratch_shapes=[pltpu.VMEM((tm, wa), BF16)],
        compiler_params=_params(("parallel", "arbitrary")),
        name="merge",
    )(o1, l1, o2, l2, o3, l3, yb, proj, proj, pa, pb)


def _oproj_kernel(x_ref, m_ref, w_ref, g_ref, o_ref):
    z = jnp.dot(m_ref[...], w_ref[...], preferred_element_type=F32)
    o_ref[...] = x_ref[...] + g_ref[...] * z


def _oproj(x, merged, w_bf, mod, gate_blk, tm=512, tn=1024):
    t, d_model = x.shape
    k = merged.shape[1]
    per = d_model // tn
    return pl.pallas_call(
        _oproj_kernel,
        grid=(t // tm, d_model // tn),
        in_specs=[
            pl.BlockSpec((tm, tn), lambda i, j: (i, j)),
            pl.BlockSpec((tm, k), lambda i, j: (i, 0)),
            pl.BlockSpec((k, tn), lambda i, j: (0, j)),
            pl.BlockSpec((1, tn), lambda i, j: (0, gate_blk * per + j)),
        ],
        out_specs=pl.BlockSpec((tm, tn), lambda i, j: (i, j)),
        out_shape=jax.ShapeDtypeStruct((t, d_model), F32),
        compiler_params=_params(("parallel", "arbitrary")),
        name="oproj",
    )(x, merged, w_bf, mod)


def _pack_pair(lo, hi):
    lo_b = pltpu.bitcast(lo.astype(BF16).astype(F32), U32)
    hi_b = pltpu.bitcast(hi.astype(BF16).astype(F32), U32)
    return (lo_b >> 16) | (hi_b & jnp.uint32(0xFFFF0000))


def _unpack_pair(w):
    lo = pltpu.bitcast(w << 16, F32)
    hi = pltpu.bitcast(w & jnp.uint32(0xFFFF0000), F32)
    return lo, hi


def _route_kernel(x_ref, g_ref, sc_ref, sh_ref, wt_ref, rb_ref, h_ref, hp_ref, idx_ref, rank_ref,
                  wgt_ref, cnt_ref):
    @pl.when(pl.program_id(0) == 0)
    def _():
        cnt_ref[...] = jnp.zeros_like(cnt_ref)

    x = x_ref[...]
    tm, d_model = x.shape
    inv = lax.rsqrt(jnp.mean(x * x, axis=-1, keepdims=True) + RMS_EPS)
    h = (x * inv * g_ref[...]) * (1.0 + sc_ref[...]) + sh_ref[...]
    h_ref[...] = h.astype(h_ref.dtype)
    half = d_model // 2
    hp_ref[...] = _pack_pair(h[:, :half], h[:, half:])

    ne = N_EXPERTS
    per = ne // N_GROUPS
    logits = lax.dot_general(wt_ref[...], h, (((1,), (1,)), ((), ())),
                             precision=lax.Precision.HIGHEST,
                             preferred_element_type=F32)
    scores = jax.nn.sigmoid(logits)
    sel = scores + rb_ref[...]
    eidx = lax.broadcasted_iota(I32, (ne, tm), 0).astype(F32)
    minus_inf = -jnp.inf

    sel3 = sel.reshape(N_GROUPS, per, tm)
    sub = lax.broadcasted_iota(I32, (N_GROUPS, per, tm), 1).astype(F32)
    m1 = jnp.max(sel3, axis=1, keepdims=True)
    first = jnp.min(jnp.where(sel3 == m1, sub, float(per)), axis=1, keepdims=True)
    m2 = jnp.max(jnp.where(sub == first, minus_inf, sel3), axis=1, keepdims=True)
    grp = (m1 + m2).reshape(N_GROUPS, tm)

    gidx = lax.broadcasted_iota(I32, (N_GROUPS, tm), 0).astype(F32)
    gmask = jnp.zeros((N_GROUPS, tm), F32)
    work = grp
    for _ in range(TOPK_GROUPS):
        mx = jnp.max(work, axis=0, keepdims=True)
        pick = jnp.min(jnp.where(work == mx, gidx, float(N_GROUPS)), axis=0, keepdims=True)
        hit = gidx == pick
        gmask = jnp.where(hit, 1.0, gmask)
        work = jnp.where(hit, minus_inf, work)
    emask = jnp.broadcast_to(gmask.reshape(N_GROUPS, 1, tm), (N_GROUPS, per, tm)).reshape(ne, tm)

    work = jnp.where(emask > 0.0, sel, minus_inf)
    onehot = jnp.zeros((ne, tm), F32)
    idx_rows, w_rows = [], []
    for _ in range(TOP_K):
        mx = jnp.max(work, axis=0, keepdims=True)
        pick = jnp.min(jnp.where(work == mx, eidx, float(ne)), axis=0, keepdims=True)
        hit = eidx == pick
        onehot = jnp.where(hit, 1.0, onehot)
        work = jnp.where(hit, minus_inf, work)
        idx_rows.append(pick)
        w_rows.append(jnp.sum(jnp.where(hit, scores, 0.0), axis=0, keepdims=True))
    w_all = jnp.concatenate(w_rows, axis=0)
    wgt_ref[...] = w_all / jnp.sum(w_all, axis=0, keepdims=True) * ROUTED_SCALE
    idx_ref[...] = jnp.concatenate(idx_rows, axis=0).astype(I32)

    ra = lax.broadcasted_iota(I32, (tm, tm), 0)
    rb = lax.broadcasted_iota(I32, (tm, tm), 1)
    tri = jnp.where(ra <= rb, 1.0, 0.0).astype(BF16)
    incl = jnp.dot(onehot.astype(BF16), tri, preferred_element_type=F32)
    before = incl - onehot + cnt_ref[...]
    rank_rows = [jnp.sum(jnp.where(eidx == idx_rows[kk], before, 0.0), axis=0, keepdims=True)
                 for kk in range(TOP_K)]
    rank_ref[...] = jnp.concatenate(rank_rows, axis=0).astype(I32)
    cnt_ref[...] = cnt_ref[...] + jnp.sum(onehot, axis=1, keepdims=True)


def _route(x1, g, mod, sc_blk, sh_blk, router_w, router_bias, tm=256):
    t, d_model = x1.shape
    ne = N_EXPERTS
    vec = lambda k: pl.BlockSpec((1, d_model), lambda i, k=k: (0, k))
    tok = lambda: pl.BlockSpec((TOP_K, tm), lambda i: (0, i))
    return pl.pallas_call(
        _route_kernel,
        grid=(t // tm,),
        in_specs=[pl.BlockSpec((tm, d_model), lambda i: (i, 0)),
                  pl.BlockSpec((1, d_model), lambda i: (0, 0)),
                  vec(sc_blk), vec(sh_blk),
                  pl.BlockSpec((ne, d_model), lambda i: (0, 0)),
                  pl.BlockSpec((ne, 1), lambda i: (0, 0))],
        out_specs=[pl.BlockSpec((tm, d_model), lambda i: (i, 0)),
                   pl.BlockSpec((tm, d_model // 2), lambda i: (i, 0)),
                   tok(), tok(), tok(),
                   pl.BlockSpec((ne, 1), lambda i: (0, 0))],
        out_shape=[jax.ShapeDtypeStruct((t, d_model), BF16),
                   jax.ShapeDtypeStruct((t, d_model // 2), U32),
                   jax.ShapeDtypeStruct((TOP_K, t), I32),
                   jax.ShapeDtypeStruct((TOP_K, t), I32),
                   jax.ShapeDtypeStruct((TOP_K, t), F32),
                   jax.ShapeDtypeStruct((ne, 1), F32)],
        compiler_params=_params(("arbitrary",)),
        name="route",
    )(x1, g.reshape(1, d_model), mod, mod, router_w.T, router_bias.reshape(ne, 1))


SUBLANES = 8


def _pad_chunks(bm):
    sizes, s = [], bm // 2
    while s >= SUBLANES:
        sizes.append(s)
        s //= 2
    return sizes


def _dispatch_kernel(bm, n_tok, pos_ref, fill_start_ref, fill_len_ref, nv_ref, hp_ref, xs_ref, inv_ref,
                     sem, pad_sem):
    tm = hp_ref.shape[0]
    tile = pl.program_id(0)

    @pl.when(pl.program_id(0) == 0)
    def _():
        def pad_copies(action):
            def per_expert(e, carry):
                start = fill_start_ref[e]
                n = fill_len_ref[e]
                head = (-start) & (SUBLANES - 1)
                for r in range(SUBLANES - 1):
                    @pl.when(r < head)
                    def _(r=r):
                        action(pltpu.make_async_copy(hp_ref.at[pl.ds(0, 1)],
                                                     xs_ref.at[pl.ds(start + r, 1)], pad_sem))

                start = start + head
                n = n - head
                for size in _pad_chunks(bm):
                    take = (n & size) != 0

                    @pl.when(take)
                    def _(start=start, size=size):
                        dst = pl.multiple_of(start, SUBLANES)
                        action(pltpu.make_async_copy(hp_ref.at[pl.ds(0, size)],
                                                     xs_ref.at[pl.ds(dst, size)], pad_sem))

                    start = start + jnp.where(take, size, 0)
                return carry

            lax.fori_loop(0, N_EXPERTS, per_expert, 0)

            def unused_block(b, carry):
                dst = pl.multiple_of(b * bm, bm)
                action(pltpu.make_async_copy(hp_ref.at[pl.ds(0, bm)],
                                             xs_ref.at[pl.ds(dst, bm)], pad_sem))
                return carry

            lax.fori_loop(nv_ref[0], xs_ref.shape[0] // bm, unused_block, 0)

        pad_copies(lambda cp: cp.start())

        def mark_padding(e, carry):
            def mark(r, c):
                inv_ref[fill_start_ref[e] + r] = -1
                return c

            lax.fori_loop(0, fill_len_ref[e], mark, 0)
            return carry

        lax.fori_loop(0, N_EXPERTS, mark_padding, 0)

        def mark_unused(p, c):
            inv_ref[p] = -1
            return c

        lax.fori_loop(nv_ref[0] * bm, inv_ref.shape[0], mark_unused, 0)
        pad_copies(lambda cp: cp.wait())

    def body(tt, carry):
        for kk in range(TOP_K):
            dst = pos_ref[0, 0, kk * tm + tt]
            inv_ref[dst] = kk * n_tok + tile * tm + tt
            pltpu.make_async_copy(hp_ref.at[pl.ds(tt, 1)], xs_ref.at[pl.ds(dst, 1)], sem).start()
        return carry

    lax.fori_loop(0, tm, body, 0)
    pltpu.make_async_copy(xs_ref.at[pl.ds(0, tm * TOP_K)], xs_ref.at[pl.ds(0, tm * TOP_K)], sem).wait()


def _tile_major(a_t, tm):
    k, t = a_t.shape
    return a_t.reshape(k, t // tm, tm).transpose(1, 0, 2).reshape(t // tm, 1, k * tm)


def _dispatch(hp, pos_t, fill_start, fill_len, n_valid, rows, bm, tm=256):
    t, width = hp.shape
    assert tm >= bm
    smem = lambda: pl.BlockSpec(memory_space=pltpu.SMEM)
    return pl.pallas_call(
        functools.partial(_dispatch_kernel, bm, t),
        grid=(t // tm,),
        in_specs=[pl.BlockSpec((1, 1, tm * TOP_K), lambda i: (i, 0, 0), memory_space=pltpu.SMEM),
                  smem(), smem(), smem(),
                  pl.BlockSpec((tm, width), lambda i: (i, 0))],
        out_specs=[pl.BlockSpec(memory_space=pl.ANY), smem()],
        out_shape=[jax.ShapeDtypeStruct((rows, width), U32),
                   jax.ShapeDtypeStruct((rows,), I32)],
        scratch_shapes=[pltpu.SemaphoreType.DMA(()), pltpu.SemaphoreType.DMA(())],
        compiler_params=_params(("arbitrary",)),
        name="dispatch",
    )(_tile_major(pos_t, tm), fill_start, fill_len, n_valid, hp)


def _experts_kernel(bm, n_slot_rows, seg_ref, sege_ref, nv_ref, inv_ref, x_ref, wg_hbm, wu_hbm, wd_hbm,
                    ysl_ref, wg_f32, wu_f32, wd_f32, wg_bf, wu_bf, wd_bf, y0_ref, y1_ref, sems, ysems):
    b = pl.program_id(0)
    nb = seg_ref.shape[0]
    n_valid = nv_ref[0]
    seg = seg_ref[jnp.minimum(b, nb - 1)]
    slot = seg % 2
    first = (b < nb) & ((b == 0) | (seg_ref[jnp.clip(b - 1, 0, nb - 1)] != seg))
    ybufs = (y0_ref, y1_ref)

    def weight_copies(which_seg, which_slot):
        e = sege_ref[which_seg]
        return [pltpu.make_async_copy(src.at[e], dst.at[which_slot], sems.at[which_slot])
                for src, dst in ((wg_hbm, wg_f32), (wu_hbm, wu_f32), (wd_hbm, wd_f32))]

    @pl.when(b == 0)
    def _():
        for cp in weight_copies(0, 0):
            cp.start()

    @pl.when(first)
    def _():
        for cp in weight_copies(seg, slot):
            cp.wait()
        wg_bf[...] = wg_f32[slot].astype(BF16)
        wu_bf[...] = wu_f32[slot].astype(BF16)
        wd_bf[...] = wd_f32[slot].astype(BF16)

        @pl.when(seg + 1 < nv_ref[1])
        def _():
            for cp in weight_copies(seg + 1, 1 - slot):
                cp.start()

    def compute(y_ref):
        lo, hi = _unpack_pair(x_ref[...])
        half = lo.shape[1]
        lo = lo.astype(BF16)
        hi = hi.astype(BF16)
        gate = (jnp.dot(lo, wg_bf[:half, :], preferred_element_type=F32)
                + jnp.dot(hi, wg_bf[half:, :], preferred_element_type=F32))
        up = (jnp.dot(lo, wu_bf[:half, :], preferred_element_type=F32)
              + jnp.dot(hi, wu_bf[half:, :], preferred_element_type=F32))
        act = (_silu(gate) * up).astype(BF16)
        y = jnp.dot(act, wd_bf[...], preferred_element_type=F32)
        y_ref[...] = _pack_pair(y[:, :half], y[:, half:])

    def scatter(block, parity):
        base = block * bm
        spare = n_slot_rows + parity * bm
        for r in range(bm):
            d = inv_ref[base + r]
            d = jnp.where(d < 0, spare + r, d)
            pltpu.make_async_copy(ybufs[parity].at[pl.ds(r, 1)], ysl_ref.at[pl.ds(d, 1)],
                                  ysems.at[parity]).start()

    for p in (0, 1):
        mine = (b % 2) == p

        @pl.when(mine & (b >= 2) & (b - 2 < n_valid))
        def _(p=p):
            pltpu.make_async_copy(ybufs[p], ysl_ref.at[pl.ds(0, bm)], ysems.at[p]).wait()

        @pl.when(mine & (b >= 1) & (b < n_valid))
        def _(p=p):
            scatter(b - 1, 1 - p)
            compute(ybufs[p])

        @pl.when(mine & (b >= 1) & (b == n_valid))
        def _(p=p):
            scatter(b - 1, 1 - p)

    @pl.when(b == 0)
    def _():
        spare_fill = [pltpu.make_async_copy(
            x_ref, ysl_ref.at[pl.ds(n_slot_rows + parity * bm, bm)], ysems.at[parity])
            for parity in (0, 1)]
        for cp in spare_fill:
            cp.start()
        for cp in spare_fill:
            cp.wait()
        compute(ybufs[0])


def _experts(xs, inv, seg_of, seg_e, n_valid, wg, wu, wd, n_tok, bm=EXPERT_ROWS):
    rows, width = xs.shape
    _, d_model, de = wg.shape
    nb = rows // bm
    n_slot_rows = TOP_K * n_tok
    row_map = lambda b, sg, se, nv, iv: (jnp.minimum(b, nv[0] - 1), 0)
    hbm = lambda: pl.BlockSpec(memory_space=pl.ANY)
    grid_spec = pltpu.PrefetchScalarGridSpec(
        num_scalar_prefetch=4,
        grid=(nb + 2,),
        in_specs=[pl.BlockSpec((bm, width), row_map), hbm(), hbm(), hbm()],
        out_specs=hbm(),
        scratch_shapes=[pltpu.VMEM((2, d_model, de), F32),
                        pltpu.VMEM((2, d_model, de), F32),
                        pltpu.VMEM((2, de, d_model), F32),
                        pltpu.VMEM((d_model, de), BF16),
                        pltpu.VMEM((d_model, de), BF16),
                        pltpu.VMEM((de, d_model), BF16),
                        pltpu.VMEM((bm, width), U32),
                        pltpu.VMEM((bm, width), U32),
                        pltpu.SemaphoreType.DMA((2,)),
                        pltpu.SemaphoreType.DMA((2,))],
    )
    return pl.pallas_call(
        functools.partial(_experts_kernel, bm, n_slot_rows),
        grid_spec=grid_spec,
        out_shape=jax.ShapeDtypeStruct((n_slot_rows + 2 * bm, width), U32),
        compiler_params=_params(("arbitrary",)),
        name="experts",
    )(seg_of, seg_e, n_valid, inv, xs, wg, wu, wd)


def _combine_kernel(x_ref, h_ref, wt_ref, g_ref, sg_ref, su_ref, sd_ref, *rest):
    y_refs, o_ref = rest[:TOP_K], rest[TOP_K]
    h = h_ref[...]
    act = (_silu(jnp.dot(h, sg_ref[...], preferred_element_type=F32))
           * jnp.dot(h, su_ref[...], preferred_element_type=F32)).astype(BF16)
    shared = jnp.dot(act, sd_ref[...], preferred_element_type=F32)
    half = y_refs[0].shape[1]
    wt = wt_ref[...]
    lo_acc = shared[:, :half]
    hi_acc = shared[:, half:]
    for kk in range(TOP_K):
        lo, hi = _unpack_pair(y_refs[kk][...])
        wk = wt[:, kk:kk + 1]
        lo_acc = lo_acc + wk * lo
        hi_acc = hi_acc + wk * hi
    g = g_ref[...]
    o_ref[:, :half] = x_ref[:, :half] + g[:, :half] * lo_acc
    o_ref[:, half:] = x_ref[:, half:] + g[:, half:] * hi_acc


def _combine(x1, h2, wts, mod, gate_blk, sg, su, sd, ysl, tm=256):
    t, d_model = x1.shape
    ds_ = sg.shape[1]
    width = ysl.shape[1]
    tiles = t // tm
    slot = lambda kk: pl.BlockSpec((tm, width), lambda i, kk=kk: (kk * tiles + i, 0))
    return pl.pallas_call(
        _combine_kernel,
        grid=(tiles,),
        in_specs=[pl.BlockSpec((tm, d_model), lambda i: (i, 0)),
                  pl.BlockSpec((tm, d_model), lambda i: (i, 0)),
                  pl.BlockSpec((tm, TOP_K), lambda i: (i, 0)),
                  pl.BlockSpec((1, d_model), lambda i: (0, gate_blk)),
                  pl.BlockSpec((d_model, ds_), lambda i: (0, 0)),
                  pl.BlockSpec((d_model, ds_), lambda i: (0, 0)),
                  pl.BlockSpec((ds_, d_model), lambda i: (0, 0))]
                 + [slot(kk) for kk in range(TOP_K)],
        out_specs=pl.BlockSpec((tm, d_model), lambda i: (i, 0)),
        out_shape=jax.ShapeDtypeStruct((t, d_model), F32),
        compiler_params=_params(("parallel",)),
        name="combine",
    )(x1, h2, wts, mod, sg, su, sd, *([ysl] * TOP_K))


def _layout_kernel(bm, cnt_ref, idx_ref, rank_ref, pos_ref, seg_ref, sege_ref, nv_ref, fs_ref, fl_ref):
    shift = bm.bit_length() - 1
    pos_ref[...] = rank_ref[...]

    def per_expert(e, carry):
        start, blk, seg = carry
        cnt = cnt_ref[e]
        nblk = (cnt + (bm - 1)) >> shift
        pos_ref[...] = pos_ref[...] + jnp.where(idx_ref[...] == e, start, 0)

        def mark(b, c):
            seg_ref[blk + b] = seg
            return c

        lax.fori_loop(0, nblk, mark, 0)

        @pl.when(nblk > 0)
        def _():
            sege_ref[seg] = e

        fs_ref[e] = start + cnt
        fl_ref[e] = (nblk << shift) - cnt
        return start + (nblk << shift), blk + nblk, seg + jnp.where(nblk > 0, 1, 0)

    zero = jnp.int32(0)
    _, n_valid, n_seg = lax.fori_loop(0, N_EXPERTS, per_expert, (zero, zero, zero))
    nv_ref[0] = n_valid
    nv_ref[1] = n_seg

    def tail_blocks(b, c):
        seg_ref[b] = n_seg - 1
        return c

    lax.fori_loop(n_valid, seg_ref.shape[0], tail_blocks, 0)

    def tail_segs(s, c):
        sege_ref[s] = N_EXPERTS - 1
        return c

    lax.fori_loop(n_seg, N_EXPERTS, tail_segs, 0)


def _layout(counts, idx_t, rank_t, bm, n_blocks):
    assert bm & (bm - 1) == 0
    k, t = idx_t.shape
    smem = lambda: pl.BlockSpec(memory_space=pltpu.SMEM)
    full = lambda: pl.BlockSpec((k, t), lambda: (0, 0))
    return pl.pallas_call(
        functools.partial(_layout_kernel, bm),
        in_specs=[smem(), full(), full()],
        out_specs=[full(), smem(), smem(), smem(), smem(), smem()],
        out_shape=[jax.ShapeDtypeStruct((k, t), I32),
                   jax.ShapeDtypeStruct((n_blocks,), I32),
                   jax.ShapeDtypeStruct((N_EXPERTS,), I32),
                   jax.ShapeDtypeStruct((2,), I32),
                   jax.ShapeDtypeStruct((N_EXPERTS,), I32),
                   jax.ShapeDtypeStruct((N_EXPERTS,), I32)],
        name="layout",
    )(counts.reshape(-1).astype(I32), idx_t, rank_t)


def _layer(x, c, rel_bias, w_ada, b_ada, ln1_g, w_in, q_norm_g, k_norm_g, ret_gn_g, p_a, p_b, w_o,
           ln2_g, router_w, router_bias, w_gate_e, w_up_e, w_down_e, w_gate_s, w_up_s, w_down_s):
    t, d_model = x.shape
    dils = tuple(d for _, d in DILATED_GROUPS)

    mod = _ada(c.reshape(d_model), w_ada, b_ada)
    h = _norm1(x, ln1_g, mod)
    cos_tab, sin_tab = _rotary_tables(t)
    w_bf = w_in.astype(BF16)
    projs = []
    for order, (cols, epis) in enumerate(_inproj_plan(d_model)):
        h_in = h if order == 0 else _to_residue_major(h, dils[order])
        projs.append(_inproj(h_in, w_bf, cols, epis, q_norm_g, k_norm_g, cos_tab, sin_tab,
                             f"inproj_d{dils[order]}"))
    proj = projs[0]

    attn = [_attn_group(projs[gi], rel_bias, gi, win, dil, 0, 1, 2)
            for gi, (win, dil) in enumerate(DILATED_GROUPS)]
    base = 3
    rq = RET_HEADS * RET_QK_DIM // COLBLK
    vw_blk = RET_HEADS * RET_V_DIM // COLBLK
    qcol = base
    kcol = base + rq
    vcol_blk = base + 2 * rq
    gcol_blk = vcol_blk + vw_blk
    ga_blk = gcol_blk + vw_blk
    gb_blk = ga_blk + d_model // COLBLK
    y_b = _retention(proj, ret_gn_g, qcol, kcol, vcol_blk, gcol_blk)
    (o1, l1), (o2, l2), (o3, l3) = attn
    o2, l2 = _from_residue_major(o2, dils[1]), _from_residue_major(l2, dils[1])
    o3, l3 = _from_residue_major(o3, dils[2]), _from_residue_major(l3, dils[2])
    merged = _merge(o1, l1, o2, l2, o3, l3, y_b, proj, ga_blk, gb_blk,
                    p_a.astype(BF16), p_b.astype(BF16))
    x1 = _oproj(x, merged, w_o.astype(BF16), mod, 2)

    h2, h2p, idx_t, rank_t, wgt_t, counts = _route(x1, ln2_g, mod, 4, 3, router_w, router_bias)
    bm = EXPERT_ROWS
    n_blocks = (t * TOP_K + N_EXPERTS * (bm - 1) + bm - 1) // bm
    pos_t, seg_of, seg_e, n_valid, fill_start, fill_len = _layout(counts, idx_t, rank_t, bm, n_blocks)
    xs, inv = _dispatch(h2p, pos_t, fill_start, fill_len, n_valid, n_blocks * bm, bm)
    ysl = _experts(xs, inv, seg_of, seg_e, n_valid, w_gate_e, w_up_e, w_down_e, t)
    return _combine(x1, h2, wgt_t.T, mod, 5, w_gate_s.astype(BF16), w_up_s.astype(BF16),
                    w_down_s.astype(BF16), ysl)


def kernel(x, c, rel_bias, w_ada, b_ada, ln1_g, w_in, q_norm_g, k_norm_g, ret_gn_g, p_a, p_b, w_o,
           ln2_g, router_w, router_bias, w_gate_e, w_up_e, w_down_e, w_gate_s, w_up_s, w_down_s):
    b, s, d_model = x.shape
    depth = w_ada.shape[0]
    outs = []
    for bi in range(b):
        xb = x[bi]
        for l in range(depth):
            xb = _layer(xb, c[bi], rel_bias, w_ada[l], b_ada[l], ln1_g[l], w_in[l], q_norm_g[l],
                        k_norm_g[l], ret_gn_g[l], p_a[l], p_b[l], w_o[l], ln2_g[l], router_w[l],
                        router_bias[l], w_gate_e[l], w_up_e[l], w_down_e[l], w_gate_s[l],
                        w_up_s[l], w_down_s[l])
        outs.append(xb)
    return jnp.stack(outs, axis=0)
```

```python
import functools

import numpy as np
import jax
import jax.numpy as jnp
from jax import lax
from jax.experimental import pallas as pl
from jax.experimental.pallas import tpu as pltpu

F32 = jnp.float32
BF16 = jnp.bfloat16
U32 = jnp.uint32
I32 = jnp.int32

HEAD_DIM = 128
DILATED_GROUPS = ((128, 1), (512, 4), (2048, 16))
HEADS_PER_GROUP = 8
N_HEADS_A = HEADS_PER_GROUP * len(DILATED_GROUPS)
A_GROUP_WIDTH = HEADS_PER_GROUP * HEAD_DIM
ATTN_BLOCK = 128
NUM_BUCKETS = 32
MAX_DISTANCE = 2048
NEG_INF = -1e30
RET_HEADS = 8
RET_QK_DIM = 128
RET_V_DIM = 256
RET_CHUNK = 128
ROPE_BASE = 10000.0
GN_EPS = 1e-5
N_EXPERTS = 64
N_GROUPS = 8
TOPK_GROUPS = 4
TOP_K = 8
ROUTED_SCALE = 2.5
RMS_EPS = 1e-6

LANE = 128
COLBLK = 1024
VMEM_LIMIT = 56 * 1024 * 1024
EXPERT_ROWS = 256


def _params(sem, vmem=VMEM_LIMIT):
    return pltpu.CompilerParams(dimension_semantics=sem, vmem_limit_bytes=vmem)


def _sigmoid(v):
    return 0.5 * jnp.tanh(0.5 * v) + 0.5


def _silu(v):
    return v * _sigmoid(v)


def _ada_kernel(c_ref, w_ref, b_ref, o_ref):
    sc = _silu(c_ref[...])
    o_ref[...] = jnp.sum(w_ref[...] * sc, axis=0, keepdims=True) + b_ref[...]


def _ada(c, w, b, tn=512):
    d, n = w.shape
    return pl.pallas_call(
        _ada_kernel,
        grid=(n // tn,),
        in_specs=[pl.BlockSpec((d, 1), lambda j: (0, 0)),
                  pl.BlockSpec((d, tn), lambda j: (0, j)),
                  pl.BlockSpec((1, tn), lambda j: (0, j))],
        out_specs=pl.BlockSpec((1, tn), lambda j: (0, j)),
        out_shape=jax.ShapeDtypeStruct((1, n), F32),
        compiler_params=_params(("parallel",)),
        name="ada",
    )(c.reshape(d, 1), w, b.reshape(1, n))


def _norm1_kernel(x_ref, g_ref, sc_ref, sh_ref, o_ref):
    x = x_ref[...]
    inv = lax.rsqrt(jnp.mean(x * x, axis=-1, keepdims=True) + RMS_EPS)
    o_ref[...] = ((x * inv * g_ref[...]) * (1.0 + sc_ref[...]) + sh_ref[...]).astype(o_ref.dtype)


def _norm1(x, g, mod, tm=512):
    t, d_model = x.shape
    vec = lambda k: pl.BlockSpec((1, d_model), lambda i, k=k: (0, k))
    return pl.pallas_call(
        _norm1_kernel,
        grid=(t // tm,),
        in_specs=[pl.BlockSpec((tm, d_model), lambda i: (i, 0)),
                  pl.BlockSpec((1, d_model), lambda i: (0, 0)),
                  vec(1), vec(0)],
        out_specs=pl.BlockSpec((tm, d_model), lambda i: (i, 0)),
        out_shape=jax.ShapeDtypeStruct((t, d_model), BF16),
        compiler_params=_params(("parallel",)),
        name="norm1",
    )(x, g.reshape(1, d_model), mod, mod)


def _to_residue_major(a, d):
    t, w = a.shape
    return a.reshape(t // d, d, w).transpose(1, 0, 2).reshape(t, w)


def _from_residue_major(a, d):
    t, w = a.shape
    return a.reshape(d, t // d, w).transpose(1, 0, 2).reshape(t, w)


EPI_QNORM, EPI_KNORM, EPI_PLAIN, EPI_ROT_Q, EPI_ROT_K, EPI_SILU, EPI_SIGMOID = range(7)
INPROJ_ROW_CHUNK = 256


def _inproj_kernel(epis_present, colblk_ref, epi_ref, h_ref, w_ref, qg_ref, kg_ref, cos_ref, sin_ref,
                   o_ref):
    del colblk_ref
    epi = epi_ref[pl.program_id(1)]
    tm = h_ref.shape[0]
    nh = o_ref.shape[1] // HEAD_DIM

    def head_norm(gain, scale):
        def fn(acc, rows):
            for hh in range(nh):
                sl = slice(hh * HEAD_DIM, (hh + 1) * HEAD_DIM)
                a = acc[:, sl]
                inv = lax.rsqrt(jnp.mean(a * a, axis=-1, keepdims=True) + RMS_EPS)
                o_ref[rows, sl] = ((a * inv * gain) * scale).astype(o_ref.dtype)
        return fn

    def rotary(scale):
        def fn(acc, rows):
            cos = cos_ref[rows, :]
            sin = sin_ref[rows, :]
            for hh in range(nh):
                sl = slice(hh * HEAD_DIM, (hh + 1) * HEAD_DIM)
                a = acc[:, sl]
                rot = pltpu.roll(a, HEAD_DIM // 2, 1)
                o_ref[rows, sl] = ((a * cos + rot * sin) * scale).astype(o_ref.dtype)
        return fn

    def elementwise(f):
        def fn(acc, rows):
            o_ref[rows, :] = f(acc).astype(o_ref.dtype)
        return fn

    epilogues = {
        EPI_QNORM: lambda: head_norm(qg_ref[...], HEAD_DIM ** -0.5),
        EPI_KNORM: lambda: head_norm(kg_ref[...], 1.0),
        EPI_PLAIN: lambda: elementwise(lambda a: a),
        EPI_ROT_Q: lambda: rotary(1.0),
        EPI_ROT_K: lambda: rotary(RET_QK_DIM ** -0.5),
        EPI_SILU: lambda: elementwise(_silu),
        EPI_SIGMOID: lambda: elementwise(_sigmoid),
    }
    for code in epis_present:
        @pl.when(epi == code)
        def _(code=code):
            fn = epilogues[code]()
            for c in range(tm // INPROJ_ROW_CHUNK):
                rows = slice(c * INPROJ_ROW_CHUNK, (c + 1) * INPROJ_ROW_CHUNK)
                acc = jnp.dot(h_ref[rows, :], w_ref[...], preferred_element_type=F32)
                fn(acc, rows)


def _inproj_plan(d_model):
    a_blocks = N_HEADS_A * HEAD_DIM // COLBLK
    groups = len(DILATED_GROUPS)
    per_group = a_blocks // groups
    rq = RET_HEADS * RET_QK_DIM // COLBLK
    rv = RET_HEADS * RET_V_DIM // COLBLK
    gd = d_model // COLBLK
    seg_epi = ([EPI_QNORM] * a_blocks + [EPI_KNORM] * a_blocks + [EPI_PLAIN] * a_blocks
               + [EPI_ROT_Q] * rq + [EPI_ROT_K] * rq + [EPI_PLAIN] * rv + [EPI_SILU] * rv
               + [EPI_SIGMOID] * (2 * gd))
    order_of = [0] * len(seg_epi)
    for seg in range(3):
        for blk in range(a_blocks):
            order_of[seg * a_blocks + blk] = blk // per_group
    plans = []
    for order in range(groups):
        cols = [cb for cb in range(len(seg_epi)) if order_of[cb] == order]
        plans.append((cols, [seg_epi[cb] for cb in cols]))
    return plans


def _inproj(h, w_bf, cols, epis, qg, kg, cos_tab, sin_tab, name, tm=1024):
    t, d_model = h.shape
    row = lambda width: pl.BlockSpec((tm, width), lambda i, j, cb, ep: (i, 0))
    one = lambda width: pl.BlockSpec((1, width), lambda i, j, cb, ep: (0, 0))
    grid_spec = pltpu.PrefetchScalarGridSpec(
        num_scalar_prefetch=2,
        grid=(t // tm, len(cols)),
        in_specs=[
            row(d_model),
            pl.BlockSpec((d_model, COLBLK), lambda i, j, cb, ep: (0, cb[j])),
            one(HEAD_DIM), one(HEAD_DIM), row(HEAD_DIM), row(HEAD_DIM),
        ],
        out_specs=pl.BlockSpec((tm, COLBLK), lambda i, j, cb, ep: (i, j)),
    )
    return pl.pallas_call(
        functools.partial(_inproj_kernel, tuple(sorted(set(epis)))),
        grid_spec=grid_spec,
        out_shape=jax.ShapeDtypeStruct((t, len(cols) * COLBLK), BF16),
        compiler_params=_params(("parallel", "arbitrary")),
        name=name,
    )(jnp.asarray(np.array(cols, np.int32)), jnp.asarray(np.array(epis, np.int32)),
      h, w_bf, qg.reshape(1, HEAD_DIM), kg.reshape(1, HEAD_DIM), cos_tab, sin_tab)


def _rotary_tables(t):
    half = RET_QK_DIM // 2
    inv = ROPE_BASE ** (-np.arange(0, RET_QK_DIM, 2, dtype=np.float64) / RET_QK_DIM)
    ang = np.arange(t, dtype=np.float64)[:, None] * inv[None, :]
    cos, sin = np.cos(ang), np.sin(ang)
    del half
    cos_tab = np.concatenate([cos, cos], axis=1).astype(np.float32)
    sin_tab = np.concatenate([-sin, sin], axis=1).astype(np.float32)
    return jnp.asarray(cos_tab), jnp.asarray(sin_tab)


def _t5_bucket(dist):
    max_exact = NUM_BUCKETS // 2
    safe = np.maximum(dist, 1).astype(np.float32)
    large = max_exact + (np.log(safe / max_exact) / np.log(MAX_DISTANCE / max_exact)
                         * (NUM_BUCKETS - max_exact)).astype(np.int32)
    return np.where(dist < max_exact, dist, np.minimum(large, NUM_BUCKETS - 1)).astype(np.int32)


def _attn_kernel(head0, w_steps, blocks_per_res, tab_ref, bucket_ref, q_ref, kp_ref, kc_ref,
                 vp_ref, vc_ref, o_ref, lse_ref, bias_ref):
    m_idx = pl.program_id(0)
    blk = ATTN_BLOCK

    @pl.when(m_idx == 0)
    def _():
        bucket = bucket_ref[...]
        for hh in range(HEADS_PER_GROUP):
            bias = jnp.zeros(bucket.shape, F32)
            for b in range(NUM_BUCKETS):
                bias = jnp.where(bucket == b, tab_ref[b, head0 + hh], bias)
            bias_ref[hh] = bias

    a = lax.broadcasted_iota(I32, (blk, blk), 0)
    cc = lax.broadcasted_iota(I32, (blk, blk), 1)
    has_prev = (m_idx % blocks_per_res) > 0
    ok_prev = ((blk + a - cc) <= w_steps) & has_prev
    ok_cur = ((a - cc) >= 0) & ((a - cc) <= w_steps)
    nt = (((1,), (1,)), ((), ()))
    lses = []
    for hh in range(HEADS_PER_GROUP):
        sl = slice(hh * HEAD_DIM, (hh + 1) * HEAD_DIM)
        q = q_ref[:, sl]
        s_p = lax.dot_general(q, kp_ref[:, sl], nt, preferred_element_type=F32)
        s_c = lax.dot_general(q, kc_ref[:, sl], nt, preferred_element_type=F32)
        s_p = jnp.where(ok_prev, s_p + bias_ref[hh, :, :blk], NEG_INF)
        s_c = jnp.where(ok_cur, s_c + bias_ref[hh, :, blk:], NEG_INF)
        mx = jnp.maximum(jnp.max(s_p, axis=-1, keepdims=True),
                         jnp.max(s_c, axis=-1, keepdims=True))
        p_p = jnp.exp(s_p - mx)
        p_c = jnp.exp(s_c - mx)
        den = jnp.sum(p_p, axis=-1, keepdims=True) + jnp.sum(p_c, axis=-1, keepdims=True)
        acc = (jnp.dot(p_p.astype(BF16), vp_ref[:, sl], preferred_element_type=F32)
               + jnp.dot(p_c.astype(BF16), vc_ref[:, sl], preferred_element_type=F32))
        o_ref[:, sl] = acc / den
        lses.append(mx + jnp.log(den))
    lse_ref[...] = jnp.concatenate(lses, axis=-1)


def _attn_group(proj, rel_bias, gi, window, dilation, qcol, kcol, vcol):
    t = proj.shape[0]
    blk = ATTN_BLOCK
    w_steps = window // dilation
    blocks_per_res = t // dilation // blk
    nblk = t // blk
    a = np.arange(blk)[:, None]
    cc = np.arange(2 * blk)[None, :]
    bucket = _t5_bucket(np.maximum(blk + a - cc, 0) * dilation)

    def prev_map(m):
        return jnp.where(m % blocks_per_res > 0, m - 1, m)

    kern = functools.partial(_attn_kernel, gi * HEADS_PER_GROUP, w_steps, blocks_per_res)
    width = A_GROUP_WIDTH
    return pl.pallas_call(
        kern,
        grid=(nblk,),
        in_specs=[
            pl.BlockSpec(memory_space=pltpu.SMEM),
            pl.BlockSpec((blk, 2 * blk), lambda m: (0, 0)),
            pl.BlockSpec((blk, width), lambda m: (m, qcol)),
            pl.BlockSpec((blk, width), lambda m: (prev_map(m), kcol)),
            pl.BlockSpec((blk, width), lambda m: (m, kcol)),
            pl.BlockSpec((blk, width), lambda m: (prev_map(m), vcol)),
            pl.BlockSpec((blk, width), lambda m: (m, vcol)),
        ],
        out_specs=[pl.BlockSpec((blk, width), lambda m: (m, 0)),
                   pl.BlockSpec((blk, HEADS_PER_GROUP), lambda m: (m, 0))],
        out_shape=[jax.ShapeDtypeStruct((t, width), F32),
                   jax.ShapeDtypeStruct((t, HEADS_PER_GROUP), F32)],
        scratch_shapes=[pltpu.VMEM((HEADS_PER_GROUP, blk, 2 * blk), F32)],
        compiler_params=_params(("arbitrary",)),
        name=f"attn_d{dilation}",
    )(rel_bias, jnp.asarray(bucket), proj, proj, proj, proj, proj)


def _retention_kernel(q_ref, k_ref, v0_ref, v1_ref, g0_ref, g1_ref, dmat_ref, zeta_ref, xi_ref,
                      gch_ref, gn_ref, o_ref, state_ref):
    @pl.when(pl.program_id(0) == 0)
    def _():
        state_ref[...] = jnp.zeros_like(state_ref)

    nt = (((1,), (1,)), ((), ()))
    tn = (((0,), (0,)), ((), ()))
    per_half = RET_HEADS // 2
    for hh in range(RET_HEADS):
        qs = slice(hh * RET_QK_DIM, (hh + 1) * RET_QK_DIM)
        vs = slice(hh * RET_V_DIM, (hh + 1) * RET_V_DIM)
        hs = slice((hh % per_half) * RET_V_DIM, (hh % per_half + 1) * RET_V_DIM)
        v_ref, g_ref = (v0_ref, g0_ref) if hh < per_half else (v1_ref, g1_ref)
        q = q_ref[:, qs]
        k = k_ref[:, qs]
        v = v_ref[:, hs]
        state = state_ref[hh]
        s = lax.dot_general(q, k, nt, preferred_element_type=F32) * dmat_ref[hh]
        inner = jnp.dot(s.astype(BF16), v, preferred_element_type=F32)
        cross = jnp.dot(q, state.astype(BF16), preferred_element_type=F32) * xi_ref[hh]
        vz = (v.astype(F32) * zeta_ref[hh]).astype(BF16)
        upd = lax.dot_general(k, vz, tn, preferred_element_type=F32)
        state_ref[hh] = gch_ref[hh] * state + upd
        ret = inner + cross
        mu = jnp.mean(ret, axis=-1, keepdims=True)
        cen = ret - mu
        var = jnp.mean(cen * cen, axis=-1, keepdims=True)
        y = cen * lax.rsqrt(var + GN_EPS) * gn_ref[:, vs]
        o_ref[:, vs] = (y * g_ref[:, hs].astype(F32)).astype(o_ref.dtype)


def _retention_tables():
    c = RET_CHUNK
    hh = np.arange(RET_HEADS, dtype=np.float64)
    log_g = np.log1p(-np.exp2(-5.0 - hh))
    idx = np.arange(c, dtype=np.float64)
    diff = idx[:, None] - idx[None, :]
    dmat = np.where(diff >= 0, np.exp(log_g[:, None, None] * np.maximum(diff, 0.0)), 0.0)
    zeta = np.exp(log_g[:, None] * (c - 1 - idx))[:, :, None]
    xi = np.exp(log_g[:, None] * (idx + 1.0))[:, :, None]
    gch = np.exp(log_g * c)
    f = lambda v: jnp.asarray(v.astype(np.float32))
    return f(dmat), f(zeta), f(xi), f(gch)


def _retention(proj, gn_g, qcol, kcol, vcol, gcol):
    t = proj.shape[0]
    c = RET_CHUNK
    qw = RET_HEADS * RET_QK_DIM
    vw = RET_HEADS * RET_V_DIM
    dmat, zeta, xi, gch = _retention_tables()
    full3 = lambda shp: pl.BlockSpec(shp, lambda n: (0, 0, 0))
    return pl.pallas_call(
        _retention_kernel,
        grid=(t // c,),
        in_specs=[
            pl.BlockSpec((c, qw), lambda n: (n, qcol)),
            pl.BlockSpec((c, qw), lambda n: (n, kcol)),
            pl.BlockSpec((c, vw // 2), lambda n: (n, vcol)),
            pl.BlockSpec((c, vw // 2), lambda n: (n, vcol + 1)),
            pl.BlockSpec((c, vw // 2), lambda n: (n, gcol)),
            pl.BlockSpec((c, vw // 2), lambda n: (n, gcol + 1)),
            full3((RET_HEADS, c, c)),
            full3((RET_HEADS, c, 1)),
            full3((RET_HEADS, c, 1)),
            pl.BlockSpec(memory_space=pltpu.SMEM),
            pl.BlockSpec((1, vw), lambda n: (0, 0)),
        ],
        out_specs=pl.BlockSpec((c, vw), lambda n: (n, 0)),
        out_shape=jax.ShapeDtypeStruct((t, vw), BF16),
        scratch_shapes=[pltpu.VMEM((RET_HEADS, RET_QK_DIM, RET_V_DIM), F32)],
        compiler_params=_params(("arbitrary",)),
        name="retention",
    )(proj, proj, proj, proj, proj, proj, dmat, zeta, xi, gch, gn_g.reshape(1, vw))


def _merge_kernel(o1_ref, l1_ref, o2_ref, l2_ref, o3_ref, l3_ref, yb_ref, ga_ref, gb_ref,
                  pa_ref, pb_ref, out_ref, ya_ref):
    @pl.when(pl.program_id(1) == 0)
    def _():
        l1 = l1_ref[...]
        l2 = l2_ref[...]
        l3 = l3_ref[...]
        mx = jnp.maximum(jnp.maximum(l1, l2), l3)
        e1 = jnp.exp(l1 - mx)
        e2 = jnp.exp(l2 - mx)
        e3 = jnp.exp(l3 - mx)
        den = e1 + e2 + e3
        a1, a2, a3 = e1 / den, e2 / den, e3 / den
        for hh in range(HEADS_PER_GROUP):
            sl = slice(hh * HEAD_DIM, (hh + 1) * HEAD_DIM)
            ya = (a1[:, hh:hh + 1] * o1_ref[:, sl] + a2[:, hh:hh + 1] * o2_ref[:, sl]
                  + a3[:, hh:hh + 1] * o3_ref[:, sl])
            ya_ref[:, sl] = ya.astype(ya_ref.dtype)

    za = jnp.dot(ya_ref[...], pa_ref[...], preferred_element_type=F32)
    zb = jnp.dot(yb_ref[...], pb_ref[...], preferred_element_type=F32)
    out_ref[...] = (ga_ref[...].astype(F32) * za + gb_ref[...].astype(F32) * zb).astype(out_ref.dtype)


def _merge(o1, l1, o2, l2, o3, l3, yb, proj, ga_col, gb_col, pa, pb, tm=512, tn=1024):
    t = o1.shape[0]
    wa = o1.shape[1]
    wb = yb.shape[1]
    n = pa.shape[1]
    hg = HEADS_PER_GROUP
    ratio = tn // COLBLK
    o_spec = lambda: pl.BlockSpec((tm, wa), lambda i, j: (i, 0))
    l_spec = lambda: pl.BlockSpec((tm, hg), lambda i, j: (i, 0))
    return pl.pallas_call(
        _merge_kernel,
        grid=(t // tm, n // tn),
        in_specs=[
            o_spec(), l_spec(), o_spec(), l_spec(), o_spec(), l_spec(),
            pl.BlockSpec((tm, wb), lambda i, j: (i, 0)),
            pl.BlockSpec((tm, tn), lambda i, j: (i, ga_col // ratio + j)),
            pl.BlockSpec((tm, tn), lambda i, j: (i, gb_col // ratio + j)),
            pl.BlockSpec((wa, tn), lambda i, j: (0, j)),
            pl.BlockSpec((wb, tn), lambda i, j: (0, j)),
        ],
        out_specs=pl.BlockSpec((tm, tn), lambda i, j: (i, j)),
        out_shape=jax.ShapeDtypeStruct((t, n), BF16),
        scratch_shapes=[pltpu.VMEM((tm, wa), BF16)],
        compiler_params=_params(("parallel", "arbitrary")),
        name="merge",
    )(o1, l1, o2, l2, o3, l3, yb, proj, proj, pa, pb)


def _oproj_kernel(x_ref, m_ref, w_ref, g_ref, o_ref):
    z = jnp.dot(m_ref[...], w_ref[...], preferred_element_type=F32)
    o_ref[...] = x_ref[...] + g_ref[...] * z


def _oproj(x, merged, w_bf, mod, gate_blk, tm=512, tn=1024):
    t, d_model = x.shape
    k = merged.shape[1]
    per = d_model // tn
    return pl.pallas_call(
        _oproj_kernel,
        grid=(t // tm, d_model // tn),
        in_specs=[
            pl.BlockSpec((tm, tn), lambda i, j: (i, j)),
            pl.BlockSpec((tm, k), lambda i, j: (i, 0)),
            pl.BlockSpec((k, tn), lambda i, j: (0, j)),
            pl.BlockSpec((1, tn), lambda i, j: (0, gate_blk * per + j)),
        ],
        out_specs=pl.BlockSpec((tm, tn), lambda i, j: (i, j)),
        out_shape=jax.ShapeDtypeStruct((t, d_model), F32),
        compiler_params=_params(("parallel", "arbitrary")),
        name="oproj",
    )(x, merged, w_bf, mod)


def _pack_pair(lo, hi):
    lo_b = pltpu.bitcast(lo.astype(BF16).astype(F32), U32)
    hi_b = pltpu.bitcast(hi.astype(BF16).astype(F32), U32)
    return (lo_b >> 16) | (hi_b & jnp.uint32(0xFFFF0000))


def _unpack_pair(w):
    lo = pltpu.bitcast(w << 16, F32)
    hi = pltpu.bitcast(w & jnp.uint32(0xFFFF0000), F32)
    return lo, hi


def _route_kernel(x_ref, g_ref, sc_ref, sh_ref, wt_ref, rb_ref, h_ref, hp_ref, idx_ref, rank_ref,
                  wgt_ref, cnt_ref):
    @pl.when(pl.program_id(0) == 0)
    def _():
        cnt_ref[...] = jnp.zeros_like(cnt_ref)

    x = x_ref[...]
    tm, d_model = x.shape
    inv = lax.rsqrt(jnp.mean(x * x, axis=-1, keepdims=True) + RMS_EPS)
    h = (x * inv * g_ref[...]) * (1.0 + sc_ref[...]) + sh_ref[...]
    h_ref[...] = h.astype(h_ref.dtype)
    half = d_model // 2
    hp_ref[...] = _pack_pair(h[:, :half], h[:, half:])

    ne = N_EXPERTS
    per = ne // N_GROUPS
    logits = lax.dot_general(wt_ref[...], h, (((1,), (1,)), ((), ())),
                             precision=lax.Precision.HIGHEST,
                             preferred_element_type=F32)
    scores = jax.nn.sigmoid(logits)
    sel = scores + rb_ref[...]
    eidx = lax.broadcasted_iota(I32, (ne, tm), 0).astype(F32)
    minus_inf = -jnp.inf

    sel3 = sel.reshape(N_GROUPS, per, tm)
    sub = lax.broadcasted_iota(I32, (N_GROUPS, per, tm), 1).astype(F32)
    m1 = jnp.max(sel3, axis=1, keepdims=True)
    first = jnp.min(jnp.where(sel3 == m1, sub, float(per)), axis=1, keepdims=True)
    m2 = jnp.max(jnp.where(sub == first, minus_inf, sel3), axis=1, keepdims=True)
    grp = (m1 + m2).reshape(N_GROUPS, tm)

    gidx = lax.broadcasted_iota(I32, (N_GROUPS, tm), 0).astype(F32)
    gmask = jnp.zeros((N_GROUPS, tm), F32)
    work = grp
    for _ in range(TOPK_GROUPS):
        mx = jnp.max(work, axis=0, keepdims=True)
        pick = jnp.min(jnp.where(work == mx, gidx, float(N_GROUPS)), axis=0, keepdims=True)
        hit = gidx == pick
        gmask = jnp.where(hit, 1.0, gmask)
        work = jnp.where(hit, minus_inf, work)
    emask = jnp.broadcast_to(gmask.reshape(N_GROUPS, 1, tm), (N_GROUPS, per, tm)).reshape(ne, tm)

    work = jnp.where(emask > 0.0, sel, minus_inf)
    onehot = jnp.zeros((ne, tm), F32)
    idx_rows, w_rows = [], []
    for _ in range(TOP_K):
        mx = jnp.max(work, axis=0, keepdims=True)
        pick = jnp.min(jnp.where(work == mx, eidx, float(ne)), axis=0, keepdims=True)
        hit = eidx == pick
        onehot = jnp.where(hit, 1.0, onehot)
        work = jnp.where(hit, minus_inf, work)
        idx_rows.append(pick)
        w_rows.append(jnp.sum(jnp.where(hit, scores, 0.0), axis=0, keepdims=True))
    w_all = jnp.concatenate(w_rows, axis=0)
    wgt_ref[...] = w_all / jnp.sum(w_all, axis=0, keepdims=True) * ROUTED_SCALE
    idx_ref[...] = jnp.concatenate(idx_rows, axis=0).astype(I32)

    ra = lax.broadcasted_iota(I32, (tm, tm), 0)
    rb = lax.broadcasted_iota(I32, (tm, tm), 1)
    tri = jnp.where(ra <= rb, 1.0, 0.0).astype(BF16)
    incl = jnp.dot(onehot.astype(BF16), tri, preferred_element_type=F32)
    before = incl - onehot + cnt_ref[...]
    rank_rows = [jnp.sum(jnp.where(eidx == idx_rows[kk], before, 0.0), axis=0, keepdims=True)
                 for kk in range(TOP_K)]
    rank_ref[...] = jnp.concatenate(rank_rows, axis=0).astype(I32)
    cnt_ref[...] = cnt_ref[...] + jnp.sum(onehot, axis=1, keepdims=True)


def _route(x1, g, mod, sc_blk, sh_blk, router_w, router_bias, tm=256):
    t, d_model = x1.shape
    ne = N_EXPERTS
    vec = lambda k: pl.BlockSpec((1, d_model), lambda i, k=k: (0, k))
    tok = lambda: pl.BlockSpec((TOP_K, tm), lambda i: (0, i))
    return pl.pallas_call(
        _route_kernel,
        grid=(t // tm,),
        in_specs=[pl.BlockSpec((tm, d_model), lambda i: (i, 0)),
                  pl.BlockSpec((1, d_model), lambda i: (0, 0)),
                  vec(sc_blk), vec(sh_blk),
                  pl.BlockSpec((ne, d_model), lambda i: (0, 0)),
                  pl.BlockSpec((ne, 1), lambda i: (0, 0))],
        out_specs=[pl.BlockSpec((tm, d_model), lambda i: (i, 0)),
                   pl.BlockSpec((tm, d_model // 2), lambda i: (i, 0)),
                   tok(), tok(), tok(),
                   pl.BlockSpec((ne, 1), lambda i: (0, 0))],
        out_shape=[jax.ShapeDtypeStruct((t, d_model), BF16),
                   jax.ShapeDtypeStruct((t, d_model // 2), U32),
                   jax.ShapeDtypeStruct((TOP_K, t), I32),
                   jax.ShapeDtypeStruct((TOP_K, t), I32),
                   jax.ShapeDtypeStruct((TOP_K, t), F32),
                   jax.ShapeDtypeStruct((ne, 1), F32)],
        compiler_params=_params(("arbitrary",)),
        name="route",
    )(x1, g.reshape(1, d_model), mod, mod, router_w.T, router_bias.reshape(ne, 1))


INVERT_UNROLL = 8


def _invert_kernel(bm, n_tok, tm, pos_ref, fill_start_ref, fill_len_ref, nv_ref, inv_ref):
    tile = pl.program_id(0)

    @pl.when(tile == 0)
    def _():
        def mark_block(first_row):
            def mark(i, c):
                for u in range(INVERT_UNROLL):
                    inv_ref[first_row + i * INVERT_UNROLL + u] = -1
                return c

            lax.fori_loop(0, bm // INVERT_UNROLL, mark, 0)

        def mark_padding(e, carry):
            @pl.when(fill_len_ref[e] > 0)
            def _():
                mark_block(fill_start_ref[e] + fill_len_ref[e] - bm)

            return carry

        lax.fori_loop(0, N_EXPERTS, mark_padding, 0)

        def mark_unused(blk, c):
            mark_block(blk * bm)
            return c

        lax.fori_loop(nv_ref[0], inv_ref.shape[0] // bm, mark_unused, 0)

    def body(tt, carry):
        for kk in range(TOP_K):
            inv_ref[pos_ref[0, 0, kk * tm + tt]] = kk * n_tok + tile * tm + tt
        return carry

    lax.fori_loop(0, tm, body, 0, unroll=INVERT_UNROLL)


def _tile_major(a_t, tm):
    k, t = a_t.shape
    return a_t.reshape(k, t // tm, tm).transpose(1, 0, 2).reshape(t // tm, 1, k * tm)


def _invert(pos_t, fill_start, fill_len, n_valid, rows, bm, tm=1024):
    _, t = pos_t.shape
    smem = lambda: pl.BlockSpec(memory_space=pltpu.SMEM)
    return pl.pallas_call(
        functools.partial(_invert_kernel, bm, t, tm),
        grid=(t // tm,),
        in_specs=[pl.BlockSpec((1, 1, tm * TOP_K), lambda i: (i, 0, 0), memory_space=pltpu.SMEM),
                  smem(), smem(), smem()],
        out_specs=smem(),
        out_shape=jax.ShapeDtypeStruct((rows,), I32),
        compiler_params=_params(("arbitrary",)),
        name="invert",
    )(_tile_major(pos_t, tm), fill_start, fill_len, n_valid)


X_BUFFERS = 4
Y_BUFFERS = 2


def _experts_kernel(bm, n_tok, seg_ref, sege_ref, nv_ref, inv_ref, hp_hbm, wg_hbm, wu_hbm, wd_hbm,
                    ysl_ref, wg_f32, wu_f32, wd_f32, wg_bf, wu_bf, wd_bf,
                    x0_ref, x1_ref, x2_ref, x3_ref, y0_ref, y1_ref, sems, xsems, ysems):
    b = pl.program_id(0)
    nb = seg_ref.shape[0]
    n_valid = nv_ref[0]
    n_slot_rows = TOP_K * n_tok
    seg = seg_ref[jnp.minimum(b, nb - 1)]
    slot = seg % 2
    first = (b < nb) & ((b == 0) | (seg_ref[jnp.clip(b - 1, 0, nb - 1)] != seg))
    xbufs = (x0_ref, x1_ref, x2_ref, x3_ref)
    ybufs = (y0_ref, y1_ref)

    def weight_copies(which_seg, which_slot):
        e = sege_ref[which_seg]
        return [pltpu.make_async_copy(src.at[e], dst.at[which_slot], sems.at[which_slot])
                for src, dst in ((wg_hbm, wg_f32), (wu_hbm, wu_f32), (wd_hbm, wd_f32))]

    @pl.when(b == 0)
    def _():
        for cp in weight_copies(0, 0):
            cp.start()

    @pl.when(first)
    def _():
        for cp in weight_copies(seg, slot):
            cp.wait()
        wg_bf[...] = wg_f32[slot].astype(BF16)
        wu_bf[...] = wu_f32[slot].astype(BF16)
        wd_bf[...] = wd_f32[slot].astype(BF16)

        @pl.when(seg + 1 < nv_ref[1])
        def _():
            for cp in weight_copies(seg + 1, 1 - slot):
                cp.start()

    def compute(x_ref, y_ref):
        lo, hi = _unpack_pair(x_ref[...])
        half = lo.shape[1]
        lo = lo.astype(BF16)
        hi = hi.astype(BF16)
        gate = (jnp.dot(lo, wg_bf[:half, :], preferred_element_type=F32)
                + jnp.dot(hi, wg_bf[half:, :], preferred_element_type=F32))
        up = (jnp.dot(lo, wu_bf[:half, :], preferred_element_type=F32)
              + jnp.dot(hi, wu_bf[half:, :], preferred_element_type=F32))
        act = (_silu(gate) * up).astype(BF16)
        y = jnp.dot(act, wd_bf[...], preferred_element_type=F32)
        y_ref[...] = _pack_pair(y[:, :half], y[:, half:])

    def scatter(block, parity):
        base = block * bm
        spare = n_slot_rows + parity * bm
        for r in range(bm):
            d = inv_ref[base + r]
            d = jnp.where(d < 0, spare + r, d)
            pltpu.make_async_copy(ybufs[parity].at[pl.ds(r, 1)], ysl_ref.at[pl.ds(d, 1)],
                                  ysems.at[parity]).start()

    def gather(block, xslot):
        base = jnp.minimum(block, nb - 1) * bm
        for r in range(bm):
            d = inv_ref[base + r]
            tok = jnp.where(d < 0, 0, d & (n_tok - 1))
            pltpu.make_async_copy(hp_hbm.at[pl.ds(tok, 1)], xbufs[xslot].at[pl.ds(r, 1)],
                                  xsems.at[xslot]).start()

    def wait_rows(buf, sem):
        pltpu.make_async_copy(hp_hbm.at[pl.ds(0, bm)], buf, sem).wait()

    lookahead = X_BUFFERS // 2

    @pl.when(b == 0)
    def _():
        for blk in range(lookahead):
            gather(blk, blk)

    for c in range(X_BUFFERS):
        mine = (b % X_BUFFERS) == c
        p = c % Y_BUFFERS

        @pl.when(mine & (b < n_valid + lookahead))
        def _(c=c):
            wait_rows(xbufs[c], xsems.at[c])

        @pl.when(mine & (b >= Y_BUFFERS) & (b - Y_BUFFERS < n_valid))
        def _(p=p):
            pltpu.make_async_copy(ybufs[p], ysl_ref.at[pl.ds(0, bm)], ysems.at[p]).wait()

        @pl.when(mine & (b >= 1) & (b < n_valid))
        def _(c=c, p=p):
            gather(b + lookahead, (c + lookahead) % X_BUFFERS)
            scatter(b - 1, 1 - p)
            compute(xbufs[c], ybufs[p])

        @pl.when(mine & (b >= 1) & (b == n_valid))
        def _(p=p):
            scatter(b - 1, 1 - p)

    @pl.when(b == 0)
    def _():
        spare_fill = [pltpu.make_async_copy(
            xbufs[0], ysl_ref.at[pl.ds(n_slot_rows + parity * bm, bm)], ysems.at[parity])
            for parity in range(Y_BUFFERS)]
        for cp in spare_fill:
            cp.start()
        for cp in spare_fill:
            cp.wait()
        gather(lookahead, lookahead)
        compute(xbufs[0], ybufs[0])


def _experts(hp, inv, seg_of, seg_e, n_valid, wg, wu, wd, bm=EXPERT_ROWS):
    n_tok, width = hp.shape
    assert n_tok & (n_tok - 1) == 0
    _, d_model, de = wg.shape
    nb = seg_of.shape[0]
    n_slot_rows = TOP_K * n_tok
    hbm = lambda: pl.BlockSpec(memory_space=pl.ANY)
    grid_spec = pltpu.PrefetchScalarGridSpec(
        num_scalar_prefetch=4,
        grid=(nb + 2,),
        in_specs=[hbm(), hbm(), hbm(), hbm()],
        out_specs=hbm(),
        scratch_shapes=[pltpu.VMEM((2, d_model, de), F32),
                        pltpu.VMEM((2, d_model, de), F32),
                        pltpu.VMEM((2, de, d_model), F32),
                        pltpu.VMEM((d_model, de), BF16),
                        pltpu.VMEM((d_model, de), BF16),
                        pltpu.VMEM((de, d_model), BF16),
                        ]
                       + [pltpu.VMEM((bm, width), U32)] * (X_BUFFERS + Y_BUFFERS)
                       + [pltpu.SemaphoreType.DMA((2,)),
                          pltpu.SemaphoreType.DMA((X_BUFFERS,)),
                          pltpu.SemaphoreType.DMA((Y_BUFFERS,))],
    )
    return pl.pallas_call(
        functools.partial(_experts_kernel, bm, n_tok),
        grid_spec=grid_spec,
        out_shape=jax.ShapeDtypeStruct((n_slot_rows + Y_BUFFERS * bm, width), U32),
        compiler_params=_params(("arbitrary",)),
        name="experts",
    )(seg_of, seg_e, n_valid, inv, hp, wg, wu, wd)


def _combine_kernel(x_ref, h_ref, wt_ref, g_ref, sg_ref, su_ref, sd_ref, *rest):
    y_refs, o_ref = rest[:TOP_K], rest[TOP_K]
    h = h_ref[...]
    act = (_silu(jnp.dot(h, sg_ref[...], preferred_element_type=F32))
           * jnp.dot(h, su_ref[...], preferred_element_type=F32)).astype(BF16)
    shared = jnp.dot(act, sd_ref[...], preferred_element_type=F32)
    half = y_refs[0].shape[1]
    wt = wt_ref[...]
    lo_acc = shared[:, :half]
    hi_acc = shared[:, half:]
    for kk in range(TOP_K):
        lo, hi = _unpack_pair(y_refs[kk][...])
        wk = wt[:, kk:kk + 1]
        lo_acc = lo_acc + wk * lo
        hi_acc = hi_acc + wk * hi
    g = g_ref[...]
    o_ref[:, :half] = x_ref[:, :half] + g[:, :half] * lo_acc
    o_ref[:, half:] = x_ref[:, half:] + g[:, half:] * hi_acc


def _combine(x1, h2, wts, mod, gate_blk, sg, su, sd, ysl, tm=256):
    t, d_model = x1.shape
    ds_ = sg.shape[1]
    width = ysl.shape[1]
    tiles = t // tm
    slot = lambda kk: pl.BlockSpec((tm, width), lambda i, kk=kk: (kk * tiles + i, 0))
    return pl.pallas_call(
        _combine_kernel,
        grid=(tiles,),
        in_specs=[pl.BlockSpec((tm, d_model), lambda i: (i, 0)),
                  pl.BlockSpec((tm, d_model), lambda i: (i, 0)),
                  pl.BlockSpec((tm, TOP_K), lambda i: (i, 0)),
                  pl.BlockSpec((1, d_model), lambda i: (0, gate_blk)),
                  pl.BlockSpec((d_model, ds_), lambda i: (0, 0)),
                  pl.BlockSpec((d_model, ds_), lambda i: (0, 0)),
                  pl.BlockSpec((ds_, d_model), lambda i: (0, 0))]
                 + [slot(kk) for kk in range(TOP_K)],
        out_specs=pl.BlockSpec((tm, d_model), lambda i: (i, 0)),
        out_shape=jax.ShapeDtypeStruct((t, d_model), F32),
        compiler_params=_params(("parallel",)),
        name="combine",
    )(x1, h2, wts, mod, sg, su, sd, *([ysl] * TOP_K))


def _layout_kernel(bm, cnt_ref, idx_ref, rank_ref, pos_ref, seg_ref, sege_ref, nv_ref, fs_ref, fl_ref):
    shift = bm.bit_length() - 1
    pos_ref[...] = rank_ref[...]

    def per_expert(e, carry):
        start, blk, seg = carry
        cnt = cnt_ref[e]
        nblk = (cnt + (bm - 1)) >> shift
        pos_ref[...] = pos_ref[...] + jnp.where(idx_ref[...] == e, start, 0)

        def mark(b, c):
            seg_ref[blk + b] = seg
            return c

        lax.fori_loop(0, nblk, mark, 0)

        @pl.when(nblk > 0)
        def _():
            sege_ref[seg] = e

        fs_ref[e] = start + cnt
        fl_ref[e] = (nblk << shift) - cnt
        return start + (nblk << shift), blk + nblk, seg + jnp.where(nblk > 0, 1, 0)

    zero = jnp.int32(0)
    _, n_valid, n_seg = lax.fori_loop(0, N_EXPERTS, per_expert, (zero, zero, zero))
    nv_ref[0] = n_valid
    nv_ref[1] = n_seg

    def tail_blocks(b, c):
        seg_ref[b] = n_seg - 1
        return c

    lax.fori_loop(n_valid, seg_ref.shape[0], tail_blocks, 0)

    def tail_segs(s, c):
        sege_ref[s] = N_EXPERTS - 1
        return c

    lax.fori_loop(n_seg, N_EXPERTS, tail_segs, 0)


def _layout(counts, idx_t, rank_t, bm, n_blocks):
    assert bm & (bm - 1) == 0
    k, t = idx_t.shape
    smem = lambda: pl.BlockSpec(memory_space=pltpu.SMEM)
    full = lambda: pl.BlockSpec((k, t), lambda: (0, 0))
    return pl.pallas_call(
        functools.partial(_layout_kernel, bm),
        in_specs=[smem(), full(), full()],
        out_specs=[full(), smem(), smem(), smem(), smem(), smem()],
        out_shape=[jax.ShapeDtypeStruct((k, t), I32),
                   jax.ShapeDtypeStruct((n_blocks,), I32),
                   jax.ShapeDtypeStruct((N_EXPERTS,), I32),
                   jax.ShapeDtypeStruct((2,), I32),
                   jax.ShapeDtypeStruct((N_EXPERTS,), I32),
                   jax.ShapeDtypeStruct((N_EXPERTS,), I32)],
        name="layout",
    )(counts.reshape(-1).astype(I32), idx_t, rank_t)


def _layer(x, c, rel_bias, w_ada, b_ada, ln1_g, w_in, q_norm_g, k_norm_g, ret_gn_g, p_a, p_b, w_o,
           ln2_g, router_w, router_bias, w_gate_e, w_up_e, w_down_e, w_gate_s, w_up_s, w_down_s):
    t, d_model = x.shape
    dils = tuple(d for _, d in DILATED_GROUPS)

    mod = _ada(c.reshape(d_model), w_ada, b_ada)
    h = _norm1(x, ln1_g, mod)
    cos_tab, sin_tab = _rotary_tables(t)
    w_bf = w_in.astype(BF16)
    projs = []
    for order, (cols, epis) in enumerate(_inproj_plan(d_model)):
        h_in = h if order == 0 else _to_residue_major(h, dils[order])
        projs.append(_inproj(h_in, w_bf, cols, epis, q_norm_g, k_norm_g, cos_tab, sin_tab,
                             f"inproj_d{dils[order]}"))
    proj = projs[0]

    attn = [_attn_group(projs[gi], rel_bias, gi, win, dil, 0, 1, 2)
            for gi, (win, dil) in enumerate(DILATED_GROUPS)]
    base = 3
    rq = RET_HEADS * RET_QK_DIM // COLBLK
    vw_blk = RET_HEADS * RET_V_DIM // COLBLK
    qcol = base
    kcol = base + rq
    vcol_blk = base + 2 * rq
    gcol_blk = vcol_blk + vw_blk
    ga_blk = gcol_blk + vw_blk
    gb_blk = ga_blk + d_model // COLBLK
    y_b = _retention(proj, ret_gn_g, qcol, kcol, vcol_blk, gcol_blk)
    (o1, l1), (o2, l2), (o3, l3) = attn
    o2, l2 = _from_residue_major(o2, dils[1]), _from_residue_major(l2, dils[1])
    o3, l3 = _from_residue_major(o3, dils[2]), _from_residue_major(l3, dils[2])
    merged = _merge(o1, l1, o2, l2, o3, l3, y_b, proj, ga_blk, gb_blk,
                    p_a.astype(BF16), p_b.astype(BF16))
    x1 = _oproj(x, merged, w_o.astype(BF16), mod, 2)

    h2, h2p, idx_t, rank_t, wgt_t, counts = _route(x1, ln2_g, mod, 4, 3, router_w, router_bias)
    bm = EXPERT_ROWS
    n_blocks = (t * TOP_K + N_EXPERTS * (bm - 1) + bm - 1) // bm
    pos_t, seg_of, seg_e, n_valid, fill_start, fill_len = _layout(counts, idx_t, rank_t, bm, n_blocks)
    inv = _invert(pos_t, fill_start, fill_len, n_valid, n_blocks * bm, bm)
    ysl = _experts(h2p, inv, seg_of, seg_e, n_valid, w_gate_e, w_up_e, w_down_e)
    return _combine(x1, h2, wgt_t.T, mod, 5, w_gate_s.astype(BF16), w_up_s.astype(BF16),
                    w_down_s.astype(BF16), ysl)


def kernel(x, c, rel_bias, w_ada, b_ada, ln1_g, w_in, q_norm_g, k_norm_g, ret_gn_g, p_a, p_b, w_o,
           ln2_g, router_w, router_bias, w_gate_e, w_up_e, w_down_e, w_gate_s, w_up_s, w_down_s):
    b, s, d_model = x.shape
    depth = w_ada.shape[0]
    outs = []
    for bi in range(b):
        xb = x[bi]
        for l in range(depth):
            xb = _layer(xb, c[bi], rel_bias, w_ada[l], b_ada[l], ln1_g[l], w_in[l], q_norm_g[l],
                        k_norm_g[l], ret_gn_g[l], p_a[l], p_b[l], w_o[l], ln2_g[l], router_w[l],
                        router_bias[l], w_gate_e[l], w_up_e[l], w_down_e[l], w_gate_s[l],
                        w_up_s[l], w_down_s[l])
        outs.append(xb)
    return jnp.stack(outs, axis=0)
```

```python
import functools

import numpy as np
import jax
import jax.numpy as jnp
from jax import lax
from jax.experimental import pallas as pl
from jax.experimental.pallas import tpu as pltpu

F32 = jnp.float32
BF16 = jnp.bfloat16
U32 = jnp.uint32
I32 = jnp.int32

HEAD_DIM = 128
DILATED_GROUPS = ((128, 1), (512, 4), (2048, 16))
HEADS_PER_GROUP = 8
N_HEADS_A = HEADS_PER_GROUP * len(DILATED_GROUPS)
A_GROUP_WIDTH = HEADS_PER_GROUP * HEAD_DIM
ATTN_BLOCK = 128
NUM_BUCKETS = 32
MAX_DISTANCE = 2048
NEG_INF = -1e30
RET_HEADS = 8
RET_QK_DIM = 128
RET_V_DIM = 256
RET_CHUNK = 128
ROPE_BASE = 10000.0
GN_EPS = 1e-5
N_EXPERTS = 64
N_GROUPS = 8
TOPK_GROUPS = 4
TOP_K = 8
ROUTED_SCALE = 2.5
RMS_EPS = 1e-6

LANE = 128
COLBLK = 1024
VMEM_LIMIT = 56 * 1024 * 1024
EXPERT_ROWS = 256


def _params(sem, vmem=VMEM_LIMIT):
    return pltpu.CompilerParams(dimension_semantics=sem, vmem_limit_bytes=vmem)


def _sigmoid(v):
    return 0.5 * jnp.tanh(0.5 * v) + 0.5


def _silu(v):
    return v * _sigmoid(v)


def _ada_kernel(c_ref, w_ref, b_ref, o_ref):
    sc = _silu(c_ref[...])
    o_ref[...] = jnp.sum(w_ref[...] * sc, axis=0, keepdims=True) + b_ref[...]


def _ada(c, w, b, tn=512):
    d, n = w.shape
    return pl.pallas_call(
        _ada_kernel,
        grid=(n // tn,),
        in_specs=[pl.BlockSpec((d, 1), lambda j: (0, 0)),
                  pl.BlockSpec((d, tn), lambda j: (0, j)),
                  pl.BlockSpec((1, tn), lambda j: (0, j))],
        out_specs=pl.BlockSpec((1, tn), lambda j: (0, j)),
        out_shape=jax.ShapeDtypeStruct((1, n), F32),
        compiler_params=_params(("parallel",)),
        name="ada",
    )(c.reshape(d, 1), w, b.reshape(1, n))


def _norm1_kernel(x_ref, g_ref, sc_ref, sh_ref, o_ref):
    x = x_ref[...]
    inv = lax.rsqrt(jnp.mean(x * x, axis=-1, keepdims=True) + RMS_EPS)
    o_ref[...] = ((x * inv * g_ref[...]) * (1.0 + sc_ref[...]) + sh_ref[...]).astype(o_ref.dtype)


def _norm1(x, g, mod, tm=512):
    t, d_model = x.shape
    vec = lambda k: pl.BlockSpec((1, d_model), lambda i, k=k: (0, k))
    return pl.pallas_call(
        _norm1_kernel,
        grid=(t // tm,),
        in_specs=[pl.BlockSpec((tm, d_model), lambda i: (i, 0)),
                  pl.BlockSpec((1, d_model), lambda i: (0, 0)),
                  vec(1), vec(0)],
        out_specs=pl.BlockSpec((tm, d_model), lambda i: (i, 0)),
        out_shape=jax.ShapeDtypeStruct((t, d_model), BF16),
        compiler_params=_params(("parallel",)),
        name="norm1",
    )(x, g.reshape(1, d_model), mod, mod)


def _to_residue_major(a, d):
    t, w = a.shape
    return a.reshape(t // d, d, w).transpose(1, 0, 2).reshape(t, w)


def _from_residue_major(a, d):
    t, w = a.shape
    return a.reshape(d, t // d, w).transpose(1, 0, 2).reshape(t, w)


EPI_QNORM, EPI_KNORM, EPI_PLAIN, EPI_ROT_Q, EPI_ROT_K, EPI_SILU, EPI_SIGMOID = range(7)
INPROJ_ROW_CHUNK = 256


def _inproj_kernel(epis_present, colblk_ref, epi_ref, h_ref, w_ref, qg_ref, kg_ref, cos_ref, sin_ref,
                   o_ref):
    del colblk_ref
    epi = epi_ref[pl.program_id(1)]
    tm = h_ref.shape[0]
    nh = o_ref.shape[1] // HEAD_DIM

    def head_norm(gain, scale):
        def fn(acc, rows):
            for hh in range(nh):
                sl = slice(hh * HEAD_DIM, (hh + 1) * HEAD_DIM)
                a = acc[:, sl]
                inv = lax.rsqrt(jnp.mean(a * a, axis=-1, keepdims=True) + RMS_EPS)
                o_ref[rows, sl] = ((a * inv * gain) * scale).astype(o_ref.dtype)
        return fn

    def rotary(scale):
        def fn(acc, rows):
            cos = cos_ref[rows, :]
            sin = sin_ref[rows, :]
            for hh in range(nh):
                sl = slice(hh * HEAD_DIM, (hh + 1) * HEAD_DIM)
                a = acc[:, sl]
                rot = pltpu.roll(a, HEAD_DIM // 2, 1)
                o_ref[rows, sl] = ((a * cos + rot * sin) * scale).astype(o_ref.dtype)
        return fn

    def elementwise(f):
        def fn(acc, rows):
            o_ref[rows, :] = f(acc).astype(o_ref.dtype)
        return fn

    epilogues = {
        EPI_QNORM: lambda: head_norm(qg_ref[...], HEAD_DIM ** -0.5),
        EPI_KNORM: lambda: head_norm(kg_ref[...], 1.0),
        EPI_PLAIN: lambda: elementwise(lambda a: a),
        EPI_ROT_Q: lambda: rotary(1.0),
        EPI_ROT_K: lambda: rotary(RET_QK_DIM ** -0.5),
        EPI_SILU: lambda: elementwise(_silu),
        EPI_SIGMOID: lambda: elementwise(_sigmoid),
    }
    for code in epis_present:
        @pl.when(epi == code)
        def _(code=code):
            fn = epilogues[code]()
            for c in range(tm // INPROJ_ROW_CHUNK):
                rows = slice(c * INPROJ_ROW_CHUNK, (c + 1) * INPROJ_ROW_CHUNK)
                acc = jnp.dot(h_ref[rows, :], w_ref[...], preferred_element_type=F32)
                fn(acc, rows)


def _inproj_plan(d_model):
    a_blocks = N_HEADS_A * HEAD_DIM // COLBLK
    groups = len(DILATED_GROUPS)
    per_group = a_blocks // groups
    rq = RET_HEADS * RET_QK_DIM // COLBLK
    rv = RET_HEADS * RET_V_DIM // COLBLK
    gd = d_model // COLBLK
    seg_epi = ([EPI_QNORM] * a_blocks + [EPI_KNORM] * a_blocks + [EPI_PLAIN] * a_blocks
               + [EPI_ROT_Q] * rq + [EPI_ROT_K] * rq + [EPI_PLAIN] * rv + [EPI_SILU] * rv
               + [EPI_SIGMOID] * (2 * gd))
    order_of = [0] * len(seg_epi)
    for seg in range(3):
        for blk in range(a_blocks):
            order_of[seg * a_blocks + blk] = blk // per_group
    plans = []
    for order in range(groups):
        cols = [cb for cb in range(len(seg_epi)) if order_of[cb] == order]
        plans.append((cols, [seg_epi[cb] for cb in cols]))
    return plans


def _inproj(h, w_bf, cols, epis, qg, kg, cos_tab, sin_tab, name, tm=1024):
    t, d_model = h.shape
    row = lambda width: pl.BlockSpec((tm, width), lambda i, j, cb, ep: (i, 0))
    one = lambda width: pl.BlockSpec((1, width), lambda i, j, cb, ep: (0, 0))
    grid_spec = pltpu.PrefetchScalarGridSpec(
        num_scalar_prefetch=2,
        grid=(t // tm, len(cols)),
        in_specs=[
            row(d_model),
            pl.BlockSpec((d_model, COLBLK), lambda i, j, cb, ep: (0, cb[j])),
            one(HEAD_DIM), one(HEAD_DIM), row(HEAD_DIM), row(HEAD_DIM),
        ],
        out_specs=pl.BlockSpec((tm, COLBLK), lambda i, j, cb, ep: (i, j)),
    )
    return pl.pallas_call(
        functools.partial(_inproj_kernel, tuple(sorted(set(epis)))),
        grid_spec=grid_spec,
        out_shape=jax.ShapeDtypeStruct((t, len(cols) * COLBLK), BF16),
        compiler_params=_params(("parallel", "arbitrary")),
        name=name,
    )(jnp.asarray(np.array(cols, np.int32)), jnp.asarray(np.array(epis, np.int32)),
      h, w_bf, qg.reshape(1, HEAD_DIM), kg.reshape(1, HEAD_DIM), cos_tab, sin_tab)


def _rotary_tables(t):
    half = RET_QK_DIM // 2
    inv = ROPE_BASE ** (-np.arange(0, RET_QK_DIM, 2, dtype=np.float64) / RET_QK_DIM)
    ang = np.arange(t, dtype=np.float64)[:, None] * inv[None, :]
    cos, sin = np.cos(ang), np.sin(ang)
    del half
    cos_tab = np.concatenate([cos, cos], axis=1).astype(np.float32)
    sin_tab = np.concatenate([-sin, sin], axis=1).astype(np.float32)
    return jnp.asarray(cos_tab), jnp.asarray(sin_tab)


def _t5_bucket(dist):
    max_exact = NUM_BUCKETS // 2
    safe = np.maximum(dist, 1).astype(np.float32)
    large = max_exact + (np.log(safe / max_exact) / np.log(MAX_DISTANCE / max_exact)
                         * (NUM_BUCKETS - max_exact)).astype(np.int32)
    return np.where(dist < max_exact, dist, np.minimum(large, NUM_BUCKETS - 1)).astype(np.int32)


def _attn_kernel(head0, w_steps, blocks_per_res, tab_ref, bucket_ref, q_ref, kp_ref, kc_ref,
                 vp_ref, vc_ref, o_ref, lse_ref, bias_ref):
    m_idx = pl.program_id(0)
    blk = ATTN_BLOCK

    @pl.when(m_idx == 0)
    def _():
        bucket = bucket_ref[...]
        for hh in range(HEADS_PER_GROUP):
            bias = jnp.zeros(bucket.shape, F32)
            for b in range(NUM_BUCKETS):
                bias = jnp.where(bucket == b, tab_ref[b, head0 + hh], bias)
            bias_ref[hh] = bias

    a = lax.broadcasted_iota(I32, (blk, blk), 0)
    cc = lax.broadcasted_iota(I32, (blk, blk), 1)
    has_prev = (m_idx % blocks_per_res) > 0
    ok_prev = ((blk + a - cc) <= w_steps) & has_prev
    ok_cur = ((a - cc) >= 0) & ((a - cc) <= w_steps)
    nt = (((1,), (1,)), ((), ()))
    lses = []
    for hh in range(HEADS_PER_GROUP):
        sl = slice(hh * HEAD_DIM, (hh + 1) * HEAD_DIM)
        q = q_ref[:, sl]
        s_p = lax.dot_general(q, kp_ref[:, sl], nt, preferred_element_type=F32)
        s_c = lax.dot_general(q, kc_ref[:, sl], nt, preferred_element_type=F32)
        s_p = jnp.where(ok_prev, s_p + bias_ref[hh, :, :blk], NEG_INF)
        s_c = jnp.where(ok_cur, s_c + bias_ref[hh, :, blk:], NEG_INF)
        mx = jnp.maximum(jnp.max(s_p, axis=-1, keepdims=True),
                         jnp.max(s_c, axis=-1, keepdims=True))
        p_p = jnp.exp(s_p - mx)
        p_c = jnp.exp(s_c - mx)
        den = jnp.sum(p_p, axis=-1, keepdims=True) + jnp.sum(p_c, axis=-1, keepdims=True)
        acc = (jnp.dot(p_p.astype(BF16), vp_ref[:, sl], preferred_element_type=F32)
               + jnp.dot(p_c.astype(BF16), vc_ref[:, sl], preferred_element_type=F32))
        o_ref[:, sl] = acc / den
        lses.append(mx + jnp.log(den))
    lse_ref[...] = jnp.concatenate(lses, axis=-1)


def _attn_group(proj, rel_bias, gi, window, dilation, qcol, kcol, vcol):
    t = proj.shape[0]
    blk = ATTN_BLOCK
    w_steps = window // dilation
    blocks_per_res = t // dilation // blk
    nblk = t // blk
    a = np.arange(blk)[:, None]
    cc = np.arange(2 * blk)[None, :]
    bucket = _t5_bucket(np.maximum(blk + a - cc, 0) * dilation)

    def prev_map(m):
        return jnp.where(m % blocks_per_res > 0, m - 1, m)

    kern = functools.partial(_attn_kernel, gi * HEADS_PER_GROUP, w_steps, blocks_per_res)
    width = A_GROUP_WIDTH
    return pl.pallas_call(
        kern,
        grid=(nblk,),
        in_specs=[
            pl.BlockSpec(memory_space=pltpu.SMEM),
            pl.BlockSpec((blk, 2 * blk), lambda m: (0, 0)),
            pl.BlockSpec((blk, width), lambda m: (m, qcol)),
            pl.BlockSpec((blk, width), lambda m: (prev_map(m), kcol)),
            pl.BlockSpec((blk, width), lambda m: (m, kcol)),
            pl.BlockSpec((blk, width), lambda m: (prev_map(m), vcol)),
            pl.BlockSpec((blk, width), lambda m: (m, vcol)),
        ],
        out_specs=[pl.BlockSpec((blk, width), lambda m: (m, 0)),
                   pl.BlockSpec((blk, HEADS_PER_GROUP), lambda m: (m, 0))],
        out_shape=[jax.ShapeDtypeStruct((t, width), F32),
                   jax.ShapeDtypeStruct((t, HEADS_PER_GROUP), F32)],
        scratch_shapes=[pltpu.VMEM((HEADS_PER_GROUP, blk, 2 * blk), F32)],
        compiler_params=_params(("arbitrary",)),
        name=f"attn_d{dilation}",
    )(rel_bias, jnp.asarray(bucket), proj, proj, proj, proj, proj)


def _retention_kernel(q_ref, k_ref, v0_ref, v1_ref, g0_ref, g1_ref, dmat_ref, zeta_ref, xi_ref,
                      gch_ref, gn_ref, o_ref, state_ref):
    @pl.when(pl.program_id(0) == 0)
    def _():
        state_ref[...] = jnp.zeros_like(state_ref)

    nt = (((1,), (1,)), ((), ()))
    tn = (((0,), (0,)), ((), ()))
    per_half = RET_HEADS // 2
    for hh in range(RET_HEADS):
        qs = slice(hh * RET_QK_DIM, (hh + 1) * RET_QK_DIM)
        vs = slice(hh * RET_V_DIM, (hh + 1) * RET_V_DIM)
        hs = slice((hh % per_half) * RET_V_DIM, (hh % per_half + 1) * RET_V_DIM)
        v_ref, g_ref = (v0_ref, g0_ref) if hh < per_half else (v1_ref, g1_ref)
        q = q_ref[:, qs]
        k = k_ref[:, qs]
        v = v_ref[:, hs]
        state = state_ref[hh]
        s = lax.dot_general(q, k, nt, preferred_element_type=F32) * dmat_ref[hh]
        inner = jnp.dot(s.astype(BF16), v, preferred_element_type=F32)
        cross = jnp.dot(q, state.astype(BF16), preferred_element_type=F32) * xi_ref[hh]
        vz = (v.astype(F32) * zeta_ref[hh]).astype(BF16)
        upd = lax.dot_general(k, vz, tn, preferred_element_type=F32)
        state_ref[hh] = gch_ref[hh] * state + upd
        ret = inner + cross
        mu = jnp.mean(ret, axis=-1, keepdims=True)
        cen = ret - mu
        var = jnp.mean(cen * cen, axis=-1, keepdims=True)
        y = cen * lax.rsqrt(var + GN_EPS) * gn_ref[:, vs]
        o_ref[:, vs] = (y * g_ref[:, hs].astype(F32)).astype(o_ref.dtype)


def _retention_tables():
    c = RET_CHUNK
    hh = np.arange(RET_HEADS, dtype=np.float64)
    log_g = np.log1p(-np.exp2(-5.0 - hh))
    idx = np.arange(c, dtype=np.float64)
    diff = idx[:, None] - idx[None, :]
    dmat = np.where(diff >= 0, np.exp(log_g[:, None, None] * np.maximum(diff, 0.0)), 0.0)
    zeta = np.exp(log_g[:, None] * (c - 1 - idx))[:, :, None]
    xi = np.exp(log_g[:, None] * (idx + 1.0))[:, :, None]
    gch = np.exp(log_g * c)
    f = lambda v: jnp.asarray(v.astype(np.float32))
    return f(dmat), f(zeta), f(xi), f(gch)


def _retention(proj, gn_g, qcol, kcol, vcol, gcol):
    t = proj.shape[0]
    c = RET_CHUNK
    qw = RET_HEADS * RET_QK_DIM
    vw = RET_HEADS * RET_V_DIM
    dmat, zeta, xi, gch = _retention_tables()
    full3 = lambda shp: pl.BlockSpec(shp, lambda n: (0, 0, 0))
    return pl.pallas_call(
        _retention_kernel,
        grid=(t // c,),
        in_specs=[
            pl.BlockSpec((c, qw), lambda n: (n, qcol)),
            pl.BlockSpec((c, qw), lambda n: (n, kcol)),
            pl.BlockSpec((c, vw // 2), lambda n: (n, vcol)),
            pl.BlockSpec((c, vw // 2), lambda n: (n, vcol + 1)),
            pl.BlockSpec((c, vw // 2), lambda n: (n, gcol)),
            pl.BlockSpec((c, vw // 2), lambda n: (n, gcol + 1)),
            full3((RET_HEADS, c, c)),
            full3((RET_HEADS, c, 1)),
            full3((RET_HEADS, c, 1)),
            pl.BlockSpec(memory_space=pltpu.SMEM),
            pl.BlockSpec((1, vw), lambda n: (0, 0)),
        ],
        out_specs=pl.BlockSpec((c, vw), lambda n: (n, 0)),
        out_shape=jax.ShapeDtypeStruct((t, vw), BF16),
        scratch_shapes=[pltpu.VMEM((RET_HEADS, RET_QK_DIM, RET_V_DIM), F32)],
        compiler_params=_params(("arbitrary",)),
        name="retention",
    )(proj, proj, proj, proj, proj, proj, dmat, zeta, xi, gch, gn_g.reshape(1, vw))


def _merge_kernel(o1_ref, l1_ref, o2_ref, l2_ref, o3_ref, l3_ref, yb_ref, ga_ref, gb_ref,
                  pa_ref, pb_ref, out_ref, ya_ref):
    @pl.when(pl.program_id(1) == 0)
    def _():
        l1 = l1_ref[...]
        l2 = l2_ref[...]
        l3 = l3_ref[...]
        mx = jnp.maximum(jnp.maximum(l1, l2), l3)
        e1 = jnp.exp(l1 - mx)
        e2 = jnp.exp(l2 - mx)
        e3 = jnp.exp(l3 - mx)
        den = e1 + e2 + e3
        a1, a2, a3 = e1 / den, e2 / den, e3 / den
        for hh in range(HEADS_PER_GROUP):
            sl = slice(hh * HEAD_DIM, (hh + 1) * HEAD_DIM)
            ya = (a1[:, hh:hh + 1] * o1_ref[:, sl] + a2[:, hh:hh + 1] * o2_ref[:, sl]
                  + a3[:, hh:hh + 1] * o3_ref[:, sl])
            ya_ref[:, sl] = ya.astype(ya_ref.dtype)

    za = jnp.dot(ya_ref[...], pa_ref[...], preferred_element_type=F32)
    zb = jnp.dot(yb_ref[...], pb_ref[...], preferred_element_type=F32)
    out_ref[...] = (ga_ref[...].astype(F32) * za + gb_ref[...].astype(F32) * zb).astype(out_ref.dtype)


def _merge(o1, l1, o2, l2, o3, l3, yb, proj, ga_col, gb_col, pa, pb, tm=512, tn=1024):
    t = o1.shape[0]
    wa = o1.shape[1]
    wb = yb.shape[1]
    n = pa.shape[1]
    hg = HEADS_PER_GROUP
    ratio = tn // COLBLK
    o_spec = lambda: pl.BlockSpec((tm, wa), lambda i, j: (i, 0))
    l_spec = lambda: pl.BlockSpec((tm, hg), lambda i, j: (i, 0))
    return pl.pallas_call(
        _merge_kernel,
        grid=(t // tm, n // tn),
        in_specs=[
            o_spec(), l_spec(), o_spec(), l_spec(), o_spec(), l_spec(),
            pl.BlockSpec((tm, wb), lambda i, j: (i, 0)),
            pl.BlockSpec((tm, tn), lambda i, j: (i, ga_col // ratio + j)),
            pl.BlockSpec((tm, tn), lambda i, j: (i, gb_col // ratio + j)),
            pl.BlockSpec((wa, tn), lambda i, j: (0, j)),
            pl.BlockSpec((wb, tn), lambda i, j: (0, j)),
        ],
        out_specs=pl.BlockSpec((tm, tn), lambda i, j: (i, j)),
        out_shape=jax.ShapeDtypeStruct((t, n), BF16),
        scratch_shapes=[pltpu.VMEM((tm, wa), BF16)],
        compiler_params=_params(("parallel", "arbitrary")),
        name="merge",
    )(o1, l1, o2, l2, o3, l3, yb, proj, proj, pa, pb)


def _oproj_kernel(x_ref, m_ref, w_ref, g_ref, o_ref):
    z = jnp.dot(m_ref[...], w_ref[...], preferred_element_type=F32)
    o_ref[...] = x_ref[...] + g_ref[...] * z


def _oproj(x, merged, w_bf, mod, gate_blk, tm=512, tn=1024):
    t, d_model = x.shape
    k = merged.shape[1]
    per = d_model // tn
    return pl.pallas_call(
        _oproj_kernel,
        grid=(t // tm, d_model // tn),
        in_specs=[
            pl.BlockSpec((tm, tn), lambda i, j: (i, j)),
            pl.BlockSpec((tm, k), lambda i, j: (i, 0)),
            pl.BlockSpec((k, tn), lambda i, j: (0, j)),
            pl.BlockSpec((1, tn), lambda i, j: (0, gate_blk * per + j)),
        ],
        out_specs=pl.BlockSpec((tm, tn), lambda i, j: (i, j)),
        out_shape=jax.ShapeDtypeStruct((t, d_model), F32),
        compiler_params=_params(("parallel", "arbitrary")),
        name="oproj",
    )(x, merged, w_bf, mod)


def _pack_pair(lo, hi):
    lo_b = pltpu.bitcast(lo.astype(BF16).astype(F32), U32)
    hi_b = pltpu.bitcast(hi.astype(BF16).astype(F32), U32)
    return (lo_b >> 16) | (hi_b & jnp.uint32(0xFFFF0000))


def _store_rows(ref, words):
    for s in range(ref.shape[1]):
        ref[:, s, :] = words[:, s * LANE:(s + 1) * LANE]


def _load_rows(ref):
    return jnp.concatenate([ref[:, s, :] for s in range(ref.shape[1])], axis=1)


def _unpack_pair(w):
    lo = pltpu.bitcast(w << 16, F32)
    hi = pltpu.bitcast(w & jnp.uint32(0xFFFF0000), F32)
    return lo, hi


def _route_kernel(x_ref, g_ref, sc_ref, sh_ref, wt_ref, rb_ref, h_ref, hp_ref, idx_ref, rank_ref,
                  wgt_ref, cnt_ref):
    @pl.when(pl.program_id(0) == 0)
    def _():
        cnt_ref[...] = jnp.zeros_like(cnt_ref)

    x = x_ref[...]
    tm, d_model = x.shape
    inv = lax.rsqrt(jnp.mean(x * x, axis=-1, keepdims=True) + RMS_EPS)
    h = (x * inv * g_ref[...]) * (1.0 + sc_ref[...]) + sh_ref[...]
    h_ref[...] = h.astype(h_ref.dtype)
    half = d_model // 2
    _store_rows(hp_ref, _pack_pair(h[:, :half], h[:, half:]))

    ne = N_EXPERTS
    per = ne // N_GROUPS
    logits = lax.dot_general(wt_ref[...], h, (((1,), (1,)), ((), ())),
                             precision=lax.Precision.HIGHEST,
                             preferred_element_type=F32)
    scores = jax.nn.sigmoid(logits)
    sel = scores + rb_ref[...]
    eidx = lax.broadcasted_iota(I32, (ne, tm), 0).astype(F32)
    minus_inf = -jnp.inf

    sel3 = sel.reshape(N_GROUPS, per, tm)
    sub = lax.broadcasted_iota(I32, (N_GROUPS, per, tm), 1).astype(F32)
    m1 = jnp.max(sel3, axis=1, keepdims=True)
    first = jnp.min(jnp.where(sel3 == m1, sub, float(per)), axis=1, keepdims=True)
    m2 = jnp.max(jnp.where(sub == first, minus_inf, sel3), axis=1, keepdims=True)
    grp = (m1 + m2).reshape(N_GROUPS, tm)

    gidx = lax.broadcasted_iota(I32, (N_GROUPS, tm), 0).astype(F32)
    gmask = jnp.zeros((N_GROUPS, tm), F32)
    work = grp
    for _ in range(TOPK_GROUPS):
        mx = jnp.max(work, axis=0, keepdims=True)
        pick = jnp.min(jnp.where(work == mx, gidx, float(N_GROUPS)), axis=0, keepdims=True)
        hit = gidx == pick
        gmask = jnp.where(hit, 1.0, gmask)
        work = jnp.where(hit, minus_inf, work)
    emask = jnp.broadcast_to(gmask.reshape(N_GROUPS, 1, tm), (N_GROUPS, per, tm)).reshape(ne, tm)

    work = jnp.where(emask > 0.0, sel, minus_inf)
    onehot = jnp.zeros((ne, tm), F32)
    idx_rows, w_rows = [], []
    for _ in range(TOP_K):
        mx = jnp.max(work, axis=0, keepdims=True)
        pick = jnp.min(jnp.where(work == mx, eidx, float(ne)), axis=0, keepdims=True)
        hit = eidx == pick
        onehot = jnp.where(hit, 1.0, onehot)
        work = jnp.where(hit, minus_inf, work)
        idx_rows.append(pick)
        w_rows.append(jnp.sum(jnp.where(hit, scores, 0.0), axis=0, keepdims=True))
    w_all = jnp.concatenate(w_rows, axis=0)
    wgt_ref[...] = w_all / jnp.sum(w_all, axis=0, keepdims=True) * ROUTED_SCALE
    idx_ref[...] = jnp.concatenate(idx_rows, axis=0).astype(I32)

    ra = lax.broadcasted_iota(I32, (tm, tm), 0)
    rb = lax.broadcasted_iota(I32, (tm, tm), 1)
    tri = jnp.where(ra <= rb, 1.0, 0.0).astype(BF16)
    incl = jnp.dot(onehot.astype(BF16), tri, preferred_element_type=F32)
    before = incl - onehot + cnt_ref[...]
    rank_rows = [jnp.sum(jnp.where(eidx == idx_rows[kk], before, 0.0), axis=0, keepdims=True)
                 for kk in range(TOP_K)]
    rank_ref[...] = jnp.concatenate(rank_rows, axis=0).astype(I32)
    cnt_ref[...] = cnt_ref[...] + jnp.sum(onehot, axis=1, keepdims=True)


def _route(x1, g, mod, sc_blk, sh_blk, router_w, router_bias, tm=256):
    t, d_model = x1.shape
    ne = N_EXPERTS
    vec = lambda k: pl.BlockSpec((1, d_model), lambda i, k=k: (0, k))
    tok = lambda: pl.BlockSpec((TOP_K, tm), lambda i: (0, i))
    return pl.pallas_call(
        _route_kernel,
        grid=(t // tm,),
        in_specs=[pl.BlockSpec((tm, d_model), lambda i: (i, 0)),
                  pl.BlockSpec((1, d_model), lambda i: (0, 0)),
                  vec(sc_blk), vec(sh_blk),
                  pl.BlockSpec((ne, d_model), lambda i: (0, 0)),
                  pl.BlockSpec((ne, 1), lambda i: (0, 0))],
        out_specs=[pl.BlockSpec((tm, d_model), lambda i: (i, 0)),
                   pl.BlockSpec((tm, d_model // 2 // LANE, LANE), lambda i: (i, 0, 0)),
                   tok(), tok(), tok(),
                   pl.BlockSpec((ne, 1), lambda i: (0, 0))],
        out_shape=[jax.ShapeDtypeStruct((t, d_model), BF16),
                   jax.ShapeDtypeStruct((t, d_model // 2 // LANE, LANE), U32),
                   jax.ShapeDtypeStruct((TOP_K, t), I32),
                   jax.ShapeDtypeStruct((TOP_K, t), I32),
                   jax.ShapeDtypeStruct((TOP_K, t), F32),
                   jax.ShapeDtypeStruct((ne, 1), F32)],
        compiler_params=_params(("arbitrary",)),
        name="route",
    )(x1, g.reshape(1, d_model), mod, mod, router_w.T, router_bias.reshape(ne, 1))


INVERT_UNROLL = 8


def _invert_kernel(bm, n_tok, tm, pos_ref, fill_start_ref, fill_len_ref, nv_ref, inv_ref):
    tile = pl.program_id(0)

    @pl.when(tile == 0)
    def _():
        def mark_block(first_row):
            def mark(i, c):
                for u in range(INVERT_UNROLL):
                    inv_ref[first_row + i * INVERT_UNROLL + u] = -1
                return c

            lax.fori_loop(0, bm // INVERT_UNROLL, mark, 0)

        def mark_padding(e, carry):
            @pl.when(fill_len_ref[e] > 0)
            def _():
                mark_block(fill_start_ref[e] + fill_len_ref[e] - bm)

            return carry

        lax.fori_loop(0, N_EXPERTS, mark_padding, 0)

        def mark_unused(blk, c):
            mark_block(blk * bm)
            return c

        lax.fori_loop(nv_ref[0], inv_ref.shape[0] // bm, mark_unused, 0)

    def body(tt, carry):
        for kk in range(TOP_K):
            inv_ref[pos_ref[0, 0, kk * tm + tt]] = kk * n_tok + tile * tm + tt
        return carry

    lax.fori_loop(0, tm, body, 0, unroll=INVERT_UNROLL)


def _tile_major(a_t, tm):
    k, t = a_t.shape
    return a_t.reshape(k, t // tm, tm).transpose(1, 0, 2).reshape(t // tm, 1, k * tm)


def _invert(pos_t, fill_start, fill_len, n_valid, rows, bm, tm=1024):
    _, t = pos_t.shape
    smem = lambda: pl.BlockSpec(memory_space=pltpu.SMEM)
    return pl.pallas_call(
        functools.partial(_invert_kernel, bm, t, tm),
        grid=(t // tm,),
        in_specs=[pl.BlockSpec((1, 1, tm * TOP_K), lambda i: (i, 0, 0), memory_space=pltpu.SMEM),
                  smem(), smem(), smem()],
        out_specs=smem(),
        out_shape=jax.ShapeDtypeStruct((rows,), I32),
        compiler_params=_params(("arbitrary",)),
        name="invert",
    )(_tile_major(pos_t, tm), fill_start, fill_len, n_valid)


X_BUFFERS = 4
Y_BUFFERS = 2


def _experts_kernel(bm, n_tok, seg_ref, sege_ref, nv_ref, inv_ref, hp_hbm, wg_hbm, wu_hbm, wd_hbm,
                    ysl_ref, wg_f32, wu_f32, wd_f32, wg_bf, wu_bf, wd_bf,
                    x0_ref, x1_ref, x2_ref, x3_ref, y0_ref, y1_ref, sems, xsems, ysems):
    b = pl.program_id(0)
    nb = seg_ref.shape[0]
    n_valid = nv_ref[0]
    n_slot_rows = TOP_K * n_tok
    seg = seg_ref[jnp.minimum(b, nb - 1)]
    slot = seg % 2
    first = (b < nb) & ((b == 0) | (seg_ref[jnp.clip(b - 1, 0, nb - 1)] != seg))
    xbufs = (x0_ref, x1_ref, x2_ref, x3_ref)
    ybufs = (y0_ref, y1_ref)

    def weight_copies(which_seg, which_slot):
        e = sege_ref[which_seg]
        return [pltpu.make_async_copy(src.at[e], dst.at[which_slot], sems.at[which_slot])
                for src, dst in ((wg_hbm, wg_f32), (wu_hbm, wu_f32), (wd_hbm, wd_f32))]

    @pl.when(b == 0)
    def _():
        for cp in weight_copies(0, 0):
            cp.start()

    @pl.when(first)
    def _():
        for cp in weight_copies(seg, slot):
            cp.wait()
        wg_bf[...] = wg_f32[slot].astype(BF16)
        wu_bf[...] = wu_f32[slot].astype(BF16)
        wd_bf[...] = wd_f32[slot].astype(BF16)

        @pl.when(seg + 1 < nv_ref[1])
        def _():
            for cp in weight_copies(seg + 1, 1 - slot):
                cp.start()

    def compute(x_ref, y_ref):
        lo, hi = _unpack_pair(_load_rows(x_ref))
        half = lo.shape[1]
        lo = lo.astype(BF16)
        hi = hi.astype(BF16)
        gate = (jnp.dot(lo, wg_bf[:half, :], preferred_element_type=F32)
                + jnp.dot(hi, wg_bf[half:, :], preferred_element_type=F32))
        up = (jnp.dot(lo, wu_bf[:half, :], preferred_element_type=F32)
              + jnp.dot(hi, wu_bf[half:, :], preferred_element_type=F32))
        act = (_silu(gate) * up).astype(BF16)
        y = jnp.dot(act, wd_bf[...], preferred_element_type=F32)
        _store_rows(y_ref, _pack_pair(y[:, :half], y[:, half:]))

    def scatter(block, parity):
        base = block * bm
        spare = n_slot_rows + parity * bm
        for r in range(bm):
            d = inv_ref[base + r]
            d = jnp.where(d < 0, spare + r, d)
            pltpu.make_async_copy(ybufs[parity].at[pl.ds(r, 1)], ysl_ref.at[pl.ds(d, 1)],
                                  ysems.at[parity]).start()

    def gather(block, xslot):
        base = jnp.minimum(block, nb - 1) * bm
        for r in range(bm):
            d = inv_ref[base + r]
            tok = jnp.where(d < 0, 0, d & (n_tok - 1))
            pltpu.make_async_copy(hp_hbm.at[pl.ds(tok, 1)], xbufs[xslot].at[pl.ds(r, 1)],
                                  xsems.at[xslot]).start()

    def wait_rows(buf, sem):
        pltpu.make_async_copy(hp_hbm.at[pl.ds(0, bm)], buf, sem).wait()

    lookahead = X_BUFFERS // 2

    @pl.when(b == 0)
    def _():
        for blk in range(lookahead):
            gather(blk, blk)

    for c in range(X_BUFFERS):
        mine = (b % X_BUFFERS) == c
        p = c % Y_BUFFERS

        @pl.when(mine & (b < n_valid + lookahead))
        def _(c=c):
            wait_rows(xbufs[c], xsems.at[c])

        @pl.when(mine & (b >= Y_BUFFERS) & (b - Y_BUFFERS < n_valid))
        def _(p=p):
            pltpu.make_async_copy(ybufs[p], ysl_ref.at[pl.ds(0, bm)], ysems.at[p]).wait()

        @pl.when(mine & (b >= 1) & (b < n_valid))
        def _(c=c, p=p):
            gather(b + lookahead, (c + lookahead) % X_BUFFERS)
            scatter(b - 1, 1 - p)
            compute(xbufs[c], ybufs[p])

        @pl.when(mine & (b >= 1) & (b == n_valid))
        def _(p=p):
            scatter(b - 1, 1 - p)

    @pl.when(b == 0)
    def _():
        spare_fill = [pltpu.make_async_copy(
            xbufs[0], ysl_ref.at[pl.ds(n_slot_rows + parity * bm, bm)], ysems.at[parity])
            for parity in range(Y_BUFFERS)]
        for cp in spare_fill:
            cp.start()
        for cp in spare_fill:
            cp.wait()
        gather(lookahead, lookahead)
        compute(xbufs[0], ybufs[0])


def _experts(hp, inv, seg_of, seg_e, n_valid, wg, wu, wd, bm=EXPERT_ROWS):
    n_tok, chunks, lanes = hp.shape
    assert n_tok & (n_tok - 1) == 0
    _, d_model, de = wg.shape
    nb = seg_of.shape[0]
    n_slot_rows = TOP_K * n_tok
    hbm = lambda: pl.BlockSpec(memory_space=pl.ANY)
    grid_spec = pltpu.PrefetchScalarGridSpec(
        num_scalar_prefetch=4,
        grid=(nb + 2,),
        in_specs=[hbm(), hbm(), hbm(), hbm()],
        out_specs=hbm(),
        scratch_shapes=[pltpu.VMEM((2, d_model, de), F32),
                        pltpu.VMEM((2, d_model, de), F32),
                        pltpu.VMEM((2, de, d_model), F32),
                        pltpu.VMEM((d_model, de), BF16),
                        pltpu.VMEM((d_model, de), BF16),
                        pltpu.VMEM((de, d_model), BF16),
                        ]
                       + [pltpu.VMEM((bm, chunks, lanes), U32)] * (X_BUFFERS + Y_BUFFERS)
                       + [pltpu.SemaphoreType.DMA((2,)),
                          pltpu.SemaphoreType.DMA((X_BUFFERS,)),
                          pltpu.SemaphoreType.DMA((Y_BUFFERS,))],
    )
    return pl.pallas_call(
        functools.partial(_experts_kernel, bm, n_tok),
        grid_spec=grid_spec,
        out_shape=jax.ShapeDtypeStruct((n_slot_rows + Y_BUFFERS * bm, chunks, lanes), U32),
        compiler_params=_params(("arbitrary",)),
        name="experts",
    )(seg_of, seg_e, n_valid, inv, hp, wg, wu, wd)


def _combine_kernel(x_ref, h_ref, wt_ref, g_ref, sg_ref, su_ref, sd_ref, *rest):
    y_refs, o_ref = rest[:TOP_K], rest[TOP_K]
    h = h_ref[...]
    act = (_silu(jnp.dot(h, sg_ref[...], preferred_element_type=F32))
           * jnp.dot(h, su_ref[...], preferred_element_type=F32)).astype(BF16)
    shared = jnp.dot(act, sd_ref[...], preferred_element_type=F32)
    half = y_refs[0].shape[1] * y_refs[0].shape[2]
    wt = wt_ref[...]
    lo_acc = shared[:, :half]
    hi_acc = shared[:, half:]
    for kk in range(TOP_K):
        lo, hi = _unpack_pair(_load_rows(y_refs[kk]))
        wk = wt[:, kk:kk + 1]
        lo_acc = lo_acc + wk * lo
        hi_acc = hi_acc + wk * hi
    g = g_ref[...]
    o_ref[:, :half] = x_ref[:, :half] + g[:, :half] * lo_acc
    o_ref[:, half:] = x_ref[:, half:] + g[:, half:] * hi_acc


def _combine(x1, h2, wts, mod, gate_blk, sg, su, sd, ysl, tm=256):
    t, d_model = x1.shape
    ds_ = sg.shape[1]
    _, chunks, lanes = ysl.shape
    tiles = t // tm
    slot = lambda kk: pl.BlockSpec((tm, chunks, lanes), lambda i, kk=kk: (kk * tiles + i, 0, 0))
    return pl.pallas_call(
        _combine_kernel,
        grid=(tiles,),
        in_specs=[pl.BlockSpec((tm, d_model), lambda i: (i, 0)),
                  pl.BlockSpec((tm, d_model), lambda i: (i, 0)),
                  pl.BlockSpec((tm, TOP_K), lambda i: (i, 0)),
                  pl.BlockSpec((1, d_model), lambda i: (0, gate_blk)),
                  pl.BlockSpec((d_model, ds_), lambda i: (0, 0)),
                  pl.BlockSpec((d_model, ds_), lambda i: (0, 0)),
                  pl.BlockSpec((ds_, d_model), lambda i: (0, 0))]
                 + [slot(kk) for kk in range(TOP_K)],
        out_specs=pl.BlockSpec((tm, d_model), lambda i: (i, 0)),
        out_shape=jax.ShapeDtypeStruct((t, d_model), F32),
        compiler_params=_params(("parallel",)),
        name="combine",
    )(x1, h2, wts, mod, sg, su, sd, *([ysl] * TOP_K))


def _layout_kernel(bm, cnt_ref, idx_ref, rank_ref, pos_ref, seg_ref, sege_ref, nv_ref, fs_ref, fl_ref):
    shift = bm.bit_length() - 1
    pos_ref[...] = rank_ref[...]

    def per_expert(e, carry):
        start, blk, seg = carry
        cnt = cnt_ref[e]
        nblk = (cnt + (bm - 1)) >> shift
        pos_ref[...] = pos_ref[...] + jnp.where(idx_ref[...] == e, start, 0)

        def mark(b, c):
            seg_ref[blk + b] = seg
            return c

        lax.fori_loop(0, nblk, mark, 0)

        @pl.when(nblk > 0)
        def _():
            sege_ref[seg] = e

        fs_ref[e] = start + cnt
        fl_ref[e] = (nblk << shift) - cnt
        return start + (nblk << shift), blk + nblk, seg + jnp.where(nblk > 0, 1, 0)

    zero = jnp.int32(0)
    _, n_valid, n_seg = lax.fori_loop(0, N_EXPERTS, per_expert, (zero, zero, zero))
    nv_ref[0] = n_valid
    nv_ref[1] = n_seg

    def tail_blocks(b, c):
        seg_ref[b] = n_seg - 1
        return c

    lax.fori_loop(n_valid, seg_ref.shape[0], tail_blocks, 0)

    def tail_segs(s, c):
        sege_ref[s] = N_EXPERTS - 1
        return c

    lax.fori_loop(n_seg, N_EXPERTS, tail_segs, 0)


def _layout(counts, idx_t, rank_t, bm, n_blocks):
    assert bm & (bm - 1) == 0
    k, t = idx_t.shape
    smem = lambda: pl.BlockSpec(memory_space=pltpu.SMEM)
    full = lambda: pl.BlockSpec((k, t), lambda: (0, 0))
    return pl.pallas_call(
        functools.partial(_layout_kernel, bm),
        in_specs=[smem(), full(), full()],
        out_specs=[full(), smem(), smem(), smem(), smem(), smem()],
        out_shape=[jax.ShapeDtypeStruct((k, t), I32),
                   jax.ShapeDtypeStruct((n_blocks,), I32),
                   jax.ShapeDtypeStruct((N_EXPERTS,), I32),
                   jax.ShapeDtypeStruct((2,), I32),
                   jax.ShapeDtypeStruct((N_EXPERTS,), I32),
                   jax.ShapeDtypeStruct((N_EXPERTS,), I32)],
        name="layout",
    )(counts.reshape(-1).astype(I32), idx_t, rank_t)


def _layer(x, c, rel_bias, w_ada, b_ada, ln1_g, w_in, q_norm_g, k_norm_g, ret_gn_g, p_a, p_b, w_o,
           ln2_g, router_w, router_bias, w_gate_e, w_up_e, w_down_e, w_gate_s, w_up_s, w_down_s):
    t, d_model = x.shape
    dils = tuple(d for _, d in DILATED_GROUPS)

    mod = _ada(c.reshape(d_model), w_ada, b_ada)
    h = _norm1(x, ln1_g, mod)
    cos_tab, sin_tab = _rotary_tables(t)
    w_bf = w_in.astype(BF16)
    projs = []
    for order, (cols, epis) in enumerate(_inproj_plan(d_model)):
        h_in = h if order == 0 else _to_residue_major(h, dils[order])
        projs.append(_inproj(h_in, w_bf, cols, epis, q_norm_g, k_norm_g, cos_tab, sin_tab,
                             f"inproj_d{dils[order]}"))
    proj = projs[0]

    attn = [_attn_group(projs[gi], rel_bias, gi, win, dil, 0, 1, 2)
            for gi, (win, dil) in enumerate(DILATED_GROUPS)]
    base = 3
    rq = RET_HEADS * RET_QK_DIM // COLBLK
    vw_blk = RET_HEADS * RET_V_DIM // COLBLK
    qcol = base
    kcol = base + rq
    vcol_blk = base + 2 * rq
    gcol_blk = vcol_blk + vw_blk
    ga_blk = gcol_blk + vw_blk
    gb_blk = ga_blk + d_model // COLBLK
    y_b = _retention(proj, ret_gn_g, qcol, kcol, vcol_blk, gcol_blk)
    (o1, l1), (o2, l2), (o3, l3) = attn
    o2, l2 = _from_residue_major(o2, dils[1]), _from_residue_major(l2, dils[1])
    o3, l3 = _from_residue_major(o3, dils[2]), _from_residue_major(l3, dils[2])
    merged = _merge(o1, l1, o2, l2, o3, l3, y_b, proj, ga_blk, gb_blk,
                    p_a.astype(BF16), p_b.astype(BF16))
    x1 = _oproj(x, merged, w_o.astype(BF16), mod, 2)

    h2, h2p, idx_t, rank_t, wgt_t, counts = _route(x1, ln2_g, mod, 4, 3, router_w, router_bias)
    bm = EXPERT_ROWS
    n_blocks = (t * TOP_K + N_EXPERTS * (bm - 1) + bm - 1) // bm
    pos_t, seg_of, seg_e, n_valid, fill_start, fill_len = _layout(counts, idx_t, rank_t, bm, n_blocks)
    inv = _invert(pos_t, fill_start, fill_len, n_valid, n_blocks * bm, bm)
    ysl = _experts(h2p, inv, seg_of, seg_e, n_valid, w_gate_e, w_up_e, w_down_e)
    return _combine(x1, h2, wgt_t.T, mod, 5, w_gate_s.astype(BF16), w_up_s.astype(BF16),
                    w_down_s.astype(BF16), ysl)


def kernel(x, c, rel_bias, w_ada, b_ada, ln1_g, w_in, q_norm_g, k_norm_g, ret_gn_g, p_a, p_b, w_o,
           ln2_g, router_w, router_bias, w_gate_e, w_up_e, w_down_e, w_gate_s, w_up_s, w_down_s):
    b, s, d_model = x.shape
    depth = w_ada.shape[0]
    outs = []
    for bi in range(b):
        xb = x[bi]
        for l in range(depth):
            xb = _layer(xb, c[bi], rel_bias, w_ada[l], b_ada[l], ln1_g[l], w_in[l], q_norm_g[l],
                        k_norm_g[l], ret_gn_g[l], p_a[l], p_b[l], w_o[l], ln2_g[l], router_w[l],
                        router_bias[l], w_gate_e[l], w_up_e[l], w_down_e[l], w_gate_s[l],
                        w_up_s[l], w_down_s[l])
        outs.append(xb)
    return jnp.stack(outs, axis=0)
```

```python
import functools

import numpy as np
import jax
import jax.numpy as jnp
from jax import lax
from jax.experimental import pallas as pl
from jax.experimental.pallas import tpu as pltpu

F32 = jnp.float32
BF16 = jnp.bfloat16
U32 = jnp.uint32
I32 = jnp.int32

HEAD_DIM = 128
DILATED_GROUPS = ((128, 1), (512, 4), (2048, 16))
HEADS_PER_GROUP = 8
N_HEADS_A = HEADS_PER_GROUP * len(DILATED_GROUPS)
A_GROUP_WIDTH = HEADS_PER_GROUP * HEAD_DIM
ATTN_BLOCK = 128
NUM_BUCKETS = 32
MAX_DISTANCE = 2048
NEG_INF = -1e30
RET_HEADS = 8
RET_QK_DIM = 128
RET_V_DIM = 256
RET_CHUNK = 128
ROPE_BASE = 10000.0
GN_EPS = 1e-5
N_EXPERTS = 64
N_GROUPS = 8
TOPK_GROUPS = 4
TOP_K = 8
ROUTED_SCALE = 2.5
RMS_EPS = 1e-6

LANE = 128
COLBLK = 1024
VMEM_LIMIT = 56 * 1024 * 1024
EXPERT_ROWS = 256


def _params(sem, vmem=VMEM_LIMIT):
    return pltpu.CompilerParams(dimension_semantics=sem, vmem_limit_bytes=vmem)


def _sigmoid(v):
    return 0.5 * jnp.tanh(0.5 * v) + 0.5


def _silu(v):
    return v * _sigmoid(v)


def _ada_kernel(c_ref, w_ref, b_ref, o_ref):
    sc = _silu(c_ref[...])
    o_ref[...] = jnp.sum(w_ref[...] * sc, axis=0, keepdims=True) + b_ref[...]


def _ada(c, w, b, tn=512):
    d, n = w.shape
    return pl.pallas_call(
        _ada_kernel,
        grid=(n // tn,),
        in_specs=[pl.BlockSpec((d, 1), lambda j: (0, 0)),
                  pl.BlockSpec((d, tn), lambda j: (0, j)),
                  pl.BlockSpec((1, tn), lambda j: (0, j))],
        out_specs=pl.BlockSpec((1, tn), lambda j: (0, j)),
        out_shape=jax.ShapeDtypeStruct((1, n), F32),
        compiler_params=_params(("parallel",)),
        name="ada",
    )(c.reshape(d, 1), w, b.reshape(1, n))


def _norm1_kernel(x_ref, g_ref, sc_ref, sh_ref, o_ref):
    x = x_ref[...]
    inv = lax.rsqrt(jnp.mean(x * x, axis=-1, keepdims=True) + RMS_EPS)
    o_ref[...] = ((x * inv * g_ref[...]) * (1.0 + sc_ref[...]) + sh_ref[...]).astype(o_ref.dtype)


def _norm1(x, g, mod, tm=512):
    t, d_model = x.shape
    vec = lambda k: pl.BlockSpec((1, d_model), lambda i, k=k: (0, k))
    return pl.pallas_call(
        _norm1_kernel,
        grid=(t // tm,),
        in_specs=[pl.BlockSpec((tm, d_model), lambda i: (i, 0)),
                  pl.BlockSpec((1, d_model), lambda i: (0, 0)),
                  vec(1), vec(0)],
        out_specs=pl.BlockSpec((tm, d_model), lambda i: (i, 0)),
        out_shape=jax.ShapeDtypeStruct((t, d_model), BF16),
        compiler_params=_params(("parallel",)),
        name="norm1",
    )(x, g.reshape(1, d_model), mod, mod)


def _to_residue_major(a, d):
    t, w = a.shape
    return a.reshape(t // d, d, w).transpose(1, 0, 2).reshape(t, w)


def _from_residue_major(a, d):
    t, w = a.shape
    return a.reshape(d, t // d, w).transpose(1, 0, 2).reshape(t, w)


EPI_QNORM, EPI_KNORM, EPI_PLAIN, EPI_ROT_Q, EPI_ROT_K, EPI_SILU, EPI_SIGMOID = range(7)
INPROJ_ROW_CHUNK = 256


def _inproj_kernel(epis_present, colblk_ref, epi_ref, h_ref, w_ref, qg_ref, kg_ref, cos_ref, sin_ref,
                   o_ref, wbf_ref):
    del colblk_ref
    epi = epi_ref[pl.program_id(0)]
    tm = h_ref.shape[0]
    nh = o_ref.shape[1] // HEAD_DIM

    @pl.when(pl.program_id(1) == 0)
    def _():
        wbf_ref[...] = w_ref[...].astype(BF16)

    def head_norm(gain, scale):
        def fn(acc, rows):
            for hh in range(nh):
                sl = slice(hh * HEAD_DIM, (hh + 1) * HEAD_DIM)
                a = acc[:, sl]
                inv = lax.rsqrt(jnp.mean(a * a, axis=-1, keepdims=True) + RMS_EPS)
                o_ref[rows, sl] = ((a * inv * gain) * scale).astype(o_ref.dtype)
        return fn

    def rotary(scale):
        def fn(acc, rows):
            cos = cos_ref[rows, :]
            sin = sin_ref[rows, :]
            for hh in range(nh):
                sl = slice(hh * HEAD_DIM, (hh + 1) * HEAD_DIM)
                a = acc[:, sl]
                rot = pltpu.roll(a, HEAD_DIM // 2, 1)
                o_ref[rows, sl] = ((a * cos + rot * sin) * scale).astype(o_ref.dtype)
        return fn

    def elementwise(f):
        def fn(acc, rows):
            o_ref[rows, :] = f(acc).astype(o_ref.dtype)
        return fn

    epilogues = {
        EPI_QNORM: lambda: head_norm(qg_ref[...], HEAD_DIM ** -0.5),
        EPI_KNORM: lambda: head_norm(kg_ref[...], 1.0),
        EPI_PLAIN: lambda: elementwise(lambda a: a),
        EPI_ROT_Q: lambda: rotary(1.0),
        EPI_ROT_K: lambda: rotary(RET_QK_DIM ** -0.5),
        EPI_SILU: lambda: elementwise(_silu),
        EPI_SIGMOID: lambda: elementwise(_sigmoid),
    }
    for code in epis_present:
        @pl.when(epi == code)
        def _(code=code):
            fn = epilogues[code]()
            for c in range(tm // INPROJ_ROW_CHUNK):
                rows = slice(c * INPROJ_ROW_CHUNK, (c + 1) * INPROJ_ROW_CHUNK)
                acc = jnp.dot(h_ref[rows, :], wbf_ref[...], preferred_element_type=F32)
                fn(acc, rows)


def _inproj_plan(d_model):
    a_blocks = N_HEADS_A * HEAD_DIM // COLBLK
    groups = len(DILATED_GROUPS)
    per_group = a_blocks // groups
    rq = RET_HEADS * RET_QK_DIM // COLBLK
    rv = RET_HEADS * RET_V_DIM // COLBLK
    gd = d_model // COLBLK
    seg_epi = ([EPI_QNORM] * a_blocks + [EPI_KNORM] * a_blocks + [EPI_PLAIN] * a_blocks
               + [EPI_ROT_Q] * rq + [EPI_ROT_K] * rq + [EPI_PLAIN] * rv + [EPI_SILU] * rv
               + [EPI_SIGMOID] * (2 * gd))
    order_of = [0] * len(seg_epi)
    for seg in range(3):
        for blk in range(a_blocks):
            order_of[seg * a_blocks + blk] = blk // per_group
    plans = []
    for order in range(groups):
        cols = [cb for cb in range(len(seg_epi)) if order_of[cb] == order]
        plans.append((cols, [seg_epi[cb] for cb in cols]))
    return plans


def _inproj(h, w, cols, epis, qg, kg, cos_tab, sin_tab, name, tm=1024):
    t, d_model = h.shape
    row = lambda width: pl.BlockSpec((tm, width), lambda j, i, cb, ep: (i, 0))
    one = lambda width: pl.BlockSpec((1, width), lambda j, i, cb, ep: (0, 0))
    grid_spec = pltpu.PrefetchScalarGridSpec(
        num_scalar_prefetch=2,
        grid=(len(cols), t // tm),
        in_specs=[
            row(d_model),
            pl.BlockSpec((d_model, COLBLK), lambda j, i, cb, ep: (0, cb[j])),
            one(HEAD_DIM), one(HEAD_DIM), row(HEAD_DIM), row(HEAD_DIM),
        ],
        out_specs=pl.BlockSpec((tm, COLBLK), lambda j, i, cb, ep: (i, j)),
        scratch_shapes=[pltpu.VMEM((d_model, COLBLK), BF16)],
    )
    return pl.pallas_call(
        functools.partial(_inproj_kernel, tuple(sorted(set(epis)))),
        grid_spec=grid_spec,
        out_shape=jax.ShapeDtypeStruct((t, len(cols) * COLBLK), BF16),
        compiler_params=_params(("arbitrary", "arbitrary")),
        name=name,
    )(jnp.asarray(np.array(cols, np.int32)), jnp.asarray(np.array(epis, np.int32)),
      h, w, qg.reshape(1, HEAD_DIM), kg.reshape(1, HEAD_DIM), cos_tab, sin_tab)


def _rotary_tables(t):
    inv = ROPE_BASE ** (-np.arange(0, RET_QK_DIM, 2, dtype=np.float64) / RET_QK_DIM)
    ang = np.arange(t, dtype=np.float64)[:, None] * inv[None, :]
    cos, sin = np.cos(ang), np.sin(ang)
    cos_tab = np.concatenate([cos, cos], axis=1).astype(np.float32)
    sin_tab = np.concatenate([-sin, sin], axis=1).astype(np.float32)
    return jnp.asarray(cos_tab), jnp.asarray(sin_tab)


def _t5_bucket(dist):
    max_exact = NUM_BUCKETS // 2
    safe = np.maximum(dist, 1).astype(np.float32)
    large = max_exact + (np.log(safe / max_exact) / np.log(MAX_DISTANCE / max_exact)
                         * (NUM_BUCKETS - max_exact)).astype(np.int32)
    return np.where(dist < max_exact, dist, np.minimum(large, NUM_BUCKETS - 1)).astype(np.int32)


def _attn_kernel(head0, w_steps, blocks_per_res, tab_ref, bucket_ref, q_ref, kp_ref, kc_ref,
                 vp_ref, vc_ref, o_ref, lse_ref, bias_ref, band_ref, s_ref, p_ref):
    m_idx = pl.program_id(0)
    blk = ATTN_BLOCK

    @pl.when(m_idx == 0)
    def _():
        bucket = bucket_ref[...]
        for hh in range(HEADS_PER_GROUP):
            bias = jnp.zeros(bucket.shape, F32)
            for b in range(NUM_BUCKETS):
                bias = jnp.where(bucket == b, tab_ref[b, head0 + hh], bias)
            bias_ref[hh] = bias
        a = lax.broadcasted_iota(I32, (blk, 2 * blk), 0)
        cc = lax.broadcasted_iota(I32, (blk, 2 * blk), 1)
        delta = blk + a - cc
        band_ref[...] = jnp.where((delta >= 0) & (delta <= w_steps), 1.0, 0.0)

    prev_thr = jnp.where((m_idx % blocks_per_res) > 0, 0.5, 2.0)
    nt = (((1,), (1,)), ((), ()))
    heads = range(HEADS_PER_GROUP)
    head_cols = [slice(hh * HEAD_DIM, (hh + 1) * HEAD_DIM) for hh in heads]
    for hh, sl in zip(heads, head_cols):
        q = q_ref[:, sl]
        s_p = lax.dot_general(q, kp_ref[:, sl], nt, preferred_element_type=F32)
        s_c = lax.dot_general(q, kc_ref[:, sl], nt, preferred_element_type=F32)
        s_ref[hh, :, :blk] = jnp.where(band_ref[:, :blk] > prev_thr,
                                       s_p + bias_ref[hh, :, :blk], NEG_INF)
        s_ref[hh, :, blk:] = jnp.where(band_ref[:, blk:] > 0.5,
                                       s_c + bias_ref[hh, :, blk:], NEG_INF)
    dens, lses = [], []
    for hh in heads:
        s = s_ref[hh]
        mx = jnp.max(s, axis=-1, keepdims=True)
        p = jnp.exp(s - mx)
        den = jnp.sum(p, axis=-1, keepdims=True)
        p_ref[hh] = p.astype(BF16)
        dens.append(den)
        lses.append(mx + jnp.log(den))
    for hh, sl in zip(heads, head_cols):
        v_both = jnp.concatenate([vp_ref[:, sl], vc_ref[:, sl]], axis=0)
        acc = jnp.dot(p_ref[hh], v_both, preferred_element_type=F32)
        o_ref[:, sl] = acc / dens[hh]
    lse_ref[...] = jnp.concatenate(lses, axis=-1)


def _attn_group(proj, rel_bias, gi, window, dilation, qcol, kcol, vcol):
    t = proj.shape[0]
    blk = ATTN_BLOCK
    w_steps = window // dilation
    blocks_per_res = t // dilation // blk
    nblk = t // blk
    a = np.arange(blk)[:, None]
    cc = np.arange(2 * blk)[None, :]
    bucket = _t5_bucket(np.maximum(blk + a - cc, 0) * dilation)

    def prev_map(m):
        return jnp.where(m % blocks_per_res > 0, m - 1, m)

    kern = functools.partial(_attn_kernel, gi * HEADS_PER_GROUP, w_steps, blocks_per_res)
    width = A_GROUP_WIDTH
    return pl.pallas_call(
        kern,
        grid=(nblk,),
        in_specs=[
            pl.BlockSpec(memory_space=pltpu.SMEM),
            pl.BlockSpec((blk, 2 * blk), lambda m: (0, 0)),
            pl.BlockSpec((blk, width), lambda m: (m, qcol)),
            pl.BlockSpec((blk, width), lambda m: (prev_map(m), kcol)),
            pl.BlockSpec((blk, width), lambda m: (m, kcol)),
            pl.BlockSpec((blk, width), lambda m: (prev_map(m), vcol)),
            pl.BlockSpec((blk, width), lambda m: (m, vcol)),
        ],
        out_specs=[pl.BlockSpec((blk, width), lambda m: (m, 0)),
                   pl.BlockSpec((blk, HEADS_PER_GROUP), lambda m: (m, 0))],
        out_shape=[jax.ShapeDtypeStruct((t, width), F32),
                   jax.ShapeDtypeStruct((t, HEADS_PER_GROUP), F32)],
        scratch_shapes=[pltpu.VMEM((HEADS_PER_GROUP, blk, 2 * blk), F32),
                        pltpu.VMEM((blk, 2 * blk), F32),
                        pltpu.VMEM((HEADS_PER_GROUP, blk, 2 * blk), F32),
                        pltpu.VMEM((HEADS_PER_GROUP, blk, 2 * blk), BF16)],
        compiler_params=_params(("arbitrary",)),
        name=f"attn_d{dilation}",
    )(rel_bias, jnp.asarray(bucket), proj, proj, proj, proj, proj)


def _retention_kernel(q_ref, k_ref, v0_ref, v1_ref, g0_ref, g1_ref, dmat_ref, zeta_ref, xi_ref,
                      gch_ref, gn_ref, o_ref, state_ref):
    @pl.when(pl.program_id(0) == 0)
    def _():
        state_ref[...] = jnp.zeros_like(state_ref)

    nt = (((1,), (1,)), ((), ()))
    tn = (((0,), (0,)), ((), ()))
    per_half = RET_HEADS // 2
    for hh in range(RET_HEADS):
        qs = slice(hh * RET_QK_DIM, (hh + 1) * RET_QK_DIM)
        vs = slice(hh * RET_V_DIM, (hh + 1) * RET_V_DIM)
        hs = slice((hh % per_half) * RET_V_DIM, (hh % per_half + 1) * RET_V_DIM)
        v_ref, g_ref = (v0_ref, g0_ref) if hh < per_half else (v1_ref, g1_ref)
        q = q_ref[:, qs]
        k = k_ref[:, qs]
        v = v_ref[:, hs]
        state = state_ref[hh]
        s = lax.dot_general(q, k, nt, preferred_element_type=F32) * dmat_ref[hh]
        inner = jnp.dot(s.astype(BF16), v, preferred_element_type=F32)
        cross = jnp.dot(q, state.astype(BF16), preferred_element_type=F32) * xi_ref[hh]
        vz = (v.astype(F32) * zeta_ref[hh]).astype(BF16)
        upd = lax.dot_general(k, vz, tn, preferred_element_type=F32)
        state_ref[hh] = gch_ref[hh] * state + upd
        ret = inner + cross
        mu = jnp.mean(ret, axis=-1, keepdims=True)
        cen = ret - mu
        var = jnp.mean(cen * cen, axis=-1, keepdims=True)
        y = cen * lax.rsqrt(var + GN_EPS) * gn_ref[:, vs]
        o_ref[:, vs] = (y * g_ref[:, hs].astype(F32)).astype(o_ref.dtype)


def _retention_tables():
    c = RET_CHUNK
    hh = np.arange(RET_HEADS, dtype=np.float64)
    log_g = np.log1p(-np.exp2(-5.0 - hh))
    idx = np.arange(c, dtype=np.float64)
    diff = idx[:, None] - idx[None, :]
    dmat = np.where(diff >= 0, np.exp(log_g[:, None, None] * np.maximum(diff, 0.0)), 0.0)
    zeta = np.exp(log_g[:, None] * (c - 1 - idx))[:, :, None]
    xi = np.exp(log_g[:, None] * (idx + 1.0))[:, :, None]
    gch = np.exp(log_g * c)
    f = lambda v: jnp.asarray(v.astype(np.float32))
    return f(dmat), f(zeta), f(xi), f(gch)


def _retention(proj, gn_g, qcol, kcol, vcol, gcol):
    t = proj.shape[0]
    c = RET_CHUNK
    qw = RET_HEADS * RET_QK_DIM
    vw = RET_HEADS * RET_V_DIM
    dmat, zeta, xi, gch = _retention_tables()
    full3 = lambda shp: pl.BlockSpec(shp, lambda n: (0, 0, 0))
    return pl.pallas_call(
        _retention_kernel,
        grid=(t // c,),
        in_specs=[
            pl.BlockSpec((c, qw), lambda n: (n, qcol)),
            pl.BlockSpec((c, qw), lambda n: (n, kcol)),
            pl.BlockSpec((c, vw // 2), lambda n: (n, vcol)),
            pl.BlockSpec((c, vw // 2), lambda n: (n, vcol + 1)),
            pl.BlockSpec((c, vw // 2), lambda n: (n, gcol)),
            pl.BlockSpec((c, vw // 2), lambda n: (n, gcol + 1)),
            full3((RET_HEADS, c, c)),
            full3((RET_HEADS, c, 1)),
            full3((RET_HEADS, c, 1)),
            pl.BlockSpec(memory_space=pltpu.SMEM),
            pl.BlockSpec((1, vw), lambda n: (0, 0)),
        ],
        out_specs=pl.BlockSpec((c, vw), lambda n: (n, 0)),
        out_shape=jax.ShapeDtypeStruct((t, vw), BF16),
        scratch_shapes=[pltpu.VMEM((RET_HEADS, RET_QK_DIM, RET_V_DIM), F32)],
        compiler_params=_params(("arbitrary",)),
        name="retention",
    )(proj, proj, proj, proj, proj, proj, dmat, zeta, xi, gch, gn_g.reshape(1, vw))


def _merge_kernel(o1_ref, l1_ref, o2_ref, l2_ref, o3_ref, l3_ref, yb_ref, ga_ref, gb_ref,
                  pa_ref, pb_ref, out_ref, ya_ref):
    @pl.when(pl.program_id(1) == 0)
    def _():
        l1 = l1_ref[...]
        l2 = l2_ref[...]
        l3 = l3_ref[...]
        mx = jnp.maximum(jnp.maximum(l1, l2), l3)
        e1 = jnp.exp(l1 - mx)
        e2 = jnp.exp(l2 - mx)
        e3 = jnp.exp(l3 - mx)
        den = e1 + e2 + e3
        a1, a2, a3 = e1 / den, e2 / den, e3 / den
        for hh in range(HEADS_PER_GROUP):
            sl = slice(hh * HEAD_DIM, (hh + 1) * HEAD_DIM)
            ya = (a1[:, hh:hh + 1] * o1_ref[:, sl] + a2[:, hh:hh + 1] * o2_ref[:, sl]
                  + a3[:, hh:hh + 1] * o3_ref[:, sl])
            ya_ref[:, sl] = ya.astype(ya_ref.dtype)

    za = jnp.dot(ya_ref[...], pa_ref[...], preferred_element_type=F32)
    zb = jnp.dot(yb_ref[...], pb_ref[...], preferred_element_type=F32)
    out_ref[...] = (ga_ref[...].astype(F32) * za + gb_ref[...].astype(F32) * zb).astype(out_ref.dtype)


def _merge(o1, l1, o2, l2, o3, l3, yb, proj, ga_col, gb_col, pa, pb, tm=512, tn=1024):
    t = o1.shape[0]
    wa = o1.shape[1]
    wb = yb.shape[1]
    n = pa.shape[1]
    hg = HEADS_PER_GROUP
    ratio = tn // COLBLK
    o_spec = lambda: pl.BlockSpec((tm, wa), lambda i, j: (i, 0))
    l_spec = lambda: pl.BlockSpec((tm, hg), lambda i, j: (i, 0))
    return pl.pallas_call(
        _merge_kernel,
        grid=(t // tm, n // tn),
        in_specs=[
            o_spec(), l_spec(), o_spec(), l_spec(), o_spec(), l_spec(),
            pl.BlockSpec((tm, wb), lambda i, j: (i, 0)),
            pl.BlockSpec((tm, tn), lambda i, j: (i, ga_col // ratio + j)),
            pl.BlockSpec((tm, tn), lambda i, j: (i, gb_col // ratio + j)),
            pl.BlockSpec((wa, tn), lambda i, j: (0, j)),
            pl.BlockSpec((wb, tn), lambda i, j: (0, j)),
        ],
        out_specs=pl.BlockSpec((tm, tn), lambda i, j: (i, j)),
        out_shape=jax.ShapeDtypeStruct((t, n), BF16),
        scratch_shapes=[pltpu.VMEM((tm, wa), BF16)],
        compiler_params=_params(("parallel", "arbitrary")),
        name="merge",
    )(o1, l1, o2, l2, o3, l3, yb, proj, proj, pa, pb)


def _oproj_kernel(x_ref, m_ref, w_ref, g_ref, o_ref):
    z = jnp.dot(m_ref[...], w_ref[...], preferred_element_type=F32)
    o_ref[...] = x_ref[...] + g_ref[...] * z


def _oproj(x, merged, w_bf, mod, gate_blk, tm=512, tn=1024):
    t, d_model = x.shape
    k = merged.shape[1]
    per = d_model // tn
    return pl.pallas_call(
        _oproj_kernel,
        grid=(t // tm, d_model // tn),
        in_specs=[
            pl.BlockSpec((tm, tn), lambda i, j: (i, j)),
            pl.BlockSpec((tm, k), lambda i, j: (i, 0)),
            pl.BlockSpec((k, tn), lambda i, j: (0, j)),
            pl.BlockSpec((1, tn), lambda i, j: (0, gate_blk * per + j)),
        ],
        out_specs=pl.BlockSpec((tm, tn), lambda i, j: (i, j)),
        out_shape=jax.ShapeDtypeStruct((t, d_model), F32),
        compiler_params=_params(("parallel", "arbitrary")),
        name="oproj",
    )(x, merged, w_bf, mod)


def _pack_pair(lo, hi):
    lo_b = pltpu.bitcast(lo.astype(BF16).astype(F32), U32)
    hi_b = pltpu.bitcast(hi.astype(BF16).astype(F32), U32)
    return (lo_b >> 16) | (hi_b & jnp.uint32(0xFFFF0000))


def _unpack_pair(w):
    lo = pltpu.bitcast(w << 16, F32)
    hi = pltpu.bitcast(w & jnp.uint32(0xFFFF0000), F32)
    return lo, hi


def _route_kernel(x_ref, g_ref, sc_ref, sh_ref, wt_ref, rb_ref, h_ref, hp_ref, idx_ref, rank_ref,
                  wgt_ref, cnt_ref):
    @pl.when(pl.program_id(0) == 0)
    def _():
        cnt_ref[...] = jnp.zeros_like(cnt_ref)

    x = x_ref[...]
    tm, d_model = x.shape
    inv = lax.rsqrt(jnp.mean(x * x, axis=-1, keepdims=True) + RMS_EPS)
    h = (x * inv * g_ref[...]) * (1.0 + sc_ref[...]) + sh_ref[...]
    h_ref[...] = h.astype(h_ref.dtype)
    half = d_model // 2
    hp_ref[...] = _pack_pair(h[:, :half], h[:, half:])

    ne = N_EXPERTS
    per = ne // N_GROUPS
    logits = lax.dot_general(wt_ref[...], h, (((1,), (1,)), ((), ())),
                             precision=lax.Precision.HIGHEST,
                             preferred_element_type=F32)
    scores = jax.nn.sigmoid(logits)
    sel = scores + rb_ref[...]
    eidx = lax.broadcasted_iota(I32, (ne, tm), 0).astype(F32)
    minus_inf = -jnp.inf

    sel3 = sel.reshape(N_GROUPS, per, tm)
    sub = lax.broadcasted_iota(I32, (N_GROUPS, per, tm), 1).astype(F32)
    m1 = jnp.max(sel3, axis=1, keepdims=True)
    first = jnp.min(jnp.where(sel3 == m1, sub, float(per)), axis=1, keepdims=True)
    m2 = jnp.max(jnp.where(sub == first, minus_inf, sel3), axis=1, keepdims=True)
    grp = (m1 + m2).reshape(N_GROUPS, tm)

    gidx = lax.broadcasted_iota(I32, (N_GROUPS, tm), 0).astype(F32)
    gmask = jnp.zeros((N_GROUPS, tm), F32)
    work = grp
    for _ in range(TOPK_GROUPS):
        mx = jnp.max(work, axis=0, keepdims=True)
        pick = jnp.min(jnp.where(work == mx, gidx, float(N_GROUPS)), axis=0, keepdims=True)
        hit = gidx == pick
        gmask = jnp.where(hit, 1.0, gmask)
        work = jnp.where(hit, minus_inf, work)
    emask = jnp.broadcast_to(gmask.reshape(N_GROUPS, 1, tm), (N_GROUPS, per, tm)).reshape(ne, tm)

    work = jnp.where(emask > 0.0, sel, minus_inf)
    onehot = jnp.zeros((ne, tm), F32)
    idx_rows, w_rows = [], []
    for _ in range(TOP_K):
        mx = jnp.max(work, axis=0, keepdims=True)
        pick = jnp.min(jnp.where(work == mx, eidx, float(ne)), axis=0, keepdims=True)
        hit = eidx == pick
        onehot = jnp.where(hit, 1.0, onehot)
        work = jnp.where(hit, minus_inf, work)
        idx_rows.append(pick)
        w_rows.append(jnp.sum(jnp.where(hit, scores, 0.0), axis=0, keepdims=True))
    w_all = jnp.concatenate(w_rows, axis=0)
    wgt_ref[...] = w_all / jnp.sum(w_all, axis=0, keepdims=True) * ROUTED_SCALE
    idx_ref[...] = jnp.concatenate(idx_rows, axis=0).astype(I32)

    ra = lax.broadcasted_iota(I32, (tm, tm), 0)
    rb = lax.broadcasted_iota(I32, (tm, tm), 1)
    tri = jnp.where(ra <= rb, 1.0, 0.0).astype(BF16)
    incl = jnp.dot(onehot.astype(BF16), tri, preferred_element_type=F32)
    before = incl - onehot + cnt_ref[...]
    rank_rows = [jnp.sum(jnp.where(eidx == idx_rows[kk], before, 0.0), axis=0, keepdims=True)
                 for kk in range(TOP_K)]
    rank_ref[...] = jnp.concatenate(rank_rows, axis=0).astype(I32)
    cnt_ref[...] = cnt_ref[...] + jnp.sum(onehot, axis=1, keepdims=True)


def _route(x1, g, mod, sc_blk, sh_blk, router_w, router_bias, tm=256):
    t, d_model = x1.shape
    ne = N_EXPERTS
    vec = lambda k: pl.BlockSpec((1, d_model), lambda i, k=k: (0, k))
    tok = lambda: pl.BlockSpec((TOP_K, tm), lambda i: (0, i))
    return pl.pallas_call(
        _route_kernel,
        grid=(t // tm,),
        in_specs=[pl.BlockSpec((tm, d_model), lambda i: (i, 0)),
                  pl.BlockSpec((1, d_model), lambda i: (0, 0)),
                  vec(sc_blk), vec(sh_blk),
                  pl.BlockSpec((ne, d_model), lambda i: (0, 0)),
                  pl.BlockSpec((ne, 1), lambda i: (0, 0))],
        out_specs=[pl.BlockSpec((tm, d_model), lambda i: (i, 0)),
                   pl.BlockSpec((tm, d_model // 2), lambda i: (i, 0)),
                   tok(), tok(), tok(),
                   pl.BlockSpec((ne, 1), lambda i: (0, 0))],
        out_shape=[jax.ShapeDtypeStruct((t, d_model), BF16),
                   jax.ShapeDtypeStruct((t, d_model // 2), U32),
                   jax.ShapeDtypeStruct((TOP_K, t), I32),
                   jax.ShapeDtypeStruct((TOP_K, t), I32),
                   jax.ShapeDtypeStruct((TOP_K, t), F32),
                   jax.ShapeDtypeStruct((ne, 1), F32)],
        compiler_params=_params(("arbitrary",)),
        name="route",
    )(x1, g.reshape(1, d_model), mod, mod, router_w.T, router_bias.reshape(ne, 1))


SUBLANES = 8


def _pad_chunks(bm):
    sizes, s = [], bm // 2
    while s >= SUBLANES:
        sizes.append(s)
        s //= 2
    return sizes


def _dispatch_kernel(bm, n_tok, pos_ref, fill_start_ref, fill_len_ref, nv_ref, hp_ref, xs_ref, inv_ref,
                     sem, pad_sem):
    tm = hp_ref.shape[0]
    tile = pl.program_id(0)

    @pl.when(pl.program_id(0) == 0)
    def _():
        def pad_copies(action):
            def per_expert(e, carry):
                start = fill_start_ref[e]
                n = fill_len_ref[e]
                head = (-start) & (SUBLANES - 1)
                for r in range(SUBLANES - 1):
                    @pl.when(r < head)
                    def _(r=r):
                        action(pltpu.make_async_copy(hp_ref.at[pl.ds(0, 1)],
                                                     xs_ref.at[pl.ds(start + r, 1)], pad_sem))

                start = start + head
                n = n - head
                for size in _pad_chunks(bm):
                    take = (n & size) != 0

                    @pl.when(take)
                    def _(start=start, size=size):
                        dst = pl.multiple_of(start, SUBLANES)
                        action(pltpu.make_async_copy(hp_ref.at[pl.ds(0, size)],
                                                     xs_ref.at[pl.ds(dst, size)], pad_sem))

                    start = start + jnp.where(take, size, 0)
                return carry

            lax.fori_loop(0, N_EXPERTS, per_expert, 0)

            def unused_block(b, carry):
                dst = pl.multiple_of(b * bm, bm)
                action(pltpu.make_async_copy(hp_ref.at[pl.ds(0, bm)],
                                             xs_ref.at[pl.ds(dst, bm)], pad_sem))
                return carry

            lax.fori_loop(nv_ref[0], xs_ref.shape[0] // bm, unused_block, 0)

        pad_copies(lambda cp: cp.start())

        def mark_padding(e, carry):
            def mark(r, c):
                inv_ref[fill_start_ref[e] + r] = -1
                return c

            lax.fori_loop(0, fill_len_ref[e], mark, 0)
            return carry

        lax.fori_loop(0, N_EXPERTS, mark_padding, 0)

        def mark_unused(p, c):
            inv_ref[p] = -1
            return c

        lax.fori_loop(nv_ref[0] * bm, inv_ref.shape[0], mark_unused, 0)
        pad_copies(lambda cp: cp.wait())

    def body(tt, carry):
        for kk in range(TOP_K):
            dst = pos_ref[0, 0, kk * tm + tt]
            inv_ref[dst] = kk * n_tok + tile * tm + tt
            pltpu.make_async_copy(hp_ref.at[pl.ds(tt, 1)], xs_ref.at[pl.ds(dst, 1)], sem).start()
        return carry

    lax.fori_loop(0, tm, body, 0)
    pltpu.make_async_copy(xs_ref.at[pl.ds(0, tm * TOP_K)], xs_ref.at[pl.ds(0, tm * TOP_K)], sem).wait()


def _tile_major(a_t, tm):
    k, t = a_t.shape
    return a_t.reshape(k, t // tm, tm).transpose(1, 0, 2).reshape(t // tm, 1, k * tm)


def _dispatch(hp, pos_t, fill_start, fill_len, n_valid, rows, bm, tm=256):
    t, width = hp.shape
    assert tm >= bm
    smem = lambda: pl.BlockSpec(memory_space=pltpu.SMEM)
    return pl.pallas_call(
        functools.partial(_dispatch_kernel, bm, t),
        grid=(t // tm,),
        in_specs=[pl.BlockSpec((1, 1, tm * TOP_K), lambda i: (i, 0, 0), memory_space=pltpu.SMEM),
                  smem(), smem(), smem(),
                  pl.BlockSpec((tm, width), lambda i: (i, 0))],
        out_specs=[pl.BlockSpec(memory_space=pl.ANY), smem()],
        out_shape=[jax.ShapeDtypeStruct((rows, width), U32),
                   jax.ShapeDtypeStruct((rows,), I32)],
        scratch_shapes=[pltpu.SemaphoreType.DMA(()), pltpu.SemaphoreType.DMA(())],
        compiler_params=_params(("arbitrary",)),
        name="dispatch",
    )(_tile_major(pos_t, tm), fill_start, fill_len, n_valid, hp)


def _experts_kernel(bm, n_slot_rows, seg_ref, sege_ref, nv_ref, inv_ref, x_ref, wg_hbm, wu_hbm, wd_hbm,
                    ysl_ref, wg_f32, wu_f32, wd_f32, wg_bf, wu_bf, wd_bf, y0_ref, y1_ref, sems, ysems):
    b = pl.program_id(0)
    nb = seg_ref.shape[0]
    n_valid = nv_ref[0]
    seg = seg_ref[jnp.minimum(b, nb - 1)]
    slot = seg % 2
    first = (b < nb) & ((b == 0) | (seg_ref[jnp.clip(b - 1, 0, nb - 1)] != seg))
    ybufs = (y0_ref, y1_ref)

    def weight_copies(which_seg, which_slot):
        e = sege_ref[which_seg]
        return [pltpu.make_async_copy(src.at[e], dst.at[which_slot], sems.at[which_slot])
                for src, dst in ((wg_hbm, wg_f32), (wu_hbm, wu_f32), (wd_hbm, wd_f32))]

    @pl.when(b == 0)
    def _():
        for cp in weight_copies(0, 0):
            cp.start()

    @pl.when(first)
    def _():
        for cp in weight_copies(seg, slot):
            cp.wait()
        wg_bf[...] = wg_f32[slot].astype(BF16)
        wu_bf[...] = wu_f32[slot].astype(BF16)
        wd_bf[...] = wd_f32[slot].astype(BF16)

        @pl.when(seg + 1 < nv_ref[1])
        def _():
            for cp in weight_copies(seg + 1, 1 - slot):
                cp.start()

    def compute(y_ref):
        lo, hi = _unpack_pair(x_ref[...])
        half = lo.shape[1]
        lo = lo.astype(BF16)
        hi = hi.astype(BF16)
        gate = (jnp.dot(lo, wg_bf[:half, :], preferred_element_type=F32)
                + jnp.dot(hi, wg_bf[half:, :], preferred_element_type=F32))
        up = (jnp.dot(lo, wu_bf[:half, :], preferred_element_type=F32)
              + jnp.dot(hi, wu_bf[half:, :], preferred_element_type=F32))
        act = (_silu(gate) * up).astype(BF16)
        y = jnp.dot(act, wd_bf[...], preferred_element_type=F32)
        y_ref[...] = _pack_pair(y[:, :half], y[:, half:])

    def scatter(block, parity):
        base = block * bm
        spare = n_slot_rows + parity * bm
        for r in range(bm):
            d = inv_ref[base + r]
            d = jnp.where(d < 0, spare + r, d)
            pltpu.make_async_copy(ybufs[parity].at[pl.ds(r, 1)], ysl_ref.at[pl.ds(d, 1)],
                                  ysems.at[parity]).start()

    for p in (0, 1):
        mine = (b % 2) == p

        @pl.when(mine & (b >= 2) & (b - 2 < n_valid))
        def _(p=p):
            pltpu.make_async_copy(ybufs[p], ysl_ref.at[pl.ds(0, bm)], ysems.at[p]).wait()

        @pl.when(mine & (b >= 1) & (b < n_valid))
        def _(p=p):
            scatter(b - 1, 1 - p)
            compute(ybufs[p])

        @pl.when(mine & (b >= 1) & (b == n_valid))
        def _(p=p):
            scatter(b - 1, 1 - p)

    @pl.when(b == 0)
    def _():
        spare_fill = [pltpu.make_async_copy(
            x_ref, ysl_ref.at[pl.ds(n_slot_rows + parity * bm, bm)], ysems.at[parity])
            for parity in (0, 1)]
        for cp in spare_fill:
            cp.start()
        for cp in spare_fill:
            cp.wait()
        compute(ybufs[0])


def _experts(xs, inv, seg_of, seg_e, n_valid, wg, wu, wd, n_tok, bm=EXPERT_ROWS):
    rows, width = xs.shape
    _, d_model, de = wg.shape
    nb = rows // bm
    n_slot_rows = TOP_K * n_tok
    row_map = lambda b, sg, se, nv, iv: (jnp.minimum(b, nv[0] - 1), 0)
    hbm = lambda: pl.BlockSpec(memory_space=pl.ANY)
    grid_spec = pltpu.PrefetchScalarGridSpec(
        num_scalar_prefetch=4,
        grid=(nb + 2,),
        in_specs=[pl.BlockSpec((bm, width), row_map), hbm(), hbm(), hbm()],
        out_specs=hbm(),
        scratch_shapes=[pltpu.VMEM((2, d_model, de), F32),
                        pltpu.VMEM((2, d_model, de), F32),
                        pltpu.VMEM((2, de, d_model), F32),
                        pltpu.VMEM((d_model, de), BF16),
                        pltpu.VMEM((d_model, de), BF16),
                        pltpu.VMEM((de, d_model), BF16),
                        pltpu.VMEM((bm, width), U32),
                        pltpu.VMEM((bm, width), U32),
                        pltpu.SemaphoreType.DMA((2,)),
                        pltpu.SemaphoreType.DMA((2,))],
    )
    return pl.pallas_call(
        functools.partial(_experts_kernel, bm, n_slot_rows),
        grid_spec=grid_spec,
        out_shape=jax.ShapeDtypeStruct((n_slot_rows + 2 * bm, width), U32),
        compiler_params=_params(("arbitrary",)),
        name="experts",
    )(seg_of, seg_e, n_valid, inv, xs, wg, wu, wd)


def _combine_kernel(x_ref, h_ref, wt_ref, g_ref, sg_ref, su_ref, sd_ref, *rest):
    y_refs, o_ref = rest[:TOP_K], rest[TOP_K]
    h = h_ref[...]
    act = (_silu(jnp.dot(h, sg_ref[...], preferred_element_type=F32))
           * jnp.dot(h, su_ref[...], preferred_element_type=F32)).astype(BF16)
    shared = jnp.dot(act, sd_ref[...], preferred_element_type=F32)
    half = y_refs[0].shape[1]
    wt = wt_ref[...]
    lo_acc = shared[:, :half]
    hi_acc = shared[:, half:]
    for kk in range(TOP_K):
        lo, hi = _unpack_pair(y_refs[kk][...])
        wk = wt[:, kk:kk + 1]
        lo_acc = lo_acc + wk * lo
        hi_acc = hi_acc + wk * hi
    g = g_ref[...]
    o_ref[:, :half] = x_ref[:, :half] + g[:, :half] * lo_acc
    o_ref[:, half:] = x_ref[:, half:] + g[:, half:] * hi_acc


def _combine(x1, h2, wts, mod, gate_blk, sg, su, sd, ysl, tm=256):
    t, d_model = x1.shape
    ds_ = sg.shape[1]
    width = ysl.shape[1]
    tiles = t // tm
    slot = lambda kk: pl.BlockSpec((tm, width), lambda i, kk=kk: (kk * tiles + i, 0))
    return pl.pallas_call(
        _combine_kernel,
        grid=(tiles,),
        in_specs=[pl.BlockSpec((tm, d_model), lambda i: (i, 0)),
                  pl.BlockSpec((tm, d_model), lambda i: (i, 0)),
                  pl.BlockSpec((tm, TOP_K), lambda i: (i, 0)),
                  pl.BlockSpec((1, d_model), lambda i: (0, gate_blk)),
                  pl.BlockSpec((d_model, ds_), lambda i: (0, 0)),
                  pl.BlockSpec((d_model, ds_), lambda i: (0, 0)),
                  pl.BlockSpec((ds_, d_model), lambda i: (0, 0))]
                 + [slot(kk) for kk in range(TOP_K)],
        out_specs=pl.BlockSpec((tm, d_model), lambda i: (i, 0)),
        out_shape=jax.ShapeDtypeStruct((t, d_model), F32),
        compiler_params=_params(("parallel",)),
        name="combine",
    )(x1, h2, wts, mod, sg, su, sd, *([ysl] * TOP_K))


def _layout_kernel(bm, cnt_ref, idx_ref, rank_ref, pos_ref, seg_ref, sege_ref, nv_ref, fs_ref, fl_ref):
    shift = bm.bit_length() - 1
    pos_ref[...] = rank_ref[...]

    def per_expert(e, carry):
        start, blk, seg = carry
        cnt = cnt_ref[e]
        nblk = (cnt + (bm - 1)) >> shift
        pos_ref[...] = pos_ref[...] + jnp.where(idx_ref[...] == e, start, 0)

        def mark(b, c):
            seg_ref[blk + b] = seg
            return c

        lax.fori_loop(0, nblk, mark, 0)

        @pl.when(nblk > 0)
        def _():
            sege_ref[seg] = e

        fs_ref[e] = start + cnt
        fl_ref[e] = (nblk << shift) - cnt
        return start + (nblk << shift), blk + nblk, seg + jnp.where(nblk > 0, 1, 0)

    zero = jnp.int32(0)
    _, n_valid, n_seg = lax.fori_loop(0, N_EXPERTS, per_expert, (zero, zero, zero))
    nv_ref[0] = n_valid
    nv_ref[1] = n_seg

    def tail_blocks(b, c):
        seg_ref[b] = n_seg - 1
        return c

    lax.fori_loop(n_valid, seg_ref.shape[0], tail_blocks, 0)

    def tail_segs(s, c):
        sege_ref[s] = N_EXPERTS - 1
        return c

    lax.fori_loop(n_seg, N_EXPERTS, tail_segs, 0)


def _layout(counts, idx_t, rank_t, bm, n_blocks):
    assert bm & (bm - 1) == 0
    k, t = idx_t.shape
    smem = lambda: pl.BlockSpec(memory_space=pltpu.SMEM)
    full = lambda: pl.BlockSpec((k, t), lambda: (0, 0))
    return pl.pallas_call(
        functools.partial(_layout_kernel, bm),
        in_specs=[smem(), full(), full()],
        out_specs=[full(), smem(), smem(), smem(), smem(), smem()],
        out_shape=[jax.ShapeDtypeStruct((k, t), I32),
                   jax.ShapeDtypeStruct((n_blocks,), I32),
                   jax.ShapeDtypeStruct((N_EXPERTS,), I32),
                   jax.ShapeDtypeStruct((2,), I32),
                   jax.ShapeDtypeStruct((N_EXPERTS,), I32),
                   jax.ShapeDtypeStruct((N_EXPERTS,), I32)],
        name="layout",
    )(counts.reshape(-1).astype(I32), idx_t, rank_t)


def _layer(x, c, rel_bias, w_ada, b_ada, ln1_g, w_in, q_norm_g, k_norm_g, ret_gn_g, p_a, p_b, w_o,
           ln2_g, router_w, router_bias, w_gate_e, w_up_e, w_down_e, w_gate_s, w_up_s, w_down_s):
    t, d_model = x.shape
    dils = tuple(d for _, d in DILATED_GROUPS)

    mod = _ada(c.reshape(d_model), w_ada, b_ada)
    h = _norm1(x, ln1_g, mod)
    cos_tab, sin_tab = _rotary_tables(t)
    projs = []
    for order, (cols, epis) in enumerate(_inproj_plan(d_model)):
        h_in = h if order == 0 else _to_residue_major(h, dils[order])
        projs.append(_inproj(h_in, w_in, cols, epis, q_norm_g, k_norm_g, cos_tab, sin_tab,
                             f"inproj_d{dils[order]}"))
    proj = projs[0]

    attn = [_attn_group(projs[gi], rel_bias, gi, win, dil, 0, 1, 2)
            for gi, (win, dil) in enumerate(DILATED_GROUPS)]
    base = 3
    rq = RET_HEADS * RET_QK_DIM // COLBLK
    vw_blk = RET_HEADS * RET_V_DIM // COLBLK
    qcol = base
    kcol = base + rq
    vcol_blk = base + 2 * rq
    gcol_blk = vcol_blk + vw_blk
    ga_blk = gcol_blk + vw_blk
    gb_blk = ga_blk + d_model // COLBLK
    y_b = _retention(proj, ret_gn_g, qcol, kcol, vcol_blk, gcol_blk)
    (o1, l1), (o2, l2), (o3, l3) = attn
    o2, l2 = _from_residue_major(o2, dils[1]), _from_residue_major(l2, dils[1])
    o3, l3 = _from_residue_major(o3, dils[2]), _from_residue_major(l3, dils[2])
    merged = _merge(o1, l1, o2, l2, o3, l3, y_b, proj, ga_blk, gb_blk,
                    p_a.astype(BF16), p_b.astype(BF16))
    x1 = _oproj(x, merged, w_o.astype(BF16), mod, 2)

    h2, h2p, idx_t, rank_t, wgt_t, counts = _route(x1, ln2_g, mod, 4, 3, router_w, router_bias)
    bm = EXPERT_ROWS
    n_blocks = (t * TOP_K + N_EXPERTS * (bm - 1) + bm - 1) // bm
    pos_t, seg_of, seg_e, n_valid, fill_start, fill_len = _layout(counts, idx_t, rank_t, bm, n_blocks)
    xs, inv = _dispatch(h2p, pos_t, fill_start, fill_len, n_valid, n_blocks * bm, bm)
    ysl = _experts(xs, inv, seg_of, seg_e, n_valid, w_gate_e, w_up_e, w_down_e, t)
    return _combine(x1, h2, wgt_t.T, mod, 5, w_gate_s.astype(BF16), w_up_s.astype(BF16),
                    w_down_s.astype(BF16), ysl)


def kernel(x, c, rel_bias, w_ada, b_ada, ln1_g, w_in, q_norm_g, k_norm_g, ret_gn_g, p_a, p_b, w_o,
           ln2_g, router_w, router_bias, w_gate_e, w_up_e, w_down_e, w_gate_s, w_up_s, w_down_s):
    b, s, d_model = x.shape
    depth = w_ada.shape[0]
    outs = []
    for bi in range(b):
        xb = x[bi]
        for l in range(depth):
            xb = _layer(xb, c[bi], rel_bias, w_ada[l], b_ada[l], ln1_g[l], w_in[l], q_norm_g[l],
                        k_norm_g[l], ret_gn_g[l], p_a[l], p_b[l], w_o[l], ln2_g[l], router_w[l],
                        router_bias[l], w_gate_e[l], w_up_e[l], w_down_e[l], w_gate_s[l],
                        w_up_s[l], w_down_s[l])
        outs.append(xb)
    return jnp.stack(outs, axis=0)
```

```python
import functools

import numpy as np
import jax
import jax.numpy as jnp
from jax import lax
from jax.experimental import pallas as pl
from jax.experimental.pallas import tpu as pltpu

F32 = jnp.float32
BF16 = jnp.bfloat16
U32 = jnp.uint32
I32 = jnp.int32

HEAD_DIM = 128
DILATED_GROUPS = ((128, 1), (512, 4), (2048, 16))
HEADS_PER_GROUP = 8
N_HEADS_A = HEADS_PER_GROUP * len(DILATED_GROUPS)
A_GROUP_WIDTH = HEADS_PER_GROUP * HEAD_DIM
ATTN_BLOCK = 128
NUM_BUCKETS = 32
MAX_DISTANCE = 2048
NEG_INF = -1e30
RET_HEADS = 8
RET_QK_DIM = 128
RET_V_DIM = 256
RET_CHUNK = 128
ROPE_BASE = 10000.0
GN_EPS = 1e-5
N_EXPERTS = 64
N_GROUPS = 8
TOPK_GROUPS = 4
TOP_K = 8
ROUTED_SCALE = 2.5
RMS_EPS = 1e-6

LANE = 128
COLBLK = 1024
VMEM_LIMIT = 56 * 1024 * 1024
EXPERT_ROWS = 256


def _params(sem, vmem=VMEM_LIMIT):
    return pltpu.CompilerParams(dimension_semantics=sem, vmem_limit_bytes=vmem)


def _sigmoid(v):
    return 0.5 * jnp.tanh(0.5 * v) + 0.5


def _silu(v):
    return v * _sigmoid(v)


def _ada_kernel(c_ref, w_ref, b_ref, o_ref):
    sc = _silu(c_ref[...])
    o_ref[...] = jnp.sum(w_ref[...] * sc, axis=0, keepdims=True) + b_ref[...]


def _ada(c, w, b, tn=512):
    d, n = w.shape
    return pl.pallas_call(
        _ada_kernel,
        grid=(n // tn,),
        in_specs=[pl.BlockSpec((d, 1), lambda j: (0, 0)),
                  pl.BlockSpec((d, tn), lambda j: (0, j)),
                  pl.BlockSpec((1, tn), lambda j: (0, j))],
        out_specs=pl.BlockSpec((1, tn), lambda j: (0, j)),
        out_shape=jax.ShapeDtypeStruct((1, n), F32),
        compiler_params=_params(("parallel",)),
        name="ada",
    )(c.reshape(d, 1), w, b.reshape(1, n))


def _norm1_kernel(x_ref, g_ref, sc_ref, sh_ref, o_ref):
    x = x_ref[...]
    inv = lax.rsqrt(jnp.mean(x * x, axis=-1, keepdims=True) + RMS_EPS)
    o_ref[...] = ((x * inv * g_ref[...]) * (1.0 + sc_ref[...]) + sh_ref[...]).astype(o_ref.dtype)


def _norm1(x, g, mod, tm=512):
    t, d_model = x.shape
    vec = lambda k: pl.BlockSpec((1, d_model), lambda i, k=k: (0, k))
    return pl.pallas_call(
        _norm1_kernel,
        grid=(t // tm,),
        in_specs=[pl.BlockSpec((tm, d_model), lambda i: (i, 0)),
                  pl.BlockSpec((1, d_model), lambda i: (0, 0)),
                  vec(1), vec(0)],
        out_specs=pl.BlockSpec((tm, d_model), lambda i: (i, 0)),
        out_shape=jax.ShapeDtypeStruct((t, d_model), BF16),
        compiler_params=_params(("parallel",)),
        name="norm1",
    )(x, g.reshape(1, d_model), mod, mod)


def _to_residue_major(a, d):
    t, w = a.shape
    return a.reshape(t // d, d, w).transpose(1, 0, 2).reshape(t, w)


def _from_residue_major(a, d):
    t, w = a.shape
    return a.reshape(d, t // d, w).transpose(1, 0, 2).reshape(t, w)


EPI_QNORM, EPI_KNORM, EPI_PLAIN, EPI_ROT_Q, EPI_ROT_K, EPI_SILU, EPI_SIGMOID = range(7)
INPROJ_ROW_CHUNK = 256


def _inproj_kernel(epis_present, colblk_ref, epi_ref, h_ref, w_ref, qg_ref, kg_ref, cos_ref, sin_ref,
                   o_ref, wbf_ref):
    del colblk_ref
    epi = epi_ref[pl.program_id(0)]
    tm = h_ref.shape[0]
    nh = o_ref.shape[1] // HEAD_DIM

    @pl.when(pl.program_id(1) == 0)
    def _():
        wbf_ref[...] = w_ref[...].astype(BF16)

    def head_norm(gain, scale):
        def fn(acc, rows):
            for hh in range(nh):
                sl = slice(hh * HEAD_DIM, (hh + 1) * HEAD_DIM)
                a = acc[:, sl]
                inv = lax.rsqrt(jnp.mean(a * a, axis=-1, keepdims=True) + RMS_EPS)
                o_ref[rows, sl] = ((a * inv * gain) * scale).astype(o_ref.dtype)
        return fn

    def rotary(scale):
        def fn(acc, rows):
            cos = cos_ref[rows, :]
            sin = sin_ref[rows, :]
            for hh in range(nh):
                sl = slice(hh * HEAD_DIM, (hh + 1) * HEAD_DIM)
                a = acc[:, sl]
                rot = pltpu.roll(a, HEAD_DIM // 2, 1)
                o_ref[rows, sl] = ((a * cos + rot * sin) * scale).astype(o_ref.dtype)
        return fn

    def elementwise(f):
        def fn(acc, rows):
            o_ref[rows, :] = f(acc).astype(o_ref.dtype)
        return fn

    epilogues = {
        EPI_QNORM: lambda: head_norm(qg_ref[...], HEAD_DIM ** -0.5),
        EPI_KNORM: lambda: head_norm(kg_ref[...], 1.0),
        EPI_PLAIN: lambda: elementwise(lambda a: a),
        EPI_ROT_Q: lambda: rotary(1.0),
        EPI_ROT_K: lambda: rotary(RET_QK_DIM ** -0.5),
        EPI_SILU: lambda: elementwise(_silu),
        EPI_SIGMOID: lambda: elementwise(_sigmoid),
    }
    for code in epis_present:
        @pl.when(epi == code)
        def _(code=code):
            fn = epilogues[code]()
            for c in range(tm // INPROJ_ROW_CHUNK):
                rows = slice(c * INPROJ_ROW_CHUNK, (c + 1) * INPROJ_ROW_CHUNK)
                acc = jnp.dot(h_ref[rows, :], wbf_ref[...], preferred_element_type=F32)
                fn(acc, rows)


def _inproj_plan(d_model):
    a_blocks = N_HEADS_A * HEAD_DIM // COLBLK
    groups = len(DILATED_GROUPS)
    per_group = a_blocks // groups
    rq = RET_HEADS * RET_QK_DIM // COLBLK
    rv = RET_HEADS * RET_V_DIM // COLBLK
    gd = d_model // COLBLK
    seg_epi = ([EPI_QNORM] * a_blocks + [EPI_KNORM] * a_blocks + [EPI_PLAIN] * a_blocks
               + [EPI_ROT_Q] * rq + [EPI_ROT_K] * rq + [EPI_PLAIN] * rv + [EPI_SILU] * rv
               + [EPI_SIGMOID] * (2 * gd))
    order_of = [0] * len(seg_epi)
    for seg in range(3):
        for blk in range(a_blocks):
            order_of[seg * a_blocks + blk] = blk // per_group
    plans = []
    for order in range(groups):
        cols = [cb for cb in range(len(seg_epi)) if order_of[cb] == order]
        plans.append((cols, [seg_epi[cb] for cb in cols]))
    return plans


def _inproj(h, w, cols, epis, qg, kg, cos_tab, sin_tab, name, tm=1024):
    t, d_model = h.shape
    row = lambda width: pl.BlockSpec((tm, width), lambda j, i, cb, ep: (i, 0))
    one = lambda width: pl.BlockSpec((1, width), lambda j, i, cb, ep: (0, 0))
    grid_spec = pltpu.PrefetchScalarGridSpec(
        num_scalar_prefetch=2,
        grid=(len(cols), t // tm),
        in_specs=[
            row(d_model),
            pl.BlockSpec((d_model, COLBLK), lambda j, i, cb, ep: (0, cb[j])),
            one(HEAD_DIM), one(HEAD_DIM), row(HEAD_DIM), row(HEAD_DIM),
        ],
        out_specs=pl.BlockSpec((tm, COLBLK), lambda j, i, cb, ep: (i, j)),
        scratch_shapes=[pltpu.VMEM((d_model, COLBLK), BF16)],
    )
    return pl.pallas_call(
        functools.partial(_inproj_kernel, tuple(sorted(set(epis)))),
        grid_spec=grid_spec,
        out_shape=jax.ShapeDtypeStruct((t, len(cols) * COLBLK), BF16),
        compiler_params=_params(("arbitrary", "arbitrary")),
        name=name,
    )(jnp.asarray(np.array(cols, np.int32)), jnp.asarray(np.array(epis, np.int32)),
      h, w, qg.reshape(1, HEAD_DIM), kg.reshape(1, HEAD_DIM), cos_tab, sin_tab)


def _rotary_tables(t):
    inv = ROPE_BASE ** (-np.arange(0, RET_QK_DIM, 2, dtype=np.float64) / RET_QK_DIM)
    ang = np.arange(t, dtype=np.float64)[:, None] * inv[None, :]
    cos, sin = np.cos(ang), np.sin(ang)
    cos_tab = np.concatenate([cos, cos], axis=1).astype(np.float32)
    sin_tab = np.concatenate([-sin, sin], axis=1).astype(np.float32)
    return jnp.asarray(cos_tab), jnp.asarray(sin_tab)


def _t5_bucket(dist):
    max_exact = NUM_BUCKETS // 2
    safe = np.maximum(dist, 1).astype(np.float32)
    large = max_exact + (np.log(safe / max_exact) / np.log(MAX_DISTANCE / max_exact)
                         * (NUM_BUCKETS - max_exact)).astype(np.int32)
    return np.where(dist < max_exact, dist, np.minimum(large, NUM_BUCKETS - 1)).astype(np.int32)


def _attn_kernel(head0, w_steps, blocks_per_res, tab_ref, bucket_ref, q_ref, kp_ref, kc_ref,
                 vp_ref, vc_ref, o_ref, lse_ref, bias_ref, band_ref, s_ref, p_ref):
    m_idx = pl.program_id(0)
    blk = ATTN_BLOCK

    @pl.when(m_idx == 0)
    def _():
        bucket = bucket_ref[...]
        for hh in range(HEADS_PER_GROUP):
            bias = jnp.zeros(bucket.shape, F32)
            for b in range(NUM_BUCKETS):
                bias = jnp.where(bucket == b, tab_ref[b, head0 + hh], bias)
            bias_ref[hh] = bias
        a = lax.broadcasted_iota(I32, (blk, 2 * blk), 0)
        cc = lax.broadcasted_iota(I32, (blk, 2 * blk), 1)
        delta = blk + a - cc
        band_ref[...] = jnp.where((delta >= 0) & (delta <= w_steps), 1.0, 0.0)

    prev_thr = jnp.where((m_idx % blocks_per_res) > 0, 0.5, 2.0)
    nt = (((1,), (1,)), ((), ()))
    heads = range(HEADS_PER_GROUP)
    head_cols = [slice(hh * HEAD_DIM, (hh + 1) * HEAD_DIM) for hh in heads]
    for hh, sl in zip(heads, head_cols):
        q = q_ref[:, sl]
        s_p = lax.dot_general(q, kp_ref[:, sl], nt, preferred_element_type=F32)
        s_c = lax.dot_general(q, kc_ref[:, sl], nt, preferred_element_type=F32)
        s_ref[hh, :, :blk] = jnp.where(band_ref[:, :blk] > prev_thr,
                                       s_p + bias_ref[hh, :, :blk], NEG_INF)
        s_ref[hh, :, blk:] = jnp.where(band_ref[:, blk:] > 0.5,
                                       s_c + bias_ref[hh, :, blk:], NEG_INF)
    dens, lses = [], []
    for hh in heads:
        s = s_ref[hh]
        mx = jnp.max(s, axis=-1, keepdims=True)
        p = jnp.exp(s - mx)
        den = jnp.sum(p, axis=-1, keepdims=True)
        p_ref[hh] = p.astype(BF16)
        dens.append(den)
        lses.append(mx + jnp.log(den))
    for hh, sl in zip(heads, head_cols):
        v_both = jnp.concatenate([vp_ref[:, sl], vc_ref[:, sl]], axis=0)
        acc = jnp.dot(p_ref[hh], v_both, preferred_element_type=F32)
        o_ref[:, sl] = acc / dens[hh]
    lse_ref[...] = jnp.concatenate(lses, axis=-1)


def _attn_group(proj, rel_bias, gi, window, dilation, qcol, kcol, vcol):
    t = proj.shape[0]
    blk = ATTN_BLOCK
    w_steps = window // dilation
    blocks_per_res = t // dilation // blk
    nblk = t // blk
    a = np.arange(blk)[:, None]
    cc = np.arange(2 * blk)[None, :]
    bucket = _t5_bucket(np.maximum(blk + a - cc, 0) * dilation)

    def prev_map(m):
        return jnp.where(m % blocks_per_res > 0, m - 1, m)

    kern = functools.partial(_attn_kernel, gi * HEADS_PER_GROUP, w_steps, blocks_per_res)
    width = A_GROUP_WIDTH
    return pl.pallas_call(
        kern,
        grid=(nblk,),
        in_specs=[
            pl.BlockSpec(memory_space=pltpu.SMEM),
            pl.BlockSpec((blk, 2 * blk), lambda m: (0, 0)),
            pl.BlockSpec((blk, width), lambda m: (m, qcol)),
            pl.BlockSpec((blk, width), lambda m: (prev_map(m), kcol)),
            pl.BlockSpec((blk, width), lambda m: (m, kcol)),
            pl.BlockSpec((blk, width), lambda m: (prev_map(m), vcol)),
            pl.BlockSpec((blk, width), lambda m: (m, vcol)),
        ],
        out_specs=[pl.BlockSpec((blk, width), lambda m: (m, 0)),
                   pl.BlockSpec((blk, HEADS_PER_GROUP), lambda m: (m, 0))],
        out_shape=[jax.ShapeDtypeStruct((t, width), F32),
                   jax.ShapeDtypeStruct((t, HEADS_PER_GROUP), F32)],
        scratch_shapes=[pltpu.VMEM((HEADS_PER_GROUP, blk, 2 * blk), F32),
                        pltpu.VMEM((blk, 2 * blk), F32),
                        pltpu.VMEM((HEADS_PER_GROUP, blk, 2 * blk), F32),
                        pltpu.VMEM((HEADS_PER_GROUP, blk, 2 * blk), BF16)],
        compiler_params=_params(("arbitrary",)),
        name=f"attn_d{dilation}",
    )(rel_bias, jnp.asarray(bucket), proj, proj, proj, proj, proj)


def _retention_kernel(q_ref, k_ref, v0_ref, v1_ref, g0_ref, g1_ref, dmat_ref, zeta_ref, xi_ref,
                      gch_ref, gn_ref, o_ref, state_ref, s_ref, cross_ref):
    @pl.when(pl.program_id(0) == 0)
    def _():
        state_ref[...] = jnp.zeros_like(state_ref)

    nt = (((1,), (1,)), ((), ()))
    tn = (((0,), (0,)), ((), ()))
    per_half = RET_HEADS // 2

    def head_refs(hh):
        qs = slice(hh * RET_QK_DIM, (hh + 1) * RET_QK_DIM)
        vs = slice(hh * RET_V_DIM, (hh + 1) * RET_V_DIM)
        hs = slice((hh % per_half) * RET_V_DIM, (hh % per_half + 1) * RET_V_DIM)
        v_ref, g_ref = (v0_ref, g0_ref) if hh < per_half else (v1_ref, g1_ref)
        return qs, vs, hs, v_ref, g_ref

    for hh in range(RET_HEADS):
        qs, _, hs, v_ref, _ = head_refs(hh)
        q = q_ref[:, qs]
        k = k_ref[:, qs]
        v = v_ref[:, hs]
        state = state_ref[hh]
        s = lax.dot_general(q, k, nt, preferred_element_type=F32) * dmat_ref[hh]
        s_ref[hh] = s.astype(BF16)
        cross_ref[hh] = jnp.dot(q, state.astype(BF16), preferred_element_type=F32) * xi_ref[hh]
        vz = (v.astype(F32) * zeta_ref[hh]).astype(BF16)
        upd = lax.dot_general(k, vz, tn, preferred_element_type=F32)
        state_ref[hh] = gch_ref[hh] * state + upd
    for hh in range(RET_HEADS):
        _, vs, hs, v_ref, g_ref = head_refs(hh)
        inner = jnp.dot(s_ref[hh], v_ref[:, hs], preferred_element_type=F32)
        ret = inner + cross_ref[hh]
        mu = jnp.mean(ret, axis=-1, keepdims=True)
        cen = ret - mu
        var = jnp.mean(cen * cen, axis=-1, keepdims=True)
        y = cen * lax.rsqrt(var + GN_EPS) * gn_ref[:, vs]
        o_ref[:, vs] = (y * g_ref[:, hs].astype(F32)).astype(o_ref.dtype)


def _retention_tables():
    c = RET_CHUNK
    hh = np.arange(RET_HEADS, dtype=np.float64)
    log_g = np.log1p(-np.exp2(-5.0 - hh))
    idx = np.arange(c, dtype=np.float64)
    diff = idx[:, None] - idx[None, :]
    dmat = np.where(diff >= 0, np.exp(log_g[:, None, None] * np.maximum(diff, 0.0)), 0.0)
    zeta = np.exp(log_g[:, None] * (c - 1 - idx))[:, :, None]
    xi = np.exp(log_g[:, None] * (idx + 1.0))[:, :, None]
    gch = np.exp(log_g * c)
    f = lambda v: jnp.asarray(v.astype(np.float32))
    return f(dmat), f(zeta), f(xi), f(gch)


def _retention(proj, gn_g, qcol, kcol, vcol, gcol):
    t = proj.shape[0]
    c = RET_CHUNK
    qw = RET_HEADS * RET_QK_DIM
    vw = RET_HEADS * RET_V_DIM
    dmat, zeta, xi, gch = _retention_tables()
    full3 = lambda shp: pl.BlockSpec(shp, lambda n: (0, 0, 0))
    return pl.pallas_call(
        _retention_kernel,
        grid=(t // c,),
        in_specs=[
            pl.BlockSpec((c, qw), lambda n: (n, qcol)),
            pl.BlockSpec((c, qw), lambda n: (n, kcol)),
            pl.BlockSpec((c, vw // 2), lambda n: (n, vcol)),
            pl.BlockSpec((c, vw // 2), lambda n: (n, vcol + 1)),
            pl.BlockSpec((c, vw // 2), lambda n: (n, gcol)),
            pl.BlockSpec((c, vw // 2), lambda n: (n, gcol + 1)),
            full3((RET_HEADS, c, c)),
            full3((RET_HEADS, c, 1)),
            full3((RET_HEADS, c, 1)),
            pl.BlockSpec(memory_space=pltpu.SMEM),
            pl.BlockSpec((1, vw), lambda n: (0, 0)),
        ],
        out_specs=pl.BlockSpec((c, vw), lambda n: (n, 0)),
        out_shape=jax.ShapeDtypeStruct((t, vw), BF16),
        scratch_shapes=[pltpu.VMEM((RET_HEADS, RET_QK_DIM, RET_V_DIM), F32),
                        pltpu.VMEM((RET_HEADS, c, c), BF16),
                        pltpu.VMEM((RET_HEADS, c, RET_V_DIM), F32)],
        compiler_params=_params(("arbitrary",)),
        name="retention",
    )(proj, proj, proj, proj, proj, proj, dmat, zeta, xi, gch, gn_g.reshape(1, vw))


MERGE_ROW_CHUNK = 256


def _merge_kernel(o1_ref, l1_ref, o2_ref, l2_ref, o3_ref, l3_ref, yb_ref, ga_ref, gb_ref,
                  pa_ref, pb_ref, out_ref):
    tm = out_ref.shape[0]
    for c in range(tm // MERGE_ROW_CHUNK):
        rows = slice(c * MERGE_ROW_CHUNK, (c + 1) * MERGE_ROW_CHUNK)
        l1 = l1_ref[rows, :]
        l2 = l2_ref[rows, :]
        l3 = l3_ref[rows, :]
        mx = jnp.maximum(jnp.maximum(l1, l2), l3)
        e1 = jnp.exp(l1 - mx)
        e2 = jnp.exp(l2 - mx)
        e3 = jnp.exp(l3 - mx)
        den = e1 + e2 + e3
        a1, a2, a3 = e1 / den, e2 / den, e3 / den
        pieces = []
        for hh in range(HEADS_PER_GROUP):
            sl = slice(hh * HEAD_DIM, (hh + 1) * HEAD_DIM)
            ya = (a1[:, hh:hh + 1] * o1_ref[rows, sl] + a2[:, hh:hh + 1] * o2_ref[rows, sl]
                  + a3[:, hh:hh + 1] * o3_ref[rows, sl])
            pieces.append(ya.astype(BF16))
        ya = jnp.concatenate(pieces, axis=1)
        za = jnp.dot(ya, pa_ref[...], preferred_element_type=F32)
        zb = jnp.dot(yb_ref[rows, :], pb_ref[...], preferred_element_type=F32)
        out_ref[rows, :] = (ga_ref[rows, :].astype(F32) * za
                            + gb_ref[rows, :].astype(F32) * zb).astype(out_ref.dtype)


def _merge(o1, l1, o2, l2, o3, l3, yb, proj, ga_col, gb_col, pa, pb, tm=512, tn=1024):
    t = o1.shape[0]
    wa = o1.shape[1]
    wb = yb.shape[1]
    n = pa.shape[1]
    hg = HEADS_PER_GROUP
    ratio = tn // COLBLK
    o_spec = lambda: pl.BlockSpec((tm, wa), lambda j, i: (i, 0))
    l_spec = lambda: pl.BlockSpec((tm, hg), lambda j, i: (i, 0))
    return pl.pallas_call(
        _merge_kernel,
        grid=(n // tn, t // tm),
        in_specs=[
            o_spec(), l_spec(), o_spec(), l_spec(), o_spec(), l_spec(),
            pl.BlockSpec((tm, wb), lambda j, i: (i, 0)),
            pl.BlockSpec((tm, tn), lambda j, i: (i, ga_col // ratio + j)),
            pl.BlockSpec((tm, tn), lambda j, i: (i, gb_col // ratio + j)),
            pl.BlockSpec((wa, tn), lambda j, i: (0, j)),
            pl.BlockSpec((wb, tn), lambda j, i: (0, j)),
        ],
        out_specs=pl.BlockSpec((tm, tn), lambda j, i: (i, j)),
        out_shape=jax.ShapeDtypeStruct((t, n), BF16),
        compiler_params=_params(("parallel", "parallel")),
        name="merge",
    )(o1, l1, o2, l2, o3, l3, yb, proj, proj, pa, pb)


def _oproj_kernel(x_ref, m_ref, w_ref, g_ref, o_ref):
    z = jnp.dot(m_ref[...], w_ref[...], preferred_element_type=F32)
    o_ref[...] = x_ref[...] + g_ref[...] * z


def _oproj(x, merged, w_bf, mod, gate_blk, tm=512, tn=1024):
    t, d_model = x.shape
    k = merged.shape[1]
    per = d_model // tn
    return pl.pallas_call(
        _oproj_kernel,
        grid=(d_model // tn, t // tm),
        in_specs=[
            pl.BlockSpec((tm, tn), lambda j, i: (i, j)),
            pl.BlockSpec((tm, k), lambda j, i: (i, 0)),
            pl.BlockSpec((k, tn), lambda j, i: (0, j)),
            pl.BlockSpec((1, tn), lambda j, i: (0, gate_blk * per + j)),
        ],
        out_specs=pl.BlockSpec((tm, tn), lambda j, i: (i, j)),
        out_shape=jax.ShapeDtypeStruct((t, d_model), F32),
        compiler_params=_params(("parallel", "parallel")),
        name="oproj",
    )(x, merged, w_bf, mod)


def _pack_pair(lo, hi):
    lo_b = pltpu.bitcast(lo.astype(BF16).astype(F32), U32)
    hi_b = pltpu.bitcast(hi.astype(BF16).astype(F32), U32)
    return (lo_b >> 16) | (hi_b & jnp.uint32(0xFFFF0000))


def _unpack_pair(w):
    lo = pltpu.bitcast(w << 16, F32)
    hi = pltpu.bitcast(w & jnp.uint32(0xFFFF0000), F32)
    return lo, hi


def _route_kernel(x_ref, g_ref, sc_ref, sh_ref, wt_ref, rb_ref, h_ref, hp_ref, idx_ref, rank_ref,
                  wgt_ref, cnt_ref):
    @pl.when(pl.program_id(0) == 0)
    def _():
        cnt_ref[...] = jnp.zeros_like(cnt_ref)

    x = x_ref[...]
    tm, d_model = x.shape
    inv = lax.rsqrt(jnp.mean(x * x, axis=-1, keepdims=True) + RMS_EPS)
    h = (x * inv * g_ref[...]) * (1.0 + sc_ref[...]) + sh_ref[...]
    h_ref[...] = h.astype(h_ref.dtype)
    half = d_model // 2
    hp_ref[...] = _pack_pair(h[:, :half], h[:, half:])

    ne = N_EXPERTS
    per = ne // N_GROUPS
    logits = lax.dot_general(wt_ref[...], h, (((1,), (1,)), ((), ())),
                             precision=lax.Precision.HIGHEST,
                             preferred_element_type=F32)
    scores = jax.nn.sigmoid(logits)
    sel = scores + rb_ref[...]
    eidx = lax.broadcasted_iota(I32, (ne, tm), 0).astype(F32)
    minus_inf = -jnp.inf

    sel3 = sel.reshape(N_GROUPS, per, tm)
    sub = lax.broadcasted_iota(I32, (N_GROUPS, per, tm), 1).astype(F32)
    m1 = jnp.max(sel3, axis=1, keepdims=True)
    first = jnp.min(jnp.where(sel3 == m1, sub, float(per)), axis=1, keepdims=True)
    m2 = jnp.max(jnp.where(sub == first, minus_inf, sel3), axis=1, keepdims=True)
    grp = (m1 + m2).reshape(N_GROUPS, tm)

    gidx = lax.broadcasted_iota(I32, (N_GROUPS, tm), 0).astype(F32)
    gmask = jnp.zeros((N_GROUPS, tm), F32)
    work = grp
    for _ in range(TOPK_GROUPS):
        mx = jnp.max(work, axis=0, keepdims=True)
        pick = jnp.min(jnp.where(work == mx, gidx, float(N_GROUPS)), axis=0, keepdims=True)
        hit = gidx == pick
        gmask = jnp.where(hit, 1.0, gmask)
        work = jnp.where(hit, minus_inf, work)
    emask = jnp.broadcast_to(gmask.reshape(N_GROUPS, 1, tm), (N_GROUPS, per, tm)).reshape(ne, tm)

    work = jnp.where(emask > 0.0, sel, minus_inf)
    onehot = jnp.zeros((ne, tm), F32)
    idx_rows, w_rows = [], []
    for _ in range(TOP_K):
        mx = jnp.max(work, axis=0, keepdims=True)
        pick = jnp.min(jnp.where(work == mx, eidx, float(ne)), axis=0, keepdims=True)
        hit = eidx == pick
        onehot = jnp.where(hit, 1.0, onehot)
        work = jnp.where(hit, minus_inf, work)
        idx_rows.append(pick)
        w_rows.append(jnp.sum(jnp.where(hit, scores, 0.0), axis=0, keepdims=True))
    w_all = jnp.concatenate(w_rows, axis=0)
    wgt_ref[...] = w_all / jnp.sum(w_all, axis=0, keepdims=True) * ROUTED_SCALE
    idx_ref[...] = jnp.concatenate(idx_rows, axis=0).astype(I32)

    ra = lax.broadcasted_iota(I32, (tm, tm), 0)
    rb = lax.broadcasted_iota(I32, (tm, tm), 1)
    tri = jnp.where(ra <= rb, 1.0, 0.0).astype(BF16)
    incl = jnp.dot(onehot.astype(BF16), tri, preferred_element_type=F32)
    before = incl - onehot + cnt_ref[...]
    rank_rows = [jnp.sum(jnp.where(eidx == idx_rows[kk], before, 0.0), axis=0, keepdims=True)
                 for kk in range(TOP_K)]
    rank_ref[...] = jnp.concatenate(rank_rows, axis=0).astype(I32)
    cnt_ref[...] = cnt_ref[...] + jnp.sum(onehot, axis=1, keepdims=True)


def _route(x1, g, mod, sc_blk, sh_blk, router_w, router_bias, tm=256):
    t, d_model = x1.shape
    ne = N_EXPERTS
    vec = lambda k: pl.BlockSpec((1, d_model), lambda i, k=k: (0, k))
    tok = lambda: pl.BlockSpec((TOP_K, tm), lambda i: (0, i))
    return pl.pallas_call(
        _route_kernel,
        grid=(t // tm,),
        in_specs=[pl.BlockSpec((tm, d_model), lambda i: (i, 0)),
                  pl.BlockSpec((1, d_model), lambda i: (0, 0)),
                  vec(sc_blk), vec(sh_blk),
                  pl.BlockSpec((ne, d_model), lambda i: (0, 0)),
                  pl.BlockSpec((ne, 1), lambda i: (0, 0))],
        out_specs=[pl.BlockSpec((tm, d_model), lambda i: (i, 0)),
                   pl.BlockSpec((tm, d_model // 2), lambda i: (i, 0)),
                   tok(), tok(), tok(),
                   pl.BlockSpec((ne, 1), lambda i: (0, 0))],
        out_shape=[jax.ShapeDtypeStruct((t, d_model), BF16),
                   jax.ShapeDtypeStruct((t, d_model // 2), U32),
                   jax.ShapeDtypeStruct((TOP_K, t), I32),
                   jax.ShapeDtypeStruct((TOP_K, t), I32),
                   jax.ShapeDtypeStruct((TOP_K, t), F32),
                   jax.ShapeDtypeStruct((ne, 1), F32)],
        compiler_params=_params(("arbitrary",)),
        name="route",
    )(x1, g.reshape(1, d_model), mod, mod, router_w.T, router_bias.reshape(ne, 1))


SUBLANES = 8


def _pad_chunks(bm):
    sizes, s = [], bm // 2
    while s >= SUBLANES:
        sizes.append(s)
        s //= 2
    return sizes


def _dispatch_kernel(bm, n_tok, pos_ref, fill_start_ref, fill_len_ref, nv_ref, hp_ref, xs_ref, inv_ref,
                     sem, pad_sem):
    tm = hp_ref.shape[0]
    tile = pl.program_id(0)

    @pl.when(pl.program_id(0) == 0)
    def _():
        def pad_copies(action):
            def per_expert(e, carry):
                start = fill_start_ref[e]
                n = fill_len_ref[e]
                head = (-start) & (SUBLANES - 1)
                for r in range(SUBLANES - 1):
                    @pl.when(r < head)
                    def _(r=r):
                        action(pltpu.make_async_copy(hp_ref.at[pl.ds(0, 1)],
                                                     xs_ref.at[pl.ds(start + r, 1)], pad_sem))

                start = start + head
                n = n - head
                for size in _pad_chunks(bm):
                    take = (n & size) != 0

                    @pl.when(take)
                    def _(start=start, size=size):
                        dst = pl.multiple_of(start, SUBLANES)
                        action(pltpu.make_async_copy(hp_ref.at[pl.ds(0, size)],
                                                     xs_ref.at[pl.ds(dst, size)], pad_sem))

                    start = start + jnp.where(take, size, 0)
                return carry

            lax.fori_loop(0, N_EXPERTS, per_expert, 0)

            def unused_block(b, carry):
                dst = pl.multiple_of(b * bm, bm)
                action(pltpu.make_async_copy(hp_ref.at[pl.ds(0, bm)],
                                             xs_ref.at[pl.ds(dst, bm)], pad_sem))
                return carry

            lax.fori_loop(nv_ref[0], xs_ref.shape[0] // bm, unused_block, 0)

        pad_copies(lambda cp: cp.start())

        def mark_padding(e, carry):
            def mark(r, c):
                inv_ref[fill_start_ref[e] + r] = -1
                return c

            lax.fori_loop(0, fill_len_ref[e], mark, 0)
            return carry

        lax.fori_loop(0, N_EXPERTS, mark_padding, 0)

        def mark_unused(p, c):
            inv_ref[p] = -1
            return c

        lax.fori_loop(nv_ref[0] * bm, inv_ref.shape[0], mark_unused, 0)
        pad_copies(lambda cp: cp.wait())

    def body(tt, carry):
        for kk in range(TOP_K):
            dst = pos_ref[0, 0, kk * tm + tt]
            inv_ref[dst] = kk * n_tok + tile * tm + tt
            pltpu.make_async_copy(hp_ref.at[pl.ds(tt, 1)], xs_ref.at[pl.ds(dst, 1)], sem).start()
        return carry

    lax.fori_loop(0, tm, body, 0)
    pltpu.make_async_copy(xs_ref.at[pl.ds(0, tm * TOP_K)], xs_ref.at[pl.ds(0, tm * TOP_K)], sem).wait()


def _tile_major(a_t, tm):
    k, t = a_t.shape
    return a_t.reshape(k, t // tm, tm).transpose(1, 0, 2).reshape(t // tm, 1, k * tm)


def _dispatch(hp, pos_t, fill_start, fill_len, n_valid, rows, bm, tm=256):
    t, width = hp.shape
    assert tm >= bm
    smem = lambda: pl.BlockSpec(memory_space=pltpu.SMEM)
    return pl.pallas_call(
        functools.partial(_dispatch_kernel, bm, t),
        grid=(t // tm,),
        in_specs=[pl.BlockSpec((1, 1, tm * TOP_K), lambda i: (i, 0, 0), memory_space=pltpu.SMEM),
                  smem(), smem(), smem(),
                  pl.BlockSpec((tm, width), lambda i: (i, 0))],
        out_specs=[pl.BlockSpec(memory_space=pl.ANY), smem()],
        out_shape=[jax.ShapeDtypeStruct((rows, width), U32),
                   jax.ShapeDtypeStruct((rows,), I32)],
        scratch_shapes=[pltpu.SemaphoreType.DMA(()), pltpu.SemaphoreType.DMA(())],
        compiler_params=_params(("arbitrary",)),
        name="dispatch",
    )(_tile_major(pos_t, tm), fill_start, fill_len, n_valid, hp)


def _experts_kernel(bm, n_slot_rows, seg_ref, sege_ref, nv_ref, inv_ref, x_ref, wg_hbm, wu_hbm, wd_hbm,
                    ysl_ref, wg_f32, wu_f32, wd_f32, wg_bf, wu_bf, wd_bf, y0_ref, y1_ref, sems, ysems):
    b = pl.program_id(0)
    nb = seg_ref.shape[0]
    n_valid = nv_ref[0]
    seg = seg_ref[jnp.minimum(b, nb - 1)]
    slot = seg % 2
    first = (b < nb) & ((b == 0) | (seg_ref[jnp.clip(b - 1, 0, nb - 1)] != seg))
    ybufs = (y0_ref, y1_ref)

    def weight_copies(which_seg, which_slot):
        e = sege_ref[which_seg]
        return [pltpu.make_async_copy(src.at[e], dst.at[which_slot], sems.at[which_slot])
                for src, dst in ((wg_hbm, wg_f32), (wu_hbm, wu_f32), (wd_hbm, wd_f32))]

    @pl.when(b == 0)
    def _():
        for cp in weight_copies(0, 0):
            cp.start()

    @pl.when(first)
    def _():
        for cp in weight_copies(seg, slot):
            cp.wait()
        wg_bf[...] = wg_f32[slot].astype(BF16)
        wu_bf[...] = wu_f32[slot].astype(BF16)
        wd_bf[...] = wd_f32[slot].astype(BF16)

        @pl.when(seg + 1 < nv_ref[1])
        def _():
            for cp in weight_copies(seg + 1, 1 - slot):
                cp.start()

    def compute(y_ref):
        lo, hi = _unpack_pair(x_ref[...])
        half = lo.shape[1]
        lo = lo.astype(BF16)
        hi = hi.astype(BF16)
        gate = (jnp.dot(lo, wg_bf[:half, :], preferred_element_type=F32)
                + jnp.dot(hi, wg_bf[half:, :], preferred_element_type=F32))
        up = (jnp.dot(lo, wu_bf[:half, :], preferred_element_type=F32)
              + jnp.dot(hi, wu_bf[half:, :], preferred_element_type=F32))
        act = (_silu(gate) * up).astype(BF16)
        y = jnp.dot(act, wd_bf[...], preferred_element_type=F32)
        y_ref[...] = _pack_pair(y[:, :half], y[:, half:])

    def scatter(block, parity):
        base = block * bm
        spare = n_slot_rows + parity * bm
        for r in range(bm):
            d = inv_ref[base + r]
            d = jnp.where(d < 0, spare + r, d)
            pltpu.make_async_copy(ybufs[parity].at[pl.ds(r, 1)], ysl_ref.at[pl.ds(d, 1)],
                                  ysems.at[parity]).start()

    for p in (0, 1):
        mine = (b % 2) == p

        @pl.when(mine & (b >= 2) & (b - 2 < n_valid))
        def _(p=p):
            pltpu.make_async_copy(ybufs[p], ysl_ref.at[pl.ds(0, bm)], ysems.at[p]).wait()

        @pl.when(mine & (b >= 1) & (b < n_valid))
        def _(p=p):
            scatter(b - 1, 1 - p)
            compute(ybufs[p])

        @pl.when(mine & (b >= 1) & (b == n_valid))
        def _(p=p):
            scatter(b - 1, 1 - p)

    @pl.when(b == 0)
    def _():
        spare_fill = [pltpu.make_async_copy(
            x_ref, ysl_ref.at[pl.ds(n_slot_rows + parity * bm, bm)], ysems.at[parity])
            for parity in (0, 1)]
        for cp in spare_fill:
            cp.start()
        for cp in spare_fill:
            cp.wait()
        compute(ybufs[0])


def _experts(xs, inv, seg_of, seg_e, n_valid, wg, wu, wd, n_tok, bm=EXPERT_ROWS):
    rows, width = xs.shape
    _, d_model, de = wg.shape
    nb = rows // bm
    n_slot_rows = TOP_K * n_tok
    row_map = lambda b, sg, se, nv, iv: (jnp.minimum(b, nv[0] - 1), 0)
    hbm = lambda: pl.BlockSpec(memory_space=pl.ANY)
    grid_spec = pltpu.PrefetchScalarGridSpec(
        num_scalar_prefetch=4,
        grid=(nb + 2,),
        in_specs=[pl.BlockSpec((bm, width), row_map), hbm(), hbm(), hbm()],
        out_specs=hbm(),
        scratch_shapes=[pltpu.VMEM((2, d_model, de), F32),
                        pltpu.VMEM((2, d_model, de), F32),
                        pltpu.VMEM((2, de, d_model), F32),
                        pltpu.VMEM((d_model, de), BF16),
                        pltpu.VMEM((d_model, de), BF16),
                        pltpu.VMEM((de, d_model), BF16),
                        pltpu.VMEM((bm, width), U32),
                        pltpu.VMEM((bm, width), U32),
                        pltpu.SemaphoreType.DMA((2,)),
                        pltpu.SemaphoreType.DMA((2,))],
    )
    return pl.pallas_call(
        functools.partial(_experts_kernel, bm, n_slot_rows),
        grid_spec=grid_spec,
        out_shape=jax.ShapeDtypeStruct((n_slot_rows + 2 * bm, width), U32),
        compiler_params=_params(("arbitrary",)),
        name="experts",
    )(seg_of, seg_e, n_valid, inv, xs, wg, wu, wd)


def _combine_kernel(x_ref, h_ref, wt_ref, g_ref, sg_ref, su_ref, sd_ref, *rest):
    y_refs, o_ref = rest[:TOP_K], rest[TOP_K]
    h = h_ref[...]
    act = (_silu(jnp.dot(h, sg_ref[...], preferred_element_type=F32))
           * jnp.dot(h, su_ref[...], preferred_element_type=F32)).astype(BF16)
    shared = jnp.dot(act, sd_ref[...], preferred_element_type=F32)
    half = y_refs[0].shape[1]
    wt = wt_ref[...]
    lo_acc = shared[:, :half]
    hi_acc = shared[:, half:]
    for kk in range(TOP_K):
        lo, hi = _unpack_pair(y_refs[kk][...])
        wk = wt[:, kk:kk + 1]
        lo_acc = lo_acc + wk * lo
        hi_acc = hi_acc + wk * hi
    g = g_ref[...]
    o_ref[:, :half] = x_ref[:, :half] + g[:, :half] * lo_acc
    o_ref[:, half:] = x_ref[:, half:] + g[:, half:] * hi_acc


def _combine(x1, h2, wts, mod, gate_blk, sg, su, sd, ysl, tm=256):
    t, d_model = x1.shape
    ds_ = sg.shape[1]
    width = ysl.shape[1]
    tiles = t // tm
    slot = lambda kk: pl.BlockSpec((tm, width), lambda i, kk=kk: (kk * tiles + i, 0))
    return pl.pallas_call(
        _combine_kernel,
        grid=(tiles,),
        in_specs=[pl.BlockSpec((tm, d_model), lambda i: (i, 0)),
                  pl.BlockSpec((tm, d_model), lambda i: (i, 0)),
                  pl.BlockSpec((tm, TOP_K), lambda i: (i, 0)),
                  pl.BlockSpec((1, d_model), lambda i: (0, gate_blk)),
                  pl.BlockSpec((d_model, ds_), lambda i: (0, 0)),
                  pl.BlockSpec((d_model, ds_), lambda i: (0, 0)),
                  pl.BlockSpec((ds_, d_model), lambda i: (0, 0))]
                 + [slot(kk) for kk in range(TOP_K)],
        out_specs=pl.BlockSpec((tm, d_model), lambda i: (i, 0)),
        out_shape=jax.ShapeDtypeStruct((t, d_model), F32),
        compiler_params=_params(("parallel",)),
        name="combine",
    )(x1, h2, wts, mod, sg, su, sd, *([ysl] * TOP_K))


def _layout_kernel(bm, cnt_ref, idx_ref, rank_ref, pos_ref, seg_ref, sege_ref, nv_ref, fs_ref, fl_ref):
    shift = bm.bit_length() - 1
    pos_ref[...] = rank_ref[...]

    def per_expert(e, carry):
        start, blk, seg = carry
        cnt = cnt_ref[e]
        nblk = (cnt + (bm - 1)) >> shift
        pos_ref[...] = pos_ref[...] + jnp.where(idx_ref[...] == e, start, 0)

        def mark(b, c):
            seg_ref[blk + b] = seg
            return c

        lax.fori_loop(0, nblk, mark, 0)

        @pl.when(nblk > 0)
        def _():
            sege_ref[seg] = e

        fs_ref[e] = start + cnt
        fl_ref[e] = (nblk << shift) - cnt
        return start + (nblk << shift), blk + nblk, seg + jnp.where(nblk > 0, 1, 0)

    zero = jnp.int32(0)
    _, n_valid, n_seg = lax.fori_loop(0, N_EXPERTS, per_expert, (zero, zero, zero))
    nv_ref[0] = n_valid
    nv_ref[1] = n_seg

    def tail_blocks(b, c):
        seg_ref[b] = n_seg - 1
        return c

    lax.fori_loop(n_valid, seg_ref.shape[0], tail_blocks, 0)

    def tail_segs(s, c):
        sege_ref[s] = N_EXPERTS - 1
        return c

    lax.fori_loop(n_seg, N_EXPERTS, tail_segs, 0)


def _layout(counts, idx_t, rank_t, bm, n_blocks):
    assert bm & (bm - 1) == 0
    k, t = idx_t.shape
    smem = lambda: pl.BlockSpec(memory_space=pltpu.SMEM)
    full = lambda: pl.BlockSpec((k, t), lambda: (0, 0))
    return pl.pallas_call(
        functools.partial(_layout_kernel, bm),
        in_specs=[smem(), full(), full()],
        out_specs=[full(), smem(), smem(), smem(), smem(), smem()],
        out_shape=[jax.ShapeDtypeStruct((k, t), I32),
                   jax.ShapeDtypeStruct((n_blocks,), I32),
                   jax.ShapeDtypeStruct((N_EXPERTS,), I32),
                   jax.ShapeDtypeStruct((2,), I32),
                   jax.ShapeDtypeStruct((N_EXPERTS,), I32),
                   jax.ShapeDtypeStruct((N_EXPERTS,), I32)],
        name="layout",
    )(counts.reshape(-1).astype(I32), idx_t, rank_t)


def _layer(x, c, rel_bias, w_ada, b_ada, ln1_g, w_in, q_norm_g, k_norm_g, ret_gn_g, p_a, p_b, w_o,
           ln2_g, router_w, router_bias, w_gate_e, w_up_e, w_down_e, w_gate_s, w_up_s, w_down_s):
    t, d_model = x.shape
    dils = tuple(d for _, d in DILATED_GROUPS)

    mod = _ada(c.reshape(d_model), w_ada, b_ada)
    h = _norm1(x, ln1_g, mod)
    cos_tab, sin_tab = _rotary_tables(t)
    projs = []
    for order, (cols, epis) in enumerate(_inproj_plan(d_model)):
        h_in = h if order == 0 else _to_residue_major(h, dils[order])
        projs.append(_inproj(h_in, w_in, cols, epis, q_norm_g, k_norm_g, cos_tab, sin_tab,
                             f"inproj_d{dils[order]}"))
    proj = projs[0]

    attn = [_attn_group(projs[gi], rel_bias, gi, win, dil, 0, 1, 2)
            for gi, (win, dil) in enumerate(DILATED_GROUPS)]
    base = 3
    rq = RET_HEADS * RET_QK_DIM // COLBLK
    vw_blk = RET_HEADS * RET_V_DIM // COLBLK
    qcol = base
    kcol = base + rq
    vcol_blk = base + 2 * rq
    gcol_blk = vcol_blk + vw_blk
    ga_blk = gcol_blk + vw_blk
    gb_blk = ga_blk + d_model // COLBLK
    y_b = _retention(proj, ret_gn_g, qcol, kcol, vcol_blk, gcol_blk)
    (o1, l1), (o2, l2), (o3, l3) = attn
    o2, l2 = _from_residue_major(o2, dils[1]), _from_residue_major(l2, dils[1])
    o3, l3 = _from_residue_major(o3, dils[2]), _from_residue_major(l3, dils[2])
    merged = _merge(o1, l1, o2, l2, o3, l3, y_b, proj, ga_blk, gb_blk,
                    p_a.astype(BF16), p_b.astype(BF16))
    x1 = _oproj(x, merged, w_o.astype(BF16), mod, 2)

    h2, h2p, idx_t, rank_t, wgt_t, counts = _route(x1, ln2_g, mod, 4, 3, router_w, router_bias)
    bm = EXPERT_ROWS
    n_blocks = (t * TOP_K + N_EXPERTS * (bm - 1) + bm - 1) // bm
    pos_t, seg_of, seg_e, n_valid, fill_start, fill_len = _layout(counts, idx_t, rank_t, bm, n_blocks)
    xs, inv = _dispatch(h2p, pos_t, fill_start, fill_len, n_valid, n_blocks * bm, bm)
    ysl = _experts(xs, inv, seg_of, seg_e, n_valid, w_gate_e, w_up_e, w_down_e, t)
    return _combine(x1, h2, wgt_t.T, mod, 5, w_gate_s.astype(BF16), w_up_s.astype(BF16),
                    w_down_s.astype(BF16), ysl)


def kernel(x, c, rel_bias, w_ada, b_ada, ln1_g, w_in, q_norm_g, k_norm_g, ret_gn_g, p_a, p_b, w_o,
           ln2_g, router_w, router_bias, w_gate_e, w_up_e, w_down_e, w_gate_s, w_up_s, w_down_s):
    b, s, d_model = x.shape
    depth = w_ada.shape[0]
    outs = []
    for bi in range(b):
        xb = x[bi]
        for l in range(depth):
            xb = _layer(xb, c[bi], rel_bias, w_ada[l], b_ada[l], ln1_g[l], w_in[l], q_norm_g[l],
                        k_norm_g[l], ret_gn_g[l], p_a[l], p_b[l], w_o[l], ln2_g[l], router_w[l],
                        router_bias[l], w_gate_e[l], w_up_e[l], w_down_e[l], w_gate_s[l],
                        w_up_s[l], w_down_s[l])
        outs.append(xb)
    return jnp.stack(outs, axis=0)
```

```python
import functools

import numpy as np
import jax
import jax.numpy as jnp
from jax import lax
from jax.experimental import pallas as pl
from jax.experimental.pallas import tpu as pltpu

F32 = jnp.float32
BF16 = jnp.bfloat16
U32 = jnp.uint32
I32 = jnp.int32

HEAD_DIM = 128
DILATED_GROUPS = ((128, 1), (512, 4), (2048, 16))
HEADS_PER_GROUP = 8
N_HEADS_A = HEADS_PER_GROUP * len(DILATED_GROUPS)
A_GROUP_WIDTH = HEADS_PER_GROUP * HEAD_DIM
ATTN_BLOCK = 128
NUM_BUCKETS = 32
MAX_DISTANCE = 2048
NEG_INF = -1e30
RET_HEADS = 8
RET_QK_DIM = 128
RET_V_DIM = 256
RET_CHUNK = 128
ROPE_BASE = 10000.0
GN_EPS = 1e-5
N_EXPERTS = 64
N_GROUPS = 8
TOPK_GROUPS = 4
TOP_K = 8
ROUTED_SCALE = 2.5
RMS_EPS = 1e-6

LANE = 128
COLBLK = 1024
VMEM_LIMIT = 56 * 1024 * 1024
EXPERT_ROWS = 256


def _params(sem, vmem=VMEM_LIMIT):
    return pltpu.CompilerParams(dimension_semantics=sem, vmem_limit_bytes=vmem)


def _sigmoid(v):
    return 0.5 * jnp.tanh(0.5 * v) + 0.5


def _silu(v):
    return v * _sigmoid(v)


def _ada_kernel(c_ref, w_ref, b_ref, o_ref):
    sc = _silu(c_ref[...])
    o_ref[...] = jnp.sum(w_ref[...] * sc, axis=0, keepdims=True) + b_ref[...]


def _ada(c, w, b, tn=512):
    d, n = w.shape
    return pl.pallas_call(
        _ada_kernel,
        grid=(n // tn,),
        in_specs=[pl.BlockSpec((d, 1), lambda j: (0, 0)),
                  pl.BlockSpec((d, tn), lambda j: (0, j)),
                  pl.BlockSpec((1, tn), lambda j: (0, j))],
        out_specs=pl.BlockSpec((1, tn), lambda j: (0, j)),
        out_shape=jax.ShapeDtypeStruct((1, n), F32),
        compiler_params=_params(("parallel",)),
        name="ada",
    )(c.reshape(d, 1), w, b.reshape(1, n))


def _norm1_kernel(x_ref, g_ref, sc_ref, sh_ref, o_ref):
    x = x_ref[...]
    inv = lax.rsqrt(jnp.mean(x * x, axis=-1, keepdims=True) + RMS_EPS)
    o_ref[...] = ((x * inv * g_ref[...]) * (1.0 + sc_ref[...]) + sh_ref[...]).astype(o_ref.dtype)


def _norm1(x, g, mod, tm=512):
    t, d_model = x.shape
    vec = lambda k: pl.BlockSpec((1, d_model), lambda i, k=k: (0, k))
    return pl.pallas_call(
        _norm1_kernel,
        grid=(t // tm,),
        in_specs=[pl.BlockSpec((tm, d_model), lambda i: (i, 0)),
                  pl.BlockSpec((1, d_model), lambda i: (0, 0)),
                  vec(1), vec(0)],
        out_specs=pl.BlockSpec((tm, d_model), lambda i: (i, 0)),
        out_shape=jax.ShapeDtypeStruct((t, d_model), BF16),
        compiler_params=_params(("parallel",)),
        name="norm1",
    )(x, g.reshape(1, d_model), mod, mod)


def _to_residue_major(a, d):
    t, w = a.shape
    return a.reshape(t // d, d, w).transpose(1, 0, 2).reshape(t, w)


def _from_residue_major(a, d):
    t, w = a.shape
    return a.reshape(d, t // d, w).transpose(1, 0, 2).reshape(t, w)


EPI_QNORM, EPI_KNORM, EPI_PLAIN, EPI_ROT_Q, EPI_ROT_K, EPI_SILU, EPI_SIGMOID = range(7)
INPROJ_ROW_CHUNK = 256


def _inproj_kernel(epis_present, colblk_ref, epi_ref, h_ref, w_ref, qg_ref, kg_ref, cos_ref, sin_ref,
                   o_ref, wbf_ref):
    del colblk_ref
    epi = epi_ref[pl.program_id(0)]
    tm = h_ref.shape[0]
    nh = o_ref.shape[1] // HEAD_DIM

    @pl.when(pl.program_id(1) == 0)
    def _():
        wbf_ref[...] = w_ref[...].astype(BF16)

    def head_norm(gain, scale):
        def fn(acc, rows):
            for hh in range(nh):
                sl = slice(hh * HEAD_DIM, (hh + 1) * HEAD_DIM)
                a = acc[:, sl]
                inv = lax.rsqrt(jnp.mean(a * a, axis=-1, keepdims=True) + RMS_EPS)
                o_ref[rows, sl] = ((a * inv * gain) * scale).astype(o_ref.dtype)
        return fn

    def rotary(scale):
        def fn(acc, rows):
            cos = cos_ref[rows, :]
            sin = sin_ref[rows, :]
            for hh in range(nh):
                sl = slice(hh * HEAD_DIM, (hh + 1) * HEAD_DIM)
                a = acc[:, sl]
                rot = pltpu.roll(a, HEAD_DIM // 2, 1)
                o_ref[rows, sl] = ((a * cos + rot * sin) * scale).astype(o_ref.dtype)
        return fn

    def elementwise(f):
        def fn(acc, rows):
            o_ref[rows, :] = f(acc).astype(o_ref.dtype)
        return fn

    epilogues = {
        EPI_QNORM: lambda: head_norm(qg_ref[...], HEAD_DIM ** -0.5),
        EPI_KNORM: lambda: head_norm(kg_ref[...], 1.0),
        EPI_PLAIN: lambda: elementwise(lambda a: a),
        EPI_ROT_Q: lambda: rotary(1.0),
        EPI_ROT_K: lambda: rotary(RET_QK_DIM ** -0.5),
        EPI_SILU: lambda: elementwise(_silu),
        EPI_SIGMOID: lambda: elementwise(_sigmoid),
    }
    for code in epis_present:
        @pl.when(epi == code)
        def _(code=code):
            fn = epilogues[code]()
            for c in range(tm // INPROJ_ROW_CHUNK):
                rows = slice(c * INPROJ_ROW_CHUNK, (c + 1) * INPROJ_ROW_CHUNK)
                acc = jnp.dot(h_ref[rows, :], wbf_ref[...], preferred_element_type=F32)
                fn(acc, rows)


def _inproj_plan(d_model):
    a_blocks = N_HEADS_A * HEAD_DIM // COLBLK
    groups = len(DILATED_GROUPS)
    per_group = a_blocks // groups
    rq = RET_HEADS * RET_QK_DIM // COLBLK
    rv = RET_HEADS * RET_V_DIM // COLBLK
    gd = d_model // COLBLK
    seg_epi = ([EPI_QNORM] * a_blocks + [EPI_KNORM] * a_blocks + [EPI_PLAIN] * a_blocks
               + [EPI_ROT_Q] * rq + [EPI_ROT_K] * rq + [EPI_PLAIN] * rv + [EPI_SILU] * rv
               + [EPI_SIGMOID] * (2 * gd))
    order_of = [0] * len(seg_epi)
    for seg in range(3):
        for blk in range(a_blocks):
            order_of[seg * a_blocks + blk] = blk // per_group
    plans = []
    for order in range(groups):
        cols = [cb for cb in range(len(seg_epi)) if order_of[cb] == order]
        plans.append((cols, [seg_epi[cb] for cb in cols]))
    return plans


def _inproj(h, w, cols, epis, qg, kg, cos_tab, sin_tab, name, tm=1024):
    t, d_model = h.shape
    row = lambda width: pl.BlockSpec((tm, width), lambda j, i, cb, ep: (i, 0))
    one = lambda width: pl.BlockSpec((1, width), lambda j, i, cb, ep: (0, 0))
    grid_spec = pltpu.PrefetchScalarGridSpec(
        num_scalar_prefetch=2,
        grid=(len(cols), t // tm),
        in_specs=[
            row(d_model),
            pl.BlockSpec((d_model, COLBLK), lambda j, i, cb, ep: (0, cb[j])),
            one(HEAD_DIM), one(HEAD_DIM), row(HEAD_DIM), row(HEAD_DIM),
        ],
        out_specs=pl.BlockSpec((tm, COLBLK), lambda j, i, cb, ep: (i, j)),
        scratch_shapes=[pltpu.VMEM((d_model, COLBLK), BF16)],
    )
    return pl.pallas_call(
        functools.partial(_inproj_kernel, tuple(sorted(set(epis)))),
        grid_spec=grid_spec,
        out_shape=jax.ShapeDtypeStruct((t, len(cols) * COLBLK), BF16),
        compiler_params=_params(("arbitrary", "arbitrary")),
        name=name,
    )(jnp.asarray(np.array(cols, np.int32)), jnp.asarray(np.array(epis, np.int32)),
      h, w, qg.reshape(1, HEAD_DIM), kg.reshape(1, HEAD_DIM), cos_tab, sin_tab)


def _rotary_tables(t):
    inv = ROPE_BASE ** (-np.arange(0, RET_QK_DIM, 2, dtype=np.float64) / RET_QK_DIM)
    ang = np.arange(t, dtype=np.float64)[:, None] * inv[None, :]
    cos, sin = np.cos(ang), np.sin(ang)
    cos_tab = np.concatenate([cos, cos], axis=1).astype(np.float32)
    sin_tab = np.concatenate([-sin, sin], axis=1).astype(np.float32)
    return jnp.asarray(cos_tab), jnp.asarray(sin_tab)


def _t5_bucket(dist):
    max_exact = NUM_BUCKETS // 2
    safe = np.maximum(dist, 1).astype(np.float32)
    large = max_exact + (np.log(safe / max_exact) / np.log(MAX_DISTANCE / max_exact)
                         * (NUM_BUCKETS - max_exact)).astype(np.int32)
    return np.where(dist < max_exact, dist, np.minimum(large, NUM_BUCKETS - 1)).astype(np.int32)


def _attn_kernel(head0, w_steps, blocks_per_res, tab_ref, bucket_ref, q_ref, kp_ref, kc_ref,
                 vp_ref, vc_ref, o_ref, lse_ref, bias_ref, band_ref, s_ref, p_ref):
    m_idx = pl.program_id(0)
    blk = ATTN_BLOCK

    @pl.when(m_idx == 0)
    def _():
        bucket = bucket_ref[...]
        for hh in range(HEADS_PER_GROUP):
            bias = jnp.zeros(bucket.shape, F32)
            for b in range(NUM_BUCKETS):
                bias = jnp.where(bucket == b, tab_ref[b, head0 + hh], bias)
            bias_ref[hh] = bias
        a = lax.broadcasted_iota(I32, (blk, 2 * blk), 0)
        cc = lax.broadcasted_iota(I32, (blk, 2 * blk), 1)
        delta = blk + a - cc
        band_ref[...] = jnp.where((delta >= 0) & (delta <= w_steps), 1.0, 0.0)

    prev_thr = jnp.where((m_idx % blocks_per_res) > 0, 0.5, 2.0)
    nt = (((1,), (1,)), ((), ()))
    heads = range(HEADS_PER_GROUP)
    head_cols = [slice(hh * HEAD_DIM, (hh + 1) * HEAD_DIM) for hh in heads]
    for hh, sl in zip(heads, head_cols):
        q = q_ref[:, sl]
        s_p = lax.dot_general(q, kp_ref[:, sl], nt, preferred_element_type=F32)
        s_c = lax.dot_general(q, kc_ref[:, sl], nt, preferred_element_type=F32)
        s_ref[hh, :, :blk] = jnp.where(band_ref[:, :blk] > prev_thr,
                                       s_p + bias_ref[hh, :, :blk], NEG_INF)
        s_ref[hh, :, blk:] = jnp.where(band_ref[:, blk:] > 0.5,
                                       s_c + bias_ref[hh, :, blk:], NEG_INF)
    dens, lses = [], []
    for hh in heads:
        s = s_ref[hh]
        mx = jnp.max(s, axis=-1, keepdims=True)
        p = jnp.exp(s - mx)
        den = jnp.sum(p, axis=-1, keepdims=True)
        p_ref[hh] = p.astype(BF16)
        dens.append(den)
        lses.append(mx + jnp.log(den))
    for hh, sl in zip(heads, head_cols):
        v_both = jnp.concatenate([vp_ref[:, sl], vc_ref[:, sl]], axis=0)
        acc = jnp.dot(p_ref[hh], v_both, preferred_element_type=F32)
        o_ref[:, sl] = acc / dens[hh]
    lse_ref[...] = jnp.concatenate(lses, axis=-1)


def _attn_group(proj, rel_bias, gi, window, dilation, qcol, kcol, vcol):
    t = proj.shape[0]
    blk = ATTN_BLOCK
    w_steps = window // dilation
    blocks_per_res = t // dilation // blk
    nblk = t // blk
    a = np.arange(blk)[:, None]
    cc = np.arange(2 * blk)[None, :]
    bucket = _t5_bucket(np.maximum(blk + a - cc, 0) * dilation)

    def prev_map(m):
        return jnp.where(m % blocks_per_res > 0, m - 1, m)

    kern = functools.partial(_attn_kernel, gi * HEADS_PER_GROUP, w_steps, blocks_per_res)
    width = A_GROUP_WIDTH
    return pl.pallas_call(
        kern,
        grid=(nblk,),
        in_specs=[
            pl.BlockSpec(memory_space=pltpu.SMEM),
            pl.BlockSpec((blk, 2 * blk), lambda m: (0, 0)),
            pl.BlockSpec((blk, width), lambda m: (m, qcol)),
            pl.BlockSpec((blk, width), lambda m: (prev_map(m), kcol)),
            pl.BlockSpec((blk, width), lambda m: (m, kcol)),
            pl.BlockSpec((blk, width), lambda m: (prev_map(m), vcol)),
            pl.BlockSpec((blk, width), lambda m: (m, vcol)),
        ],
        out_specs=[pl.BlockSpec((blk, width), lambda m: (m, 0)),
                   pl.BlockSpec((blk, HEADS_PER_GROUP), lambda m: (m, 0))],
        out_shape=[jax.ShapeDtypeStruct((t, width), F32),
                   jax.ShapeDtypeStruct((t, HEADS_PER_GROUP), F32)],
        scratch_shapes=[pltpu.VMEM((HEADS_PER_GROUP, blk, 2 * blk), F32),
                        pltpu.VMEM((blk, 2 * blk), F32),
                        pltpu.VMEM((HEADS_PER_GROUP, blk, 2 * blk), F32),
                        pltpu.VMEM((HEADS_PER_GROUP, blk, 2 * blk), BF16)],
        compiler_params=_params(("arbitrary",)),
        name=f"attn_d{dilation}",
    )(rel_bias, jnp.asarray(bucket), proj, proj, proj, proj, proj)


def _retention_kernel(q_ref, k_ref, v0_ref, v1_ref, g0_ref, g1_ref, dmat_ref, zeta_ref, xi_ref,
                      gch_ref, gn_ref, o_ref, state_ref, s_ref, cross_ref):
    @pl.when(pl.program_id(0) == 0)
    def _():
        state_ref[...] = jnp.zeros_like(state_ref)

    nt = (((1,), (1,)), ((), ()))
    tn = (((0,), (0,)), ((), ()))
    per_half = RET_HEADS // 2

    def head_refs(hh):
        qs = slice(hh * RET_QK_DIM, (hh + 1) * RET_QK_DIM)
        vs = slice(hh * RET_V_DIM, (hh + 1) * RET_V_DIM)
        hs = slice((hh % per_half) * RET_V_DIM, (hh % per_half + 1) * RET_V_DIM)
        v_ref, g_ref = (v0_ref, g0_ref) if hh < per_half else (v1_ref, g1_ref)
        return qs, vs, hs, v_ref, g_ref

    for hh in range(RET_HEADS):
        qs, _, hs, v_ref, _ = head_refs(hh)
        q = q_ref[:, qs]
        k = k_ref[:, qs]
        v = v_ref[:, hs]
        state = state_ref[hh]
        s = lax.dot_general(q, k, nt, preferred_element_type=F32) * dmat_ref[hh]
        s_ref[hh] = s.astype(BF16)
        cross_ref[hh] = jnp.dot(q, state.astype(BF16), preferred_element_type=F32) * xi_ref[hh]
        vz = (v.astype(F32) * zeta_ref[hh]).astype(BF16)
        upd = lax.dot_general(k, vz, tn, preferred_element_type=F32)
        state_ref[hh] = gch_ref[hh] * state + upd
    for hh in range(RET_HEADS):
        _, vs, hs, v_ref, g_ref = head_refs(hh)
        inner = jnp.dot(s_ref[hh], v_ref[:, hs], preferred_element_type=F32)
        ret = inner + cross_ref[hh]
        mu = jnp.mean(ret, axis=-1, keepdims=True)
        cen = ret - mu
        var = jnp.mean(cen * cen, axis=-1, keepdims=True)
        y = cen * lax.rsqrt(var + GN_EPS) * gn_ref[:, vs]
        o_ref[:, vs] = (y * g_ref[:, hs].astype(F32)).astype(o_ref.dtype)


def _retention_tables():
    c = RET_CHUNK
    hh = np.arange(RET_HEADS, dtype=np.float64)
    log_g = np.log1p(-np.exp2(-5.0 - hh))
    idx = np.arange(c, dtype=np.float64)
    diff = idx[:, None] - idx[None, :]
    dmat = np.where(diff >= 0, np.exp(log_g[:, None, None] * np.maximum(diff, 0.0)), 0.0)
    zeta = np.exp(log_g[:, None] * (c - 1 - idx))[:, :, None]
    xi = np.exp(log_g[:, None] * (idx + 1.0))[:, :, None]
    gch = np.exp(log_g * c)
    f = lambda v: jnp.asarray(v.astype(np.float32))
    return f(dmat), f(zeta), f(xi), f(gch)


def _retention(proj, gn_g, qcol, kcol, vcol, gcol):
    t = proj.shape[0]
    c = RET_CHUNK
    qw = RET_HEADS * RET_QK_DIM
    vw = RET_HEADS * RET_V_DIM
    dmat, zeta, xi, gch = _retention_tables()
    full3 = lambda shp: pl.BlockSpec(shp, lambda n: (0, 0, 0))
    return pl.pallas_call(
        _retention_kernel,
        grid=(t // c,),
        in_specs=[
            pl.BlockSpec((c, qw), lambda n: (n, qcol)),
            pl.BlockSpec((c, qw), lambda n: (n, kcol)),
            pl.BlockSpec((c, vw // 2), lambda n: (n, vcol)),
            pl.BlockSpec((c, vw // 2), lambda n: (n, vcol + 1)),
            pl.BlockSpec((c, vw // 2), lambda n: (n, gcol)),
            pl.BlockSpec((c, vw // 2), lambda n: (n, gcol + 1)),
            full3((RET_HEADS, c, c)),
            full3((RET_HEADS, c, 1)),
            full3((RET_HEADS, c, 1)),
            pl.BlockSpec(memory_space=pltpu.SMEM),
            pl.BlockSpec((1, vw), lambda n: (0, 0)),
        ],
        out_specs=pl.BlockSpec((c, vw), lambda n: (n, 0)),
        out_shape=jax.ShapeDtypeStruct((t, vw), BF16),
        scratch_shapes=[pltpu.VMEM((RET_HEADS, RET_QK_DIM, RET_V_DIM), F32),
                        pltpu.VMEM((RET_HEADS, c, c), BF16),
                        pltpu.VMEM((RET_HEADS, c, RET_V_DIM), F32)],
        compiler_params=_params(("arbitrary",)),
        name="retention",
    )(proj, proj, proj, proj, proj, proj, dmat, zeta, xi, gch, gn_g.reshape(1, vw))


MERGE_ROW_CHUNK = 256


def _merge_kernel(o1_ref, l1_ref, o2_ref, l2_ref, o3_ref, l3_ref, yb_ref, ga_ref, gb_ref,
                  pa_ref, pb_ref, out_ref):
    tm = out_ref.shape[0]
    for c in range(tm // MERGE_ROW_CHUNK):
        rows = slice(c * MERGE_ROW_CHUNK, (c + 1) * MERGE_ROW_CHUNK)
        l1 = l1_ref[rows, :]
        l2 = l2_ref[rows, :]
        l3 = l3_ref[rows, :]
        mx = jnp.maximum(jnp.maximum(l1, l2), l3)
        e1 = jnp.exp(l1 - mx)
        e2 = jnp.exp(l2 - mx)
        e3 = jnp.exp(l3 - mx)
        den = e1 + e2 + e3
        a1, a2, a3 = e1 / den, e2 / den, e3 / den
        pieces = []
        for hh in range(HEADS_PER_GROUP):
            sl = slice(hh * HEAD_DIM, (hh + 1) * HEAD_DIM)
            ya = (a1[:, hh:hh + 1] * o1_ref[rows, sl] + a2[:, hh:hh + 1] * o2_ref[rows, sl]
                  + a3[:, hh:hh + 1] * o3_ref[rows, sl])
            pieces.append(ya.astype(BF16))
        ya = jnp.concatenate(pieces, axis=1)
        za = jnp.dot(ya, pa_ref[...], preferred_element_type=F32)
        zb = jnp.dot(yb_ref[rows, :], pb_ref[...], preferred_element_type=F32)
        out_ref[rows, :] = (ga_ref[rows, :].astype(F32) * za
                            + gb_ref[rows, :].astype(F32) * zb).astype(out_ref.dtype)


def _merge(o1, l1, o2, l2, o3, l3, yb, proj, ga_col, gb_col, pa, pb, tm=512, tn=1024):
    t = o1.shape[0]
    wa = o1.shape[1]
    wb = yb.shape[1]
    n = pa.shape[1]
    hg = HEADS_PER_GROUP
    ratio = tn // COLBLK
    o_spec = lambda: pl.BlockSpec((tm, wa), lambda j, i: (i, 0))
    l_spec = lambda: pl.BlockSpec((tm, hg), lambda j, i: (i, 0))
    return pl.pallas_call(
        _merge_kernel,
        grid=(n // tn, t // tm),
        in_specs=[
            o_spec(), l_spec(), o_spec(), l_spec(), o_spec(), l_spec(),
            pl.BlockSpec((tm, wb), lambda j, i: (i, 0)),
            pl.BlockSpec((tm, tn), lambda j, i: (i, ga_col // ratio + j)),
            pl.BlockSpec((tm, tn), lambda j, i: (i, gb_col // ratio + j)),
            pl.BlockSpec((wa, tn), lambda j, i: (0, j)),
            pl.BlockSpec((wb, tn), lambda j, i: (0, j)),
        ],
        out_specs=pl.BlockSpec((tm, tn), lambda j, i: (i, j)),
        out_shape=jax.ShapeDtypeStruct((t, n), BF16),
        compiler_params=_params(("parallel", "parallel")),
        name="merge",
    )(o1, l1, o2, l2, o3, l3, yb, proj, proj, pa, pb)


def _oproj_kernel(x_ref, m_ref, w_ref, g_ref, o_ref):
    z = jnp.dot(m_ref[...], w_ref[...], preferred_element_type=F32)
    o_ref[...] = x_ref[...] + g_ref[...] * z


def _oproj(x, merged, w_bf, mod, gate_blk, tm=512, tn=1024):
    t, d_model = x.shape
    k = merged.shape[1]
    per = d_model // tn
    return pl.pallas_call(
        _oproj_kernel,
        grid=(d_model // tn, t // tm),
        in_specs=[
            pl.BlockSpec((tm, tn), lambda j, i: (i, j)),
            pl.BlockSpec((tm, k), lambda j, i: (i, 0)),
            pl.BlockSpec((k, tn), lambda j, i: (0, j)),
            pl.BlockSpec((1, tn), lambda j, i: (0, gate_blk * per + j)),
        ],
        out_specs=pl.BlockSpec((tm, tn), lambda j, i: (i, j)),
        out_shape=jax.ShapeDtypeStruct((t, d_model), F32),
        compiler_params=_params(("parallel", "parallel")),
        name="oproj",
    )(x, merged, w_bf, mod)


def _pack_pair(lo, hi):
    lo_b = pltpu.bitcast(lo.astype(BF16).astype(F32), U32)
    hi_b = pltpu.bitcast(hi.astype(BF16).astype(F32), U32)
    return (lo_b >> 16) | (hi_b & jnp.uint32(0xFFFF0000))


def _unpack_pair(w):
    lo = pltpu.bitcast(w << 16, F32)
    hi = pltpu.bitcast(w & jnp.uint32(0xFFFF0000), F32)
    return lo, hi


def _route_kernel(x_ref, g_ref, sc_ref, sh_ref, wt_ref, rb_ref, h_ref, hp_ref, idx_ref, rank_ref,
                  wgt_ref, cnt_ref):
    @pl.when(pl.program_id(0) == 0)
    def _():
        cnt_ref[...] = jnp.zeros_like(cnt_ref)

    x = x_ref[...]
    tm, d_model = x.shape
    inv = lax.rsqrt(jnp.mean(x * x, axis=-1, keepdims=True) + RMS_EPS)
    h = (x * inv * g_ref[...]) * (1.0 + sc_ref[...]) + sh_ref[...]
    h_ref[...] = h.astype(h_ref.dtype)
    half = d_model // 2
    hp_ref[...] = _pack_pair(h[:, :half], h[:, half:])

    ne = N_EXPERTS
    per = ne // N_GROUPS
    logits = lax.dot_general(wt_ref[...], h, (((1,), (1,)), ((), ())),
                             precision=lax.Precision.HIGHEST,
                             preferred_element_type=F32)
    scores = jax.nn.sigmoid(logits)
    sel = scores + rb_ref[...]
    eidx = lax.broadcasted_iota(I32, (ne, tm), 0).astype(F32)
    minus_inf = -jnp.inf

    sel3 = sel.reshape(N_GROUPS, per, tm)
    sub = lax.broadcasted_iota(I32, (N_GROUPS, per, tm), 1).astype(F32)
    m1 = jnp.max(sel3, axis=1, keepdims=True)
    first = jnp.min(jnp.where(sel3 == m1, sub, float(per)), axis=1, keepdims=True)
    m2 = jnp.max(jnp.where(sub == first, minus_inf, sel3), axis=1, keepdims=True)
    grp = (m1 + m2).reshape(N_GROUPS, tm)

    gidx = lax.broadcasted_iota(I32, (N_GROUPS, tm), 0).astype(F32)
    gmask = jnp.zeros((N_GROUPS, tm), F32)
    work = grp
    for _ in range(TOPK_GROUPS):
        mx = jnp.max(work, axis=0, keepdims=True)
        pick = jnp.min(jnp.where(work == mx, gidx, float(N_GROUPS)), axis=0, keepdims=True)
        hit = gidx == pick
        gmask = jnp.where(hit, 1.0, gmask)
        work = jnp.where(hit, minus_inf, work)
    emask = jnp.broadcast_to(gmask.reshape(N_GROUPS, 1, tm), (N_GROUPS, per, tm)).reshape(ne, tm)

    work = jnp.where(emask > 0.0, sel, minus_inf)
    onehot = jnp.zeros((ne, tm), F32)
    idx_rows, w_rows = [], []
    for _ in range(TOP_K):
        mx = jnp.max(work, axis=0, keepdims=True)
        pick = jnp.min(jnp.where(work == mx, eidx, float(ne)), axis=0, keepdims=True)
        hit = eidx == pick
        onehot = jnp.where(hit, 1.0, onehot)
        work = jnp.where(hit, minus_inf, work)
        idx_rows.append(pick)
        w_rows.append(jnp.sum(jnp.where(hit, scores, 0.0), axis=0, keepdims=True))
    w_all = jnp.concatenate(w_rows, axis=0)
    wgt_ref[...] = w_all / jnp.sum(w_all, axis=0, keepdims=True) * ROUTED_SCALE
    idx_ref[...] = jnp.concatenate(idx_rows, axis=0).astype(I32)

    ra = lax.broadcasted_iota(I32, (tm, tm), 0)
    rb = lax.broadcasted_iota(I32, (tm, tm), 1)
    tri = jnp.where(ra <= rb, 1.0, 0.0).astype(BF16)
    incl = jnp.dot(onehot.astype(BF16), tri, preferred_element_type=F32)
    before = incl - onehot + cnt_ref[...]
    rank_rows = [jnp.sum(jnp.where(eidx == idx_rows[kk], before, 0.0), axis=0, keepdims=True)
                 for kk in range(TOP_K)]
    rank_ref[...] = jnp.concatenate(rank_rows, axis=0).astype(I32)
    cnt_ref[...] = cnt_ref[...] + jnp.sum(onehot, axis=1, keepdims=True)


def _route(x1, g, mod, sc_blk, sh_blk, router_w, router_bias, tm=256):
    t, d_model = x1.shape
    ne = N_EXPERTS
    vec = lambda k: pl.BlockSpec((1, d_model), lambda i, k=k: (0, k))
    tok = lambda: pl.BlockSpec((TOP_K, tm), lambda i: (0, i))
    return pl.pallas_call(
        _route_kernel,
        grid=(t // tm,),
        in_specs=[pl.BlockSpec((tm, d_model), lambda i: (i, 0)),
                  pl.BlockSpec((1, d_model), lambda i: (0, 0)),
                  vec(sc_blk), vec(sh_blk),
                  pl.BlockSpec((ne, d_model), lambda i: (0, 0)),
                  pl.BlockSpec((ne, 1), lambda i: (0, 0))],
        out_specs=[pl.BlockSpec((tm, d_model), lambda i: (i, 0)),
                   pl.BlockSpec((tm, d_model // 2), lambda i: (i, 0)),
                   tok(), tok(), tok(),
                   pl.BlockSpec((ne, 1), lambda i: (0, 0))],
        out_shape=[jax.ShapeDtypeStruct((t, d_model), BF16),
                   jax.ShapeDtypeStruct((t, d_model // 2), U32),
                   jax.ShapeDtypeStruct((TOP_K, t), I32),
                   jax.ShapeDtypeStruct((TOP_K, t), I32),
                   jax.ShapeDtypeStruct((TOP_K, t), F32),
                   jax.ShapeDtypeStruct((ne, 1), F32)],
        compiler_params=_params(("arbitrary",)),
        name="route",
    )(x1, g.reshape(1, d_model), mod, mod, router_w.T, router_bias.reshape(ne, 1))


SUBLANES = 8


def _pad_chunks(bm):
    sizes, s = [], bm // 2
    while s >= SUBLANES:
        sizes.append(s)
        s //= 2
    return sizes


def _dispatch_kernel(bm, n_tok, pos_ref, fill_start_ref, fill_len_ref, nv_ref, hp_ref, xs_ref, inv_ref,
                     sem, pad_sem):
    tm = hp_ref.shape[0]
    tile = pl.program_id(0)

    @pl.when(pl.program_id(0) == 0)
    def _():
        def pad_copies(action):
            def per_expert(e, carry):
                start = fill_start_ref[e]
                n = fill_len_ref[e]
                head = (-start) & (SUBLANES - 1)
                for r in range(SUBLANES - 1):
                    @pl.when(r < head)
                    def _(r=r):
                        action(pltpu.make_async_copy(hp_ref.at[pl.ds(0, 1)],
                                                     xs_ref.at[pl.ds(start + r, 1)], pad_sem))

                start = start + head
                n = n - head
                for size in _pad_chunks(bm):
                    take = (n & size) != 0

                    @pl.when(take)
                    def _(start=start, size=size):
                        dst = pl.multiple_of(start, SUBLANES)
                        action(pltpu.make_async_copy(hp_ref.at[pl.ds(0, size)],
                                                     xs_ref.at[pl.ds(dst, size)], pad_sem))

                    start = start + jnp.where(take, size, 0)
                return carry

            lax.fori_loop(0, N_EXPERTS, per_expert, 0)

            def unused_block(b, carry):
                dst = pl.multiple_of(b * bm, bm)
                action(pltpu.make_async_copy(hp_ref.at[pl.ds(0, bm)],
                                             xs_ref.at[pl.ds(dst, bm)], pad_sem))
                return carry

            lax.fori_loop(nv_ref[0], xs_ref.shape[0] // bm, unused_block, 0)

        pad_copies(lambda cp: cp.start())

        def mark_padding(e, carry):
            def mark(r, c):
                inv_ref[fill_start_ref[e] + r] = -1
                return c

            lax.fori_loop(0, fill_len_ref[e], mark, 0)
            return carry

        lax.fori_loop(0, N_EXPERTS, mark_padding, 0)

        def mark_unused(p, c):
            inv_ref[p] = -1
            return c

        lax.fori_loop(nv_ref[0] * bm, inv_ref.shape[0], mark_unused, 0)
        pad_copies(lambda cp: cp.wait())

    def copy_rows(tt, carry):
        for kk in range(TOP_K):
            dst = pos_ref[0, 0, kk * tm + tt]
            pltpu.make_async_copy(hp_ref.at[pl.ds(tt, 1)], xs_ref.at[pl.ds(dst, 1)], sem).start()
        return carry

    def record_inverse(tt, carry):
        for kk in range(TOP_K):
            inv_ref[pos_ref[0, 0, kk * tm + tt]] = kk * n_tok + tile * tm + tt
        return carry

    lax.fori_loop(0, tm, copy_rows, 0)
    lax.fori_loop(0, tm, record_inverse, 0, unroll=8)
    pltpu.make_async_copy(xs_ref.at[pl.ds(0, tm * TOP_K)], xs_ref.at[pl.ds(0, tm * TOP_K)], sem).wait()


def _tile_major(a_t, tm):
    k, t = a_t.shape
    return a_t.reshape(k, t // tm, tm).transpose(1, 0, 2).reshape(t // tm, 1, k * tm)


def _dispatch(hp, pos_t, fill_start, fill_len, n_valid, rows, bm, tm=256):
    t, width = hp.shape
    assert tm >= bm
    smem = lambda: pl.BlockSpec(memory_space=pltpu.SMEM)
    return pl.pallas_call(
        functools.partial(_dispatch_kernel, bm, t),
        grid=(t // tm,),
        in_specs=[pl.BlockSpec((1, 1, tm * TOP_K), lambda i: (i, 0, 0), memory_space=pltpu.SMEM),
                  smem(), smem(), smem(),
                  pl.BlockSpec((tm, width), lambda i: (i, 0))],
        out_specs=[pl.BlockSpec(memory_space=pl.ANY), smem()],
        out_shape=[jax.ShapeDtypeStruct((rows, width), U32),
                   jax.ShapeDtypeStruct((rows,), I32)],
        scratch_shapes=[pltpu.SemaphoreType.DMA(()), pltpu.SemaphoreType.DMA(())],
        compiler_params=_params(("arbitrary",)),
        name="dispatch",
    )(_tile_major(pos_t, tm), fill_start, fill_len, n_valid, hp)


Y_BUFFERS = 3


def _experts_kernel(bm, n_slot_rows, seg_ref, sege_ref, nv_ref, inv_ref, x_ref, wg_hbm, wu_hbm, wd_hbm,
                    ysl_ref, wg_f32, wu_f32, wd_f32, wg_bf, wu_bf, wd_bf, y0_ref, y1_ref, y2_ref,
                    sems, ysems):
    b = pl.program_id(0)
    nb = seg_ref.shape[0]
    n_valid = nv_ref[0]
    seg = seg_ref[jnp.minimum(b, nb - 1)]
    slot = seg % 2
    first = (b < nb) & ((b == 0) | (seg_ref[jnp.clip(b - 1, 0, nb - 1)] != seg))
    ybufs = (y0_ref, y1_ref, y2_ref)

    def weight_copies(which_seg, which_slot):
        e = sege_ref[which_seg]
        return [pltpu.make_async_copy(src.at[e], dst.at[which_slot], sems.at[which_slot])
                for src, dst in ((wg_hbm, wg_f32), (wu_hbm, wu_f32), (wd_hbm, wd_f32))]

    @pl.when(b == 0)
    def _():
        for cp in weight_copies(0, 0):
            cp.start()

    @pl.when(first)
    def _():
        for cp in weight_copies(seg, slot):
            cp.wait()
        wg_bf[...] = wg_f32[slot].astype(BF16)
        wu_bf[...] = wu_f32[slot].astype(BF16)
        wd_bf[...] = wd_f32[slot].astype(BF16)

        @pl.when(seg + 1 < nv_ref[1])
        def _():
            for cp in weight_copies(seg + 1, 1 - slot):
                cp.start()

    def compute(y_ref):
        lo, hi = _unpack_pair(x_ref[...])
        half = lo.shape[1]
        lo = lo.astype(BF16)
        hi = hi.astype(BF16)
        gate = (jnp.dot(lo, wg_bf[:half, :], preferred_element_type=F32)
                + jnp.dot(hi, wg_bf[half:, :], preferred_element_type=F32))
        up = (jnp.dot(lo, wu_bf[:half, :], preferred_element_type=F32)
              + jnp.dot(hi, wu_bf[half:, :], preferred_element_type=F32))
        act = (_silu(gate) * up).astype(BF16)
        y = jnp.dot(act, wd_bf[...], preferred_element_type=F32)
        y_ref[...] = _pack_pair(y[:, :half], y[:, half:])

    def scatter(block, parity):
        base = block * bm
        spare = n_slot_rows + parity * bm
        for r in range(bm):
            d = inv_ref[base + r]
            d = jnp.where(d < 0, spare + r, d)
            pltpu.make_async_copy(ybufs[parity].at[pl.ds(r, 1)], ysl_ref.at[pl.ds(d, 1)],
                                  ysems.at[parity]).start()

    for p in range(Y_BUFFERS):
        mine = (b % Y_BUFFERS) == p
        before = (p - 1) % Y_BUFFERS

        @pl.when(mine & (b >= Y_BUFFERS) & (b - Y_BUFFERS < n_valid))
        def _(p=p):
            pltpu.make_async_copy(ybufs[p], ysl_ref.at[pl.ds(0, bm)], ysems.at[p]).wait()

        @pl.when(mine & (b >= 1) & (b < n_valid))
        def _(p=p, before=before):
            scatter(b - 1, before)
            compute(ybufs[p])

        @pl.when(mine & (b >= 1) & (b == n_valid))
        def _(before=before):
            scatter(b - 1, before)

    @pl.when(b == 0)
    def _():
        spare_fill = [pltpu.make_async_copy(
            x_ref, ysl_ref.at[pl.ds(n_slot_rows + parity * bm, bm)], ysems.at[parity])
            for parity in range(Y_BUFFERS)]
        for cp in spare_fill:
            cp.start()
        for cp in spare_fill:
            cp.wait()
        compute(ybufs[0])


def _experts(xs, inv, seg_of, seg_e, n_valid, wg, wu, wd, n_tok, bm=EXPERT_ROWS):
    rows, width = xs.shape
    _, d_model, de = wg.shape
    nb = rows // bm
    n_slot_rows = TOP_K * n_tok
    row_map = lambda b, sg, se, nv, iv: (jnp.minimum(b, nv[0] - 1), 0)
    hbm = lambda: pl.BlockSpec(memory_space=pl.ANY)
    grid_spec = pltpu.PrefetchScalarGridSpec(
        num_scalar_prefetch=4,
        grid=(nb + Y_BUFFERS,),
        in_specs=[pl.BlockSpec((bm, width), row_map), hbm(), hbm(), hbm()],
        out_specs=hbm(),
        scratch_shapes=[pltpu.VMEM((2, d_model, de), F32),
                        pltpu.VMEM((2, d_model, de), F32),
                        pltpu.VMEM((2, de, d_model), F32),
                        pltpu.VMEM((d_model, de), BF16),
                        pltpu.VMEM((d_model, de), BF16),
                        pltpu.VMEM((de, d_model), BF16),
                        ]
                       + [pltpu.VMEM((bm, width), U32)] * Y_BUFFERS
                       + [pltpu.SemaphoreType.DMA((2,)),
                          pltpu.SemaphoreType.DMA((Y_BUFFERS,))],
    )
    return pl.pallas_call(
        functools.partial(_experts_kernel, bm, n_slot_rows),
        grid_spec=grid_spec,
        out_shape=jax.ShapeDtypeStruct((n_slot_rows + Y_BUFFERS * bm, width), U32),
        compiler_params=_params(("arbitrary",)),
        name="experts",
    )(seg_of, seg_e, n_valid, inv, xs, wg, wu, wd)


def _combine_kernel(x_ref, h_ref, wt_ref, g_ref, sg_ref, su_ref, sd_ref, *rest):
    y_refs, o_ref = rest[:TOP_K], rest[TOP_K]
    h = h_ref[...]
    act = (_silu(jnp.dot(h, sg_ref[...], preferred_element_type=F32))
           * jnp.dot(h, su_ref[...], preferred_element_type=F32)).astype(BF16)
    shared = jnp.dot(act, sd_ref[...], preferred_element_type=F32)
    half = y_refs[0].shape[1]
    wt = wt_ref[...]
    lo_acc = shared[:, :half]
    hi_acc = shared[:, half:]
    for kk in range(TOP_K):
        lo, hi = _unpack_pair(y_refs[kk][...])
        wk = wt[:, kk:kk + 1]
        lo_acc = lo_acc + wk * lo
        hi_acc = hi_acc + wk * hi
    g = g_ref[...]
    o_ref[:, :half] = x_ref[:, :half] + g[:, :half] * lo_acc
    o_ref[:, half:] = x_ref[:, half:] + g[:, half:] * hi_acc


def _combine(x1, h2, wts, mod, gate_blk, sg, su, sd, ysl, tm=256):
    t, d_model = x1.shape
    ds_ = sg.shape[1]
    width = ysl.shape[1]
    tiles = t // tm
    slot = lambda kk: pl.BlockSpec((tm, width), lambda i, kk=kk: (kk * tiles + i, 0))
    return pl.pallas_call(
        _combine_kernel,
        grid=(tiles,),
        in_specs=[pl.BlockSpec((tm, d_model), lambda i: (i, 0)),
                  pl.BlockSpec((tm, d_model), lambda i: (i, 0)),
                  pl.BlockSpec((tm, TOP_K), lambda i: (i, 0)),
                  pl.BlockSpec((1, d_model), lambda i: (0, gate_blk)),
                  pl.BlockSpec((d_model, ds_), lambda i: (0, 0)),
                  pl.BlockSpec((d_model, ds_), lambda i: (0, 0)),
                  pl.BlockSpec((ds_, d_model), lambda i: (0, 0))]
                 + [slot(kk) for kk in range(TOP_K)],
        out_specs=pl.BlockSpec((tm, d_model), lambda i: (i, 0)),
        out_shape=jax.ShapeDtypeStruct((t, d_model), F32),
        compiler_params=_params(("parallel",)),
        name="combine",
    )(x1, h2, wts, mod, sg, su, sd, *([ysl] * TOP_K))


def _layout_kernel(bm, cnt_ref, idx_ref, rank_ref, pos_ref, seg_ref, sege_ref, nv_ref, fs_ref, fl_ref):
    shift = bm.bit_length() - 1
    pos_ref[...] = rank_ref[...]

    def per_expert(e, carry):
        start, blk, seg = carry
        cnt = cnt_ref[e]
        nblk = (cnt + (bm - 1)) >> shift
        pos_ref[...] = pos_ref[...] + jnp.where(idx_ref[...] == e, start, 0)

        def mark(b, c):
            seg_ref[blk + b] = seg
            return c

        lax.fori_loop(0, nblk, mark, 0)

        @pl.when(nblk > 0)
        def _():
            sege_ref[seg] = e

        fs_ref[e] = start + cnt
        fl_ref[e] = (nblk << shift) - cnt
        return start + (nblk << shift), blk + nblk, seg + jnp.where(nblk > 0, 1, 0)

    zero = jnp.int32(0)
    _, n_valid, n_seg = lax.fori_loop(0, N_EXPERTS, per_expert, (zero, zero, zero))
    nv_ref[0] = n_valid
    nv_ref[1] = n_seg

    def tail_blocks(b, c):
        seg_ref[b] = n_seg - 1
        return c

    lax.fori_loop(n_valid, seg_ref.shape[0], tail_blocks, 0)

    def tail_segs(s, c):
        sege_ref[s] = N_EXPERTS - 1
        return c

    lax.fori_loop(n_seg, N_EXPERTS, tail_segs, 0)


def _layout(counts, idx_t, rank_t, bm, n_blocks):
    assert bm & (bm - 1) == 0
    k, t = idx_t.shape
    smem = lambda: pl.BlockSpec(memory_space=pltpu.SMEM)
    full = lambda: pl.BlockSpec((k, t), lambda: (0, 0))
    return pl.pallas_call(
        functools.partial(_layout_kernel, bm),
        in_specs=[smem(), full(), full()],
        out_specs=[full(), smem(), smem(), smem(), smem(), smem()],
        out_shape=[jax.ShapeDtypeStruct((k, t), I32),
                   jax.ShapeDtypeStruct((n_blocks,), I32),
                   jax.ShapeDtypeStruct((N_EXPERTS,), I32),
                   jax.ShapeDtypeStruct((2,), I32),
                   jax.ShapeDtypeStruct((N_EXPERTS,), I32),
                   jax.ShapeDtypeStruct((N_EXPERTS,), I32)],
        name="layout",
    )(counts.reshape(-1).astype(I32), idx_t, rank_t)


def _layer(x, c, rel_bias, w_ada, b_ada, ln1_g, w_in, q_norm_g, k_norm_g, ret_gn_g, p_a, p_b, w_o,
           ln2_g, router_w, router_bias, w_gate_e, w_up_e, w_down_e, w_gate_s, w_up_s, w_down_s):
    t, d_model = x.shape
    dils = tuple(d for _, d in DILATED_GROUPS)

    mod = _ada(c.reshape(d_model), w_ada, b_ada)
    h = _norm1(x, ln1_g, mod)
    cos_tab, sin_tab = _rotary_tables(t)
    projs = []
    for order, (cols, epis) in enumerate(_inproj_plan(d_model)):
        h_in = h if order == 0 else _to_residue_major(h, dils[order])
        projs.append(_inproj(h_in, w_in, cols, epis, q_norm_g, k_norm_g, cos_tab, sin_tab,
                             f"inproj_d{dils[order]}"))
    proj = projs[0]

    attn = [_attn_group(projs[gi], rel_bias, gi, win, dil, 0, 1, 2)
            for gi, (win, dil) in enumerate(DILATED_GROUPS)]
    base = 3
    rq = RET_HEADS * RET_QK_DIM // COLBLK
    vw_blk = RET_HEADS * RET_V_DIM // COLBLK
    qcol = base
    kcol = base + rq
    vcol_blk = base + 2 * rq
    gcol_blk = vcol_blk + vw_blk
    ga_blk = gcol_blk + vw_blk
    gb_blk = ga_blk + d_model // COLBLK
    y_b = _retention(proj, ret_gn_g, qcol, kcol, vcol_blk, gcol_blk)
    (o1, l1), (o2, l2), (o3, l3) = attn
    o2, l2 = _from_residue_major(o2, dils[1]), _from_residue_major(l2, dils[1])
    o3, l3 = _from_residue_major(o3, dils[2]), _from_residue_major(l3, dils[2])
    merged = _merge(o1, l1, o2, l2, o3, l3, y_b, proj, ga_blk, gb_blk,
                    p_a.astype(BF16), p_b.astype(BF16))
    x1 = _oproj(x, merged, w_o.astype(BF16), mod, 2)

    h2, h2p, idx_t, rank_t, wgt_t, counts = _route(x1, ln2_g, mod, 4, 3, router_w, router_bias)
    bm = EXPERT_ROWS
    n_blocks = (t * TOP_K + N_EXPERTS * (bm - 1) + bm - 1) // bm
    pos_t, seg_of, seg_e, n_valid, fill_start, fill_len = _layout(counts, idx_t, rank_t, bm, n_blocks)
    xs, inv = _dispatch(h2p, pos_t, fill_start, fill_len, n_valid, n_blocks * bm, bm)
    ysl = _experts(xs, inv, seg_of, seg_e, n_valid, w_gate_e, w_up_e, w_down_e, t)
    return _combine(x1, h2, wgt_t.T, mod, 5, w_gate_s.astype(BF16), w_up_s.astype(BF16),
                    w_down_s.astype(BF16), ysl)


def kernel(x, c, rel_bias, w_ada, b_ada, ln1_g, w_in, q_norm_g, k_norm_g, ret_gn_g, p_a, p_b, w_o,
           ln2_g, router_w, router_bias, w_gate_e, w_up_e, w_down_e, w_gate_s, w_up_s, w_down_s):
    b, s, d_model = x.shape
    depth = w_ada.shape[0]
    outs = []
    for bi in range(b):
        xb = x[bi]
        for l in range(depth):
            xb = _layer(xb, c[bi], rel_bias, w_ada[l], b_ada[l], ln1_g[l], w_in[l], q_norm_g[l],
                        k_norm_g[l], ret_gn_g[l], p_a[l], p_b[l], w_o[l], ln2_g[l], router_w[l],
                        router_bias[l], w_gate_e[l], w_up_e[l], w_down_e[l], w_gate_s[l],
                        w_up_s[l], w_down_s[l])
        outs.append(xb)
    return jnp.stack(outs, axis=0)
```

```python
import functools

import numpy as np
import jax
import jax.numpy as jnp
from jax import lax
from jax.experimental import pallas as pl
from jax.experimental.pallas import tpu as pltpu

F32 = jnp.float32
BF16 = jnp.bfloat16
U32 = jnp.uint32
I32 = jnp.int32

HEAD_DIM = 128
DILATED_GROUPS = ((128, 1), (512, 4), (2048, 16))
HEADS_PER_GROUP = 8
N_HEADS_A = HEADS_PER_GROUP * len(DILATED_GROUPS)
A_GROUP_WIDTH = HEADS_PER_GROUP * HEAD_DIM
ATTN_BLOCK = 128
NUM_BUCKETS = 32
MAX_DISTANCE = 2048
NEG_INF = -1e30
RET_HEADS = 8
RET_QK_DIM = 128
RET_V_DIM = 256
RET_CHUNK = 128
ROPE_BASE = 10000.0
GN_EPS = 1e-5
N_EXPERTS = 64
N_GROUPS = 8
TOPK_GROUPS = 4
TOP_K = 8
ROUTED_SCALE = 2.5
RMS_EPS = 1e-6

LANE = 128
COLBLK = 1024
VMEM_LIMIT = 56 * 1024 * 1024
EXPERT_ROWS = 256


def _params(sem, vmem=VMEM_LIMIT):
    return pltpu.CompilerParams(dimension_semantics=sem, vmem_limit_bytes=vmem)


def _sigmoid(v):
    return 0.5 * jnp.tanh(0.5 * v) + 0.5


def _silu(v):
    return v * _sigmoid(v)


def _ada_kernel(c_ref, w_ref, b_ref, o_ref):
    sc = _silu(c_ref[...])
    o_ref[...] = jnp.sum(w_ref[...] * sc, axis=0, keepdims=True) + b_ref[...]


def _ada(c, w, b, tn=512):
    d, n = w.shape
    return pl.pallas_call(
        _ada_kernel,
        grid=(n // tn,),
        in_specs=[pl.BlockSpec((d, 1), lambda j: (0, 0)),
                  pl.BlockSpec((d, tn), lambda j: (0, j)),
                  pl.BlockSpec((1, tn), lambda j: (0, j))],
        out_specs=pl.BlockSpec((1, tn), lambda j: (0, j)),
        out_shape=jax.ShapeDtypeStruct((1, n), F32),
        compiler_params=_params(("parallel",)),
        name="ada",
    )(c.reshape(d, 1), w, b.reshape(1, n))


def _norm1_kernel(x_ref, g_ref, sc_ref, sh_ref, o_ref):
    x = x_ref[...]
    inv = lax.rsqrt(jnp.mean(x * x, axis=-1, keepdims=True) + RMS_EPS)
    o_ref[...] = ((x * inv * g_ref[...]) * (1.0 + sc_ref[...]) + sh_ref[...]).astype(o_ref.dtype)


def _norm1(x, g, mod, tm=512):
    t, d_model = x.shape
    vec = lambda k: pl.BlockSpec((1, d_model), lambda i, k=k: (0, k))
    return pl.pallas_call(
        _norm1_kernel,
        grid=(t // tm,),
        in_specs=[pl.BlockSpec((tm, d_model), lambda i: (i, 0)),
                  pl.BlockSpec((1, d_model), lambda i: (0, 0)),
                  vec(1), vec(0)],
        out_specs=pl.BlockSpec((tm, d_model), lambda i: (i, 0)),
        out_shape=jax.ShapeDtypeStruct((t, d_model), BF16),
        compiler_params=_params(("parallel",)),
        name="norm1",
    )(x, g.reshape(1, d_model), mod, mod)


def _to_residue_major(a, d):
    t, w = a.shape
    return a.reshape(t // d, d, w).transpose(1, 0, 2).reshape(t, w)


def _from_residue_major(a, d):
    t, w = a.shape
    return a.reshape(d, t // d, w).transpose(1, 0, 2).reshape(t, w)


def _reorder_kernel(d, to_residue, x_ref, o_ref):
    tm = o_ref.shape[0] if not to_residue else x_ref.shape[0]
    n = tm // d
    ii = lax.broadcasted_iota(I32, (tm, tm), 0)
    jj = lax.broadcasted_iota(I32, (tm, tm), 1)
    if to_residue:
        src = (ii & (n - 1)) * d + (ii >> (n.bit_length() - 1))
        x = x_ref[...]
    else:
        src = (ii & (d - 1)) * n + (ii >> (d.bit_length() - 1))
        x = x_ref[...].reshape(tm, x_ref.shape[-1])
    perm = jnp.where(jj == src, 1.0, 0.0).astype(BF16)
    y = jnp.dot(perm, x, preferred_element_type=F32).astype(o_ref.dtype)
    o_ref[...] = y.reshape(o_ref.shape)


def _reorder(a, d, to_residue, tm=256):
    t, w = a.shape
    n = tm // d
    assert a.dtype == BF16 and d & (d - 1) == 0 and n & (n - 1) == 0
    tok_spec = pl.BlockSpec((tm, w), lambda i: (i, 0))
    res_spec = pl.BlockSpec((d, n, w), lambda i: (0, i, 0))
    if to_residue:
        operand, in_spec, out_spec = a, tok_spec, res_spec
        out_shape = jax.ShapeDtypeStruct((d, t // d, w), BF16)
    else:
        operand, in_spec, out_spec = a.reshape(d, t // d, w), res_spec, tok_spec
        out_shape = jax.ShapeDtypeStruct((t, w), BF16)
    out = pl.pallas_call(
        functools.partial(_reorder_kernel, d, to_residue),
        grid=(t // tm,),
        in_specs=[in_spec],
        out_specs=out_spec,
        out_shape=out_shape,
        compiler_params=_params(("parallel",)),
        name=f"reorder_{'to' if to_residue else 'from'}_d{d}",
    )(operand)
    return out.reshape(t, w)


EPI_QNORM, EPI_KNORM, EPI_PLAIN, EPI_ROT_Q, EPI_ROT_K, EPI_SILU, EPI_SIGMOID = range(7)
INPROJ_ROW_CHUNK = 256


def _inproj_kernel(epis_present, colblk_ref, epi_ref, h_ref, w_ref, qg_ref, kg_ref, cos_ref, sin_ref,
                   o_ref, wbf_ref):
    del colblk_ref
    epi = epi_ref[pl.program_id(0)]
    tm = h_ref.shape[0]
    nh = o_ref.shape[1] // HEAD_DIM

    @pl.when(pl.program_id(1) == 0)
    def _():
        wbf_ref[...] = w_ref[...].astype(BF16)

    def head_norm(gain, scale):
        def fn(acc, rows):
            for hh in range(nh):
                sl = slice(hh * HEAD_DIM, (hh + 1) * HEAD_DIM)
                a = acc[:, sl]
                inv = lax.rsqrt(jnp.mean(a * a, axis=-1, keepdims=True) + RMS_EPS)
                o_ref[rows, sl] = ((a * inv * gain) * scale).astype(o_ref.dtype)
        return fn

    def rotary(scale):
        def fn(acc, rows):
            cos = cos_ref[rows, :]
            sin = sin_ref[rows, :]
            for hh in range(nh):
                sl = slice(hh * HEAD_DIM, (hh + 1) * HEAD_DIM)
                a = acc[:, sl]
                rot = pltpu.roll(a, HEAD_DIM // 2, 1)
                o_ref[rows, sl] = ((a * cos + rot * sin) * scale).astype(o_ref.dtype)
        return fn

    def elementwise(f):
        def fn(acc, rows):
            o_ref[rows, :] = f(acc).astype(o_ref.dtype)
        return fn

    epilogues = {
        EPI_QNORM: lambda: head_norm(qg_ref[...], HEAD_DIM ** -0.5),
        EPI_KNORM: lambda: head_norm(kg_ref[...], 1.0),
        EPI_PLAIN: lambda: elementwise(lambda a: a),
        EPI_ROT_Q: lambda: rotary(1.0),
        EPI_ROT_K: lambda: rotary(RET_QK_DIM ** -0.5),
        EPI_SILU: lambda: elementwise(_silu),
        EPI_SIGMOID: lambda: elementwise(_sigmoid),
    }
    for code in epis_present:
        @pl.when(epi == code)
        def _(code=code):
            fn = epilogues[code]()
            for c in range(tm // INPROJ_ROW_CHUNK):
                rows = slice(c * INPROJ_ROW_CHUNK, (c + 1) * INPROJ_ROW_CHUNK)
                acc = jnp.dot(h_ref[rows, :], wbf_ref[...], preferred_element_type=F32)
                fn(acc, rows)


def _inproj_plan(d_model):
    a_blocks = N_HEADS_A * HEAD_DIM // COLBLK
    groups = len(DILATED_GROUPS)
    per_group = a_blocks // groups
    rq = RET_HEADS * RET_QK_DIM // COLBLK
    rv = RET_HEADS * RET_V_DIM // COLBLK
    gd = d_model // COLBLK
    seg_epi = ([EPI_QNORM] * a_blocks + [EPI_KNORM] * a_blocks + [EPI_PLAIN] * a_blocks
               + [EPI_ROT_Q] * rq + [EPI_ROT_K] * rq + [EPI_PLAIN] * rv + [EPI_SILU] * rv
               + [EPI_SIGMOID] * (2 * gd))
    order_of = [0] * len(seg_epi)
    for seg in range(3):
        for blk in range(a_blocks):
            order_of[seg * a_blocks + blk] = blk // per_group
    plans = []
    for order in range(groups):
        cols = [cb for cb in range(len(seg_epi)) if order_of[cb] == order]
        plans.append((cols, [seg_epi[cb] for cb in cols]))
    return plans


def _inproj(h, w, cols, epis, qg, kg, cos_tab, sin_tab, name, tm=1024):
    t, d_model = h.shape
    row = lambda width: pl.BlockSpec((tm, width), lambda j, i, cb, ep: (i, 0))
    one = lambda width: pl.BlockSpec((1, width), lambda j, i, cb, ep: (0, 0))
    grid_spec = pltpu.PrefetchScalarGridSpec(
        num_scalar_prefetch=2,
        grid=(len(cols), t // tm),
        in_specs=[
            row(d_model),
            pl.BlockSpec((d_model, COLBLK), lambda j, i, cb, ep: (0, cb[j])),
            one(HEAD_DIM), one(HEAD_DIM), row(HEAD_DIM), row(HEAD_DIM),
        ],
        out_specs=pl.BlockSpec((tm, COLBLK), lambda j, i, cb, ep: (i, j)),
        scratch_shapes=[pltpu.VMEM((d_model, COLBLK), BF16)],
    )
    return pl.pallas_call(
        functools.partial(_inproj_kernel, tuple(sorted(set(epis)))),
        grid_spec=grid_spec,
        out_shape=jax.ShapeDtypeStruct((t, len(cols) * COLBLK), BF16),
        compiler_params=_params(("arbitrary", "arbitrary")),
        name=name,
    )(jnp.asarray(np.array(cols, np.int32)), jnp.asarray(np.array(epis, np.int32)),
      h, w, qg.reshape(1, HEAD_DIM), kg.reshape(1, HEAD_DIM), cos_tab, sin_tab)


def _rotary_tables(t):
    inv = ROPE_BASE ** (-np.arange(0, RET_QK_DIM, 2, dtype=np.float64) / RET_QK_DIM)
    ang = np.arange(t, dtype=np.float64)[:, None] * inv[None, :]
    cos, sin = np.cos(ang), np.sin(ang)
    cos_tab = np.concatenate([cos, cos], axis=1).astype(np.float32)
    sin_tab = np.concatenate([-sin, sin], axis=1).astype(np.float32)
    return jnp.asarray(cos_tab), jnp.asarray(sin_tab)


def _t5_bucket(dist):
    max_exact = NUM_BUCKETS // 2
    safe = np.maximum(dist, 1).astype(np.float32)
    large = max_exact + (np.log(safe / max_exact) / np.log(MAX_DISTANCE / max_exact)
                         * (NUM_BUCKETS - max_exact)).astype(np.int32)
    return np.where(dist < max_exact, dist, np.minimum(large, NUM_BUCKETS - 1)).astype(np.int32)


def _attn_kernel(head0, w_steps, blocks_per_res, tab_ref, bucket_ref, q_ref, kp_ref, kc_ref,
                 vp_ref, vc_ref, o_ref, lse_ref, bias_ref, band_ref, s_ref, p_ref):
    m_idx = pl.program_id(0)
    blk = ATTN_BLOCK

    @pl.when(m_idx == 0)
    def _():
        bucket = bucket_ref[...]
        for hh in range(HEADS_PER_GROUP):
            bias = jnp.zeros(bucket.shape, F32)
            for b in range(NUM_BUCKETS):
                bias = jnp.where(bucket == b, tab_ref[b, head0 + hh], bias)
            bias_ref[hh] = bias
        a = lax.broadcasted_iota(I32, (blk, 2 * blk), 0)
        cc = lax.broadcasted_iota(I32, (blk, 2 * blk), 1)
        delta = blk + a - cc
        band_ref[...] = jnp.where((delta >= 0) & (delta <= w_steps), 1.0, 0.0)

    prev_thr = jnp.where((m_idx % blocks_per_res) > 0, 0.5, 2.0)
    nt = (((1,), (1,)), ((), ()))
    heads = range(HEADS_PER_GROUP)
    head_cols = [slice(hh * HEAD_DIM, (hh + 1) * HEAD_DIM) for hh in heads]
    for hh, sl in zip(heads, head_cols):
        q = q_ref[:, sl]
        s_p = lax.dot_general(q, kp_ref[:, sl], nt, preferred_element_type=F32)
        s_c = lax.dot_general(q, kc_ref[:, sl], nt, preferred_element_type=F32)
        s_ref[hh, :, :blk] = jnp.where(band_ref[:, :blk] > prev_thr,
                                       s_p + bias_ref[hh, :, :blk], NEG_INF)
        s_ref[hh, :, blk:] = jnp.where(band_ref[:, blk:] > 0.5,
                                       s_c + bias_ref[hh, :, blk:], NEG_INF)
    dens, lses = [], []
    for hh in heads:
        s = s_ref[hh]
        mx = jnp.max(s, axis=-1, keepdims=True)
        p = jnp.exp(s - mx)
        den = jnp.sum(p, axis=-1, keepdims=True)
        p_ref[hh] = p.astype(BF16)
        dens.append(den)
        lses.append(mx + jnp.log(den))
    for hh, sl in zip(heads, head_cols):
        v_both = jnp.concatenate([vp_ref[:, sl], vc_ref[:, sl]], axis=0)
        acc = jnp.dot(p_ref[hh], v_both, preferred_element_type=F32)
        o_ref[:, sl] = (acc / dens[hh]).astype(o_ref.dtype)
    lse_ref[...] = jnp.concatenate(lses, axis=-1)


def _attn_group(proj, rel_bias, gi, window, dilation, qcol, kcol, vcol):
    t = proj.shape[0]
    blk = ATTN_BLOCK
    w_steps = window // dilation
    blocks_per_res = t // dilation // blk
    nblk = t // blk
    a = np.arange(blk)[:, None]
    cc = np.arange(2 * blk)[None, :]
    bucket = _t5_bucket(np.maximum(blk + a - cc, 0) * dilation)

    def prev_map(m):
        return jnp.where(m % blocks_per_res > 0, m - 1, m)

    kern = functools.partial(_attn_kernel, gi * HEADS_PER_GROUP, w_steps, blocks_per_res)
    width = A_GROUP_WIDTH
    return pl.pallas_call(
        kern,
        grid=(nblk,),
        in_specs=[
            pl.BlockSpec(memory_space=pltpu.SMEM),
            pl.BlockSpec((blk, 2 * blk), lambda m: (0, 0)),
            pl.BlockSpec((blk, width), lambda m: (m, qcol)),
            pl.BlockSpec((blk, width), lambda m: (prev_map(m), kcol)),
            pl.BlockSpec((blk, width), lambda m: (m, kcol)),
            pl.BlockSpec((blk, width), lambda m: (prev_map(m), vcol)),
            pl.BlockSpec((blk, width), lambda m: (m, vcol)),
        ],
        out_specs=[pl.BlockSpec((blk, width), lambda m: (m, 0)),
                   pl.BlockSpec((blk, HEADS_PER_GROUP), lambda m: (m, 0))],
        out_shape=[jax.ShapeDtypeStruct((t, width), BF16),
                   jax.ShapeDtypeStruct((t, HEADS_PER_GROUP), F32)],
        scratch_shapes=[pltpu.VMEM((HEADS_PER_GROUP, blk, 2 * blk), F32),
                        pltpu.VMEM((blk, 2 * blk), F32),
                        pltpu.VMEM((HEADS_PER_GROUP, blk, 2 * blk), F32),
                        pltpu.VMEM((HEADS_PER_GROUP, blk, 2 * blk), BF16)],
        compiler_params=_params(("arbitrary",)),
        name=f"attn_d{dilation}",
    )(rel_bias, jnp.asarray(bucket), proj, proj, proj, proj, proj)


def _retention_kernel(q_ref, k_ref, v0_ref, v1_ref, g0_ref, g1_ref, dmat_ref, zeta_ref, xi_ref,
                      gch_ref, gn_ref, o_ref, state_ref, s_ref, cross_ref):
    @pl.when(pl.program_id(0) == 0)
    def _():
        state_ref[...] = jnp.zeros_like(state_ref)

    nt = (((1,), (1,)), ((), ()))
    tn = (((0,), (0,)), ((), ()))
    per_half = RET_HEADS // 2

    def head_refs(hh):
        qs = slice(hh * RET_QK_DIM, (hh + 1) * RET_QK_DIM)
        vs = slice(hh * RET_V_DIM, (hh + 1) * RET_V_DIM)
        hs = slice((hh % per_half) * RET_V_DIM, (hh % per_half + 1) * RET_V_DIM)
        v_ref, g_ref = (v0_ref, g0_ref) if hh < per_half else (v1_ref, g1_ref)
        return qs, vs, hs, v_ref, g_ref

    for hh in range(RET_HEADS):
        qs, _, hs, v_ref, _ = head_refs(hh)
        q = q_ref[:, qs]
        k = k_ref[:, qs]
        v = v_ref[:, hs]
        state = state_ref[hh]
        s = lax.dot_general(q, k, nt, preferred_element_type=F32) * dmat_ref[hh]
        s_ref[hh] = s.astype(BF16)
        cross_ref[hh] = jnp.dot(q, state.astype(BF16), preferred_element_type=F32) * xi_ref[hh]
        vz = (v.astype(F32) * zeta_ref[hh]).astype(BF16)
        upd = lax.dot_general(k, vz, tn, preferred_element_type=F32)
        state_ref[hh] = gch_ref[hh] * state + upd
    for hh in range(RET_HEADS):
        _, vs, hs, v_ref, g_ref = head_refs(hh)
        inner = jnp.dot(s_ref[hh], v_ref[:, hs], preferred_element_type=F32)
        ret = inner + cross_ref[hh]
        mu = jnp.mean(ret, axis=-1, keepdims=True)
        cen = ret - mu
        var = jnp.mean(cen * cen, axis=-1, keepdims=True)
        y = cen * lax.rsqrt(var + GN_EPS) * gn_ref[:, vs]
        o_ref[:, vs] = (y * g_ref[:, hs].astype(F32)).astype(o_ref.dtype)


def _retention_tables():
    c = RET_CHUNK
    hh = np.arange(RET_HEADS, dtype=np.float64)
    log_g = np.log1p(-np.exp2(-5.0 - hh))
    idx = np.arange(c, dtype=np.float64)
    diff = idx[:, None] - idx[None, :]
    dmat = np.where(diff >= 0, np.exp(log_g[:, None, None] * np.maximum(diff, 0.0)), 0.0)
    zeta = np.exp(log_g[:, None] * (c - 1 - idx))[:, :, None]
    xi = np.exp(log_g[:, None] * (idx + 1.0))[:, :, None]
    gch = np.exp(log_g * c)
    f = lambda v: jnp.asarray(v.astype(np.float32))
    return f(dmat), f(zeta), f(xi), f(gch)


def _retention(proj, gn_g, qcol, kcol, vcol, gcol):
    t = proj.shape[0]
    c = RET_CHUNK
    qw = RET_HEADS * RET_QK_DIM
    vw = RET_HEADS * RET_V_DIM
    dmat, zeta, xi, gch = _retention_tables()
    full3 = lambda shp: pl.BlockSpec(shp, lambda n: (0, 0, 0))
    return pl.pallas_call(
        _retention_kernel,
        grid=(t // c,),
        in_specs=[
            pl.BlockSpec((c, qw), lambda n: (n, qcol)),
            pl.BlockSpec((c, qw), lambda n: (n, kcol)),
            pl.BlockSpec((c, vw // 2), lambda n: (n, vcol)),
            pl.BlockSpec((c, vw // 2), lambda n: (n, vcol + 1)),
            pl.BlockSpec((c, vw // 2), lambda n: (n, gcol)),
            pl.BlockSpec((c, vw // 2), lambda n: (n, gcol + 1)),
            full3((RET_HEADS, c, c)),
            full3((RET_HEADS, c, 1)),
            full3((RET_HEADS, c, 1)),
            pl.BlockSpec(memory_space=pltpu.SMEM),
            pl.BlockSpec((1, vw), lambda n: (0, 0)),
        ],
        out_specs=pl.BlockSpec((c, vw), lambda n: (n, 0)),
        out_shape=jax.ShapeDtypeStruct((t, vw), BF16),
        scratch_shapes=[pltpu.VMEM((RET_HEADS, RET_QK_DIM, RET_V_DIM), F32),
                        pltpu.VMEM((RET_HEADS, c, c), BF16),
                        pltpu.VMEM((RET_HEADS, c, RET_V_DIM), F32)],
        compiler_params=_params(("arbitrary",)),
        name="retention",
    )(proj, proj, proj, proj, proj, proj, dmat, zeta, xi, gch, gn_g.reshape(1, vw))


MERGE_ROW_CHUNK = 256


def _merge_kernel(o1_ref, l1_ref, o2_ref, l2_ref, o3_ref, l3_ref, yb_ref, ga_ref, gb_ref,
                  pa_ref, pb_ref, out_ref):
    tm = out_ref.shape[0]
    for c in range(tm // MERGE_ROW_CHUNK):
        rows = slice(c * MERGE_ROW_CHUNK, (c + 1) * MERGE_ROW_CHUNK)
        l1 = l1_ref[rows, :]
        l2 = l2_ref[rows, :]
        l3 = l3_ref[rows, :]
        mx = jnp.maximum(jnp.maximum(l1, l2), l3)
        e1 = jnp.exp(l1 - mx)
        e2 = jnp.exp(l2 - mx)
        e3 = jnp.exp(l3 - mx)
        den = e1 + e2 + e3
        a1, a2, a3 = e1 / den, e2 / den, e3 / den
        pieces = []
        for hh in range(HEADS_PER_GROUP):
            sl = slice(hh * HEAD_DIM, (hh + 1) * HEAD_DIM)
            ya = (a1[:, hh:hh + 1] * o1_ref[rows, sl] + a2[:, hh:hh + 1] * o2_ref[rows, sl]
                  + a3[:, hh:hh + 1] * o3_ref[rows, sl])
            pieces.append(ya.astype(BF16))
        ya = jnp.concatenate(pieces, axis=1)
        za = jnp.dot(ya, pa_ref[...], preferred_element_type=F32)
        zb = jnp.dot(yb_ref[rows, :], pb_ref[...], preferred_element_type=F32)
        out_ref[rows, :] = (ga_ref[rows, :].astype(F32) * za
                            + gb_ref[rows, :].astype(F32) * zb).astype(out_ref.dtype)


def _merge(o1, l1, o2, l2, o3, l3, yb, proj, ga_col, gb_col, pa, pb, tm=512, tn=1024):
    t = o1.shape[0]
    wa = o1.shape[1]
    wb = yb.shape[1]
    n = pa.shape[1]
    hg = HEADS_PER_GROUP
    ratio = tn // COLBLK
    o_spec = lambda: pl.BlockSpec((tm, wa), lambda j, i: (i, 0))
    l_spec = lambda: pl.BlockSpec((tm, hg), lambda j, i: (i, 0))
    return pl.pallas_call(
        _merge_kernel,
        grid=(n // tn, t // tm),
        in_specs=[
            o_spec(), l_spec(), o_spec(), l_spec(), o_spec(), l_spec(),
            pl.BlockSpec((tm, wb), lambda j, i: (i, 0)),
            pl.BlockSpec((tm, tn), lambda j, i: (i, ga_col // ratio + j)),
            pl.BlockSpec((tm, tn), lambda j, i: (i, gb_col // ratio + j)),
            pl.BlockSpec((wa, tn), lambda j, i: (0, j)),
            pl.BlockSpec((wb, tn), lambda j, i: (0, j)),
        ],
        out_specs=pl.BlockSpec((tm, tn), lambda j, i: (i, j)),
        out_shape=jax.ShapeDtypeStruct((t, n), BF16),
        compiler_params=_params(("parallel", "parallel")),
        name="merge",
    )(o1, l1, o2, l2, o3, l3, yb, proj, proj, pa, pb)


def _oproj_kernel(x_ref, m_ref, w_ref, g_ref, o_ref):
    z = jnp.dot(m_ref[...], w_ref[...], preferred_element_type=F32)
    o_ref[...] = x_ref[...] + g_ref[...] * z


def _oproj(x, merged, w_bf, mod, gate_blk, tm=512, tn=1024):
    t, d_model = x.shape
    k = merged.shape[1]
    per = d_model // tn
    return pl.pallas_call(
        _oproj_kernel,
        grid=(d_model // tn, t // tm),
        in_specs=[
            pl.BlockSpec((tm, tn), lambda j, i: (i, j)),
            pl.BlockSpec((tm, k), lambda j, i: (i, 0)),
            pl.BlockSpec((k, tn), lambda j, i: (0, j)),
            pl.BlockSpec((1, tn), lambda j, i: (0, gate_blk * per + j)),
        ],
        out_specs=pl.BlockSpec((tm, tn), lambda j, i: (i, j)),
        out_shape=jax.ShapeDtypeStruct((t, d_model), F32),
        compiler_params=_params(("parallel", "parallel")),
        name="oproj",
    )(x, merged, w_bf, mod)


def _pack_pair(lo, hi):
    lo_b = pltpu.bitcast(lo.astype(BF16).astype(F32), U32)
    hi_b = pltpu.bitcast(hi.astype(BF16).astype(F32), U32)
    return (lo_b >> 16) | (hi_b & jnp.uint32(0xFFFF0000))


def _unpack_pair(w):
    lo = pltpu.bitcast(w << 16, F32)
    hi = pltpu.bitcast(w & jnp.uint32(0xFFFF0000), F32)
    return lo, hi


def _route_kernel(x_ref, g_ref, sc_ref, sh_ref, wt_ref, rb_ref, h_ref, hp_ref, idx_ref, rank_ref,
                  wgt_ref, cnt_ref):
    @pl.when(pl.program_id(0) == 0)
    def _():
        cnt_ref[...] = jnp.zeros_like(cnt_ref)

    x = x_ref[...]
    tm, d_model = x.shape
    inv = lax.rsqrt(jnp.mean(x * x, axis=-1, keepdims=True) + RMS_EPS)
    h = (x * inv * g_ref[...]) * (1.0 + sc_ref[...]) + sh_ref[...]
    h_ref[...] = h.astype(h_ref.dtype)
    half = d_model // 2
    hp_ref[...] = _pack_pair(h[:, :half], h[:, half:])

    ne = N_EXPERTS
    per = ne // N_GROUPS
    logits = lax.dot_general(wt_ref[...], h, (((1,), (1,)), ((), ())),
                             precision=lax.Precision.HIGHEST,
                             preferred_element_type=F32)
    scores = jax.nn.sigmoid(logits)
    sel = scores + rb_ref[...]
    eidx = lax.broadcasted_iota(I32, (ne, tm), 0).astype(F32)
    minus_inf = -jnp.inf

    sel3 = sel.reshape(N_GROUPS, per, tm)
    sub = lax.broadcasted_iota(I32, (N_GROUPS, per, tm), 1).astype(F32)
    m1 = jnp.max(sel3, axis=1, keepdims=True)
    first = jnp.min(jnp.where(sel3 == m1, sub, float(per)), axis=1, keepdims=True)
    m2 = jnp.max(jnp.where(sub == first, minus_inf, sel3), axis=1, keepdims=True)
    grp = (m1 + m2).reshape(N_GROUPS, tm)

    gidx = lax.broadcasted_iota(I32, (N_GROUPS, tm), 0).astype(F32)
    gmask = jnp.zeros((N_GROUPS, tm), F32)
    work = grp
    for _ in range(TOPK_GROUPS):
        mx = jnp.max(work, axis=0, keepdims=True)
        pick = jnp.min(jnp.where(work == mx, gidx, float(N_GROUPS)), axis=0, keepdims=True)
        hit = gidx == pick
        gmask = jnp.where(hit, 1.0, gmask)
        work = jnp.where(hit, minus_inf, work)
    emask = jnp.broadcast_to(gmask.reshape(N_GROUPS, 1, tm), (N_GROUPS, per, tm)).reshape(ne, tm)

    work = jnp.where(emask > 0.0, sel, minus_inf)
    onehot = jnp.zeros((ne, tm), F32)
    idx_rows, w_rows = [], []
    for _ in range(TOP_K):
        mx = jnp.max(work, axis=0, keepdims=True)
        pick = jnp.min(jnp.where(work == mx, eidx, float(ne)), axis=0, keepdims=True)
        hit = eidx == pick
        onehot = jnp.where(hit, 1.0, onehot)
        work = jnp.where(hit, minus_inf, work)
        idx_rows.append(pick)
        w_rows.append(jnp.sum(jnp.where(hit, scores, 0.0), axis=0, keepdims=True))
    w_all = jnp.concatenate(w_rows, axis=0)
    wgt_ref[...] = w_all / jnp.sum(w_all, axis=0, keepdims=True) * ROUTED_SCALE
    idx_ref[...] = jnp.concatenate(idx_rows, axis=0).astype(I32)

    ra = lax.broadcasted_iota(I32, (tm, tm), 0)
    rb = lax.broadcasted_iota(I32, (tm, tm), 1)
    tri = jnp.where(ra <= rb, 1.0, 0.0).astype(BF16)
    incl = jnp.dot(onehot.astype(BF16), tri, preferred_element_type=F32)
    before = incl - onehot + cnt_ref[...]
    rank_rows = [jnp.sum(jnp.where(eidx == idx_rows[kk], before, 0.0), axis=0, keepdims=True)
                 for kk in range(TOP_K)]
    rank_ref[...] = jnp.concatenate(rank_rows, axis=0).astype(I32)
    cnt_ref[...] = cnt_ref[...] + jnp.sum(onehot, axis=1, keepdims=True)


def _route(x1, g, mod, sc_blk, sh_blk, router_w, router_bias, tm=256):
    t, d_model = x1.shape
    ne = N_EXPERTS
    vec = lambda k: pl.BlockSpec((1, d_model), lambda i, k=k: (0, k))
    tok = lambda: pl.BlockSpec((TOP_K, tm), lambda i: (0, i))
    return pl.pallas_call(
        _route_kernel,
        grid=(t // tm,),
        in_specs=[pl.BlockSpec((tm, d_model), lambda i: (i, 0)),
                  pl.BlockSpec((1, d_model), lambda i: (0, 0)),
                  vec(sc_blk), vec(sh_blk),
                  pl.BlockSpec((ne, d_model), lambda i: (0, 0)),
                  pl.BlockSpec((ne, 1), lambda i: (0, 0))],
        out_specs=[pl.BlockSpec((tm, d_model), lambda i: (i, 0)),
                   pl.BlockSpec((tm, d_model // 2), lambda i: (i, 0)),
                   tok(), tok(), tok(),
                   pl.BlockSpec((ne, 1), lambda i: (0, 0))],
        out_shape=[jax.ShapeDtypeStruct((t, d_model), BF16),
                   jax.ShapeDtypeStruct((t, d_model // 2), U32),
                   jax.ShapeDtypeStruct((TOP_K, t), I32),
                   jax.ShapeDtypeStruct((TOP_K, t), I32),
                   jax.ShapeDtypeStruct((TOP_K, t), F32),
                   jax.ShapeDtypeStruct((ne, 1), F32)],
        compiler_params=_params(("arbitrary",)),
        name="route",
    )(x1, g.reshape(1, d_model), mod, mod, router_w.T, router_bias.reshape(ne, 1))


SUBLANES = 8


def _pad_chunks(bm):
    sizes, s = [], bm // 2
    while s >= SUBLANES:
        sizes.append(s)
        s //= 2
    return sizes


def _dispatch_kernel(bm, pos_ref, fill_start_ref, fill_len_ref, nv_ref, hp_ref, xs_ref, sem, pad_sem):
    tm = hp_ref.shape[0]

    @pl.when(pl.program_id(0) == 0)
    def _():
        def pad_copies(action):
            def per_expert(e, carry):
                start = fill_start_ref[e]
                n = fill_len_ref[e]
                head = (-start) & (SUBLANES - 1)
                for r in range(SUBLANES - 1):
                    @pl.when(r < head)
                    def _(r=r):
                        action(pltpu.make_async_copy(hp_ref.at[pl.ds(0, 1)],
                                                     xs_ref.at[pl.ds(start + r, 1)], pad_sem))

                start = start + head
                n = n - head
                for size in _pad_chunks(bm):
                    take = (n & size) != 0

                    @pl.when(take)
                    def _(start=start, size=size):
                        dst = pl.multiple_of(start, SUBLANES)
                        action(pltpu.make_async_copy(hp_ref.at[pl.ds(0, size)],
                                                     xs_ref.at[pl.ds(dst, size)], pad_sem))

                    start = start + jnp.where(take, size, 0)
                return carry

            lax.fori_loop(0, N_EXPERTS, per_expert, 0)

            def unused_block(b, carry):
                dst = pl.multiple_of(b * bm, bm)
                action(pltpu.make_async_copy(hp_ref.at[pl.ds(0, bm)],
                                             xs_ref.at[pl.ds(dst, bm)], pad_sem))
                return carry

            lax.fori_loop(nv_ref[0], xs_ref.shape[0] // bm, unused_block, 0)

        pad_copies(lambda cp: cp.start())
        pad_copies(lambda cp: cp.wait())

    def copy_rows(tt, carry):
        for kk in range(TOP_K):
            dst = pos_ref[0, 0, kk * tm + tt]
            pltpu.make_async_copy(hp_ref.at[pl.ds(tt, 1)], xs_ref.at[pl.ds(dst, 1)], sem).start()
        return carry

    lax.fori_loop(0, tm, copy_rows, 0)
    pltpu.make_async_copy(xs_ref.at[pl.ds(0, tm * TOP_K)], xs_ref.at[pl.ds(0, tm * TOP_K)], sem).wait()


def _tile_major(a_t, tm):
    k, t = a_t.shape
    return a_t.reshape(k, t // tm, tm).transpose(1, 0, 2).reshape(t // tm, 1, k * tm)


def _dispatch(hp, pos_t, fill_start, fill_len, n_valid, rows, bm, tm=256):
    t, width = hp.shape
    assert tm >= bm
    smem = lambda: pl.BlockSpec(memory_space=pltpu.SMEM)
    return pl.pallas_call(
        functools.partial(_dispatch_kernel, bm),
        grid=(t // tm,),
        in_specs=[pl.BlockSpec((1, 1, tm * TOP_K), lambda i: (i, 0, 0), memory_space=pltpu.SMEM),
                  smem(), smem(), smem(),
                  pl.BlockSpec((tm, width), lambda i: (i, 0))],
        out_specs=pl.BlockSpec(memory_space=pl.ANY),
        out_shape=jax.ShapeDtypeStruct((rows, width), U32),
        scratch_shapes=[pltpu.SemaphoreType.DMA(()), pltpu.SemaphoreType.DMA(())],
        compiler_params=_params(("arbitrary",)),
        name="dispatch",
    )(_tile_major(pos_t, tm), fill_start, fill_len, n_valid, hp)


INVERT_UNROLL = 8


def _invert_kernel(bm, n_tok, tm, pos_ref, fill_start_ref, fill_len_ref, nv_ref, inv_ref):
    tile = pl.program_id(0)

    @pl.when(tile == 0)
    def _():
        def mark_block(first_row):
            def mark(i, c):
                for u in range(INVERT_UNROLL):
                    inv_ref[first_row + i * INVERT_UNROLL + u] = -1
                return c

            lax.fori_loop(0, bm // INVERT_UNROLL, mark, 0)

        def mark_padding(e, carry):
            @pl.when(fill_len_ref[e] > 0)
            def _():
                mark_block(fill_start_ref[e] + fill_len_ref[e] - bm)

            return carry

        lax.fori_loop(0, N_EXPERTS, mark_padding, 0)

        def mark_unused(blk, c):
            mark_block(blk * bm)
            return c

        lax.fori_loop(nv_ref[0], inv_ref.shape[0] // bm, mark_unused, 0)

    def body(tt, carry):
        for kk in range(TOP_K):
            inv_ref[pos_ref[0, 0, kk * tm + tt]] = kk * n_tok + tile * tm + tt
        return carry

    lax.fori_loop(0, tm, body, 0, unroll=INVERT_UNROLL)


def _invert(pos_t, fill_start, fill_len, n_valid, rows, bm, tm=1024):
    _, t = pos_t.shape
    smem = lambda: pl.BlockSpec(memory_space=pltpu.SMEM)
    return pl.pallas_call(
        functools.partial(_invert_kernel, bm, t, tm),
        grid=(t // tm,),
        in_specs=[pl.BlockSpec((1, 1, tm * TOP_K), lambda i: (i, 0, 0), memory_space=pltpu.SMEM),
                  smem(), smem(), smem()],
        out_specs=smem(),
        out_shape=jax.ShapeDtypeStruct((rows,), I32),
        compiler_params=_params(("arbitrary",)),
        name="invert",
    )(_tile_major(pos_t, tm), fill_start, fill_len, n_valid)


Y_BUFFERS = 3


def _experts_kernel(bm, n_slot_rows, seg_ref, sege_ref, nv_ref, inv_ref, x_ref, wg_hbm, wu_hbm, wd_hbm,
                    ysl_ref, wg_f32, wu_f32, wd_f32, wg_bf, wu_bf, wd_bf, y0_ref, y1_ref, y2_ref,
                    sems, ysems):
    b = pl.program_id(0)
    nb = seg_ref.shape[0]
    n_valid = nv_ref[0]
    seg = seg_ref[jnp.minimum(b, nb - 1)]
    slot = seg % 2
    first = (b < nb) & ((b == 0) | (seg_ref[jnp.clip(b - 1, 0, nb - 1)] != seg))
    ybufs = (y0_ref, y1_ref, y2_ref)

    def weight_copies(which_seg, which_slot):
        e = sege_ref[which_seg]
        return [pltpu.make_async_copy(src.at[e], dst.at[which_slot], sems.at[which_slot])
                for src, dst in ((wg_hbm, wg_f32), (wu_hbm, wu_f32), (wd_hbm, wd_f32))]

    @pl.when(b == 0)
    def _():
        for cp in weight_copies(0, 0):
            cp.start()

    @pl.when(first)
    def _():
        for cp in weight_copies(seg, slot):
            cp.wait()
        wg_bf[...] = wg_f32[slot].astype(BF16)
        wu_bf[...] = wu_f32[slot].astype(BF16)
        wd_bf[...] = wd_f32[slot].astype(BF16)

        @pl.when(seg + 1 < nv_ref[1])
        def _():
            for cp in weight_copies(seg + 1, 1 - slot):
                cp.start()

    def compute(y_ref):
        lo, hi = _unpack_pair(x_ref[...])
        half = lo.shape[1]
        lo = lo.astype(BF16)
        hi = hi.astype(BF16)
        gate = (jnp.dot(lo, wg_bf[:half, :], preferred_element_type=F32)
                + jnp.dot(hi, wg_bf[half:, :], preferred_element_type=F32))
        up = (jnp.dot(lo, wu_bf[:half, :], preferred_element_type=F32)
              + jnp.dot(hi, wu_bf[half:, :], preferred_element_type=F32))
        act = (_silu(gate) * up).astype(BF16)
        y = jnp.dot(act, wd_bf[...], preferred_element_type=F32)
        y_ref[...] = _pack_pair(y[:, :half], y[:, half:])

    def scatter(block, parity):
        base = block * bm
        spare = n_slot_rows + parity * bm
        for r in range(bm):
            d = inv_ref[base + r]
            d = jnp.where(d < 0, spare + r, d)
            pltpu.make_async_copy(ybufs[parity].at[pl.ds(r, 1)], ysl_ref.at[pl.ds(d, 1)],
                                  ysems.at[parity]).start()

    for p in range(Y_BUFFERS):
        mine = (b % Y_BUFFERS) == p
        before = (p - 1) % Y_BUFFERS

        @pl.when(mine & (b >= Y_BUFFERS) & (b - Y_BUFFERS < n_valid))
        def _(p=p):
            pltpu.make_async_copy(ybufs[p], ysl_ref.at[pl.ds(0, bm)], ysems.at[p]).wait()

        @pl.when(mine & (b >= 1) & (b < n_valid))
        def _(p=p, before=before):
            scatter(b - 1, before)
            compute(ybufs[p])

        @pl.when(mine & (b >= 1) & (b == n_valid))
        def _(before=before):
            scatter(b - 1, before)

    @pl.when(b == 0)
    def _():
        spare_fill = [pltpu.make_async_copy(
            x_ref, ysl_ref.at[pl.ds(n_slot_rows + parity * bm, bm)], ysems.at[parity])
            for parity in range(Y_BUFFERS)]
        for cp in spare_fill:
            cp.start()
        for cp in spare_fill:
            cp.wait()
        compute(ybufs[0])


def _experts(xs, inv, seg_of, seg_e, n_valid, wg, wu, wd, n_tok, bm=EXPERT_ROWS):
    rows, width = xs.shape
    _, d_model, de = wg.shape
    nb = rows // bm
    n_slot_rows = TOP_K * n_tok
    row_map = lambda b, sg, se, nv, iv: (jnp.minimum(b, nv[0] - 1), 0)
    hbm = lambda: pl.BlockSpec(memory_space=pl.ANY)
    grid_spec = pltpu.PrefetchScalarGridSpec(
        num_scalar_prefetch=4,
        grid=(nb + Y_BUFFERS,),
        in_specs=[pl.BlockSpec((bm, width), row_map), hbm(), hbm(), hbm()],
        out_specs=hbm(),
        scratch_shapes=[pltpu.VMEM((2, d_model, de), F32),
                        pltpu.VMEM((2, d_model, de), F32),
                        pltpu.VMEM((2, de, d_model), F32),
                        pltpu.VMEM((d_model, de), BF16),
                        pltpu.VMEM((d_model, de), BF16),
                        pltpu.VMEM((de, d_model), BF16),
                        ]
                       + [pltpu.VMEM((bm, width), U32)] * Y_BUFFERS
                       + [pltpu.SemaphoreType.DMA((2,)),
                          pltpu.SemaphoreType.DMA((Y_BUFFERS,))],
    )
    return pl.pallas_call(
        functools.partial(_experts_kernel, bm, n_slot_rows),
        grid_spec=grid_spec,
        out_shape=jax.ShapeDtypeStruct((n_slot_rows + Y_BUFFERS * bm, width), U32),
        compiler_params=_params(("arbitrary",)),
        name="experts",
    )(seg_of, seg_e, n_valid, inv, xs, wg, wu, wd)


def _combine_kernel(x_ref, h_ref, wt_ref, g_ref, sg_ref, su_ref, sd_ref, *rest):
    y_refs, o_ref = rest[:TOP_K], rest[TOP_K]
    h = h_ref[...]
    act = (_silu(jnp.dot(h, sg_ref[...], preferred_element_type=F32))
           * jnp.dot(h, su_ref[...], preferred_element_type=F32)).astype(BF16)
    shared = jnp.dot(act, sd_ref[...], preferred_element_type=F32)
    half = y_refs[0].shape[1]
    wt = wt_ref[...]
    lo_acc = shared[:, :half]
    hi_acc = shared[:, half:]
    for kk in range(TOP_K):
        lo, hi = _unpack_pair(y_refs[kk][...])
        wk = wt[:, kk:kk + 1]
        lo_acc = lo_acc + wk * lo
        hi_acc = hi_acc + wk * hi
    g = g_ref[...]
    o_ref[:, :half] = x_ref[:, :half] + g[:, :half] * lo_acc
    o_ref[:, half:] = x_ref[:, half:] + g[:, half:] * hi_acc


def _combine(x1, h2, wts, mod, gate_blk, sg, su, sd, ysl, tm=256):
    t, d_model = x1.shape
    ds_ = sg.shape[1]
    width = ysl.shape[1]
    tiles = t // tm
    slot = lambda kk: pl.BlockSpec((tm, width), lambda i, kk=kk: (kk * tiles + i, 0))
    return pl.pallas_call(
        _combine_kernel,
        grid=(tiles,),
        in_specs=[pl.BlockSpec((tm, d_model), lambda i: (i, 0)),
                  pl.BlockSpec((tm, d_model), lambda i: (i, 0)),
                  pl.BlockSpec((tm, TOP_K), lambda i: (i, 0)),
                  pl.BlockSpec((1, d_model), lambda i: (0, gate_blk)),
                  pl.BlockSpec((d_model, ds_), lambda i: (0, 0)),
                  pl.BlockSpec((d_model, ds_), lambda i: (0, 0)),
                  pl.BlockSpec((ds_, d_model), lambda i: (0, 0))]
                 + [slot(kk) for kk in range(TOP_K)],
        out_specs=pl.BlockSpec((tm, d_model), lambda i: (i, 0)),
        out_shape=jax.ShapeDtypeStruct((t, d_model), F32),
        compiler_params=_params(("parallel",)),
        name="combine",
    )(x1, h2, wts, mod, sg, su, sd, *([ysl] * TOP_K))


def _layout_kernel(bm, cnt_ref, idx_ref, rank_ref, pos_ref, seg_ref, sege_ref, nv_ref, fs_ref, fl_ref):
    shift = bm.bit_length() - 1
    pos_ref[...] = rank_ref[...]

    def per_expert(e, carry):
        start, blk, seg = carry
        cnt = cnt_ref[e]
        nblk = (cnt + (bm - 1)) >> shift
        pos_ref[...] = pos_ref[...] + jnp.where(idx_ref[...] == e, start, 0)

        def mark(b, c):
            seg_ref[blk + b] = seg
            return c

        lax.fori_loop(0, nblk, mark, 0)

        @pl.when(nblk > 0)
        def _():
            sege_ref[seg] = e

        fs_ref[e] = start + cnt
        fl_ref[e] = (nblk << shift) - cnt
        return start + (nblk << shift), blk + nblk, seg + jnp.where(nblk > 0, 1, 0)

    zero = jnp.int32(0)
    _, n_valid, n_seg = lax.fori_loop(0, N_EXPERTS, per_expert, (zero, zero, zero))
    nv_ref[0] = n_valid
    nv_ref[1] = n_seg

    def tail_blocks(b, c):
        seg_ref[b] = n_seg - 1
        return c

    lax.fori_loop(n_valid, seg_ref.shape[0], tail_blocks, 0)

    def tail_segs(s, c):
        sege_ref[s] = N_EXPERTS - 1
        return c

    lax.fori_loop(n_seg, N_EXPERTS, tail_segs, 0)


def _layout(counts, idx_t, rank_t, bm, n_blocks):
    assert bm & (bm - 1) == 0
    k, t = idx_t.shape
    smem = lambda: pl.BlockSpec(memory_space=pltpu.SMEM)
    full = lambda: pl.BlockSpec((k, t), lambda: (0, 0))
    return pl.pallas_call(
        functools.partial(_layout_kernel, bm),
        in_specs=[smem(), full(), full()],
        out_specs=[full(), smem(), smem(), smem(), smem(), smem()],
        out_shape=[jax.ShapeDtypeStruct((k, t), I32),
                   jax.ShapeDtypeStruct((n_blocks,), I32),
                   jax.ShapeDtypeStruct((N_EXPERTS,), I32),
                   jax.ShapeDtypeStruct((2,), I32),
                   jax.ShapeDtypeStruct((N_EXPERTS,), I32),
                   jax.ShapeDtypeStruct((N_EXPERTS,), I32)],
        name="layout",
    )(counts.reshape(-1).astype(I32), idx_t, rank_t)


def _layer(x, c, rel_bias, w_ada, b_ada, ln1_g, w_in, q_norm_g, k_norm_g, ret_gn_g, p_a, p_b, w_o,
           ln2_g, router_w, router_bias, w_gate_e, w_up_e, w_down_e, w_gate_s, w_up_s, w_down_s):
    t, d_model = x.shape
    dils = tuple(d for _, d in DILATED_GROUPS)

    mod = _ada(c.reshape(d_model), w_ada, b_ada)
    h = _norm1(x, ln1_g, mod)
    cos_tab, sin_tab = _rotary_tables(t)
    projs = []
    for order, (cols, epis) in enumerate(_inproj_plan(d_model)):
        h_in = h if order == 0 else _reorder(h, dils[order], True)
        projs.append(_inproj(h_in, w_in, cols, epis, q_norm_g, k_norm_g, cos_tab, sin_tab,
                             f"inproj_d{dils[order]}"))
    proj = projs[0]

    attn = [_attn_group(projs[gi], rel_bias, gi, win, dil, 0, 1, 2)
            for gi, (win, dil) in enumerate(DILATED_GROUPS)]
    base = 3
    rq = RET_HEADS * RET_QK_DIM // COLBLK
    vw_blk = RET_HEADS * RET_V_DIM // COLBLK
    qcol = base
    kcol = base + rq
    vcol_blk = base + 2 * rq
    gcol_blk = vcol_blk + vw_blk
    ga_blk = gcol_blk + vw_blk
    gb_blk = ga_blk + d_model // COLBLK
    y_b = _retention(proj, ret_gn_g, qcol, kcol, vcol_blk, gcol_blk)
    (o1, l1), (o2, l2), (o3, l3) = attn
    o2, l2 = _reorder(o2, dils[1], False), _from_residue_major(l2, dils[1])
    o3, l3 = _reorder(o3, dils[2], False), _from_residue_major(l3, dils[2])
    merged = _merge(o1, l1, o2, l2, o3, l3, y_b, proj, ga_blk, gb_blk,
                    p_a.astype(BF16), p_b.astype(BF16))
    x1 = _oproj(x, merged, w_o.astype(BF16), mod, 2)

    h2, h2p, idx_t, rank_t, wgt_t, counts = _route(x1, ln2_g, mod, 4, 3, router_w, router_bias)
    bm = EXPERT_ROWS
    n_blocks = (t * TOP_K + N_EXPERTS * (bm - 1) + bm - 1) // bm
    pos_t, seg_of, seg_e, n_valid, fill_start, fill_len = _layout(counts, idx_t, rank_t, bm, n_blocks)
    xs = _dispatch(h2p, pos_t, fill_start, fill_len, n_valid, n_blocks * bm, bm)
    inv = _invert(pos_t, fill_start, fill_len, n_valid, n_blocks * bm, bm)
    ysl = _experts(xs, inv, seg_of, seg_e, n_valid, w_gate_e, w_up_e, w_down_e, t)
    return _combine(x1, h2, wgt_t.T, mod, 5, w_gate_s.astype(BF16), w_up_s.astype(BF16),
                    w_down_s.astype(BF16), ysl)


def kernel(x, c, rel_bias, w_ada, b_ada, ln1_g, w_in, q_norm_g, k_norm_g, ret_gn_g, p_a, p_b, w_o,
           ln2_g, router_w, router_bias, w_gate_e, w_up_e, w_down_e, w_gate_s, w_up_s, w_down_s):
    b, s, d_model = x.shape
    depth = w_ada.shape[0]
    outs = []
    for bi in range(b):
        xb = x[bi]
        for l in range(depth):
            xb = _layer(xb, c[bi], rel_bias, w_ada[l], b_ada[l], ln1_g[l], w_in[l], q_norm_g[l],
                        k_norm_g[l], ret_gn_g[l], p_a[l], p_b[l], w_o[l], ln2_g[l], router_w[l],
                        router_bias[l], w_gate_e[l], w_up_e[l], w_down_e[l], w_gate_s[l],
                        w_up_s[l], w_down_s[l])
        outs.append(xb)
    return jnp.stack(outs, axis=0)
```

```python
import functools

import numpy as np
import jax
import jax.numpy as jnp
from jax import lax
from jax.experimental import pallas as pl
from jax.experimental.pallas import tpu as pltpu

F32 = jnp.float32
BF16 = jnp.bfloat16
U32 = jnp.uint32
I32 = jnp.int32

HEAD_DIM = 128
DILATED_GROUPS = ((128, 1), (512, 4), (2048, 16))
HEADS_PER_GROUP = 8
N_HEADS_A = HEADS_PER_GROUP * len(DILATED_GROUPS)
A_GROUP_WIDTH = HEADS_PER_GROUP * HEAD_DIM
ATTN_BLOCK = 128
NUM_BUCKETS = 32
MAX_DISTANCE = 2048
NEG_INF = -1e30
RET_HEADS = 8
RET_QK_DIM = 128
RET_V_DIM = 256
RET_CHUNK = 128
ROPE_BASE = 10000.0
GN_EPS = 1e-5
N_EXPERTS = 64
N_GROUPS = 8
TOPK_GROUPS = 4
TOP_K = 8
ROUTED_SCALE = 2.5
RMS_EPS = 1e-6

LANE = 128
COLBLK = 1024
VMEM_LIMIT = 56 * 1024 * 1024
EXPERT_ROWS = 256


def _params(sem, vmem=VMEM_LIMIT):
    return pltpu.CompilerParams(dimension_semantics=sem, vmem_limit_bytes=vmem)


def _sigmoid(v):
    return 0.5 * jnp.tanh(0.5 * v) + 0.5


def _silu(v):
    return v * _sigmoid(v)


def _ada_kernel(c_ref, w_ref, b_ref, o_ref):
    sc = _silu(c_ref[...])
    o_ref[...] = jnp.sum(w_ref[...] * sc, axis=0, keepdims=True) + b_ref[...]


def _ada(c, w, b, tn=512):
    d, n = w.shape
    return pl.pallas_call(
        _ada_kernel,
        grid=(n // tn,),
        in_specs=[pl.BlockSpec((d, 1), lambda j: (0, 0)),
                  pl.BlockSpec((d, tn), lambda j: (0, j)),
                  pl.BlockSpec((1, tn), lambda j: (0, j))],
        out_specs=pl.BlockSpec((1, tn), lambda j: (0, j)),
        out_shape=jax.ShapeDtypeStruct((1, n), F32),
        compiler_params=_params(("parallel",)),
        name="ada",
    )(c.reshape(d, 1), w, b.reshape(1, n))


PERM_TILE = 256


def _perm_matrix(d, to_residue):
    tm = PERM_TILE
    n = tm // d
    assert d & (d - 1) == 0 and n & (n - 1) == 0
    ii = lax.broadcasted_iota(I32, (tm, tm), 0)
    jj = lax.broadcasted_iota(I32, (tm, tm), 1)
    if to_residue:
        src = (ii & (n - 1)) * d + (ii >> (n.bit_length() - 1))
    else:
        src = (ii & (d - 1)) * n + (ii >> (d.bit_length() - 1))
    return jnp.where(jj == src, 1.0, 0.0).astype(BF16)


def _norm1_kernel(dils, x_ref, g_ref, sc_ref, sh_ref, o_ref, *res_refs):
    x = x_ref[...]
    inv = lax.rsqrt(jnp.mean(x * x, axis=-1, keepdims=True) + RMS_EPS)
    h = ((x * inv * g_ref[...]) * (1.0 + sc_ref[...]) + sh_ref[...]).astype(o_ref.dtype)
    o_ref[...] = h
    tm, width = h.shape
    for d, r_ref in zip(dils, res_refs):
        perm = _perm_matrix(d, True)
        n = PERM_TILE // d
        for s in range(tm // PERM_TILE):
            sub = h[s * PERM_TILE:(s + 1) * PERM_TILE, :]
            y = jnp.dot(perm, sub, preferred_element_type=F32).astype(r_ref.dtype)
            r_ref[:, s * n:(s + 1) * n, :] = y.reshape(d, n, width)


def _norm1(x, g, mod, dils, tm=512):
    t, d_model = x.shape
    vec = lambda k: pl.BlockSpec((1, d_model), lambda i, k=k: (0, k))
    out_shapes = [jax.ShapeDtypeStruct((t, d_model), BF16)]
    out_specs = [pl.BlockSpec((tm, d_model), lambda i: (i, 0))]
    for d in dils:
        out_shapes.append(jax.ShapeDtypeStruct((d, t // d, d_model), BF16))
        out_specs.append(pl.BlockSpec((d, tm // d, d_model), lambda i: (0, i, 0)))
    outs = pl.pallas_call(
        functools.partial(_norm1_kernel, dils),
        grid=(t // tm,),
        in_specs=[pl.BlockSpec((tm, d_model), lambda i: (i, 0)),
                  pl.BlockSpec((1, d_model), lambda i: (0, 0)),
                  vec(1), vec(0)],
        out_specs=out_specs,
        out_shape=out_shapes,
        compiler_params=_params(("parallel",)),
        name="norm1",
    )(x, g.reshape(1, d_model), mod, mod)
    return [o.reshape(t, d_model) for o in outs]


def _from_residue_major(a, d):
    t, w = a.shape
    return a.reshape(d, t // d, w).transpose(1, 0, 2).reshape(t, w)


EPI_QNORM, EPI_KNORM, EPI_PLAIN, EPI_ROT_Q, EPI_ROT_K, EPI_SILU, EPI_SIGMOID = range(7)
INPROJ_ROW_CHUNK = 256


def _inproj_kernel(epis_present, colblk_ref, epi_ref, h_ref, w_ref, qg_ref, kg_ref, cos_ref, sin_ref,
                   o_ref, wbf_ref):
    del colblk_ref
    epi = epi_ref[pl.program_id(0)]
    tm = h_ref.shape[0]
    nh = o_ref.shape[1] // HEAD_DIM

    @pl.when(pl.program_id(1) == 0)
    def _():
        wbf_ref[...] = w_ref[...].astype(BF16)

    def head_norm(gain, scale):
        def fn(acc, rows):
            for hh in range(nh):
                sl = slice(hh * HEAD_DIM, (hh + 1) * HEAD_DIM)
                a = acc[:, sl]
                inv = lax.rsqrt(jnp.mean(a * a, axis=-1, keepdims=True) + RMS_EPS)
                o_ref[rows, sl] = ((a * inv * gain) * scale).astype(o_ref.dtype)
        return fn

    def rotary(scale):
        def fn(acc, rows):
            cos = cos_ref[rows, :]
            sin = sin_ref[rows, :]
            for hh in range(nh):
                sl = slice(hh * HEAD_DIM, (hh + 1) * HEAD_DIM)
                a = acc[:, sl]
                rot = pltpu.roll(a, HEAD_DIM // 2, 1)
                o_ref[rows, sl] = ((a * cos + rot * sin) * scale).astype(o_ref.dtype)
        return fn

    def elementwise(f):
        def fn(acc, rows):
            o_ref[rows, :] = f(acc).astype(o_ref.dtype)
        return fn

    epilogues = {
        EPI_QNORM: lambda: head_norm(qg_ref[...], HEAD_DIM ** -0.5),
        EPI_KNORM: lambda: head_norm(kg_ref[...], 1.0),
        EPI_PLAIN: lambda: elementwise(lambda a: a),
        EPI_ROT_Q: lambda: rotary(1.0),
        EPI_ROT_K: lambda: rotary(RET_QK_DIM ** -0.5),
        EPI_SILU: lambda: elementwise(_silu),
        EPI_SIGMOID: lambda: elementwise(_sigmoid),
    }
    for code in epis_present:
        @pl.when(epi == code)
        def _(code=code):
            fn = epilogues[code]()
            for c in range(tm // INPROJ_ROW_CHUNK):
                rows = slice(c * INPROJ_ROW_CHUNK, (c + 1) * INPROJ_ROW_CHUNK)
                acc = jnp.dot(h_ref[rows, :], wbf_ref[...], preferred_element_type=F32)
                fn(acc, rows)


def _inproj_plan(d_model):
    a_blocks = N_HEADS_A * HEAD_DIM // COLBLK
    groups = len(DILATED_GROUPS)
    per_group = a_blocks // groups
    rq = RET_HEADS * RET_QK_DIM // COLBLK
    rv = RET_HEADS * RET_V_DIM // COLBLK
    gd = d_model // COLBLK
    seg_epi = ([EPI_QNORM] * a_blocks + [EPI_KNORM] * a_blocks + [EPI_PLAIN] * a_blocks
               + [EPI_ROT_Q] * rq + [EPI_ROT_K] * rq + [EPI_PLAIN] * rv + [EPI_SILU] * rv
               + [EPI_SIGMOID] * (2 * gd))
    order_of = [0] * len(seg_epi)
    for seg in range(3):
        for blk in range(a_blocks):
            order_of[seg * a_blocks + blk] = blk // per_group
    plans = []
    for order in range(groups):
        cols = [cb for cb in range(len(seg_epi)) if order_of[cb] == order]
        plans.append((cols, [seg_epi[cb] for cb in cols]))
    return plans


def _inproj(h, w, cols, epis, qg, kg, cos_tab, sin_tab, name, tm=1024):
    t, d_model = h.shape
    row = lambda width: pl.BlockSpec((tm, width), lambda j, i, cb, ep: (i, 0))
    one = lambda width: pl.BlockSpec((1, width), lambda j, i, cb, ep: (0, 0))
    grid_spec = pltpu.PrefetchScalarGridSpec(
        num_scalar_prefetch=2,
        grid=(len(cols), t // tm),
        in_specs=[
            row(d_model),
            pl.BlockSpec((d_model, COLBLK), lambda j, i, cb, ep: (0, cb[j])),
            one(HEAD_DIM), one(HEAD_DIM), row(HEAD_DIM), row(HEAD_DIM),
        ],
        out_specs=pl.BlockSpec((tm, COLBLK), lambda j, i, cb, ep: (i, j)),
        scratch_shapes=[pltpu.VMEM((d_model, COLBLK), BF16)],
    )
    return pl.pallas_call(
        functools.partial(_inproj_kernel, tuple(sorted(set(epis)))),
        grid_spec=grid_spec,
        out_shape=jax.ShapeDtypeStruct((t, len(cols) * COLBLK), BF16),
        compiler_params=_params(("arbitrary", "arbitrary")),
        name=name,
    )(jnp.asarray(np.array(cols, np.int32)), jnp.asarray(np.array(epis, np.int32)),
      h, w, qg.reshape(1, HEAD_DIM), kg.reshape(1, HEAD_DIM), cos_tab, sin_tab)


def _rotary_tables(t):
    inv = ROPE_BASE ** (-np.arange(0, RET_QK_DIM, 2, dtype=np.float64) / RET_QK_DIM)
    ang = np.arange(t, dtype=np.float64)[:, None] * inv[None, :]
    cos, sin = np.cos(ang), np.sin(ang)
    cos_tab = np.concatenate([cos, cos], axis=1).astype(np.float32)
    sin_tab = np.concatenate([-sin, sin], axis=1).astype(np.float32)
    return jnp.asarray(cos_tab), jnp.asarray(sin_tab)


def _t5_bucket(dist):
    max_exact = NUM_BUCKETS // 2
    safe = np.maximum(dist, 1).astype(np.float32)
    large = max_exact + (np.log(safe / max_exact) / np.log(MAX_DISTANCE / max_exact)
                         * (NUM_BUCKETS - max_exact)).astype(np.int32)
    return np.where(dist < max_exact, dist, np.minimum(large, NUM_BUCKETS - 1)).astype(np.int32)


def _attn_kernel(head0, w_steps, blocks_per_res, tab_ref, bucket_ref, q_ref, kp_ref, kc_ref,
                 vp_ref, vc_ref, o_ref, lse_ref, bias_ref, band_ref, s_ref, p_ref):
    m_idx = pl.program_id(0)
    blk = ATTN_BLOCK

    @pl.when(m_idx == 0)
    def _():
        bucket = bucket_ref[...]
        for hh in range(HEADS_PER_GROUP):
            bias = jnp.zeros(bucket.shape, F32)
            for b in range(NUM_BUCKETS):
                bias = jnp.where(bucket == b, tab_ref[b, head0 + hh], bias)
            bias_ref[hh] = bias
        a = lax.broadcasted_iota(I32, (blk, 2 * blk), 0)
        cc = lax.broadcasted_iota(I32, (blk, 2 * blk), 1)
        delta = blk + a - cc
        band_ref[...] = jnp.where((delta >= 0) & (delta <= w_steps), 1.0, 0.0)

    prev_thr = jnp.where((m_idx % blocks_per_res) > 0, 0.5, 2.0)
    nt = (((1,), (1,)), ((), ()))
    heads = range(HEADS_PER_GROUP)
    head_cols = [slice(hh * HEAD_DIM, (hh + 1) * HEAD_DIM) for hh in heads]
    for hh, sl in zip(heads, head_cols):
        q = q_ref[:, sl]
        s_p = lax.dot_general(q, kp_ref[:, sl], nt, preferred_element_type=F32)
        s_c = lax.dot_general(q, kc_ref[:, sl], nt, preferred_element_type=F32)
        s_ref[hh, :, :blk] = jnp.where(band_ref[:, :blk] > prev_thr,
                                       s_p + bias_ref[hh, :, :blk], NEG_INF)
        s_ref[hh, :, blk:] = jnp.where(band_ref[:, blk:] > 0.5,
                                       s_c + bias_ref[hh, :, blk:], NEG_INF)
    dens, lses = [], []
    for hh in heads:
        s = s_ref[hh]
        mx = jnp.max(s, axis=-1, keepdims=True)
        p = jnp.exp(s - mx)
        den = jnp.sum(p, axis=-1, keepdims=True)
        p_ref[hh] = p.astype(BF16)
        dens.append(den)
        lses.append(mx + jnp.log(den))
    for hh, sl in zip(heads, head_cols):
        v_both = jnp.concatenate([vp_ref[:, sl], vc_ref[:, sl]], axis=0)
        acc = jnp.dot(p_ref[hh], v_both, preferred_element_type=F32)
        o_ref[:, sl] = (acc / dens[hh]).astype(o_ref.dtype)
    lse_ref[...] = jnp.concatenate(lses, axis=-1)


def _attn_group(proj, rel_bias, gi, window, dilation, qcol, kcol, vcol):
    t = proj.shape[0]
    blk = ATTN_BLOCK
    w_steps = window // dilation
    blocks_per_res = t // dilation // blk
    nblk = t // blk
    a = np.arange(blk)[:, None]
    cc = np.arange(2 * blk)[None, :]
    bucket = _t5_bucket(np.maximum(blk + a - cc, 0) * dilation)

    def prev_map(m):
        return jnp.where(m % blocks_per_res > 0, m - 1, m)

    kern = functools.partial(_attn_kernel, gi * HEADS_PER_GROUP, w_steps, blocks_per_res)
    width = A_GROUP_WIDTH
    return pl.pallas_call(
        kern,
        grid=(nblk,),
        in_specs=[
            pl.BlockSpec(memory_space=pltpu.SMEM),
            pl.BlockSpec((blk, 2 * blk), lambda m: (0, 0)),
            pl.BlockSpec((blk, width), lambda m: (m, qcol)),
            pl.BlockSpec((blk, width), lambda m: (prev_map(m), kcol)),
            pl.BlockSpec((blk, width), lambda m: (m, kcol)),
            pl.BlockSpec((blk, width), lambda m: (prev_map(m), vcol)),
            pl.BlockSpec((blk, width), lambda m: (m, vcol)),
        ],
        out_specs=[pl.BlockSpec((blk, width), lambda m: (m, 0)),
                   pl.BlockSpec((blk, HEADS_PER_GROUP), lambda m: (m, 0))],
        out_shape=[jax.ShapeDtypeStruct((t, width), BF16),
                   jax.ShapeDtypeStruct((t, HEADS_PER_GROUP), F32)],
        scratch_shapes=[pltpu.VMEM((HEADS_PER_GROUP, blk, 2 * blk), F32),
                        pltpu.VMEM((blk, 2 * blk), F32),
                        pltpu.VMEM((HEADS_PER_GROUP, blk, 2 * blk), F32),
                        pltpu.VMEM((HEADS_PER_GROUP, blk, 2 * blk), BF16)],
        compiler_params=_params(("arbitrary",)),
        name=f"attn_d{dilation}",
    )(rel_bias, jnp.asarray(bucket), proj, proj, proj, proj, proj)


def _retention_kernel(q_ref, k_ref, v0_ref, v1_ref, g0_ref, g1_ref, dmat_ref, zeta_ref, xi_ref,
                      gch_ref, gn_ref, o_ref, state_ref, s_ref, cross_ref):
    @pl.when(pl.program_id(0) == 0)
    def _():
        state_ref[...] = jnp.zeros_like(state_ref)

    nt = (((1,), (1,)), ((), ()))
    tn = (((0,), (0,)), ((), ()))
    per_half = RET_HEADS // 2

    def head_refs(hh):
        qs = slice(hh * RET_QK_DIM, (hh + 1) * RET_QK_DIM)
        vs = slice(hh * RET_V_DIM, (hh + 1) * RET_V_DIM)
        hs = slice((hh % per_half) * RET_V_DIM, (hh % per_half + 1) * RET_V_DIM)
        v_ref, g_ref = (v0_ref, g0_ref) if hh < per_half else (v1_ref, g1_ref)
        return qs, vs, hs, v_ref, g_ref

    for hh in range(RET_HEADS):
        qs, _, hs, v_ref, _ = head_refs(hh)
        q = q_ref[:, qs]
        k = k_ref[:, qs]
        v = v_ref[:, hs]
        state = state_ref[hh]
        s = lax.dot_general(q, k, nt, preferred_element_type=F32) * dmat_ref[hh]
        s_ref[hh] = s.astype(BF16)
        cross_ref[hh] = jnp.dot(q, state.astype(BF16), preferred_element_type=F32) * xi_ref[hh]
        vz = (v.astype(F32) * zeta_ref[hh]).astype(BF16)
        upd = lax.dot_general(k, vz, tn, preferred_element_type=F32)
        state_ref[hh] = gch_ref[hh] * state + upd
    for hh in range(RET_HEADS):
        _, vs, hs, v_ref, g_ref = head_refs(hh)
        inner = jnp.dot(s_ref[hh], v_ref[:, hs], preferred_element_type=F32)
        ret = inner + cross_ref[hh]
        mu = jnp.mean(ret, axis=-1, keepdims=True)
        cen = ret - mu
        var = jnp.mean(cen * cen, axis=-1, keepdims=True)
        y = cen * lax.rsqrt(var + GN_EPS) * gn_ref[:, vs]
        o_ref[:, vs] = (y * g_ref[:, hs].astype(F32)).astype(o_ref.dtype)


def _retention_tables():
    c = RET_CHUNK
    hh = np.arange(RET_HEADS, dtype=np.float64)
    log_g = np.log1p(-np.exp2(-5.0 - hh))
    idx = np.arange(c, dtype=np.float64)
    diff = idx[:, None] - idx[None, :]
    dmat = np.where(diff >= 0, np.exp(log_g[:, None, None] * np.maximum(diff, 0.0)), 0.0)
    zeta = np.exp(log_g[:, None] * (c - 1 - idx))[:, :, None]
    xi = np.exp(log_g[:, None] * (idx + 1.0))[:, :, None]
    gch = np.exp(log_g * c)
    f = lambda v: jnp.asarray(v.astype(np.float32))
    return f(dmat), f(zeta), f(xi), f(gch)


def _retention(proj, gn_g, qcol, kcol, vcol, gcol):
    t = proj.shape[0]
    c = RET_CHUNK
    qw = RET_HEADS * RET_QK_DIM
    vw = RET_HEADS * RET_V_DIM
    dmat, zeta, xi, gch = _retention_tables()
    full3 = lambda shp: pl.BlockSpec(shp, lambda n: (0, 0, 0))
    return pl.pallas_call(
        _retention_kernel,
        grid=(t // c,),
        in_specs=[
            pl.BlockSpec((c, qw), lambda n: (n, qcol)),
            pl.BlockSpec((c, qw), lambda n: (n, kcol)),
            pl.BlockSpec((c, vw // 2), lambda n: (n, vcol)),
            pl.BlockSpec((c, vw // 2), lambda n: (n, vcol + 1)),
            pl.BlockSpec((c, vw // 2), lambda n: (n, gcol)),
            pl.BlockSpec((c, vw // 2), lambda n: (n, gcol + 1)),
            full3((RET_HEADS, c, c)),
            full3((RET_HEADS, c, 1)),
            full3((RET_HEADS, c, 1)),
            pl.BlockSpec(memory_space=pltpu.SMEM),
            pl.BlockSpec((1, vw), lambda n: (0, 0)),
        ],
        out_specs=pl.BlockSpec((c, vw), lambda n: (n, 0)),
        out_shape=jax.ShapeDtypeStruct((t, vw), BF16),
        scratch_shapes=[pltpu.VMEM((RET_HEADS, RET_QK_DIM, RET_V_DIM), F32),
                        pltpu.VMEM((RET_HEADS, c, c), BF16),
                        pltpu.VMEM((RET_HEADS, c, RET_V_DIM), F32)],
        compiler_params=_params(("arbitrary",)),
        name="retention",
    )(proj, proj, proj, proj, proj, proj, dmat, zeta, xi, gch, gn_g.reshape(1, vw))


MERGE_ROW_CHUNK = PERM_TILE


def _merge_kernel(dils, o1_ref, l1_ref, o2_ref, l2_ref, o3_ref, l3_ref, yb_ref, ga_ref, gb_ref,
                  pa_ref, pb_ref, out_ref):
    tm = out_ref.shape[0]
    width = o1_ref.shape[1]
    perms = [_perm_matrix(d, False) for d in dils]

    def token_order(res_ref, gi, c):
        n = PERM_TILE // dils[gi]
        blk = res_ref[:, c * n:(c + 1) * n, :].reshape(PERM_TILE, width)
        return jnp.dot(perms[gi], blk, preferred_element_type=F32)

    for c in range(tm // MERGE_ROW_CHUNK):
        rows = slice(c * MERGE_ROW_CHUNK, (c + 1) * MERGE_ROW_CHUNK)
        o2 = token_order(o2_ref, 0, c)
        o3 = token_order(o3_ref, 1, c)
        l1 = l1_ref[rows, :]
        l2 = l2_ref[rows, :]
        l3 = l3_ref[rows, :]
        mx = jnp.maximum(jnp.maximum(l1, l2), l3)
        e1 = jnp.exp(l1 - mx)
        e2 = jnp.exp(l2 - mx)
        e3 = jnp.exp(l3 - mx)
        den = e1 + e2 + e3
        a1, a2, a3 = e1 / den, e2 / den, e3 / den
        pieces = []
        for hh in range(HEADS_PER_GROUP):
            sl = slice(hh * HEAD_DIM, (hh + 1) * HEAD_DIM)
            ya = (a1[:, hh:hh + 1] * o1_ref[rows, sl] + a2[:, hh:hh + 1] * o2[:, sl]
                  + a3[:, hh:hh + 1] * o3[:, sl])
            pieces.append(ya.astype(BF16))
        ya = jnp.concatenate(pieces, axis=1)
        za = jnp.dot(ya, pa_ref[...], preferred_element_type=F32)
        zb = jnp.dot(yb_ref[rows, :], pb_ref[...], preferred_element_type=F32)
        out_ref[rows, :] = (ga_ref[rows, :].astype(F32) * za
                            + gb_ref[rows, :].astype(F32) * zb).astype(out_ref.dtype)


def _merge(o1, l1, o2, l2, o3, l3, yb, proj, ga_col, gb_col, pa, pb, dils, tm=512, tn=1024):
    t = o1.shape[0]
    wa = o1.shape[1]
    wb = yb.shape[1]
    n = pa.shape[1]
    hg = HEADS_PER_GROUP
    ratio = tn // COLBLK
    o_spec = lambda: pl.BlockSpec((tm, wa), lambda j, i: (i, 0))
    l_spec = lambda: pl.BlockSpec((tm, hg), lambda j, i: (i, 0))
    res_spec = lambda d: pl.BlockSpec((d, tm // d, wa), lambda j, i: (0, i, 0))
    o2 = o2.reshape(dils[0], t // dils[0], wa)
    o3 = o3.reshape(dils[1], t // dils[1], wa)
    return pl.pallas_call(
        functools.partial(_merge_kernel, dils),
        grid=(n // tn, t // tm),
        in_specs=[
            o_spec(), l_spec(), res_spec(dils[0]), l_spec(), res_spec(dils[1]), l_spec(),
            pl.BlockSpec((tm, wb), lambda j, i: (i, 0)),
            pl.BlockSpec((tm, tn), lambda j, i: (i, ga_col // ratio + j)),
            pl.BlockSpec((tm, tn), lambda j, i: (i, gb_col // ratio + j)),
            pl.BlockSpec((wa, tn), lambda j, i: (0, j)),
            pl.BlockSpec((wb, tn), lambda j, i: (0, j)),
        ],
        out_specs=pl.BlockSpec((tm, tn), lambda j, i: (i, j)),
        out_shape=jax.ShapeDtypeStruct((t, n), BF16),
        compiler_params=_params(("parallel", "parallel")),
        name="merge",
    )(o1, l1, o2, l2, o3, l3, yb, proj, proj, pa, pb)


def _oproj_kernel(x_ref, m_ref, w_ref, g_ref, o_ref):
    z = jnp.dot(m_ref[...], w_ref[...], preferred_element_type=F32)
    o_ref[...] = x_ref[...] + g_ref[...] * z


def _oproj(x, merged, w_bf, mod, gate_blk, tm=512, tn=1024):
    t, d_model = x.shape
    k = merged.shape[1]
    per = d_model // tn
    return pl.pallas_call(
        _oproj_kernel,
        grid=(d_model // tn, t // tm),
        in_specs=[
            pl.BlockSpec((tm, tn), lambda j, i: (i, j)),
            pl.BlockSpec((tm, k), lambda j, i: (i, 0)),
            pl.BlockSpec((k, tn), lambda j, i: (0, j)),
            pl.BlockSpec((1, tn), lambda j, i: (0, gate_blk * per + j)),
        ],
        out_specs=pl.BlockSpec((tm, tn), lambda j, i: (i, j)),
        out_shape=jax.ShapeDtypeStruct((t, d_model), F32),
        compiler_params=_params(("parallel", "parallel")),
        name="oproj",
    )(x, merged, w_bf, mod)


def _pack_pair(lo, hi):
    lo_b = pltpu.bitcast(lo.astype(BF16).astype(F32), U32)
    hi_b = pltpu.bitcast(hi.astype(BF16).astype(F32), U32)
    return (lo_b >> 16) | (hi_b & jnp.uint32(0xFFFF0000))


def _unpack_pair(w):
    lo = pltpu.bitcast(w << 16, F32)
    hi = pltpu.bitcast(w & jnp.uint32(0xFFFF0000), F32)
    return lo, hi


def _route_kernel(x_ref, g_ref, sc_ref, sh_ref, wt_ref, rb_ref, h_ref, hp_ref, idx_ref, rank_ref,
                  wgt_ref, cnt_ref):
    @pl.when(pl.program_id(0) == 0)
    def _():
        cnt_ref[...] = jnp.zeros_like(cnt_ref)

    x = x_ref[...]
    tm, d_model = x.shape
    inv = lax.rsqrt(jnp.mean(x * x, axis=-1, keepdims=True) + RMS_EPS)
    h = (x * inv * g_ref[...]) * (1.0 + sc_ref[...]) + sh_ref[...]
    h_ref[...] = h.astype(h_ref.dtype)
    half = d_model // 2
    hp_ref[...] = _pack_pair(h[:, :half], h[:, half:])

    ne = N_EXPERTS
    per = ne // N_GROUPS
    logits = lax.dot_general(wt_ref[...], h, (((1,), (1,)), ((), ())),
                             precision=lax.Precision.HIGHEST,
                             preferred_element_type=F32)
    scores = jax.nn.sigmoid(logits)
    sel = scores + rb_ref[...]
    eidx = lax.broadcasted_iota(I32, (ne, tm), 0).astype(F32)
    minus_inf = -jnp.inf

    sel3 = sel.reshape(N_GROUPS, per, tm)
    sub = lax.broadcasted_iota(I32, (N_GROUPS, per, tm), 1).astype(F32)
    m1 = jnp.max(sel3, axis=1, keepdims=True)
    first = jnp.min(jnp.where(sel3 == m1, sub, float(per)), axis=1, keepdims=True)
    m2 = jnp.max(jnp.where(sub == first, minus_inf, sel3), axis=1, keepdims=True)
    grp = (m1 + m2).reshape(N_GROUPS, tm)

    gidx = lax.broadcasted_iota(I32, (N_GROUPS, tm), 0).astype(F32)
    gmask = jnp.zeros((N_GROUPS, tm), F32)
    work = grp
    for _ in range(TOPK_GROUPS):
        mx = jnp.max(work, axis=0, keepdims=True)
        pick = jnp.min(jnp.where(work == mx, gidx, float(N_GROUPS)), axis=0, keepdims=True)
        hit = gidx == pick
        gmask = jnp.where(hit, 1.0, gmask)
        work = jnp.where(hit, minus_inf, work)
    emask = jnp.broadcast_to(gmask.reshape(N_GROUPS, 1, tm), (N_GROUPS, per, tm)).reshape(ne, tm)

    work = jnp.where(emask > 0.0, sel, minus_inf)
    onehot = jnp.zeros((ne, tm), F32)
    idx_rows, w_rows = [], []
    for _ in range(TOP_K):
        mx = jnp.max(work, axis=0, keepdims=True)
        pick = jnp.min(jnp.where(work == mx, eidx, float(ne)), axis=0, keepdims=True)
        hit = eidx == pick
        onehot = jnp.where(hit, 1.0, onehot)
        work = jnp.where(hit, minus_inf, work)
        idx_rows.append(pick)
        w_rows.append(jnp.sum(jnp.where(hit, scores, 0.0), axis=0, keepdims=True))
    w_all = jnp.concatenate(w_rows, axis=0)
    wgt_ref[...] = w_all / jnp.sum(w_all, axis=0, keepdims=True) * ROUTED_SCALE
    idx_ref[...] = jnp.concatenate(idx_rows, axis=0).astype(I32)

    ra = lax.broadcasted_iota(I32, (tm, tm), 0)
    rb = lax.broadcasted_iota(I32, (tm, tm), 1)
    tri = jnp.where(ra <= rb, 1.0, 0.0).astype(BF16)
    incl = jnp.dot(onehot.astype(BF16), tri, preferred_element_type=F32)
    before = incl - onehot + cnt_ref[...]
    rank_rows = [jnp.sum(jnp.where(eidx == idx_rows[kk], before, 0.0), axis=0, keepdims=True)
                 for kk in range(TOP_K)]
    rank_ref[...] = jnp.concatenate(rank_rows, axis=0).astype(I32)
    cnt_ref[...] = cnt_ref[...] + jnp.sum(onehot, axis=1, keepdims=True)


def _route(x1, g, mod, sc_blk, sh_blk, router_w, router_bias, tm=256):
    t, d_model = x1.shape
    ne = N_EXPERTS
    vec = lambda k: pl.BlockSpec((1, d_model), lambda i, k=k: (0, k))
    tok = lambda: pl.BlockSpec((TOP_K, tm), lambda i: (0, i))
    return pl.pallas_call(
        _route_kernel,
        grid=(t // tm,),
        in_specs=[pl.BlockSpec((tm, d_model), lambda i: (i, 0)),
                  pl.BlockSpec((1, d_model), lambda i: (0, 0)),
                  vec(sc_blk), vec(sh_blk),
                  pl.BlockSpec((ne, d_model), lambda i: (0, 0)),
                  pl.BlockSpec((ne, 1), lambda i: (0, 0))],
        out_specs=[pl.BlockSpec((tm, d_model), lambda i: (i, 0)),
                   pl.BlockSpec((tm, d_model // 2), lambda i: (i, 0)),
                   tok(), tok(), tok(),
                   pl.BlockSpec((ne, 1), lambda i: (0, 0))],
        out_shape=[jax.ShapeDtypeStruct((t, d_model), BF16),
                   jax.ShapeDtypeStruct((t, d_model // 2), U32),
                   jax.ShapeDtypeStruct((TOP_K, t), I32),
                   jax.ShapeDtypeStruct((TOP_K, t), I32),
                   jax.ShapeDtypeStruct((TOP_K, t), F32),
                   jax.ShapeDtypeStruct((ne, 1), F32)],
        compiler_params=_params(("arbitrary",)),
        name="route",
    )(x1, g.reshape(1, d_model), mod, mod, router_w.T, router_bias.reshape(ne, 1))


SUBLANES = 8


def _pad_chunks(bm):
    sizes, s = [], bm // 2
    while s >= SUBLANES:
        sizes.append(s)
        s //= 2
    return sizes


def _dispatch_kernel(bm, pos_ref, fill_start_ref, fill_len_ref, nv_ref, hp_ref, xs_ref, sem, pad_sem):
    tm = hp_ref.shape[0]

    @pl.when(pl.program_id(0) == 0)
    def _():
        def pad_copies(action):
            def per_expert(e, carry):
                start = fill_start_ref[e]
                n = fill_len_ref[e]
                head = (-start) & (SUBLANES - 1)
                for r in range(SUBLANES - 1):
                    @pl.when(r < head)
                    def _(r=r):
                        action(pltpu.make_async_copy(hp_ref.at[pl.ds(0, 1)],
                                                     xs_ref.at[pl.ds(start + r, 1)], pad_sem))

                start = start + head
                n = n - head
                for size in _pad_chunks(bm):
                    take = (n & size) != 0

                    @pl.when(take)
                    def _(start=start, size=size):
                        dst = pl.multiple_of(start, SUBLANES)
                        action(pltpu.make_async_copy(hp_ref.at[pl.ds(0, size)],
                                                     xs_ref.at[pl.ds(dst, size)], pad_sem))

                    start = start + jnp.where(take, size, 0)
                return carry

            lax.fori_loop(0, N_EXPERTS, per_expert, 0)

            def unused_block(b, carry):
                dst = pl.multiple_of(b * bm, bm)
                action(pltpu.make_async_copy(hp_ref.at[pl.ds(0, bm)],
                                             xs_ref.at[pl.ds(dst, bm)], pad_sem))
                return carry

            lax.fori_loop(nv_ref[0], xs_ref.shape[0] // bm, unused_block, 0)

        pad_copies(lambda cp: cp.start())
        pad_copies(lambda cp: cp.wait())

    def copy_rows(tt, carry):
        for kk in range(TOP_K):
            dst = pos_ref[0, 0, kk * tm + tt]
            pltpu.make_async_copy(hp_ref.at[pl.ds(tt, 1)], xs_ref.at[pl.ds(dst, 1)], sem).start()
        return carry

    lax.fori_loop(0, tm, copy_rows, 0)
    pltpu.make_async_copy(xs_ref.at[pl.ds(0, tm * TOP_K)], xs_ref.at[pl.ds(0, tm * TOP_K)], sem).wait()


def _tile_major(a_t, tm):
    k, t = a_t.shape
    return a_t.reshape(k, t // tm, tm).transpose(1, 0, 2).reshape(t // tm, 1, k * tm)


def _dispatch(hp, pos_t, fill_start, fill_len, n_valid, rows, bm, tm=256):
    t, width = hp.shape
    assert tm >= bm
    smem = lambda: pl.BlockSpec(memory_space=pltpu.SMEM)
    return pl.pallas_call(
        functools.partial(_dispatch_kernel, bm),
        grid=(t // tm,),
        in_specs=[pl.BlockSpec((1, 1, tm * TOP_K), lambda i: (i, 0, 0), memory_space=pltpu.SMEM),
                  smem(), smem(), smem(),
                  pl.BlockSpec((tm, width), lambda i: (i, 0))],
        out_specs=pl.BlockSpec(memory_space=pl.ANY),
        out_shape=jax.ShapeDtypeStruct((rows, width), U32),
        scratch_shapes=[pltpu.SemaphoreType.DMA(()), pltpu.SemaphoreType.DMA(())],
        compiler_params=_params(("arbitrary",)),
        name="dispatch",
    )(_tile_major(pos_t, tm), fill_start, fill_len, n_valid, hp)


INVERT_UNROLL = 8


def _invert_kernel(bm, n_tok, tm, pos_ref, fill_start_ref, fill_len_ref, nv_ref, inv_ref):
    tile = pl.program_id(0)

    @pl.when(tile == 0)
    def _():
        def mark_block(first_row):
            def mark(i, c):
                for u in range(INVERT_UNROLL):
                    inv_ref[first_row + i * INVERT_UNROLL + u] = -1
                return c

            lax.fori_loop(0, bm // INVERT_UNROLL, mark, 0)

        def mark_padding(e, carry):
            @pl.when(fill_len_ref[e] > 0)
            def _():
                mark_block(fill_start_ref[e] + fill_len_ref[e] - bm)

            return carry

        lax.fori_loop(0, N_EXPERTS, mark_padding, 0)

        def mark_unused(blk, c):
            mark_block(blk * bm)
            return c

        lax.fori_loop(nv_ref[0], inv_ref.shape[0] // bm, mark_unused, 0)

    def body(tt, carry):
        for kk in range(TOP_K):
            inv_ref[pos_ref[0, 0, kk * tm + tt]] = kk * n_tok + tile * tm + tt
        return carry

    lax.fori_loop(0, tm, body, 0, unroll=INVERT_UNROLL)


def _invert(pos_t, fill_start, fill_len, n_valid, rows, bm, tm=1024):
    _, t = pos_t.shape
    smem = lambda: pl.BlockSpec(memory_space=pltpu.SMEM)
    return pl.pallas_call(
        functools.partial(_invert_kernel, bm, t, tm),
        grid=(t // tm,),
        in_specs=[pl.BlockSpec((1, 1, tm * TOP_K), lambda i: (i, 0, 0), memory_space=pltpu.SMEM),
                  smem(), smem(), smem()],
        out_specs=smem(),
        out_shape=jax.ShapeDtypeStruct((rows,), I32),
        compiler_params=_params(("arbitrary",)),
        name="invert",
    )(_tile_major(pos_t, tm), fill_start, fill_len, n_valid)


Y_BUFFERS = 3


def _experts_kernel(bm, n_slot_rows, seg_ref, sege_ref, nv_ref, inv_ref, x_ref, wg_hbm, wu_hbm, wd_hbm,
                    ysl_ref, wg_f32, wu_f32, wd_f32, wg_bf, wu_bf, wd_bf, y0_ref, y1_ref, y2_ref,
                    sems, ysems):
    b = pl.program_id(0)
    nb = seg_ref.shape[0]
    n_valid = nv_ref[0]
    seg = seg_ref[jnp.minimum(b, nb - 1)]
    slot = seg % 2
    first = (b < nb) & ((b == 0) | (seg_ref[jnp.clip(b - 1, 0, nb - 1)] != seg))
    ybufs = (y0_ref, y1_ref, y2_ref)

    def weight_copies(which_seg, which_slot):
        e = sege_ref[which_seg]
        return [pltpu.make_async_copy(src.at[e], dst.at[which_slot], sems.at[which_slot])
                for src, dst in ((wg_hbm, wg_f32), (wu_hbm, wu_f32), (wd_hbm, wd_f32))]

    @pl.when(b == 0)
    def _():
        for cp in weight_copies(0, 0):
            cp.start()

    @pl.when(first)
    def _():
        for cp in weight_copies(seg, slot):
            cp.wait()
        wg_bf[...] = wg_f32[slot].astype(BF16)
        wu_bf[...] = wu_f32[slot].astype(BF16)
        wd_bf[...] = wd_f32[slot].astype(BF16)

        @pl.when(seg + 1 < nv_ref[1])
        def _():
            for cp in weight_copies(seg + 1, 1 - slot):
                cp.start()

    def compute(y_ref):
        lo, hi = _unpack_pair(x_ref[...])
        half = lo.shape[1]
        lo = lo.astype(BF16)
        hi = hi.astype(BF16)
        gate = (jnp.dot(lo, wg_bf[:half, :], preferred_element_type=F32)
                + jnp.dot(hi, wg_bf[half:, :], preferred_element_type=F32))
        up = (jnp.dot(lo, wu_bf[:half, :], preferred_element_type=F32)
              + jnp.dot(hi, wu_bf[half:, :], preferred_element_type=F32))
        act = (_silu(gate) * up).astype(BF16)
        y = jnp.dot(act, wd_bf[...], preferred_element_type=F32)
        y_ref[...] = _pack_pair(y[:, :half], y[:, half:])

    def scatter(block, parity):
        base = block * bm
        spare = n_slot_rows + parity * bm
        for r in range(bm):
            d = inv_ref[base + r]
            d = jnp.where(d < 0, spare + r, d)
            pltpu.make_async_copy(ybufs[parity].at[pl.ds(r, 1)], ysl_ref.at[pl.ds(d, 1)],
                                  ysems.at[parity]).start()

    for p in range(Y_BUFFERS):
        mine = (b % Y_BUFFERS) == p
        before = (p - 1) % Y_BUFFERS

        @pl.when(mine & (b >= Y_BUFFERS) & (b - Y_BUFFERS < n_valid))
        def _(p=p):
            pltpu.make_async_copy(ybufs[p], ysl_ref.at[pl.ds(0, bm)], ysems.at[p]).wait()

        @pl.when(mine & (b >= 1) & (b < n_valid))
        def _(p=p, before=before):
            scatter(b - 1, before)
            compute(ybufs[p])

        @pl.when(mine & (b >= 1) & (b == n_valid))
        def _(before=before):
            scatter(b - 1, before)

    @pl.when(b == 0)
    def _():
        spare_fill = [pltpu.make_async_copy(
            x_ref, ysl_ref.at[pl.ds(n_slot_rows + parity * bm, bm)], ysems.at[parity])
            for parity in range(Y_BUFFERS)]
        for cp in spare_fill:
            cp.start()
        for cp in spare_fill:
            cp.wait()
        compute(ybufs[0])


def _experts(xs, inv, seg_of, seg_e, n_valid, wg, wu, wd, n_tok, bm=EXPERT_ROWS):
    rows, width = xs.shape
    _, d_model, de = wg.shape
    nb = rows // bm
    n_slot_rows = TOP_K * n_tok
    row_map = lambda b, sg, se, nv, iv: (jnp.minimum(b, nv[0] - 1), 0)
    hbm = lambda: pl.BlockSpec(memory_space=pl.ANY)
    grid_spec = pltpu.PrefetchScalarGridSpec(
        num_scalar_prefetch=4,
        grid=(nb + Y_BUFFERS,),
        in_specs=[pl.BlockSpec((bm, width), row_map), hbm(), hbm(), hbm()],
        out_specs=hbm(),
        scratch_shapes=[pltpu.VMEM((2, d_model, de), F32),
                        pltpu.VMEM((2, d_model, de), F32),
                        pltpu.VMEM((2, de, d_model), F32),
                        pltpu.VMEM((d_model, de), BF16),
                        pltpu.VMEM((d_model, de), BF16),
                        pltpu.VMEM((de, d_model), BF16),
                        ]
                       + [pltpu.VMEM((bm, width), U32)] * Y_BUFFERS
                       + [pltpu.SemaphoreType.DMA((2,)),
                          pltpu.SemaphoreType.DMA((Y_BUFFERS,))],
    )
    return pl.pallas_call(
        functools.partial(_experts_kernel, bm, n_slot_rows),
        grid_spec=grid_spec,
        out_shape=jax.ShapeDtypeStruct((n_slot_rows + Y_BUFFERS * bm, width), U32),
        compiler_params=_params(("arbitrary",)),
        name="experts",
    )(seg_of, seg_e, n_valid, inv, xs, wg, wu, wd)


def _combine_kernel(x_ref, h_ref, wt_ref, g_ref, sg_ref, su_ref, sd_ref, *rest):
    y_refs, o_ref = rest[:TOP_K], rest[TOP_K]
    h = h_ref[...]
    act = (_silu(jnp.dot(h, sg_ref[...], preferred_element_type=F32))
           * jnp.dot(h, su_ref[...], preferred_element_type=F32)).astype(BF16)
    shared = jnp.dot(act, sd_ref[...], preferred_element_type=F32)
    half = y_refs[0].shape[1]
    wt = wt_ref[...]
    lo_acc = shared[:, :half]
    hi_acc = shared[:, half:]
    for kk in range(TOP_K):
        lo, hi = _unpack_pair(y_refs[kk][...])
        wk = wt[:, kk:kk + 1]
        lo_acc = lo_acc + wk * lo
        hi_acc = hi_acc + wk * hi
    g = g_ref[...]
    o_ref[:, :half] = x_ref[:, :half] + g[:, :half] * lo_acc
    o_ref[:, half:] = x_ref[:, half:] + g[:, half:] * hi_acc


def _combine(x1, h2, wts, mod, gate_blk, sg, su, sd, ysl, tm=256):
    t, d_model = x1.shape
    ds_ = sg.shape[1]
    width = ysl.shape[1]
    tiles = t // tm
    slot = lambda kk: pl.BlockSpec((tm, width), lambda i, kk=kk: (kk * tiles + i, 0))
    return pl.pallas_call(
        _combine_kernel,
        grid=(tiles,),
        in_specs=[pl.BlockSpec((tm, d_model), lambda i: (i, 0)),
                  pl.BlockSpec((tm, d_model), lambda i: (i, 0)),
                  pl.BlockSpec((tm, TOP_K), lambda i: (i, 0)),
                  pl.BlockSpec((1, d_model), lambda i: (0, gate_blk)),
                  pl.BlockSpec((d_model, ds_), lambda i: (0, 0)),
                  pl.BlockSpec((d_model, ds_), lambda i: (0, 0)),
                  pl.BlockSpec((ds_, d_model), lambda i: (0, 0))]
                 + [slot(kk) for kk in range(TOP_K)],
        out_specs=pl.BlockSpec((tm, d_model), lambda i: (i, 0)),
        out_shape=jax.ShapeDtypeStruct((t, d_model), F32),
        compiler_params=_params(("parallel",)),
        name="combine",
    )(x1, h2, wts, mod, sg, su, sd, *([ysl] * TOP_K))


def _layout_kernel(bm, cnt_ref, idx_ref, rank_ref, pos_ref, seg_ref, sege_ref, nv_ref, fs_ref, fl_ref):
    shift = bm.bit_length() - 1
    pos_ref[...] = rank_ref[...]

    def per_expert(e, carry):
        start, blk, seg = carry
        cnt = cnt_ref[e]
        nblk = (cnt + (bm - 1)) >> shift
        pos_ref[...] = pos_ref[...] + jnp.where(idx_ref[...] == e, start, 0)

        def mark(b, c):
            seg_ref[blk + b] = seg
            return c

        lax.fori_loop(0, nblk, mark, 0)

        @pl.when(nblk > 0)
        def _():
            sege_ref[seg] = e

        fs_ref[e] = start + cnt
        fl_ref[e] = (nblk << shift) - cnt
        return start + (nblk << shift), blk + nblk, seg + jnp.where(nblk > 0, 1, 0)

    zero = jnp.int32(0)
    _, n_valid, n_seg = lax.fori_loop(0, N_EXPERTS, per_expert, (zero, zero, zero))
    nv_ref[0] = n_valid
    nv_ref[1] = n_seg

    def tail_blocks(b, c):
        seg_ref[b] = n_seg - 1
        return c

    lax.fori_loop(n_valid, seg_ref.shape[0], tail_blocks, 0)

    def tail_segs(s, c):
        sege_ref[s] = N_EXPERTS - 1
        return c

    lax.fori_loop(n_seg, N_EXPERTS, tail_segs, 0)


def _layout(counts, idx_t, rank_t, bm, n_blocks):
    assert bm & (bm - 1) == 0
    k, t = idx_t.shape
    smem = lambda: pl.BlockSpec(memory_space=pltpu.SMEM)
    full = lambda: pl.BlockSpec((k, t), lambda: (0, 0))
    return pl.pallas_call(
        functools.partial(_layout_kernel, bm),
        in_specs=[smem(), full(), full()],
        out_specs=[full(), smem(), smem(), smem(), smem(), smem()],
        out_shape=[jax.ShapeDtypeStruct((k, t), I32),
                   jax.ShapeDtypeStruct((n_blocks,), I32),
                   jax.ShapeDtypeStruct((N_EXPERTS,), I32),
                   jax.ShapeDtypeStruct((2,), I32),
                   jax.ShapeDtypeStruct((N_EXPERTS,), I32),
                   jax.ShapeDtypeStruct((N_EXPERTS,), I32)],
        name="layout",
    )(counts.reshape(-1).astype(I32), idx_t, rank_t)


def _layer(x, c, rel_bias, w_ada, b_ada, ln1_g, w_in, q_norm_g, k_norm_g, ret_gn_g, p_a, p_b, w_o,
           ln2_g, router_w, router_bias, w_gate_e, w_up_e, w_down_e, w_gate_s, w_up_s, w_down_s):
    t, d_model = x.shape
    dils = tuple(d for _, d in DILATED_GROUPS)

    mod = _ada(c.reshape(d_model), w_ada, b_ada)
    h_orders = _norm1(x, ln1_g, mod, dils[1:])
    cos_tab, sin_tab = _rotary_tables(t)
    projs = []
    for order, (cols, epis) in enumerate(_inproj_plan(d_model)):
        projs.append(_inproj(h_orders[order], w_in, cols, epis, q_norm_g, k_norm_g, cos_tab, sin_tab,
                             f"inproj_d{dils[order]}"))
    proj = projs[0]

    attn = [_attn_group(projs[gi], rel_bias, gi, win, dil, 0, 1, 2)
            for gi, (win, dil) in enumerate(DILATED_GROUPS)]
    base = 3
    rq = RET_HEADS * RET_QK_DIM // COLBLK
    vw_blk = RET_HEADS * RET_V_DIM // COLBLK
    qcol = base
    kcol = base + rq
    vcol_blk = base + 2 * rq
    gcol_blk = vcol_blk + vw_blk
    ga_blk = gcol_blk + vw_blk
    gb_blk = ga_blk + d_model // COLBLK
    y_b = _retention(proj, ret_gn_g, qcol, kcol, vcol_blk, gcol_blk)
    (o1, l1), (o2, l2), (o3, l3) = attn
    l2 = _from_residue_major(l2, dils[1])
    l3 = _from_residue_major(l3, dils[2])
    merged = _merge(o1, l1, o2, l2, o3, l3, y_b, proj, ga_blk, gb_blk,
                    p_a.astype(BF16), p_b.astype(BF16), dils[1:])
    x1 = _oproj(x, merged, w_o.astype(BF16), mod, 2)

    h2, h2p, idx_t, rank_t, wgt_t, counts = _route(x1, ln2_g, mod, 4, 3, router_w, router_bias)
    bm = EXPERT_ROWS
    n_blocks = (t * TOP_K + N_EXPERTS * (bm - 1) + bm - 1) // bm
    pos_t, seg_of, seg_e, n_valid, fill_start, fill_len = _layout(counts, idx_t, rank_t, bm, n_blocks)
    xs = _dispatch(h2p, pos_t, fill_start, fill_len, n_valid, n_blocks * bm, bm)
    inv = _invert(pos_t, fill_start, fill_len, n_valid, n_blocks * bm, bm)
    ysl = _experts(xs, inv, seg_of, seg_e, n_valid, w_gate_e, w_up_e, w_down_e, t)
    return _combine(x1, h2, wgt_t.T, mod, 5, w_gate_s.astype(BF16), w_up_s.astype(BF16),
                    w_down_s.astype(BF16), ysl)


def kernel(x, c, rel_bias, w_ada, b_ada, ln1_g, w_in, q_norm_g, k_norm_g, ret_gn_g, p_a, p_b, w_o,
           ln2_g, router_w, router_bias, w_gate_e, w_up_e, w_down_e, w_gate_s, w_up_s, w_down_s):
    b, s, d_model = x.shape
    depth = w_ada.shape[0]
    outs = []
    for bi in range(b):
        xb = x[bi]
        for l in range(depth):
            xb = _layer(xb, c[bi], rel_bias, w_ada[l], b_ada[l], ln1_g[l], w_in[l], q_norm_g[l],
                        k_norm_g[l], ret_gn_g[l], p_a[l], p_b[l], w_o[l], ln2_g[l], router_w[l],
                        router_bias[l], w_gate_e[l], w_up_e[l], w_down_e[l], w_gate_s[l],
                        w_up_s[l], w_down_s[l])
        outs.append(xb)
    return jnp.stack(outs, axis=0)
```

```python
import functools

import numpy as np
import jax
import jax.numpy as jnp
from jax import lax
from jax.experimental import pallas as pl
from jax.experimental.pallas import tpu as pltpu

F32 = jnp.float32
BF16 = jnp.bfloat16
U32 = jnp.uint32
I32 = jnp.int32

HEAD_DIM = 128
DILATED_GROUPS = ((128, 1), (512, 4), (2048, 16))
HEADS_PER_GROUP = 8
N_HEADS_A = HEADS_PER_GROUP * len(DILATED_GROUPS)
A_GROUP_WIDTH = HEADS_PER_GROUP * HEAD_DIM
ATTN_BLOCK = 128
NUM_BUCKETS = 32
MAX_DISTANCE = 2048
NEG_INF = -1e30
RET_HEADS = 8
RET_QK_DIM = 128
RET_V_DIM = 256
RET_CHUNK = 128
ROPE_BASE = 10000.0
GN_EPS = 1e-5
N_EXPERTS = 64
N_GROUPS = 8
TOPK_GROUPS = 4
TOP_K = 8
ROUTED_SCALE = 2.5
RMS_EPS = 1e-6

LANE = 128
COLBLK = 1024
VMEM_LIMIT = 56 * 1024 * 1024
EXPERT_ROWS = 256


def _params(sem, vmem=VMEM_LIMIT):
    return pltpu.CompilerParams(dimension_semantics=sem, vmem_limit_bytes=vmem)


def _sigmoid(v):
    return 0.5 * jnp.tanh(0.5 * v) + 0.5


def _silu(v):
    return v * _sigmoid(v)


def _ada_kernel(c_ref, w_ref, b_ref, o_ref):
    sc = _silu(c_ref[...])
    o_ref[...] = jnp.sum(w_ref[...] * sc, axis=0, keepdims=True) + b_ref[...]


def _ada(c, w, b, tn=512):
    d, n = w.shape
    return pl.pallas_call(
        _ada_kernel,
        grid=(n // tn,),
        in_specs=[pl.BlockSpec((d, 1), lambda j: (0, 0)),
                  pl.BlockSpec((d, tn), lambda j: (0, j)),
                  pl.BlockSpec((1, tn), lambda j: (0, j))],
        out_specs=pl.BlockSpec((1, tn), lambda j: (0, j)),
        out_shape=jax.ShapeDtypeStruct((1, n), F32),
        compiler_params=_params(("parallel",)),
        name="ada",
    )(c.reshape(d, 1), w, b.reshape(1, n))


PERM_TILE = 256


def _perm_matrix(d, to_residue):
    tm = PERM_TILE
    n = tm // d
    assert d & (d - 1) == 0 and n & (n - 1) == 0
    ii = lax.broadcasted_iota(I32, (tm, tm), 0)
    jj = lax.broadcasted_iota(I32, (tm, tm), 1)
    if to_residue:
        src = (ii & (n - 1)) * d + (ii >> (n.bit_length() - 1))
    else:
        src = (ii & (d - 1)) * n + (ii >> (d.bit_length() - 1))
    return jnp.where(jj == src, 1.0, 0.0).astype(BF16)


def _norm1_kernel(dils, x_ref, g_ref, sc_ref, sh_ref, o_ref, *res_refs):
    x = x_ref[...]
    inv = lax.rsqrt(jnp.mean(x * x, axis=-1, keepdims=True) + RMS_EPS)
    h = ((x * inv * g_ref[...]) * (1.0 + sc_ref[...]) + sh_ref[...]).astype(o_ref.dtype)
    o_ref[...] = h
    tm, width = h.shape
    for d, r_ref in zip(dils, res_refs):
        perm = _perm_matrix(d, True)
        n = PERM_TILE // d
        for s in range(tm // PERM_TILE):
            sub = h[s * PERM_TILE:(s + 1) * PERM_TILE, :]
            y = jnp.dot(perm, sub, preferred_element_type=F32).astype(r_ref.dtype)
            r_ref[:, s * n:(s + 1) * n, :] = y.reshape(d, n, width)


def _norm1(x, g, mod, dils, tm=512):
    t, d_model = x.shape
    vec = lambda k: pl.BlockSpec((1, d_model), lambda i, k=k: (0, k))
    out_shapes = [jax.ShapeDtypeStruct((t, d_model), BF16)]
    out_specs = [pl.BlockSpec((tm, d_model), lambda i: (i, 0))]
    for d in dils:
        out_shapes.append(jax.ShapeDtypeStruct((d, t // d, d_model), BF16))
        out_specs.append(pl.BlockSpec((d, tm // d, d_model), lambda i: (0, i, 0)))
    outs = pl.pallas_call(
        functools.partial(_norm1_kernel, dils),
        grid=(t // tm,),
        in_specs=[pl.BlockSpec((tm, d_model), lambda i: (i, 0)),
                  pl.BlockSpec((1, d_model), lambda i: (0, 0)),
                  vec(1), vec(0)],
        out_specs=out_specs,
        out_shape=out_shapes,
        compiler_params=_params(("parallel",)),
        name="norm1",
    )(x, g.reshape(1, d_model), mod, mod)
    return [o.reshape(t, d_model) for o in outs]


def _from_residue_major(a, d):
    t, w = a.shape
    return a.reshape(d, t // d, w).transpose(1, 0, 2).reshape(t, w)


EPI_QNORM, EPI_KNORM, EPI_PLAIN, EPI_ROT_Q, EPI_ROT_K, EPI_SILU, EPI_SIGMOID = range(7)
INPROJ_ROW_CHUNK = 256


def _inproj_kernel(epis_present, colblk_ref, epi_ref, h_ref, w_ref, qg_ref, kg_ref, cos_ref, sin_ref,
                   o_ref, wbf_ref):
    del colblk_ref
    epi = epi_ref[pl.program_id(0)]
    tm = h_ref.shape[0]
    nh = o_ref.shape[1] // HEAD_DIM

    @pl.when(pl.program_id(1) == 0)
    def _():
        wbf_ref[...] = w_ref[...].astype(BF16)

    def head_norm(gain, scale):
        def fn(acc, rows):
            for hh in range(nh):
                sl = slice(hh * HEAD_DIM, (hh + 1) * HEAD_DIM)
                a = acc[:, sl]
                inv = lax.rsqrt(jnp.mean(a * a, axis=-1, keepdims=True) + RMS_EPS)
                o_ref[rows, sl] = ((a * inv * gain) * scale).astype(o_ref.dtype)
        return fn

    def rotary(scale):
        def fn(acc, rows):
            cos = cos_ref[rows, :]
            sin = sin_ref[rows, :]
            for hh in range(nh):
                sl = slice(hh * HEAD_DIM, (hh + 1) * HEAD_DIM)
                a = acc[:, sl]
                rot = pltpu.roll(a, HEAD_DIM // 2, 1)
                o_ref[rows, sl] = ((a * cos + rot * sin) * scale).astype(o_ref.dtype)
        return fn

    def elementwise(f):
        def fn(acc, rows):
            o_ref[rows, :] = f(acc).astype(o_ref.dtype)
        return fn

    epilogues = {
        EPI_QNORM: lambda: head_norm(qg_ref[...], HEAD_DIM ** -0.5),
        EPI_KNORM: lambda: head_norm(kg_ref[...], 1.0),
        EPI_PLAIN: lambda: elementwise(lambda a: a),
        EPI_ROT_Q: lambda: rotary(1.0),
        EPI_ROT_K: lambda: rotary(RET_QK_DIM ** -0.5),
        EPI_SILU: lambda: elementwise(_silu),
        EPI_SIGMOID: lambda: elementwise(_sigmoid),
    }
    for code in epis_present:
        @pl.when(epi == code)
        def _(code=code):
            fn = epilogues[code]()
            for c in range(tm // INPROJ_ROW_CHUNK):
                rows = slice(c * INPROJ_ROW_CHUNK, (c + 1) * INPROJ_ROW_CHUNK)
                acc = jnp.dot(h_ref[rows, :], wbf_ref[...], preferred_element_type=F32)
                fn(acc, rows)


def _inproj_plan(d_model):
    a_blocks = N_HEADS_A * HEAD_DIM // COLBLK
    groups = len(DILATED_GROUPS)
    per_group = a_blocks // groups
    rq = RET_HEADS * RET_QK_DIM // COLBLK
    rv = RET_HEADS * RET_V_DIM // COLBLK
    gd = d_model // COLBLK
    seg_epi = ([EPI_QNORM] * a_blocks + [EPI_KNORM] * a_blocks + [EPI_PLAIN] * a_blocks
               + [EPI_ROT_Q] * rq + [EPI_ROT_K] * rq + [EPI_PLAIN] * rv + [EPI_SILU] * rv
               + [EPI_SIGMOID] * (2 * gd))
    order_of = [0] * len(seg_epi)
    for seg in range(3):
        for blk in range(a_blocks):
            order_of[seg * a_blocks + blk] = blk // per_group
    plans = []
    for order in range(groups):
        cols = [cb for cb in range(len(seg_epi)) if order_of[cb] == order]
        plans.append((cols, [seg_epi[cb] for cb in cols]))
    return plans


def _inproj(h, w, cols, epis, qg, kg, cos_tab, sin_tab, name, tm=1024):
    t, d_model = h.shape
    row = lambda width: pl.BlockSpec((tm, width), lambda j, i, cb, ep: (i, 0))
    one = lambda width: pl.BlockSpec((1, width), lambda j, i, cb, ep: (0, 0))
    grid_spec = pltpu.PrefetchScalarGridSpec(
        num_scalar_prefetch=2,
        grid=(len(cols), t // tm),
        in_specs=[
            row(d_model),
            pl.BlockSpec((d_model, COLBLK), lambda j, i, cb, ep: (0, cb[j])),
            one(HEAD_DIM), one(HEAD_DIM), row(HEAD_DIM), row(HEAD_DIM),
        ],
        out_specs=pl.BlockSpec((tm, COLBLK), lambda j, i, cb, ep: (i, j)),
        scratch_shapes=[pltpu.VMEM((d_model, COLBLK), BF16)],
    )
    return pl.pallas_call(
        functools.partial(_inproj_kernel, tuple(sorted(set(epis)))),
        grid_spec=grid_spec,
        out_shape=jax.ShapeDtypeStruct((t, len(cols) * COLBLK), BF16),
        compiler_params=_params(("arbitrary", "arbitrary")),
        name=name,
    )(jnp.asarray(np.array(cols, np.int32)), jnp.asarray(np.array(epis, np.int32)),
      h, w, qg.reshape(1, HEAD_DIM), kg.reshape(1, HEAD_DIM), cos_tab, sin_tab)


def _rotary_tables(t):
    inv = ROPE_BASE ** (-np.arange(0, RET_QK_DIM, 2, dtype=np.float64) / RET_QK_DIM)
    ang = np.arange(t, dtype=np.float64)[:, None] * inv[None, :]
    cos, sin = np.cos(ang), np.sin(ang)
    cos_tab = np.concatenate([cos, cos], axis=1).astype(np.float32)
    sin_tab = np.concatenate([-sin, sin], axis=1).astype(np.float32)
    return jnp.asarray(cos_tab), jnp.asarray(sin_tab)


def _t5_bucket(dist):
    max_exact = NUM_BUCKETS // 2
    safe = np.maximum(dist, 1).astype(np.float32)
    large = max_exact + (np.log(safe / max_exact) / np.log(MAX_DISTANCE / max_exact)
                         * (NUM_BUCKETS - max_exact)).astype(np.int32)
    return np.where(dist < max_exact, dist, np.minimum(large, NUM_BUCKETS - 1)).astype(np.int32)


def _attn_kernel(head0, w_steps, blocks_per_res, tab_ref, bucket_ref, q_ref, kp_ref, kc_ref,
                 vp_ref, vc_ref, o_ref, lse_ref, bias_ref, band_ref, s_ref, p_ref):
    m_idx = pl.program_id(0)
    blk = ATTN_BLOCK

    @pl.when(m_idx == 0)
    def _():
        bucket = bucket_ref[...]
        for hh in range(HEADS_PER_GROUP):
            bias = jnp.zeros(bucket.shape, F32)
            for b in range(NUM_BUCKETS):
                bias = jnp.where(bucket == b, tab_ref[b, head0 + hh], bias)
            bias_ref[hh] = bias
        a = lax.broadcasted_iota(I32, (blk, 2 * blk), 0)
        cc = lax.broadcasted_iota(I32, (blk, 2 * blk), 1)
        delta = blk + a - cc
        band_ref[...] = jnp.where((delta >= 0) & (delta <= w_steps), 1.0, 0.0)

    prev_thr = jnp.where((m_idx % blocks_per_res) > 0, 0.5, 2.0)
    nt = (((1,), (1,)), ((), ()))
    heads = range(HEADS_PER_GROUP)
    head_cols = [slice(hh * HEAD_DIM, (hh + 1) * HEAD_DIM) for hh in heads]
    for hh, sl in zip(heads, head_cols):
        q = q_ref[:, sl]
        s_p = lax.dot_general(q, kp_ref[:, sl], nt, preferred_element_type=F32)
        s_c = lax.dot_general(q, kc_ref[:, sl], nt, preferred_element_type=F32)
        s_ref[hh, :, :blk] = jnp.where(band_ref[:, :blk] > prev_thr,
                                       s_p + bias_ref[hh, :, :blk], NEG_INF)
        s_ref[hh, :, blk:] = jnp.where(band_ref[:, blk:] > 0.5,
                                       s_c + bias_ref[hh, :, blk:], NEG_INF)
    dens, lses = [], []
    for hh in heads:
        s = s_ref[hh]
        mx = jnp.max(s, axis=-1, keepdims=True)
        p = jnp.exp(s - mx)
        den = jnp.sum(p, axis=-1, keepdims=True)
        p_ref[hh] = p.astype(BF16)
        dens.append(den)
        lses.append(mx + jnp.log(den))
    for hh, sl in zip(heads, head_cols):
        v_both = jnp.concatenate([vp_ref[:, sl], vc_ref[:, sl]], axis=0)
        acc = jnp.dot(p_ref[hh], v_both, preferred_element_type=F32)
        o_ref[:, sl] = (acc / dens[hh]).astype(o_ref.dtype)
    lse_ref[...] = jnp.concatenate(lses, axis=-1)


def _attn_group(proj, rel_bias, gi, window, dilation, qcol, kcol, vcol):
    t = proj.shape[0]
    blk = ATTN_BLOCK
    w_steps = window // dilation
    blocks_per_res = t // dilation // blk
    nblk = t // blk
    a = np.arange(blk)[:, None]
    cc = np.arange(2 * blk)[None, :]
    bucket = _t5_bucket(np.maximum(blk + a - cc, 0) * dilation)

    def prev_map(m):
        return jnp.where(m % blocks_per_res > 0, m - 1, m)

    kern = functools.partial(_attn_kernel, gi * HEADS_PER_GROUP, w_steps, blocks_per_res)
    width = A_GROUP_WIDTH
    return pl.pallas_call(
        kern,
        grid=(nblk,),
        in_specs=[
            pl.BlockSpec(memory_space=pltpu.SMEM),
            pl.BlockSpec((blk, 2 * blk), lambda m: (0, 0)),
            pl.BlockSpec((blk, width), lambda m: (m, qcol)),
            pl.BlockSpec((blk, width), lambda m: (prev_map(m), kcol)),
            pl.BlockSpec((blk, width), lambda m: (m, kcol)),
            pl.BlockSpec((blk, width), lambda m: (prev_map(m), vcol)),
            pl.BlockSpec((blk, width), lambda m: (m, vcol)),
        ],
        out_specs=[pl.BlockSpec((blk, width), lambda m: (m, 0)),
                   pl.BlockSpec((blk, HEADS_PER_GROUP), lambda m: (m, 0))],
        out_shape=[jax.ShapeDtypeStruct((t, width), BF16),
                   jax.ShapeDtypeStruct((t, HEADS_PER_GROUP), F32)],
        scratch_shapes=[pltpu.VMEM((HEADS_PER_GROUP, blk, 2 * blk), F32),
                        pltpu.VMEM((blk, 2 * blk), F32),
                        pltpu.VMEM((HEADS_PER_GROUP, blk, 2 * blk), F32),
                        pltpu.VMEM((HEADS_PER_GROUP, blk, 2 * blk), BF16)],
        compiler_params=_params(("arbitrary",)),
        name=f"attn_d{dilation}",
    )(rel_bias, jnp.asarray(bucket), proj, proj, proj, proj, proj)


def _retention_kernel(q_ref, k_ref, v0_ref, v1_ref, g0_ref, g1_ref, dmat_ref, zeta_ref, xi_ref,
                      gch_ref, gn_ref, o_ref, state_ref, s_ref, cross_ref):
    @pl.when(pl.program_id(0) == 0)
    def _():
        state_ref[...] = jnp.zeros_like(state_ref)

    nt = (((1,), (1,)), ((), ()))
    tn = (((0,), (0,)), ((), ()))
    per_half = RET_HEADS // 2

    def head_refs(hh):
        qs = slice(hh * RET_QK_DIM, (hh + 1) * RET_QK_DIM)
        vs = slice(hh * RET_V_DIM, (hh + 1) * RET_V_DIM)
        hs = slice((hh % per_half) * RET_V_DIM, (hh % per_half + 1) * RET_V_DIM)
        v_ref, g_ref = (v0_ref, g0_ref) if hh < per_half else (v1_ref, g1_ref)
        return qs, vs, hs, v_ref, g_ref

    for hh in range(RET_HEADS):
        qs, _, hs, v_ref, _ = head_refs(hh)
        q = q_ref[:, qs]
        k = k_ref[:, qs]
        v = v_ref[:, hs]
        state = state_ref[hh]
        s = lax.dot_general(q, k, nt, preferred_element_type=F32) * dmat_ref[hh]
        s_ref[hh] = s.astype(BF16)
        cross_ref[hh] = jnp.dot(q, state.astype(BF16), preferred_element_type=F32) * xi_ref[hh]
        vz = (v.astype(F32) * zeta_ref[hh]).astype(BF16)
        upd = lax.dot_general(k, vz, tn, preferred_element_type=F32)
        state_ref[hh] = gch_ref[hh] * state + upd
    for hh in range(RET_HEADS):
        _, vs, hs, v_ref, g_ref = head_refs(hh)
        inner = jnp.dot(s_ref[hh], v_ref[:, hs], preferred_element_type=F32)
        ret = inner + cross_ref[hh]
        mu = jnp.mean(ret, axis=-1, keepdims=True)
        cen = ret - mu
        var = jnp.mean(cen * cen, axis=-1, keepdims=True)
        y = cen * lax.rsqrt(var + GN_EPS) * gn_ref[:, vs]
        o_ref[:, vs] = (y * g_ref[:, hs].astype(F32)).astype(o_ref.dtype)


def _retention_tables():
    c = RET_CHUNK
    hh = np.arange(RET_HEADS, dtype=np.float64)
    log_g = np.log1p(-np.exp2(-5.0 - hh))
    idx = np.arange(c, dtype=np.float64)
    diff = idx[:, None] - idx[None, :]
    dmat = np.where(diff >= 0, np.exp(log_g[:, None, None] * np.maximum(diff, 0.0)), 0.0)
    zeta = np.exp(log_g[:, None] * (c - 1 - idx))[:, :, None]
    xi = np.exp(log_g[:, None] * (idx + 1.0))[:, :, None]
    gch = np.exp(log_g * c)
    f = lambda v: jnp.asarray(v.astype(np.float32))
    return f(dmat), f(zeta), f(xi), f(gch)


def _retention(proj, gn_g, qcol, kcol, vcol, gcol):
    t = proj.shape[0]
    c = RET_CHUNK
    qw = RET_HEADS * RET_QK_DIM
    vw = RET_HEADS * RET_V_DIM
    dmat, zeta, xi, gch = _retention_tables()
    full3 = lambda shp: pl.BlockSpec(shp, lambda n: (0, 0, 0))
    return pl.pallas_call(
        _retention_kernel,
        grid=(t // c,),
        in_specs=[
            pl.BlockSpec((c, qw), lambda n: (n, qcol)),
            pl.BlockSpec((c, qw), lambda n: (n, kcol)),
            pl.BlockSpec((c, vw // 2), lambda n: (n, vcol)),
            pl.BlockSpec((c, vw // 2), lambda n: (n, vcol + 1)),
            pl.BlockSpec((c, vw // 2), lambda n: (n, gcol)),
            pl.BlockSpec((c, vw // 2), lambda n: (n, gcol + 1)),
            full3((RET_HEADS, c, c)),
            full3((RET_HEADS, c, 1)),
            full3((RET_HEADS, c, 1)),
            pl.BlockSpec(memory_space=pltpu.SMEM),
            pl.BlockSpec((1, vw), lambda n: (0, 0)),
        ],
        out_specs=pl.BlockSpec((c, vw), lambda n: (n, 0)),
        out_shape=jax.ShapeDtypeStruct((t, vw), BF16),
        scratch_shapes=[pltpu.VMEM((RET_HEADS, RET_QK_DIM, RET_V_DIM), F32),
                        pltpu.VMEM((RET_HEADS, c, c), BF16),
                        pltpu.VMEM((RET_HEADS, c, RET_V_DIM), F32)],
        compiler_params=_params(("arbitrary",)),
        name="retention",
    )(proj, proj, proj, proj, proj, proj, dmat, zeta, xi, gch, gn_g.reshape(1, vw))


MERGE_ROW_CHUNK = PERM_TILE


def _merge_kernel(dils, o1_ref, l1_ref, o2_ref, l2_ref, o3_ref, l3_ref, yb_ref, ga_ref, gb_ref,
                  pa_ref, pb_ref, out_ref):
    tm = out_ref.shape[0]
    width = o1_ref.shape[1]
    perms = [_perm_matrix(d, False) for d in dils]

    def token_order(res_ref, gi, c):
        n = PERM_TILE // dils[gi]
        blk = res_ref[:, c * n:(c + 1) * n, :].reshape(PERM_TILE, width)
        return jnp.dot(perms[gi], blk, preferred_element_type=F32)

    for c in range(tm // MERGE_ROW_CHUNK):
        rows = slice(c * MERGE_ROW_CHUNK, (c + 1) * MERGE_ROW_CHUNK)
        o2 = token_order(o2_ref, 0, c)
        o3 = token_order(o3_ref, 1, c)
        l1 = l1_ref[rows, :]
        l2 = l2_ref[rows, :]
        l3 = l3_ref[rows, :]
        mx = jnp.maximum(jnp.maximum(l1, l2), l3)
        e1 = jnp.exp(l1 - mx)
        e2 = jnp.exp(l2 - mx)
        e3 = jnp.exp(l3 - mx)
        den = e1 + e2 + e3
        a1, a2, a3 = e1 / den, e2 / den, e3 / den
        pieces = []
        for hh in range(HEADS_PER_GROUP):
            sl = slice(hh * HEAD_DIM, (hh + 1) * HEAD_DIM)
            ya = (a1[:, hh:hh + 1] * o1_ref[rows, sl] + a2[:, hh:hh + 1] * o2[:, sl]
                  + a3[:, hh:hh + 1] * o3[:, sl])
            pieces.append(ya.astype(BF16))
        ya = jnp.concatenate(pieces, axis=1)
        za = jnp.dot(ya, pa_ref[...], preferred_element_type=F32)
        zb = jnp.dot(yb_ref[rows, :], pb_ref[...], preferred_element_type=F32)
        out_ref[rows, :] = (ga_ref[rows, :].astype(F32) * za
                            + gb_ref[rows, :].astype(F32) * zb).astype(out_ref.dtype)


def _merge(o1, l1, o2, l2, o3, l3, yb, proj, ga_col, gb_col, pa, pb, dils, tm=512, tn=1024):
    t = o1.shape[0]
    wa = o1.shape[1]
    wb = yb.shape[1]
    n = pa.shape[1]
    hg = HEADS_PER_GROUP
    ratio = tn // COLBLK
    o_spec = lambda: pl.BlockSpec((tm, wa), lambda j, i: (i, 0))
    l_spec = lambda: pl.BlockSpec((tm, hg), lambda j, i: (i, 0))
    res_spec = lambda d: pl.BlockSpec((d, tm // d, wa), lambda j, i: (0, i, 0))
    o2 = o2.reshape(dils[0], t // dils[0], wa)
    o3 = o3.reshape(dils[1], t // dils[1], wa)
    return pl.pallas_call(
        functools.partial(_merge_kernel, dils),
        grid=(n // tn, t // tm),
        in_specs=[
            o_spec(), l_spec(), res_spec(dils[0]), l_spec(), res_spec(dils[1]), l_spec(),
            pl.BlockSpec((tm, wb), lambda j, i: (i, 0)),
            pl.BlockSpec((tm, tn), lambda j, i: (i, ga_col // ratio + j)),
            pl.BlockSpec((tm, tn), lambda j, i: (i, gb_col // ratio + j)),
            pl.BlockSpec((wa, tn), lambda j, i: (0, j)),
            pl.BlockSpec((wb, tn), lambda j, i: (0, j)),
        ],
        out_specs=pl.BlockSpec((tm, tn), lambda j, i: (i, j)),
        out_shape=jax.ShapeDtypeStruct((t, n), BF16),
        compiler_params=_params(("parallel", "parallel")),
        name="merge",
    )(o1, l1, o2, l2, o3, l3, yb, proj, proj, pa, pb)


def _oproj_kernel(x_ref, m_ref, w_ref, g_ref, o_ref):
    z = jnp.dot(m_ref[...], w_ref[...], preferred_element_type=F32)
    o_ref[...] = x_ref[...] + g_ref[...] * z


def _oproj(x, merged, w_bf, mod, gate_blk, tm=512, tn=1024):
    t, d_model = x.shape
    k = merged.shape[1]
    per = d_model // tn
    return pl.pallas_call(
        _oproj_kernel,
        grid=(d_model // tn, t // tm),
        in_specs=[
            pl.BlockSpec((tm, tn), lambda j, i: (i, j)),
            pl.BlockSpec((tm, k), lambda j, i: (i, 0)),
            pl.BlockSpec((k, tn), lambda j, i: (0, j)),
            pl.BlockSpec((1, tn), lambda j, i: (0, gate_blk * per + j)),
        ],
        out_specs=pl.BlockSpec((tm, tn), lambda j, i: (i, j)),
        out_shape=jax.ShapeDtypeStruct((t, d_model), F32),
        compiler_params=_params(("parallel", "parallel")),
        name="oproj",
    )(x, merged, w_bf, mod)


def _pack_pair(lo, hi):
    lo_b = pltpu.bitcast(lo.astype(BF16).astype(F32), U32)
    hi_b = pltpu.bitcast(hi.astype(BF16).astype(F32), U32)
    return (lo_b >> 16) | (hi_b & jnp.uint32(0xFFFF0000))


def _unpack_pair(w):
    lo = pltpu.bitcast(w << 16, F32)
    hi = pltpu.bitcast(w & jnp.uint32(0xFFFF0000), F32)
    return lo, hi


def _route_kernel(x_ref, g_ref, sc_ref, sh_ref, wt_ref, rb_ref, h_ref, hp_ref, idx_ref, rank_ref,
                  wgt_ref, cnt_ref):
    @pl.when(pl.program_id(0) == 0)
    def _():
        cnt_ref[...] = jnp.zeros_like(cnt_ref)

    x = x_ref[...]
    tm, d_model = x.shape
    inv = lax.rsqrt(jnp.mean(x * x, axis=-1, keepdims=True) + RMS_EPS)
    h = (x * inv * g_ref[...]) * (1.0 + sc_ref[...]) + sh_ref[...]
    h_ref[...] = h.astype(h_ref.dtype)
    half = d_model // 2
    hp_ref[...] = _pack_pair(h[:, :half], h[:, half:])

    ne = N_EXPERTS
    per = ne // N_GROUPS
    logits = lax.dot_general(wt_ref[...], h, (((1,), (1,)), ((), ())),
                             precision=lax.Precision.HIGHEST,
                             preferred_element_type=F32)
    scores = jax.nn.sigmoid(logits)
    sel = scores + rb_ref[...]
    eidx = lax.broadcasted_iota(I32, (ne, tm), 0).astype(F32)
    minus_inf = -jnp.inf

    sel3 = sel.reshape(N_GROUPS, per, tm)
    sub = lax.broadcasted_iota(I32, (N_GROUPS, per, tm), 1).astype(F32)
    m1 = jnp.max(sel3, axis=1, keepdims=True)
    first = jnp.min(jnp.where(sel3 == m1, sub, float(per)), axis=1, keepdims=True)
    m2 = jnp.max(jnp.where(sub == first, minus_inf, sel3), axis=1, keepdims=True)
    grp = (m1 + m2).reshape(N_GROUPS, tm)

    gidx = lax.broadcasted_iota(I32, (N_GROUPS, tm), 0).astype(F32)
    gmask = jnp.zeros((N_GROUPS, tm), F32)
    work = grp
    for _ in range(TOPK_GROUPS):
        mx = jnp.max(work, axis=0, keepdims=True)
        pick = jnp.min(jnp.where(work == mx, gidx, float(N_GROUPS)), axis=0, keepdims=True)
        hit = gidx == pick
        gmask = jnp.where(hit, 1.0, gmask)
        work = jnp.where(hit, minus_inf, work)
    emask = jnp.broadcast_to(gmask.reshape(N_GROUPS, 1, tm), (N_GROUPS, per, tm)).reshape(ne, tm)

    work = jnp.where(emask > 0.0, sel, minus_inf)
    onehot = jnp.zeros((ne, tm), F32)
    idx_rows, w_rows = [], []
    for _ in range(TOP_K):
        mx = jnp.max(work, axis=0, keepdims=True)
        pick = jnp.min(jnp.where(work == mx, eidx, float(ne)), axis=0, keepdims=True)
        hit = eidx == pick
        onehot = jnp.where(hit, 1.0, onehot)
        work = jnp.where(hit, minus_inf, work)
        idx_rows.append(pick)
        w_rows.append(jnp.sum(jnp.where(hit, scores, 0.0), axis=0, keepdims=True))
    w_all = jnp.concatenate(w_rows, axis=0)
    wgt_ref[...] = w_all / jnp.sum(w_all, axis=0, keepdims=True) * ROUTED_SCALE
    idx_ref[...] = jnp.concatenate(idx_rows, axis=0).astype(I32)

    ra = lax.broadcasted_iota(I32, (tm, tm), 0)
    rb = lax.broadcasted_iota(I32, (tm, tm), 1)
    tri = jnp.where(ra <= rb, 1.0, 0.0).astype(BF16)
    incl = jnp.dot(onehot.astype(BF16), tri, preferred_element_type=F32)
    before = incl - onehot + cnt_ref[...]
    rank_rows = [jnp.sum(jnp.where(eidx == idx_rows[kk], before, 0.0), axis=0, keepdims=True)
                 for kk in range(TOP_K)]
    rank_ref[...] = jnp.concatenate(rank_rows, axis=0).astype(I32)
    cnt_ref[...] = cnt_ref[...] + jnp.sum(onehot, axis=1, keepdims=True)


def _route(x1, g, mod, sc_blk, sh_blk, router_w, router_bias, tm=256):
    t, d_model = x1.shape
    ne = N_EXPERTS
    vec = lambda k: pl.BlockSpec((1, d_model), lambda i, k=k: (0, k))
    tok = lambda: pl.BlockSpec((TOP_K, tm), lambda i: (0, i))
    return pl.pallas_call(
        _route_kernel,
        grid=(t // tm,),
        in_specs=[pl.BlockSpec((tm, d_model), lambda i: (i, 0)),
                  pl.BlockSpec((1, d_model), lambda i: (0, 0)),
                  vec(sc_blk), vec(sh_blk),
                  pl.BlockSpec((ne, d_model), lambda i: (0, 0)),
                  pl.BlockSpec((ne, 1), lambda i: (0, 0))],
        out_specs=[pl.BlockSpec((tm, d_model), lambda i: (i, 0)),
                   pl.BlockSpec((tm, d_model // 2), lambda i: (i, 0)),
                   tok(), tok(), tok(),
                   pl.BlockSpec((ne, 1), lambda i: (0, 0))],
        out_shape=[jax.ShapeDtypeStruct((t, d_model), BF16),
                   jax.ShapeDtypeStruct((t, d_model // 2), U32),
                   jax.ShapeDtypeStruct((TOP_K, t), I32),
                   jax.ShapeDtypeStruct((TOP_K, t), I32),
                   jax.ShapeDtypeStruct((TOP_K, t), F32),
                   jax.ShapeDtypeStruct((ne, 1), F32)],
        compiler_params=_params(("arbitrary",)),
        name="route",
    )(x1, g.reshape(1, d_model), mod, mod, router_w.T, router_bias.reshape(ne, 1))


SUBLANES = 8


def _pad_chunks(bm):
    sizes, s = [], bm // 2
    while s >= SUBLANES:
        sizes.append(s)
        s //= 2
    return sizes


def _dispatch_kernel(bm, pos_ref, fill_start_ref, fill_len_ref, nv_ref, hp_ref, hp_hbm, xs_ref, sem,
                     pad_sem):
    tm = hp_ref.shape[0]

    @pl.when(pl.program_id(0) == 0)
    def _():
        def pad_copies(action):
            def per_expert(e, carry):
                start = fill_start_ref[e]
                n = fill_len_ref[e]
                head = (-start) & (SUBLANES - 1)
                for r in range(SUBLANES - 1):
                    @pl.when(r < head)
                    def _(r=r):
                        action(pltpu.make_async_copy(hp_ref.at[pl.ds(0, 1)],
                                                     xs_ref.at[pl.ds(start + r, 1)], pad_sem))

                start = start + head
                n = n - head
                for size in _pad_chunks(bm):
                    take = (n & size) != 0

                    @pl.when(take)
                    def _(start=start, size=size):
                        dst = pl.multiple_of(start, SUBLANES)
                        action(pltpu.make_async_copy(hp_ref.at[pl.ds(0, size)],
                                                     xs_ref.at[pl.ds(dst, size)], pad_sem))

                    start = start + jnp.where(take, size, 0)
                return carry

            lax.fori_loop(0, N_EXPERTS, per_expert, 0)

            def unused_block(b, carry):
                dst = pl.multiple_of(b * bm, bm)
                action(pltpu.make_async_copy(hp_ref.at[pl.ds(0, bm)],
                                             xs_ref.at[pl.ds(dst, bm)], pad_sem))
                return carry

            lax.fori_loop(nv_ref[0], xs_ref.shape[0] // bm, unused_block, 0)

        pad_copies(lambda cp: cp.start())
        pad_copies(lambda cp: cp.wait())

    first_token = pl.program_id(0) * tm

    def copy_rows(tt, carry):
        for kk in range(TOP_K):
            dst = pos_ref[0, 0, kk * tm + tt]
            src = (hp_ref.at[pl.ds(tt, 1)] if kk % 2 == 0
                   else hp_hbm.at[pl.ds(first_token + tt, 1)])
            pltpu.make_async_copy(src, xs_ref.at[pl.ds(dst, 1)], sem).start()
        return carry

    lax.fori_loop(0, tm, copy_rows, 0)
    pltpu.make_async_copy(xs_ref.at[pl.ds(0, tm * TOP_K)], xs_ref.at[pl.ds(0, tm * TOP_K)], sem).wait()


def _tile_major(a_t, tm):
    k, t = a_t.shape
    return a_t.reshape(k, t // tm, tm).transpose(1, 0, 2).reshape(t // tm, 1, k * tm)


def _dispatch(hp, pos_t, fill_start, fill_len, n_valid, rows, bm, tm=256):
    t, width = hp.shape
    assert tm >= bm
    smem = lambda: pl.BlockSpec(memory_space=pltpu.SMEM)
    return pl.pallas_call(
        functools.partial(_dispatch_kernel, bm),
        grid=(t // tm,),
        in_specs=[pl.BlockSpec((1, 1, tm * TOP_K), lambda i: (i, 0, 0), memory_space=pltpu.SMEM),
                  smem(), smem(), smem(),
                  pl.BlockSpec((tm, width), lambda i: (i, 0)),
                  pl.BlockSpec(memory_space=pl.ANY)],
        out_specs=pl.BlockSpec(memory_space=pl.ANY),
        out_shape=jax.ShapeDtypeStruct((rows, width), U32),
        scratch_shapes=[pltpu.SemaphoreType.DMA(()), pltpu.SemaphoreType.DMA(())],
        compiler_params=_params(("arbitrary",)),
        name="dispatch",
    )(_tile_major(pos_t, tm), fill_start, fill_len, n_valid, hp, hp)


INVERT_UNROLL = 8


def _invert_kernel(bm, n_tok, tm, pos_ref, fill_start_ref, fill_len_ref, nv_ref, inv_ref):
    tile = pl.program_id(0)

    @pl.when(tile == 0)
    def _():
        def mark_block(first_row):
            def mark(i, c):
                for u in range(INVERT_UNROLL):
                    inv_ref[first_row + i * INVERT_UNROLL + u] = -1
                return c

            lax.fori_loop(0, bm // INVERT_UNROLL, mark, 0)

        def mark_padding(e, carry):
            @pl.when(fill_len_ref[e] > 0)
            def _():
                mark_block(fill_start_ref[e] + fill_len_ref[e] - bm)

            return carry

        lax.fori_loop(0, N_EXPERTS, mark_padding, 0)

        def mark_unused(blk, c):
            mark_block(blk * bm)
            return c

        lax.fori_loop(nv_ref[0], inv_ref.shape[0] // bm, mark_unused, 0)

    def body(tt, carry):
        for kk in range(TOP_K):
            inv_ref[pos_ref[0, 0, kk * tm + tt]] = kk * n_tok + tile * tm + tt
        return carry

    lax.fori_loop(0, tm, body, 0, unroll=INVERT_UNROLL)


def _invert(pos_t, fill_start, fill_len, n_valid, rows, bm, tm=1024):
    _, t = pos_t.shape
    smem = lambda: pl.BlockSpec(memory_space=pltpu.SMEM)
    return pl.pallas_call(
        functools.partial(_invert_kernel, bm, t, tm),
        grid=(t // tm,),
        in_specs=[pl.BlockSpec((1, 1, tm * TOP_K), lambda i: (i, 0, 0), memory_space=pltpu.SMEM),
                  smem(), smem(), smem()],
        out_specs=smem(),
        out_shape=jax.ShapeDtypeStruct((rows,), I32),
        compiler_params=_params(("arbitrary",)),
        name="invert",
    )(_tile_major(pos_t, tm), fill_start, fill_len, n_valid)


Y_BUFFERS = 3


def _experts_kernel(bm, n_slot_rows, seg_ref, sege_ref, nv_ref, inv_ref, x_ref, wg_hbm, wu_hbm, wd_hbm,
                    ysl_ref, wg_f32, wu_f32, wd_f32, wg_bf, wu_bf, wd_bf, y0_ref, y1_ref, y2_ref,
                    sems, ysems):
    b = pl.program_id(0)
    nb = seg_ref.shape[0]
    n_valid = nv_ref[0]
    seg = seg_ref[jnp.minimum(b, nb - 1)]
    slot = seg % 2
    first = (b < nb) & ((b == 0) | (seg_ref[jnp.clip(b - 1, 0, nb - 1)] != seg))
    ybufs = (y0_ref, y1_ref, y2_ref)

    def weight_copies(which_seg, which_slot):
        e = sege_ref[which_seg]
        return [pltpu.make_async_copy(src.at[e], dst.at[which_slot], sems.at[which_slot])
                for src, dst in ((wg_hbm, wg_f32), (wu_hbm, wu_f32), (wd_hbm, wd_f32))]

    @pl.when(b == 0)
    def _():
        for cp in weight_copies(0, 0):
            cp.start()

    @pl.when(first)
    def _():
        for cp in weight_copies(seg, slot):
            cp.wait()
        wg_bf[...] = wg_f32[slot].astype(BF16)
        wu_bf[...] = wu_f32[slot].astype(BF16)
        wd_bf[...] = wd_f32[slot].astype(BF16)

        @pl.when(seg + 1 < nv_ref[1])
        def _():
            for cp in weight_copies(seg + 1, 1 - slot):
                cp.start()

    def compute(y_ref):
        lo, hi = _unpack_pair(x_ref[...])
        half = lo.shape[1]
        lo = lo.astype(BF16)
        hi = hi.astype(BF16)
        gate = (jnp.dot(lo, wg_bf[:half, :], preferred_element_type=F32)
                + jnp.dot(hi, wg_bf[half:, :], preferred_element_type=F32))
        up = (jnp.dot(lo, wu_bf[:half, :], preferred_element_type=F32)
              + jnp.dot(hi, wu_bf[half:, :], preferred_element_type=F32))
        act = (_silu(gate) * up).astype(BF16)
        y = jnp.dot(act, wd_bf[...], preferred_element_type=F32)
        y_ref[...] = _pack_pair(y[:, :half], y[:, half:])

    def scatter(block, parity):
        base = block * bm
        spare = n_slot_rows + parity * bm
        for r in range(bm):
            d = inv_ref[base + r]
            d = jnp.where(d < 0, spare + r, d)
            pltpu.make_async_copy(ybufs[parity].at[pl.ds(r, 1)], ysl_ref.at[pl.ds(d, 1)],
                                  ysems.at[parity]).start()

    for p in range(Y_BUFFERS):
        mine = (b % Y_BUFFERS) == p
        before = (p - 1) % Y_BUFFERS

        @pl.when(mine & (b >= Y_BUFFERS) & (b - Y_BUFFERS < n_valid))
        def _(p=p):
            pltpu.make_async_copy(ybufs[p], ysl_ref.at[pl.ds(0, bm)], ysems.at[p]).wait()

        @pl.when(mine & (b >= 1) & (b < n_valid))
        def _(p=p, before=before):
            scatter(b - 1, before)
            compute(ybufs[p])

        @pl.when(mine & (b >= 1) & (b == n_valid))
        def _(before=before):
            scatter(b - 1, before)

    @pl.when(b == 0)
    def _():
        spare_fill = [pltpu.make_async_copy(
            x_ref, ysl_ref.at[pl.ds(n_slot_rows + parity * bm, bm)], ysems.at[parity])
            for parity in range(Y_BUFFERS)]
        for cp in spare_fill:
            cp.start()
        for cp in spare_fill:
            cp.wait()
        compute(ybufs[0])


def _experts(xs, inv, seg_of, seg_e, n_valid, wg, wu, wd, n_tok, bm=EXPERT_ROWS):
    rows, width = xs.shape
    _, d_model, de = wg.shape
    nb = rows // bm
    n_slot_rows = TOP_K * n_tok
    row_map = lambda b, sg, se, nv, iv: (jnp.minimum(b, nv[0] - 1), 0)
    hbm = lambda: pl.BlockSpec(memory_space=pl.ANY)
    grid_spec = pltpu.PrefetchScalarGridSpec(
        num_scalar_prefetch=4,
        grid=(nb + Y_BUFFERS,),
        in_specs=[pl.BlockSpec((bm, width), row_map), hbm(), hbm(), hbm()],
        out_specs=hbm(),
        scratch_shapes=[pltpu.VMEM((2, d_model, de), F32),
                        pltpu.VMEM((2, d_model, de), F32),
                        pltpu.VMEM((2, de, d_model), F32),
                        pltpu.VMEM((d_model, de), BF16),
                        pltpu.VMEM((d_model, de), BF16),
                        pltpu.VMEM((de, d_model), BF16),
                        ]
                       + [pltpu.VMEM((bm, width), U32)] * Y_BUFFERS
                       + [pltpu.SemaphoreType.DMA((2,)),
                          pltpu.SemaphoreType.DMA((Y_BUFFERS,))],
    )
    return pl.pallas_call(
        functools.partial(_experts_kernel, bm, n_slot_rows),
        grid_spec=grid_spec,
        out_shape=jax.ShapeDtypeStruct((n_slot_rows + Y_BUFFERS * bm, width), U32),
        compiler_params=_params(("arbitrary",)),
        name="experts",
    )(seg_of, seg_e, n_valid, inv, xs, wg, wu, wd)


def _combine_kernel(x_ref, h_ref, wt_ref, g_ref, sg_ref, su_ref, sd_ref, *rest):
    y_refs, o_ref = rest[:TOP_K], rest[TOP_K]
    h = h_ref[...]
    act = (_silu(jnp.dot(h, sg_ref[...], preferred_element_type=F32))
           * jnp.dot(h, su_ref[...], preferred_element_type=F32)).astype(BF16)
    shared = jnp.dot(act, sd_ref[...], preferred_element_type=F32)
    half = y_refs[0].shape[1]
    wt = wt_ref[...]
    lo_acc = shared[:, :half]
    hi_acc = shared[:, half:]
    for kk in range(TOP_K):
        lo, hi = _unpack_pair(y_refs[kk][...])
        wk = wt[:, kk:kk + 1]
        lo_acc = lo_acc + wk * lo
        hi_acc = hi_acc + wk * hi
    g = g_ref[...]
    o_ref[:, :half] = x_ref[:, :half] + g[:, :half] * lo_acc
    o_ref[:, half:] = x_ref[:, half:] + g[:, half:] * hi_acc


def _combine(x1, h2, wts, mod, gate_blk, sg, su, sd, ysl, tm=256):
    t, d_model = x1.shape
    ds_ = sg.shape[1]
    width = ysl.shape[1]
    tiles = t // tm
    slot = lambda kk: pl.BlockSpec((tm, width), lambda i, kk=kk: (kk * tiles + i, 0))
    return pl.pallas_call(
        _combine_kernel,
        grid=(tiles,),
        in_specs=[pl.BlockSpec((tm, d_model), lambda i: (i, 0)),
                  pl.BlockSpec((tm, d_model), lambda i: (i, 0)),
                  pl.BlockSpec((tm, TOP_K), lambda i: (i, 0)),
                  pl.BlockSpec((1, d_model), lambda i: (0, gate_blk)),
                  pl.BlockSpec((d_model, ds_), lambda i: (0, 0)),
                  pl.BlockSpec((d_model, ds_), lambda i: (0, 0)),
                  pl.BlockSpec((ds_, d_model), lambda i: (0, 0))]
                 + [slot(kk) for kk in range(TOP_K)],
        out_specs=pl.BlockSpec((tm, d_model), lambda i: (i, 0)),
        out_shape=jax.ShapeDtypeStruct((t, d_model), F32),
        compiler_params=_params(("parallel",)),
        name="combine",
    )(x1, h2, wts, mod, sg, su, sd, *([ysl] * TOP_K))


def _layout_kernel(bm, cnt_ref, idx_ref, rank_ref, pos_ref, seg_ref, sege_ref, nv_ref, fs_ref, fl_ref):
    shift = bm.bit_length() - 1
    pos_ref[...] = rank_ref[...]

    def per_expert(e, carry):
        start, blk, seg = carry
        cnt = cnt_ref[e]
        nblk = (cnt + (bm - 1)) >> shift
        pos_ref[...] = pos_ref[...] + jnp.where(idx_ref[...] == e, start, 0)

        def mark(b, c):
            seg_ref[blk + b] = seg
            return c

        lax.fori_loop(0, nblk, mark, 0)

        @pl.when(nblk > 0)
        def _():
            sege_ref[seg] = e

        fs_ref[e] = start + cnt
        fl_ref[e] = (nblk << shift) - cnt
        return start + (nblk << shift), blk + nblk, seg + jnp.where(nblk > 0, 1, 0)

    zero = jnp.int32(0)
    _, n_valid, n_seg = lax.fori_loop(0, N_EXPERTS, per_expert, (zero, zero, zero))
    nv_ref[0] = n_valid
    nv_ref[1] = n_seg

    def tail_blocks(b, c):
        seg_ref[b] = n_seg - 1
        return c

    lax.fori_loop(n_valid, seg_ref.shape[0], tail_blocks, 0)

    def tail_segs(s, c):
        sege_ref[s] = N_EXPERTS - 1
        return c

    lax.fori_loop(n_seg, N_EXPERTS, tail_segs, 0)


def _layout(counts, idx_t, rank_t, bm, n_blocks):
    assert bm & (bm - 1) == 0
    k, t = idx_t.shape
    smem = lambda: pl.BlockSpec(memory_space=pltpu.SMEM)
    full = lambda: pl.BlockSpec((k, t), lambda: (0, 0))
    return pl.pallas_call(
        functools.partial(_layout_kernel, bm),
        in_specs=[smem(), full(), full()],
        out_specs=[full(), smem(), smem(), smem(), smem(), smem()],
        out_shape=[jax.ShapeDtypeStruct((k, t), I32),
                   jax.ShapeDtypeStruct((n_blocks,), I32),
                   jax.ShapeDtypeStruct((N_EXPERTS,), I32),
                   jax.ShapeDtypeStruct((2,), I32),
                   jax.ShapeDtypeStruct((N_EXPERTS,), I32),
                   jax.ShapeDtypeStruct((N_EXPERTS,), I32)],
        name="layout",
    )(counts.reshape(-1).astype(I32), idx_t, rank_t)


def _layer(x, c, rel_bias, w_ada, b_ada, ln1_g, w_in, q_norm_g, k_norm_g, ret_gn_g, p_a, p_b, w_o,
           ln2_g, router_w, router_bias, w_gate_e, w_up_e, w_down_e, w_gate_s, w_up_s, w_down_s):
    t, d_model = x.shape
    dils = tuple(d for _, d in DILATED_GROUPS)

    mod = _ada(c.reshape(d_model), w_ada, b_ada)
    h_orders = _norm1(x, ln1_g, mod, dils[1:])
    cos_tab, sin_tab = _rotary_tables(t)
    projs = []
    for order, (cols, epis) in enumerate(_inproj_plan(d_model)):
        projs.append(_inproj(h_orders[order], w_in, cols, epis, q_norm_g, k_norm_g, cos_tab, sin_tab,
                             f"inproj_d{dils[order]}"))
    proj = projs[0]

    attn = [_attn_group(projs[gi], rel_bias, gi, win, dil, 0, 1, 2)
            for gi, (win, dil) in enumerate(DILATED_GROUPS)]
    base = 3
    rq = RET_HEADS * RET_QK_DIM // COLBLK
    vw_blk = RET_HEADS * RET_V_DIM // COLBLK
    qcol = base
    kcol = base + rq
    vcol_blk = base + 2 * rq
    gcol_blk = vcol_blk + vw_blk
    ga_blk = gcol_blk + vw_blk
    gb_blk = ga_blk + d_model // COLBLK
    y_b = _retention(proj, ret_gn_g, qcol, kcol, vcol_blk, gcol_blk)
    (o1, l1), (o2, l2), (o3, l3) = attn
    l2 = _from_residue_major(l2, dils[1])
    l3 = _from_residue_major(l3, dils[2])
    merged = _merge(o1, l1, o2, l2, o3, l3, y_b, proj, ga_blk, gb_blk,
                    p_a.astype(BF16), p_b.astype(BF16), dils[1:])
    x1 = _oproj(x, merged, w_o.astype(BF16), mod, 2)

    h2, h2p, idx_t, rank_t, wgt_t, counts = _route(x1, ln2_g, mod, 4, 3, router_w, router_bias)
    bm = EXPERT_ROWS
    n_blocks = (t * TOP_K + N_EXPERTS * (bm - 1) + bm - 1) // bm
    pos_t, seg_of, seg_e, n_valid, fill_start, fill_len = _layout(counts, idx_t, rank_t, bm, n_blocks)
    xs = _dispatch(h2p, pos_t, fill_start, fill_len, n_valid, n_blocks * bm, bm)
    inv = _invert(pos_t, fill_start, fill_len, n_valid, n_blocks * bm, bm)
    ysl = _experts(xs, inv, seg_of, seg_e, n_valid, w_gate_e, w_up_e, w_down_e, t)
    return _combine(x1, h2, wgt_t.T, mod, 5, w_gate_s.astype(BF16), w_up_s.astype(BF16),
                    w_down_s.astype(BF16), ysl)


def kernel(x, c, rel_bias, w_ada, b_ada, ln1_g, w_in, q_norm_g, k_norm_g, ret_gn_g, p_a, p_b, w_o,
           ln2_g, router_w, router_bias, w_gate_e, w_up_e, w_down_e, w_gate_s, w_up_s, w_down_s):
    b, s, d_model = x.shape
    depth = w_ada.shape[0]
    outs = []
    for bi in range(b):
        xb = x[bi]
        for l in range(depth):
            xb = _layer(xb, c[bi], rel_bias, w_ada[l], b_ada[l], ln1_g[l], w_in[l], q_norm_g[l],
                        k_norm_g[l], ret_gn_g[l], p_a[l], p_b[l], w_o[l], ln2_g[l], router_w[l],
                        router_bias[l], w_gate_e[l], w_up_e[l], w_down_e[l], w_gate_s[l],
                        w_up_s[l], w_down_s[l])
        outs.append(xb)
    return jnp.stack(outs, axis=0)
```

```python
import functools

import numpy as np
import jax
import jax.numpy as jnp
from jax import lax
from jax.experimental import pallas as pl
from jax.experimental.pallas import tpu as pltpu

F32 = jnp.float32
BF16 = jnp.bfloat16
U32 = jnp.uint32
I32 = jnp.int32

HEAD_DIM = 128
DILATED_GROUPS = ((128, 1), (512, 4), (2048, 16))
HEADS_PER_GROUP = 8
N_HEADS_A = HEADS_PER_GROUP * len(DILATED_GROUPS)
A_GROUP_WIDTH = HEADS_PER_GROUP * HEAD_DIM
ATTN_BLOCK = 128
NUM_BUCKETS = 32
MAX_DISTANCE = 2048
NEG_INF = -1e30
RET_HEADS = 8
RET_QK_DIM = 128
RET_V_DIM = 256
RET_CHUNK = 128
ROPE_BASE = 10000.0
GN_EPS = 1e-5
N_EXPERTS = 64
N_GROUPS = 8
TOPK_GROUPS = 4
TOP_K = 8
ROUTED_SCALE = 2.5
RMS_EPS = 1e-6

LANE = 128
COLBLK = 1024
VMEM_LIMIT = 56 * 1024 * 1024
EXPERT_ROWS = 256


def _params(sem, vmem=VMEM_LIMIT):
    return pltpu.CompilerParams(dimension_semantics=sem, vmem_limit_bytes=vmem)


def _sigmoid(v):
    return 0.5 * jnp.tanh(0.5 * v) + 0.5


def _silu(v):
    return v * _sigmoid(v)


def _ada_kernel(c_ref, w_ref, b_ref, o_ref):
    sc = _silu(c_ref[...])
    o_ref[...] = jnp.sum(w_ref[...] * sc, axis=0, keepdims=True) + b_ref[...]


def _ada(c, w, b, tn=512):
    d, n = w.shape
    return pl.pallas_call(
        _ada_kernel,
        grid=(n // tn,),
        in_specs=[pl.BlockSpec((d, 1), lambda j: (0, 0)),
                  pl.BlockSpec((d, tn), lambda j: (0, j)),
                  pl.BlockSpec((1, tn), lambda j: (0, j))],
        out_specs=pl.BlockSpec((1, tn), lambda j: (0, j)),
        out_shape=jax.ShapeDtypeStruct((1, n), F32),
        compiler_params=_params(("parallel",)),
        name="ada",
    )(c.reshape(d, 1), w, b.reshape(1, n))


PERM_TILE = 256


def _perm_matrix(d, to_residue):
    tm = PERM_TILE
    n = tm // d
    assert d & (d - 1) == 0 and n & (n - 1) == 0
    ii = lax.broadcasted_iota(I32, (tm, tm), 0)
    jj = lax.broadcasted_iota(I32, (tm, tm), 1)
    if to_residue:
        src = (ii & (n - 1)) * d + (ii >> (n.bit_length() - 1))
    else:
        src = (ii & (d - 1)) * n + (ii >> (d.bit_length() - 1))
    return jnp.where(jj == src, 1.0, 0.0).astype(BF16)


def _norm1_kernel(dils, x_ref, g_ref, sc_ref, sh_ref, o_ref, *res_refs):
    x = x_ref[...]
    inv = lax.rsqrt(jnp.mean(x * x, axis=-1, keepdims=True) + RMS_EPS)
    h = ((x * inv * g_ref[...]) * (1.0 + sc_ref[...]) + sh_ref[...]).astype(o_ref.dtype)
    o_ref[...] = h
    tm, width = h.shape
    for d, r_ref in zip(dils, res_refs):
        perm = _perm_matrix(d, True)
        n = PERM_TILE // d
        for s in range(tm // PERM_TILE):
            sub = h[s * PERM_TILE:(s + 1) * PERM_TILE, :]
            y = jnp.dot(perm, sub, preferred_element_type=F32).astype(r_ref.dtype)
            r_ref[:, s * n:(s + 1) * n, :] = y.reshape(d, n, width)


def _norm1(x, g, mod, dils, tm=512):
    t, d_model = x.shape
    vec = lambda k: pl.BlockSpec((1, d_model), lambda i, k=k: (0, k))
    out_shapes = [jax.ShapeDtypeStruct((t, d_model), BF16)]
    out_specs = [pl.BlockSpec((tm, d_model), lambda i: (i, 0))]
    for d in dils:
        out_shapes.append(jax.ShapeDtypeStruct((d, t // d, d_model), BF16))
        out_specs.append(pl.BlockSpec((d, tm // d, d_model), lambda i: (0, i, 0)))
    outs = pl.pallas_call(
        functools.partial(_norm1_kernel, dils),
        grid=(t // tm,),
        in_specs=[pl.BlockSpec((tm, d_model), lambda i: (i, 0)),
                  pl.BlockSpec((1, d_model), lambda i: (0, 0)),
                  vec(1), vec(0)],
        out_specs=out_specs,
        out_shape=out_shapes,
        compiler_params=_params(("parallel",)),
        name="norm1",
    )(x, g.reshape(1, d_model), mod, mod)
    return [o.reshape(t, d_model) for o in outs]


def _from_residue_major(a, d):
    t, w = a.shape
    return a.reshape(d, t // d, w).transpose(1, 0, 2).reshape(t, w)


EPI_QNORM, EPI_KNORM, EPI_PLAIN, EPI_ROT_Q, EPI_ROT_K, EPI_SILU, EPI_SIGMOID = range(7)
INPROJ_ROW_CHUNK = 256


def _inproj_kernel(epis_present, colblk_ref, epi_ref, h_ref, w_ref, qg_ref, kg_ref, cos_ref, sin_ref,
                   o_ref, wbf_ref):
    del colblk_ref
    epi = epi_ref[pl.program_id(0)]
    tm = h_ref.shape[0]
    nh = o_ref.shape[1] // HEAD_DIM

    @pl.when(pl.program_id(1) == 0)
    def _():
        wbf_ref[...] = w_ref[...].astype(BF16)

    def head_norm(gain, scale):
        def fn(acc, rows):
            for hh in range(nh):
                sl = slice(hh * HEAD_DIM, (hh + 1) * HEAD_DIM)
                a = acc[:, sl]
                inv = lax.rsqrt(jnp.mean(a * a, axis=-1, keepdims=True) + RMS_EPS)
                o_ref[rows, sl] = ((a * inv * gain) * scale).astype(o_ref.dtype)
        return fn

    def rotary(scale):
        def fn(acc, rows):
            cos = cos_ref[rows, :]
            sin = sin_ref[rows, :]
            for hh in range(nh):
                sl = slice(hh * HEAD_DIM, (hh + 1) * HEAD_DIM)
                a = acc[:, sl]
                rot = pltpu.roll(a, HEAD_DIM // 2, 1)
                o_ref[rows, sl] = ((a * cos + rot * sin) * scale).astype(o_ref.dtype)
        return fn

    def elementwise(f):
        def fn(acc, rows):
            o_ref[rows, :] = f(acc).astype(o_ref.dtype)
        return fn

    epilogues = {
        EPI_QNORM: lambda: head_norm(qg_ref[...], HEAD_DIM ** -0.5),
        EPI_KNORM: lambda: head_norm(kg_ref[...], 1.0),
        EPI_PLAIN: lambda: elementwise(lambda a: a),
        EPI_ROT_Q: lambda: rotary(1.0),
        EPI_ROT_K: lambda: rotary(RET_QK_DIM ** -0.5),
        EPI_SILU: lambda: elementwise(_silu),
        EPI_SIGMOID: lambda: elementwise(_sigmoid),
    }
    for code in epis_present:
        @pl.when(epi == code)
        def _(code=code):
            fn = epilogues[code]()
            for c in range(tm // INPROJ_ROW_CHUNK):
                rows = slice(c * INPROJ_ROW_CHUNK, (c + 1) * INPROJ_ROW_CHUNK)
                acc = jnp.dot(h_ref[rows, :], wbf_ref[...], preferred_element_type=F32)
                fn(acc, rows)


def _inproj_plan(d_model):
    a_blocks = N_HEADS_A * HEAD_DIM // COLBLK
    groups = len(DILATED_GROUPS)
    per_group = a_blocks // groups
    rq = RET_HEADS * RET_QK_DIM // COLBLK
    rv = RET_HEADS * RET_V_DIM // COLBLK
    gd = d_model // COLBLK
    seg_epi = ([EPI_QNORM] * a_blocks + [EPI_KNORM] * a_blocks + [EPI_PLAIN] * a_blocks
               + [EPI_ROT_Q] * rq + [EPI_ROT_K] * rq + [EPI_PLAIN] * rv + [EPI_SILU] * rv
               + [EPI_SIGMOID] * (2 * gd))
    order_of = [0] * len(seg_epi)
    for seg in range(3):
        for blk in range(a_blocks):
            order_of[seg * a_blocks + blk] = blk // per_group
    plans = []
    for order in range(groups):
        cols = [cb for cb in range(len(seg_epi)) if order_of[cb] == order]
        plans.append((cols, [seg_epi[cb] for cb in cols]))
    return plans


def _inproj(h, w, cols, epis, qg, kg, cos_tab, sin_tab, name, tm=1024):
    t, d_model = h.shape
    row = lambda width: pl.BlockSpec((tm, width), lambda j, i, cb, ep: (i, 0))
    one = lambda width: pl.BlockSpec((1, width), lambda j, i, cb, ep: (0, 0))
    grid_spec = pltpu.PrefetchScalarGridSpec(
        num_scalar_prefetch=2,
        grid=(len(cols), t // tm),
        in_specs=[
            row(d_model),
            pl.BlockSpec((d_model, COLBLK), lambda j, i, cb, ep: (0, cb[j])),
            one(HEAD_DIM), one(HEAD_DIM), row(HEAD_DIM), row(HEAD_DIM),
        ],
        out_specs=pl.BlockSpec((tm, COLBLK), lambda j, i, cb, ep: (i, j)),
        scratch_shapes=[pltpu.VMEM((d_model, COLBLK), BF16)],
    )
    return pl.pallas_call(
        functools.partial(_inproj_kernel, tuple(sorted(set(epis)))),
        grid_spec=grid_spec,
        out_shape=jax.ShapeDtypeStruct((t, len(cols) * COLBLK), BF16),
        compiler_params=_params(("arbitrary", "arbitrary")),
        name=name,
    )(jnp.asarray(np.array(cols, np.int32)), jnp.asarray(np.array(epis, np.int32)),
      h, w, qg.reshape(1, HEAD_DIM), kg.reshape(1, HEAD_DIM), cos_tab, sin_tab)


def _rotary_tables(t):
    inv = ROPE_BASE ** (-np.arange(0, RET_QK_DIM, 2, dtype=np.float64) / RET_QK_DIM)
    ang = np.arange(t, dtype=np.float64)[:, None] * inv[None, :]
    cos, sin = np.cos(ang), np.sin(ang)
    cos_tab = np.concatenate([cos, cos], axis=1).astype(np.float32)
    sin_tab = np.concatenate([-sin, sin], axis=1).astype(np.float32)
    return jnp.asarray(cos_tab), jnp.asarray(sin_tab)


def _t5_bucket(dist):
    max_exact = NUM_BUCKETS // 2
    safe = np.maximum(dist, 1).astype(np.float32)
    large = max_exact + (np.log(safe / max_exact) / np.log(MAX_DISTANCE / max_exact)
                         * (NUM_BUCKETS - max_exact)).astype(np.int32)
    return np.where(dist < max_exact, dist, np.minimum(large, NUM_BUCKETS - 1)).astype(np.int32)


def _attn_kernel(head0, w_steps, blocks_per_res, tab_ref, bucket_ref, q_ref, kp_ref, kc_ref,
                 vp_ref, vc_ref, o_ref, lse_ref, bias_ref, band_ref, s_ref, p_ref):
    m_idx = pl.program_id(0)
    blk = ATTN_BLOCK

    @pl.when(m_idx == 0)
    def _():
        bucket = bucket_ref[...]
        for hh in range(HEADS_PER_GROUP):
            bias = jnp.zeros(bucket.shape, F32)
            for b in range(NUM_BUCKETS):
                bias = jnp.where(bucket == b, tab_ref[b, head0 + hh], bias)
            bias_ref[hh] = bias
        a = lax.broadcasted_iota(I32, (blk, 2 * blk), 0)
        cc = lax.broadcasted_iota(I32, (blk, 2 * blk), 1)
        delta = blk + a - cc
        band_ref[...] = jnp.where((delta >= 0) & (delta <= w_steps), 1.0, 0.0)

    prev_thr = jnp.where((m_idx % blocks_per_res) > 0, 0.5, 2.0)
    nt = (((1,), (1,)), ((), ()))
    heads = range(HEADS_PER_GROUP)
    head_cols = [slice(hh * HEAD_DIM, (hh + 1) * HEAD_DIM) for hh in heads]
    for hh, sl in zip(heads, head_cols):
        q = q_ref[:, sl]
        s_p = lax.dot_general(q, kp_ref[:, sl], nt, preferred_element_type=F32)
        s_c = lax.dot_general(q, kc_ref[:, sl], nt, preferred_element_type=F32)
        s_ref[hh, :, :blk] = jnp.where(band_ref[:, :blk] > prev_thr,
                                       s_p + bias_ref[hh, :, :blk], NEG_INF)
        s_ref[hh, :, blk:] = jnp.where(band_ref[:, blk:] > 0.5,
                                       s_c + bias_ref[hh, :, blk:], NEG_INF)
    dens, lses = [], []
    for hh in heads:
        s = s_ref[hh]
        mx = jnp.max(s, axis=-1, keepdims=True)
        p = jnp.exp(s - mx)
        den = jnp.sum(p, axis=-1, keepdims=True)
        p_ref[hh] = p.astype(BF16)
        dens.append(den)
        lses.append(mx + jnp.log(den))
    for hh, sl in zip(heads, head_cols):
        v_both = jnp.concatenate([vp_ref[:, sl], vc_ref[:, sl]], axis=0)
        acc = jnp.dot(p_ref[hh], v_both, preferred_element_type=F32)
        o_ref[:, sl] = (acc / dens[hh]).astype(o_ref.dtype)
    lse_ref[...] = jnp.concatenate(lses, axis=-1)


def _attn_group(proj, rel_bias, gi, window, dilation, qcol, kcol, vcol):
    t = proj.shape[0]
    blk = ATTN_BLOCK
    w_steps = window // dilation
    blocks_per_res = t // dilation // blk
    nblk = t // blk
    a = np.arange(blk)[:, None]
    cc = np.arange(2 * blk)[None, :]
    bucket = _t5_bucket(np.maximum(blk + a - cc, 0) * dilation)

    def prev_map(m):
        return jnp.where(m % blocks_per_res > 0, m - 1, m)

    kern = functools.partial(_attn_kernel, gi * HEADS_PER_GROUP, w_steps, blocks_per_res)
    width = A_GROUP_WIDTH
    return pl.pallas_call(
        kern,
        grid=(nblk,),
        in_specs=[
            pl.BlockSpec(memory_space=pltpu.SMEM),
            pl.BlockSpec((blk, 2 * blk), lambda m: (0, 0)),
            pl.BlockSpec((blk, width), lambda m: (m, qcol)),
            pl.BlockSpec((blk, width), lambda m: (prev_map(m), kcol)),
            pl.BlockSpec((blk, width), lambda m: (m, kcol)),
            pl.BlockSpec((blk, width), lambda m: (prev_map(m), vcol)),
            pl.BlockSpec((blk, width), lambda m: (m, vcol)),
        ],
        out_specs=[pl.BlockSpec((blk, width), lambda m: (m, 0)),
                   pl.BlockSpec((blk, HEADS_PER_GROUP), lambda m: (m, 0))],
        out_shape=[jax.ShapeDtypeStruct((t, width), BF16),
                   jax.ShapeDtypeStruct((t, HEADS_PER_GROUP), F32)],
        scratch_shapes=[pltpu.VMEM((HEADS_PER_GROUP, blk, 2 * blk), F32),
                        pltpu.VMEM((blk, 2 * blk), F32),
                        pltpu.VMEM((HEADS_PER_GROUP, blk, 2 * blk), F32),
                        pltpu.VMEM((HEADS_PER_GROUP, blk, 2 * blk), BF16)],
        compiler_params=_params(("arbitrary",)),
        name=f"attn_d{dilation}",
    )(rel_bias, jnp.asarray(bucket), proj, proj, proj, proj, proj)


def _retention_kernel(q_ref, k_ref, v0_ref, v1_ref, g0_ref, g1_ref, dmat_ref, zeta_ref, xi_ref,
                      gch_ref, gn_ref, o_ref, state_ref, s_ref, cross_ref):
    @pl.when(pl.program_id(0) == 0)
    def _():
        state_ref[...] = jnp.zeros_like(state_ref)

    nt = (((1,), (1,)), ((), ()))
    tn = (((0,), (0,)), ((), ()))
    per_half = RET_HEADS // 2

    def head_refs(hh):
        qs = slice(hh * RET_QK_DIM, (hh + 1) * RET_QK_DIM)
        vs = slice(hh * RET_V_DIM, (hh + 1) * RET_V_DIM)
        hs = slice((hh % per_half) * RET_V_DIM, (hh % per_half + 1) * RET_V_DIM)
        v_ref, g_ref = (v0_ref, g0_ref) if hh < per_half else (v1_ref, g1_ref)
        return qs, vs, hs, v_ref, g_ref

    for hh in range(RET_HEADS):
        qs, _, hs, v_ref, _ = head_refs(hh)
        q = q_ref[:, qs]
        k = k_ref[:, qs]
        v = v_ref[:, hs]
        state = state_ref[hh]
        s = lax.dot_general(q, k, nt, preferred_element_type=F32) * dmat_ref[hh]
        s_ref[hh] = s.astype(BF16)
        cross_ref[hh] = jnp.dot(q, state.astype(BF16), preferred_element_type=F32) * xi_ref[hh]
        vz = (v.astype(F32) * zeta_ref[hh]).astype(BF16)
        upd = lax.dot_general(k, vz, tn, preferred_element_type=F32)
        state_ref[hh] = gch_ref[hh] * state + upd
    for hh in range(RET_HEADS):
        _, vs, hs, v_ref, g_ref = head_refs(hh)
        inner = jnp.dot(s_ref[hh], v_ref[:, hs], preferred_element_type=F32)
        ret = inner + cross_ref[hh]
        mu = jnp.mean(ret, axis=-1, keepdims=True)
        cen = ret - mu
        var = jnp.mean(cen * cen, axis=-1, keepdims=True)
        y = cen * lax.rsqrt(var + GN_EPS) * gn_ref[:, vs]
        o_ref[:, vs] = (y * g_ref[:, hs].astype(F32)).astype(o_ref.dtype)


def _retention_tables():
    c = RET_CHUNK
    hh = np.arange(RET_HEADS, dtype=np.float64)
    log_g = np.log1p(-np.exp2(-5.0 - hh))
    idx = np.arange(c, dtype=np.float64)
    diff = idx[:, None] - idx[None, :]
    dmat = np.where(diff >= 0, np.exp(log_g[:, None, None] * np.maximum(diff, 0.0)), 0.0)
    zeta = np.exp(log_g[:, None] * (c - 1 - idx))[:, :, None]
    xi = np.exp(log_g[:, None] * (idx + 1.0))[:, :, None]
    gch = np.exp(log_g * c)
    f = lambda v: jnp.asarray(v.astype(np.float32))
    return f(dmat), f(zeta), f(xi), f(gch)


def _retention(proj, gn_g, qcol, kcol, vcol, gcol):
    t = proj.shape[0]
    c = RET_CHUNK
    qw = RET_HEADS * RET_QK_DIM
    vw = RET_HEADS * RET_V_DIM
    dmat, zeta, xi, gch = _retention_tables()
    full3 = lambda shp: pl.BlockSpec(shp, lambda n: (0, 0, 0))
    return pl.pallas_call(
        _retention_kernel,
        grid=(t // c,),
        in_specs=[
            pl.BlockSpec((c, qw), lambda n: (n, qcol)),
            pl.BlockSpec((c, qw), lambda n: (n, kcol)),
            pl.BlockSpec((c, vw // 2), lambda n: (n, vcol)),
            pl.BlockSpec((c, vw // 2), lambda n: (n, vcol + 1)),
            pl.BlockSpec((c, vw // 2), lambda n: (n, gcol)),
            pl.BlockSpec((c, vw // 2), lambda n: (n, gcol + 1)),
            full3((RET_HEADS, c, c)),
            full3((RET_HEADS, c, 1)),
            full3((RET_HEADS, c, 1)),
            pl.BlockSpec(memory_space=pltpu.SMEM),
            pl.BlockSpec((1, vw), lambda n: (0, 0)),
        ],
        out_specs=pl.BlockSpec((c, vw), lambda n: (n, 0)),
        out_shape=jax.ShapeDtypeStruct((t, vw), BF16),
        scratch_shapes=[pltpu.VMEM((RET_HEADS, RET_QK_DIM, RET_V_DIM), F32),
                        pltpu.VMEM((RET_HEADS, c, c), BF16),
                        pltpu.VMEM((RET_HEADS, c, RET_V_DIM), F32)],
        compiler_params=_params(("arbitrary",)),
        name="retention",
    )(proj, proj, proj, proj, proj, proj, dmat, zeta, xi, gch, gn_g.reshape(1, vw))


MERGE_ROW_CHUNK = PERM_TILE


def _merge_kernel(dils, o1_ref, l1_ref, o2_ref, l2_ref, o3_ref, l3_ref, yb_ref, ga_ref, gb_ref,
                  pa_ref, pb_ref, out_ref):
    tm = out_ref.shape[0]
    width = o1_ref.shape[1]
    perms = [_perm_matrix(d, False) for d in dils]

    def token_order(res_ref, gi, c):
        n = PERM_TILE // dils[gi]
        blk = res_ref[:, c * n:(c + 1) * n, :].reshape(PERM_TILE, width)
        return jnp.dot(perms[gi], blk, preferred_element_type=F32)

    for c in range(tm // MERGE_ROW_CHUNK):
        rows = slice(c * MERGE_ROW_CHUNK, (c + 1) * MERGE_ROW_CHUNK)
        o2 = token_order(o2_ref, 0, c)
        o3 = token_order(o3_ref, 1, c)
        l1 = l1_ref[rows, :]
        l2 = l2_ref[rows, :]
        l3 = l3_ref[rows, :]
        mx = jnp.maximum(jnp.maximum(l1, l2), l3)
        e1 = jnp.exp(l1 - mx)
        e2 = jnp.exp(l2 - mx)
        e3 = jnp.exp(l3 - mx)
        den = e1 + e2 + e3
        a1, a2, a3 = e1 / den, e2 / den, e3 / den
        pieces = []
        for hh in range(HEADS_PER_GROUP):
            sl = slice(hh * HEAD_DIM, (hh + 1) * HEAD_DIM)
            ya = (a1[:, hh:hh + 1] * o1_ref[rows, sl] + a2[:, hh:hh + 1] * o2[:, sl]
                  + a3[:, hh:hh + 1] * o3[:, sl])
            pieces.append(ya.astype(BF16))
        ya = jnp.concatenate(pieces, axis=1)
        za = jnp.dot(ya, pa_ref[...], preferred_element_type=F32)
        zb = jnp.dot(yb_ref[rows, :], pb_ref[...], preferred_element_type=F32)
        out_ref[rows, :] = (ga_ref[rows, :].astype(F32) * za
                            + gb_ref[rows, :].astype(F32) * zb).astype(out_ref.dtype)


def _merge(o1, l1, o2, l2, o3, l3, yb, proj, ga_col, gb_col, pa, pb, dils, tm=512, tn=1024):
    t = o1.shape[0]
    wa = o1.shape[1]
    wb = yb.shape[1]
    n = pa.shape[1]
    hg = HEADS_PER_GROUP
    ratio = tn // COLBLK
    o_spec = lambda: pl.BlockSpec((tm, wa), lambda j, i: (i, 0))
    l_spec = lambda: pl.BlockSpec((tm, hg), lambda j, i: (i, 0))
    res_spec = lambda d: pl.BlockSpec((d, tm // d, wa), lambda j, i: (0, i, 0))
    o2 = o2.reshape(dils[0], t // dils[0], wa)
    o3 = o3.reshape(dils[1], t // dils[1], wa)
    return pl.pallas_call(
        functools.partial(_merge_kernel, dils),
        grid=(n // tn, t // tm),
        in_specs=[
            o_spec(), l_spec(), res_spec(dils[0]), l_spec(), res_spec(dils[1]), l_spec(),
            pl.BlockSpec((tm, wb), lambda j, i: (i, 0)),
            pl.BlockSpec((tm, tn), lambda j, i: (i, ga_col // ratio + j)),
            pl.BlockSpec((tm, tn), lambda j, i: (i, gb_col // ratio + j)),
            pl.BlockSpec((wa, tn), lambda j, i: (0, j)),
            pl.BlockSpec((wb, tn), lambda j, i: (0, j)),
        ],
        out_specs=pl.BlockSpec((tm, tn), lambda j, i: (i, j)),
        out_shape=jax.ShapeDtypeStruct((t, n), BF16),
        compiler_params=_params(("parallel", "parallel")),
        name="merge",
    )(o1, l1, o2, l2, o3, l3, yb, proj, proj, pa, pb)


def _oproj_kernel(x_ref, m_ref, w_ref, g_ref, o_ref):
    z = jnp.dot(m_ref[...], w_ref[...], preferred_element_type=F32)
    o_ref[...] = x_ref[...] + g_ref[...] * z


def _oproj(x, merged, w_bf, mod, gate_blk, tm=512, tn=1024):
    t, d_model = x.shape
    k = merged.shape[1]
    per = d_model // tn
    return pl.pallas_call(
        _oproj_kernel,
        grid=(d_model // tn, t // tm),
        in_specs=[
            pl.BlockSpec((tm, tn), lambda j, i: (i, j)),
            pl.BlockSpec((tm, k), lambda j, i: (i, 0)),
            pl.BlockSpec((k, tn), lambda j, i: (0, j)),
            pl.BlockSpec((1, tn), lambda j, i: (0, gate_blk * per + j)),
        ],
        out_specs=pl.BlockSpec((tm, tn), lambda j, i: (i, j)),
        out_shape=jax.ShapeDtypeStruct((t, d_model), F32),
        compiler_params=_params(("parallel", "parallel")),
        name="oproj",
    )(x, merged, w_bf, mod)


def _pack_pair(lo, hi):
    lo_b = pltpu.bitcast(lo.astype(BF16).astype(F32), U32)
    hi_b = pltpu.bitcast(hi.astype(BF16).astype(F32), U32)
    return (lo_b >> 16) | (hi_b & jnp.uint32(0xFFFF0000))


def _unpack_pair(w):
    lo = pltpu.bitcast(w << 16, F32)
    hi = pltpu.bitcast(w & jnp.uint32(0xFFFF0000), F32)
    return lo, hi


def _route_kernel(x_ref, g_ref, sc_ref, sh_ref, wt_ref, rb_ref, h_ref, hp_ref, idx_ref, rank_ref,
                  wgt_ref, cnt_ref):
    @pl.when(pl.program_id(0) == 0)
    def _():
        cnt_ref[...] = jnp.zeros_like(cnt_ref)

    x = x_ref[...]
    tm, d_model = x.shape
    inv = lax.rsqrt(jnp.mean(x * x, axis=-1, keepdims=True) + RMS_EPS)
    h = (x * inv * g_ref[...]) * (1.0 + sc_ref[...]) + sh_ref[...]
    h_ref[...] = h.astype(h_ref.dtype)
    half = d_model // 2
    hp_ref[...] = _pack_pair(h[:, :half], h[:, half:])

    ne = N_EXPERTS
    per = ne // N_GROUPS
    logits = lax.dot_general(wt_ref[...], h, (((1,), (1,)), ((), ())),
                             precision=lax.Precision.HIGHEST,
                             preferred_element_type=F32)
    scores = jax.nn.sigmoid(logits)
    sel = scores + rb_ref[...]
    eidx = lax.broadcasted_iota(I32, (ne, tm), 0).astype(F32)
    minus_inf = -jnp.inf

    sel3 = sel.reshape(N_GROUPS, per, tm)
    sub = lax.broadcasted_iota(I32, (N_GROUPS, per, tm), 1).astype(F32)
    m1 = jnp.max(sel3, axis=1, keepdims=True)
    first = jnp.min(jnp.where(sel3 == m1, sub, float(per)), axis=1, keepdims=True)
    m2 = jnp.max(jnp.where(sub == first, minus_inf, sel3), axis=1, keepdims=True)
    grp = (m1 + m2).reshape(N_GROUPS, tm)

    gidx = lax.broadcasted_iota(I32, (N_GROUPS, tm), 0).astype(F32)
    gmask = jnp.zeros((N_GROUPS, tm), F32)
    work = grp
    for _ in range(TOPK_GROUPS):
        mx = jnp.max(work, axis=0, keepdims=True)
        pick = jnp.min(jnp.where(work == mx, gidx, float(N_GROUPS)), axis=0, keepdims=True)
        hit = gidx == pick
        gmask = jnp.where(hit, 1.0, gmask)
        work = jnp.where(hit, minus_inf, work)
    emask = jnp.broadcast_to(gmask.reshape(N_GROUPS, 1, tm), (N_GROUPS, per, tm)).reshape(ne, tm)

    work = jnp.where(emask > 0.0, sel, minus_inf)
    onehot = jnp.zeros((ne, tm), F32)
    idx_rows, w_rows = [], []
    for _ in range(TOP_K):
        mx = jnp.max(work, axis=0, keepdims=True)
        pick = jnp.min(jnp.where(work == mx, eidx, float(ne)), axis=0, keepdims=True)
        hit = eidx == pick
        onehot = jnp.where(hit, 1.0, onehot)
        work = jnp.where(hit, minus_inf, work)
        idx_rows.append(pick)
        w_rows.append(jnp.sum(jnp.where(hit, scores, 0.0), axis=0, keepdims=True))
    w_all = jnp.concatenate(w_rows, axis=0)
    wgt_ref[...] = w_all / jnp.sum(w_all, axis=0, keepdims=True) * ROUTED_SCALE
    idx_ref[...] = jnp.concatenate(idx_rows, axis=0).astype(I32)

    ra = lax.broadcasted_iota(I32, (tm, tm), 0)
    rb = lax.broadcasted_iota(I32, (tm, tm), 1)
    tri = jnp.where(ra <= rb, 1.0, 0.0).astype(BF16)
    incl = jnp.dot(onehot.astype(BF16), tri, preferred_element_type=F32)
    before = incl - onehot + cnt_ref[...]
    rank_rows = [jnp.sum(jnp.where(eidx == idx_rows[kk], before, 0.0), axis=0, keepdims=True)
                 for kk in range(TOP_K)]
    rank_ref[...] = jnp.concatenate(rank_rows, axis=0).astype(I32)
    cnt_ref[...] = cnt_ref[...] + jnp.sum(onehot, axis=1, keepdims=True)


def _route(x1, g, mod, sc_blk, sh_blk, router_w, router_bias, tm=256):
    t, d_model = x1.shape
    ne = N_EXPERTS
    vec = lambda k: pl.BlockSpec((1, d_model), lambda i, k=k: (0, k))
    tok = lambda: pl.BlockSpec((TOP_K, tm), lambda i: (0, i))
    return pl.pallas_call(
        _route_kernel,
        grid=(t // tm,),
        in_specs=[pl.BlockSpec((tm, d_model), lambda i: (i, 0)),
                  pl.BlockSpec((1, d_model), lambda i: (0, 0)),
                  vec(sc_blk), vec(sh_blk),
                  pl.BlockSpec((ne, d_model), lambda i: (0, 0)),
                  pl.BlockSpec((ne, 1), lambda i: (0, 0))],
        out_specs=[pl.BlockSpec((tm, d_model), lambda i: (i, 0)),
                   pl.BlockSpec((tm, d_model // 2), lambda i: (i, 0)),
                   tok(), tok(), tok(),
                   pl.BlockSpec((ne, 1), lambda i: (0, 0))],
        out_shape=[jax.ShapeDtypeStruct((t, d_model), BF16),
                   jax.ShapeDtypeStruct((t, d_model // 2), U32),
                   jax.ShapeDtypeStruct((TOP_K, t), I32),
                   jax.ShapeDtypeStruct((TOP_K, t), I32),
                   jax.ShapeDtypeStruct((TOP_K, t), F32),
                   jax.ShapeDtypeStruct((ne, 1), F32)],
        compiler_params=_params(("arbitrary",)),
        name="route",
    )(x1, g.reshape(1, d_model), mod, mod, router_w.T, router_bias.reshape(ne, 1))


SUBLANES = 8


def _pad_chunks(bm):
    sizes, s = [], bm // 2
    while s >= SUBLANES:
        sizes.append(s)
        s //= 2
    return sizes


def _dispatch_kernel(bm, pos_ref, fill_start_ref, fill_len_ref, nv_ref, hp_ref, xs_ref, sem, pad_sem):
    tm = hp_ref.shape[0]

    @pl.when(pl.program_id(0) == 0)
    def _():
        def pad_copies(action):
            def per_expert(e, carry):
                start = fill_start_ref[e]
                n = fill_len_ref[e]
                head = (-start) & (SUBLANES - 1)
                for r in range(SUBLANES - 1):
                    @pl.when(r < head)
                    def _(r=r):
                        action(pltpu.make_async_copy(hp_ref.at[pl.ds(0, 1)],
                                                     xs_ref.at[pl.ds(start + r, 1)], pad_sem))

                start = start + head
                n = n - head
                for size in _pad_chunks(bm):
                    take = (n & size) != 0

                    @pl.when(take)
                    def _(start=start, size=size):
                        dst = pl.multiple_of(start, SUBLANES)
                        action(pltpu.make_async_copy(hp_ref.at[pl.ds(0, size)],
                                                     xs_ref.at[pl.ds(dst, size)], pad_sem))

                    start = start + jnp.where(take, size, 0)
                return carry

            lax.fori_loop(0, N_EXPERTS, per_expert, 0)

            def unused_block(b, carry):
                dst = pl.multiple_of(b * bm, bm)
                action(pltpu.make_async_copy(hp_ref.at[pl.ds(0, bm)],
                                             xs_ref.at[pl.ds(dst, bm)], pad_sem))
                return carry

            lax.fori_loop(nv_ref[0], xs_ref.shape[0] // bm, unused_block, 0)

        pad_copies(lambda cp: cp.start())
        pad_copies(lambda cp: cp.wait())

    def copy_rows(tt, carry):
        for kk in range(TOP_K):
            dst = pos_ref[0, 0, kk * tm + tt]
            pltpu.make_async_copy(hp_ref.at[pl.ds(tt, 1)], xs_ref.at[pl.ds(dst, 1)], sem).start()
        return carry

    lax.fori_loop(0, tm, copy_rows, 0)
    pltpu.make_async_copy(xs_ref.at[pl.ds(0, tm * TOP_K)], xs_ref.at[pl.ds(0, tm * TOP_K)], sem).wait()


def _tile_major(a_t, tm):
    k, t = a_t.shape
    return a_t.reshape(k, t // tm, tm).transpose(1, 0, 2).reshape(t // tm, 1, k * tm)


def _dispatch(hp, pos_t, fill_start, fill_len, n_valid, rows, bm, tm=256):
    t, width = hp.shape
    assert tm >= bm
    smem = lambda: pl.BlockSpec(memory_space=pltpu.SMEM)
    return pl.pallas_call(
        functools.partial(_dispatch_kernel, bm),
        grid=(t // tm,),
        in_specs=[pl.BlockSpec((1, 1, tm * TOP_K), lambda i: (i, 0, 0), memory_space=pltpu.SMEM),
                  smem(), smem(), smem(),
                  pl.BlockSpec((tm, width), lambda i: (i, 0))],
        out_specs=pl.BlockSpec(memory_space=pl.ANY),
        out_shape=jax.ShapeDtypeStruct((rows, width), U32),
        scratch_shapes=[pltpu.SemaphoreType.DMA(()), pltpu.SemaphoreType.DMA(())],
        compiler_params=_params(("arbitrary",)),
        name="dispatch",
    )(_tile_major(pos_t, tm), fill_start, fill_len, n_valid, hp)


INVERT_UNROLL = 8


def _invert_kernel(bm, n_tok, tm, pos_ref, fill_start_ref, fill_len_ref, nv_ref, inv_ref):
    tile = pl.program_id(0)

    @pl.when(tile == 0)
    def _():
        def mark_block(first_row):
            def mark(i, c):
                for u in range(INVERT_UNROLL):
                    inv_ref[first_row + i * INVERT_UNROLL + u] = -1
                return c

            lax.fori_loop(0, bm // INVERT_UNROLL, mark, 0)

        def mark_padding(e, carry):
            @pl.when(fill_len_ref[e] > 0)
            def _():
                mark_block(fill_start_ref[e] + fill_len_ref[e] - bm)

            return carry

        lax.fori_loop(0, N_EXPERTS, mark_padding, 0)

        def mark_unused(blk, c):
            mark_block(blk * bm)
            return c

        lax.fori_loop(nv_ref[0], inv_ref.shape[0] // bm, mark_unused, 0)

    def body(tt, carry):
        for kk in range(TOP_K):
            inv_ref[pos_ref[0, 0, kk * tm + tt]] = kk * n_tok + tile * tm + tt
        return carry

    lax.fori_loop(0, tm, body, 0, unroll=INVERT_UNROLL)


def _invert(pos_t, fill_start, fill_len, n_valid, rows, bm, tm=1024):
    _, t = pos_t.shape
    smem = lambda: pl.BlockSpec(memory_space=pltpu.SMEM)
    return pl.pallas_call(
        functools.partial(_invert_kernel, bm, t, tm),
        grid=(t // tm,),
        in_specs=[pl.BlockSpec((1, 1, tm * TOP_K), lambda i: (i, 0, 0), memory_space=pltpu.SMEM),
                  smem(), smem(), smem()],
        out_specs=smem(),
        out_shape=jax.ShapeDtypeStruct((rows,), I32),
        compiler_params=_params(("arbitrary",)),
        name="invert",
    )(_tile_major(pos_t, tm), fill_start, fill_len, n_valid)


Y_BUFFERS = 3


def _experts_kernel(bm, n_slot_rows, seg_ref, sege_ref, nv_ref, inv_ref, x_ref, wg_hbm, wu_hbm, wd_hbm,
                    ysl_ref, wg_f32, wu_f32, wd_f32, wg_bf, wu_bf, wd_bf, y0_ref, y1_ref, y2_ref,
                    sems, ysems):
    b = pl.program_id(0)
    nb = seg_ref.shape[0]
    n_valid = nv_ref[0]
    seg = seg_ref[jnp.minimum(b, nb - 1)]
    slot = seg % 2
    first = (b < nb) & ((b == 0) | (seg_ref[jnp.clip(b - 1, 0, nb - 1)] != seg))
    ybufs = (y0_ref, y1_ref, y2_ref)

    def weight_copies(which_seg, which_slot):
        e = sege_ref[which_seg]
        return [pltpu.make_async_copy(src.at[e], dst.at[which_slot], sems.at[which_slot])
                for src, dst in ((wg_hbm, wg_f32), (wu_hbm, wu_f32), (wd_hbm, wd_f32))]

    @pl.when(b == 0)
    def _():
        for cp in weight_copies(0, 0):
            cp.start()

    @pl.when(first)
    def _():
        for cp in weight_copies(seg, slot):
            cp.wait()

        @pl.when(seg + 1 < nv_ref[1])
        def _():
            for cp in weight_copies(seg + 1, 1 - slot):
                cp.start()

    def convert_weights():
        wg_bf[...] = wg_f32[slot].astype(BF16)
        wu_bf[...] = wu_f32[slot].astype(BF16)
        wd_bf[...] = wd_f32[slot].astype(BF16)

    def compute(y_ref):
        lo, hi = _unpack_pair(x_ref[...])
        half = lo.shape[1]
        lo = lo.astype(BF16)
        hi = hi.astype(BF16)
        gate = (jnp.dot(lo, wg_bf[:half, :], preferred_element_type=F32)
                + jnp.dot(hi, wg_bf[half:, :], preferred_element_type=F32))
        up = (jnp.dot(lo, wu_bf[:half, :], preferred_element_type=F32)
              + jnp.dot(hi, wu_bf[half:, :], preferred_element_type=F32))
        act = (_silu(gate) * up).astype(BF16)
        y = jnp.dot(act, wd_bf[...], preferred_element_type=F32)
        y_ref[...] = _pack_pair(y[:, :half], y[:, half:])

    def scatter(block, parity):
        base = block * bm
        spare = n_slot_rows + parity * bm
        for r in range(bm):
            d = inv_ref[base + r]
            d = jnp.where(d < 0, spare + r, d)
            pltpu.make_async_copy(ybufs[parity].at[pl.ds(r, 1)], ysl_ref.at[pl.ds(d, 1)],
                                  ysems.at[parity]).start()

    for p in range(Y_BUFFERS):
        mine = (b % Y_BUFFERS) == p
        before = (p - 1) % Y_BUFFERS

        @pl.when(mine & (b >= Y_BUFFERS) & (b - Y_BUFFERS < n_valid))
        def _(p=p):
            pltpu.make_async_copy(ybufs[p], ysl_ref.at[pl.ds(0, bm)], ysems.at[p]).wait()

        @pl.when(mine & (b >= 1) & (b < n_valid) & jnp.logical_not(first))
        def _(p=p, before=before):
            scatter(b - 1, before)
            compute(ybufs[p])

        @pl.when(mine & (b >= 1) & (b < n_valid) & first)
        def _(p=p, before=before):
            convert_weights()
            scatter(b - 1, before)
            compute(ybufs[p])

        @pl.when(mine & (b >= 1) & (b == n_valid))
        def _(before=before):
            scatter(b - 1, before)

    @pl.when(b == 0)
    def _():
        spare_fill = [pltpu.make_async_copy(
            x_ref, ysl_ref.at[pl.ds(n_slot_rows + parity * bm, bm)], ysems.at[parity])
            for parity in range(Y_BUFFERS)]
        for cp in spare_fill:
            cp.start()
        for cp in spare_fill:
            cp.wait()
        convert_weights()
        compute(ybufs[0])


def _experts(xs, inv, seg_of, seg_e, n_valid, wg, wu, wd, n_tok, bm=EXPERT_ROWS):
    rows, width = xs.shape
    _, d_model, de = wg.shape
    nb = rows // bm
    n_slot_rows = TOP_K * n_tok
    row_map = lambda b, sg, se, nv, iv: (jnp.minimum(b, nv[0] - 1), 0)
    hbm = lambda: pl.BlockSpec(memory_space=pl.ANY)
    grid_spec = pltpu.PrefetchScalarGridSpec(
        num_scalar_prefetch=4,
        grid=(nb + Y_BUFFERS,),
        in_specs=[pl.BlockSpec((bm, width), row_map), hbm(), hbm(), hbm()],
        out_specs=hbm(),
        scratch_shapes=[pltpu.VMEM((2, d_model, de), F32),
                        pltpu.VMEM((2, d_model, de), F32),
                        pltpu.VMEM((2, de, d_model), F32),
                        pltpu.VMEM((d_model, de), BF16),
                        pltpu.VMEM((d_model, de), BF16),
                        pltpu.VMEM((de, d_model), BF16),
                        ]
                       + [pltpu.VMEM((bm, width), U32)] * Y_BUFFERS
                       + [pltpu.SemaphoreType.DMA((2,)),
                          pltpu.SemaphoreType.DMA((Y_BUFFERS,))],
    )
    return pl.pallas_call(
        functools.partial(_experts_kernel, bm, n_slot_rows),
        grid_spec=grid_spec,
        out_shape=jax.ShapeDtypeStruct((n_slot_rows + Y_BUFFERS * bm, width), U32),
        compiler_params=_params(("arbitrary",)),
        name="experts",
    )(seg_of, seg_e, n_valid, inv, xs, wg, wu, wd)


def _combine_kernel(x_ref, h_ref, wt_ref, g_ref, sg_ref, su_ref, sd_ref, *rest):
    y_refs, o_ref = rest[:TOP_K], rest[TOP_K]
    h = h_ref[...]
    act = (_silu(jnp.dot(h, sg_ref[...], preferred_element_type=F32))
           * jnp.dot(h, su_ref[...], preferred_element_type=F32)).astype(BF16)
    shared = jnp.dot(act, sd_ref[...], preferred_element_type=F32)
    half = y_refs[0].shape[1]
    wt = wt_ref[...]
    lo_acc = shared[:, :half]
    hi_acc = shared[:, half:]
    for kk in range(TOP_K):
        lo, hi = _unpack_pair(y_refs[kk][...])
        wk = wt[:, kk:kk + 1]
        lo_acc = lo_acc + wk * lo
        hi_acc = hi_acc + wk * hi
    g = g_ref[...]
    o_ref[:, :half] = x_ref[:, :half] + g[:, :half] * lo_acc
    o_ref[:, half:] = x_ref[:, half:] + g[:, half:] * hi_acc


def _combine(x1, h2, wts, mod, gate_blk, sg, su, sd, ysl, tm=256):
    t, d_model = x1.shape
    ds_ = sg.shape[1]
    width = ysl.shape[1]
    tiles = t // tm
    slot = lambda kk: pl.BlockSpec((tm, width), lambda i, kk=kk: (kk * tiles + i, 0))
    return pl.pallas_call(
        _combine_kernel,
        grid=(tiles,),
        in_specs=[pl.BlockSpec((tm, d_model), lambda i: (i, 0)),
                  pl.BlockSpec((tm, d_model), lambda i: (i, 0)),
                  pl.BlockSpec((tm, TOP_K), lambda i: (i, 0)),
                  pl.BlockSpec((1, d_model), lambda i: (0, gate_blk)),
                  pl.BlockSpec((d_model, ds_), lambda i: (0, 0)),
                  pl.BlockSpec((d_model, ds_), lambda i: (0, 0)),
                  pl.BlockSpec((ds_, d_model), lambda i: (0, 0))]
                 + [slot(kk) for kk in range(TOP_K)],
        out_specs=pl.BlockSpec((tm, d_model), lambda i: (i, 0)),
        out_shape=jax.ShapeDtypeStruct((t, d_model), F32),
        compiler_params=_params(("parallel",)),
        name="combine",
    )(x1, h2, wts, mod, sg, su, sd, *([ysl] * TOP_K))


def _layout_kernel(bm, cnt_ref, idx_ref, rank_ref, pos_ref, seg_ref, sege_ref, nv_ref, fs_ref, fl_ref):
    shift = bm.bit_length() - 1
    pos_ref[...] = rank_ref[...]

    def per_expert(e, carry):
        start, blk, seg = carry
        cnt = cnt_ref[e]
        nblk = (cnt + (bm - 1)) >> shift
        pos_ref[...] = pos_ref[...] + jnp.where(idx_ref[...] == e, start, 0)

        def mark(b, c):
            seg_ref[blk + b] = seg
            return c

        lax.fori_loop(0, nblk, mark, 0)

        @pl.when(nblk > 0)
        def _():
            sege_ref[seg] = e

        fs_ref[e] = start + cnt
        fl_ref[e] = (nblk << shift) - cnt
        return start + (nblk << shift), blk + nblk, seg + jnp.where(nblk > 0, 1, 0)

    zero = jnp.int32(0)
    _, n_valid, n_seg = lax.fori_loop(0, N_EXPERTS, per_expert, (zero, zero, zero))
    nv_ref[0] = n_valid
    nv_ref[1] = n_seg

    def tail_blocks(b, c):
        seg_ref[b] = n_seg - 1
        return c

    lax.fori_loop(n_valid, seg_ref.shape[0], tail_blocks, 0)

    def tail_segs(s, c):
        sege_ref[s] = N_EXPERTS - 1
        return c

    lax.fori_loop(n_seg, N_EXPERTS, tail_segs, 0)


def _layout(counts, idx_t, rank_t, bm, n_blocks):
    assert bm & (bm - 1) == 0
    k, t = idx_t.shape
    smem = lambda: pl.BlockSpec(memory_space=pltpu.SMEM)
    full = lambda: pl.BlockSpec((k, t), lambda: (0, 0))
    return pl.pallas_call(
        functools.partial(_layout_kernel, bm),
        in_specs=[smem(), full(), full()],
        out_specs=[full(), smem(), smem(), smem(), smem(), smem()],
        out_shape=[jax.ShapeDtypeStruct((k, t), I32),
                   jax.ShapeDtypeStruct((n_blocks,), I32),
                   jax.ShapeDtypeStruct((N_EXPERTS,), I32),
                   jax.ShapeDtypeStruct((2,), I32),
                   jax.ShapeDtypeStruct((N_EXPERTS,), I32),
                   jax.ShapeDtypeStruct((N_EXPERTS,), I32)],
        name="layout",
    )(counts.reshape(-1).astype(I32), idx_t, rank_t)


def _layer(x, c, rel_bias, w_ada, b_ada, ln1_g, w_in, q_norm_g, k_norm_g, ret_gn_g, p_a, p_b, w_o,
           ln2_g, router_w, router_bias, w_gate_e, w_up_e, w_down_e, w_gate_s, w_up_s, w_down_s):
    t, d_model = x.shape
    dils = tuple(d for _, d in DILATED_GROUPS)

    mod = _ada(c.reshape(d_model), w_ada, b_ada)
    h_orders = _norm1(x, ln1_g, mod, dils[1:])
    cos_tab, sin_tab = _rotary_tables(t)
    projs = []
    for order, (cols, epis) in enumerate(_inproj_plan(d_model)):
        projs.append(_inproj(h_orders[order], w_in, cols, epis, q_norm_g, k_norm_g, cos_tab, sin_tab,
                             f"inproj_d{dils[order]}"))
    proj = projs[0]

    attn = [_attn_group(projs[gi], rel_bias, gi, win, dil, 0, 1, 2)
            for gi, (win, dil) in enumerate(DILATED_GROUPS)]
    base = 3
    rq = RET_HEADS * RET_QK_DIM // COLBLK
    vw_blk = RET_HEADS * RET_V_DIM // COLBLK
    qcol = base
    kcol = base + rq
    vcol_blk = base + 2 * rq
    gcol_blk = vcol_blk + vw_blk
    ga_blk = gcol_blk + vw_blk
    gb_blk = ga_blk + d_model // COLBLK
    y_b = _retention(proj, ret_gn_g, qcol, kcol, vcol_blk, gcol_blk)
    (o1, l1), (o2, l2), (o3, l3) = attn
    l2 = _from_residue_major(l2, dils[1])
    l3 = _from_residue_major(l3, dils[2])
    merged = _merge(o1, l1, o2, l2, o3, l3, y_b, proj, ga_blk, gb_blk,
                    p_a.astype(BF16), p_b.astype(BF16), dils[1:])
    x1 = _oproj(x, merged, w_o.astype(BF16), mod, 2)

    h2, h2p, idx_t, rank_t, wgt_t, counts = _route(x1, ln2_g, mod, 4, 3, router_w, router_bias)
    bm = EXPERT_ROWS
    n_blocks = (t * TOP_K + N_EXPERTS * (bm - 1) + bm - 1) // bm
    pos_t, seg_of, seg_e, n_valid, fill_start, fill_len = _layout(counts, idx_t, rank_t, bm, n_blocks)
    xs = _dispatch(h2p, pos_t, fill_start, fill_len, n_valid, n_blocks * bm, bm)
    inv = _invert(pos_t, fill_start, fill_len, n_valid, n_blocks * bm, bm)
    ysl = _experts(xs, inv, seg_of, seg_e, n_valid, w_gate_e, w_up_e, w_down_e, t)
    return _combine(x1, h2, wgt_t.T, mod, 5, w_gate_s.astype(BF16), w_up_s.astype(BF16),
                    w_down_s.astype(BF16), ysl)


def kernel(x, c, rel_bias, w_ada, b_ada, ln1_g, w_in, q_norm_g, k_norm_g, ret_gn_g, p_a, p_b, w_o,
           ln2_g, router_w, router_bias, w_gate_e, w_up_e, w_down_e, w_gate_s, w_up_s, w_down_s):
    b, s, d_model = x.shape
    depth = w_ada.shape[0]
    outs = []
    for bi in range(b):
        xb = x[bi]
        for l in range(depth):
            xb = _layer(xb, c[bi], rel_bias, w_ada[l], b_ada[l], ln1_g[l], w_in[l], q_norm_g[l],
                        k_norm_g[l], ret_gn_g[l], p_a[l], p_b[l], w_o[l], ln2_g[l], router_w[l],
                        router_bias[l], w_gate_e[l], w_up_e[l], w_down_e[l], w_gate_s[l],
                        w_up_s[l], w_down_s[l])
        outs.append(xb)
    return jnp.stack(outs, axis=0)
```

```python
import functools

import numpy as np
import jax
import jax.numpy as jnp
from jax import lax
from jax.experimental import pallas as pl
from jax.experimental.pallas import tpu as pltpu

F32 = jnp.float32
BF16 = jnp.bfloat16
U32 = jnp.uint32
I32 = jnp.int32

HEAD_DIM = 128
DILATED_GROUPS = ((128, 1), (512, 4), (2048, 16))
HEADS_PER_GROUP = 8
N_HEADS_A = HEADS_PER_GROUP * len(DILATED_GROUPS)
A_GROUP_WIDTH = HEADS_PER_GROUP * HEAD_DIM
ATTN_BLOCK = 128
NUM_BUCKETS = 32
MAX_DISTANCE = 2048
NEG_INF = -1e30
RET_HEADS = 8
RET_QK_DIM = 128
RET_V_DIM = 256
RET_CHUNK = 128
ROPE_BASE = 10000.0
GN_EPS = 1e-5
N_EXPERTS = 64
N_GROUPS = 8
TOPK_GROUPS = 4
TOP_K = 8
ROUTED_SCALE = 2.5
RMS_EPS = 1e-6

LANE = 128
COLBLK = 1024
VMEM_LIMIT = 56 * 1024 * 1024
EXPERT_ROWS = 256


def _params(sem, vmem=VMEM_LIMIT):
    return pltpu.CompilerParams(dimension_semantics=sem, vmem_limit_bytes=vmem)


def _sigmoid(v):
    return 0.5 * jnp.tanh(0.5 * v) + 0.5


def _silu(v):
    return v * _sigmoid(v)


def _ada_kernel(c_ref, w_ref, b_ref, o_ref):
    sc = _silu(c_ref[...])
    o_ref[...] = jnp.sum(w_ref[...] * sc, axis=0, keepdims=True) + b_ref[...]


def _ada(c, w, b, tn=512):
    d, n = w.shape
    return pl.pallas_call(
        _ada_kernel,
        grid=(n // tn,),
        in_specs=[pl.BlockSpec((d, 1), lambda j: (0, 0)),
                  pl.BlockSpec((d, tn), lambda j: (0, j)),
                  pl.BlockSpec((1, tn), lambda j: (0, j))],
        out_specs=pl.BlockSpec((1, tn), lambda j: (0, j)),
        out_shape=jax.ShapeDtypeStruct((1, n), F32),
        compiler_params=_params(("parallel",)),
        name="ada",
    )(c.reshape(d, 1), w, b.reshape(1, n))


PERM_TILE = 256


def _perm_matrix(d, to_residue):
    tm = PERM_TILE
    n = tm // d
    assert d & (d - 1) == 0 and n & (n - 1) == 0
    ii = lax.broadcasted_iota(I32, (tm, tm), 0)
    jj = lax.broadcasted_iota(I32, (tm, tm), 1)
    if to_residue:
        src = (ii & (n - 1)) * d + (ii >> (n.bit_length() - 1))
    else:
        src = (ii & (d - 1)) * n + (ii >> (d.bit_length() - 1))
    return jnp.where(jj == src, 1.0, 0.0).astype(BF16)


def _norm1_kernel(dils, x_ref, g_ref, sc_ref, sh_ref, o_ref, *res_refs):
    x = x_ref[...]
    inv = lax.rsqrt(jnp.mean(x * x, axis=-1, keepdims=True) + RMS_EPS)
    h = ((x * inv * g_ref[...]) * (1.0 + sc_ref[...]) + sh_ref[...]).astype(o_ref.dtype)
    o_ref[...] = h
    tm, width = h.shape
    for d, r_ref in zip(dils, res_refs):
        perm = _perm_matrix(d, True)
        n = PERM_TILE // d
        for s in range(tm // PERM_TILE):
            sub = h[s * PERM_TILE:(s + 1) * PERM_TILE, :]
            y = jnp.dot(perm, sub, preferred_element_type=F32).astype(r_ref.dtype)
            r_ref[:, s * n:(s + 1) * n, :] = y.reshape(d, n, width)


def _norm1(x, g, mod, dils, tm=512):
    t, d_model = x.shape
    vec = lambda k: pl.BlockSpec((1, d_model), lambda i, k=k: (0, k))
    out_shapes = [jax.ShapeDtypeStruct((t, d_model), BF16)]
    out_specs = [pl.BlockSpec((tm, d_model), lambda i: (i, 0))]
    for d in dils:
        out_shapes.append(jax.ShapeDtypeStruct((d, t // d, d_model), BF16))
        out_specs.append(pl.BlockSpec((d, tm // d, d_model), lambda i: (0, i, 0)))
    outs = pl.pallas_call(
        functools.partial(_norm1_kernel, dils),
        grid=(t // tm,),
        in_specs=[pl.BlockSpec((tm, d_model), lambda i: (i, 0)),
                  pl.BlockSpec((1, d_model), lambda i: (0, 0)),
                  vec(1), vec(0)],
        out_specs=out_specs,
        out_shape=out_shapes,
        compiler_params=_params(("parallel",)),
        name="norm1",
    )(x, g.reshape(1, d_model), mod, mod)
    return [o.reshape(t, d_model) for o in outs]


def _from_residue_major(a, d):
    t, w = a.shape
    return a.reshape(d, t // d, w).transpose(1, 0, 2).reshape(t, w)


EPI_QNORM, EPI_KNORM, EPI_PLAIN, EPI_ROT_Q, EPI_ROT_K, EPI_SILU, EPI_SIGMOID = range(7)
INPROJ_ROW_CHUNK = 256


def _inproj_kernel(epis_present, colblk_ref, epi_ref, h_ref, w_ref, qg_ref, kg_ref, cos_ref, sin_ref,
                   o_ref, wbf_ref):
    del colblk_ref
    epi = epi_ref[pl.program_id(0)]
    tm = h_ref.shape[0]
    nh = o_ref.shape[1] // HEAD_DIM

    @pl.when(pl.program_id(1) == 0)
    def _():
        wbf_ref[...] = w_ref[...].astype(BF16)

    def head_norm(gain, scale):
        def fn(acc, rows):
            for hh in range(nh):
                sl = slice(hh * HEAD_DIM, (hh + 1) * HEAD_DIM)
                a = acc[:, sl]
                inv = lax.rsqrt(jnp.mean(a * a, axis=-1, keepdims=True) + RMS_EPS)
                o_ref[rows, sl] = ((a * inv * gain) * scale).astype(o_ref.dtype)
        return fn

    def rotary(scale):
        def fn(acc, rows):
            cos = cos_ref[rows, :]
            sin = sin_ref[rows, :]
            for hh in range(nh):
                sl = slice(hh * HEAD_DIM, (hh + 1) * HEAD_DIM)
                a = acc[:, sl]
                rot = pltpu.roll(a, HEAD_DIM // 2, 1)
                o_ref[rows, sl] = ((a * cos + rot * sin) * scale).astype(o_ref.dtype)
        return fn

    def elementwise(f):
        def fn(acc, rows):
            o_ref[rows, :] = f(acc).astype(o_ref.dtype)
        return fn

    epilogues = {
        EPI_QNORM: lambda: head_norm(qg_ref[...], HEAD_DIM ** -0.5),
        EPI_KNORM: lambda: head_norm(kg_ref[...], 1.0),
        EPI_PLAIN: lambda: elementwise(lambda a: a),
        EPI_ROT_Q: lambda: rotary(1.0),
        EPI_ROT_K: lambda: rotary(RET_QK_DIM ** -0.5),
        EPI_SILU: lambda: elementwise(_silu),
        EPI_SIGMOID: lambda: elementwise(_sigmoid),
    }
    for code in epis_present:
        @pl.when(epi == code)
        def _(code=code):
            fn = epilogues[code]()
            for c in range(tm // INPROJ_ROW_CHUNK):
                rows = slice(c * INPROJ_ROW_CHUNK, (c + 1) * INPROJ_ROW_CHUNK)
                acc = jnp.dot(h_ref[rows, :], wbf_ref[...], preferred_element_type=F32)
                fn(acc, rows)


def _inproj_plan(d_model):
    a_blocks = N_HEADS_A * HEAD_DIM // COLBLK
    groups = len(DILATED_GROUPS)
    per_group = a_blocks // groups
    rq = RET_HEADS * RET_QK_DIM // COLBLK
    rv = RET_HEADS * RET_V_DIM // COLBLK
    gd = d_model // COLBLK
    seg_epi = ([EPI_QNORM] * a_blocks + [EPI_KNORM] * a_blocks + [EPI_PLAIN] * a_blocks
               + [EPI_ROT_Q] * rq + [EPI_ROT_K] * rq + [EPI_PLAIN] * rv + [EPI_SILU] * rv
               + [EPI_SIGMOID] * (2 * gd))
    order_of = [0] * len(seg_epi)
    for seg in range(3):
        for blk in range(a_blocks):
            order_of[seg * a_blocks + blk] = blk // per_group
    plans = []
    for order in range(groups):
        cols = [cb for cb in range(len(seg_epi)) if order_of[cb] == order]
        plans.append((cols, [seg_epi[cb] for cb in cols]))
    return plans


def _inproj(h, w, cols, epis, qg, kg, cos_tab, sin_tab, name, tm=1024):
    t, d_model = h.shape
    row = lambda width: pl.BlockSpec((tm, width), lambda j, i, cb, ep: (i, 0))
    one = lambda width: pl.BlockSpec((1, width), lambda j, i, cb, ep: (0, 0))
    grid_spec = pltpu.PrefetchScalarGridSpec(
        num_scalar_prefetch=2,
        grid=(len(cols), t // tm),
        in_specs=[
            row(d_model),
            pl.BlockSpec((d_model, COLBLK), lambda j, i, cb, ep: (0, cb[j])),
            one(HEAD_DIM), one(HEAD_DIM), row(HEAD_DIM), row(HEAD_DIM),
        ],
        out_specs=pl.BlockSpec((tm, COLBLK), lambda j, i, cb, ep: (i, j)),
        scratch_shapes=[pltpu.VMEM((d_model, COLBLK), BF16)],
    )
    return pl.pallas_call(
        functools.partial(_inproj_kernel, tuple(sorted(set(epis)))),
        grid_spec=grid_spec,
        out_shape=jax.ShapeDtypeStruct((t, len(cols) * COLBLK), BF16),
        compiler_params=_params(("arbitrary", "arbitrary")),
        name=name,
    )(jnp.asarray(np.array(cols, np.int32)), jnp.asarray(np.array(epis, np.int32)),
      h, w, qg.reshape(1, HEAD_DIM), kg.reshape(1, HEAD_DIM), cos_tab, sin_tab)


def _rotary_tables(t):
    inv = ROPE_BASE ** (-np.arange(0, RET_QK_DIM, 2, dtype=np.float64) / RET_QK_DIM)
    ang = np.arange(t, dtype=np.float64)[:, None] * inv[None, :]
    cos, sin = np.cos(ang), np.sin(ang)
    cos_tab = np.concatenate([cos, cos], axis=1).astype(np.float32)
    sin_tab = np.concatenate([-sin, sin], axis=1).astype(np.float32)
    return jnp.asarray(cos_tab), jnp.asarray(sin_tab)


def _t5_bucket(dist):
    max_exact = NUM_BUCKETS // 2
    safe = np.maximum(dist, 1).astype(np.float32)
    large = max_exact + (np.log(safe / max_exact) / np.log(MAX_DISTANCE / max_exact)
                         * (NUM_BUCKETS - max_exact)).astype(np.int32)
    return np.where(dist < max_exact, dist, np.minimum(large, NUM_BUCKETS - 1)).astype(np.int32)


def _attn_kernel(head0, w_steps, blocks_per_res, tab_ref, bucket_ref, q_ref, kp_ref, kc_ref,
                 vp_ref, vc_ref, o_ref, lse_ref, bias_ref, band_ref, s_ref, p_ref):
    m_idx = pl.program_id(0)
    blk = ATTN_BLOCK

    @pl.when(m_idx == 0)
    def _():
        bucket = bucket_ref[...]
        for hh in range(HEADS_PER_GROUP):
            bias = jnp.zeros(bucket.shape, F32)
            for b in range(NUM_BUCKETS):
                bias = jnp.where(bucket == b, tab_ref[b, head0 + hh], bias)
            bias_ref[hh] = bias
        a = lax.broadcasted_iota(I32, (blk, 2 * blk), 0)
        cc = lax.broadcasted_iota(I32, (blk, 2 * blk), 1)
        delta = blk + a - cc
        band_ref[...] = jnp.where((delta >= 0) & (delta <= w_steps), 1.0, 0.0)

    prev_thr = jnp.where((m_idx % blocks_per_res) > 0, 0.5, 2.0)
    nt = (((1,), (1,)), ((), ()))
    heads = range(HEADS_PER_GROUP)
    head_cols = [slice(hh * HEAD_DIM, (hh + 1) * HEAD_DIM) for hh in heads]
    for hh, sl in zip(heads, head_cols):
        q = q_ref[:, sl]
        s_p = lax.dot_general(q, kp_ref[:, sl], nt, preferred_element_type=F32)
        s_c = lax.dot_general(q, kc_ref[:, sl], nt, preferred_element_type=F32)
        s_ref[hh, :, :blk] = jnp.where(band_ref[:, :blk] > prev_thr,
                                       s_p + bias_ref[hh, :, :blk], NEG_INF)
        s_ref[hh, :, blk:] = jnp.where(band_ref[:, blk:] > 0.5,
                                       s_c + bias_ref[hh, :, blk:], NEG_INF)
    dens, lses = [], []
    for hh in heads:
        s = s_ref[hh]
        mx = jnp.max(s, axis=-1, keepdims=True)
        p = jnp.exp(s - mx)
        den = jnp.sum(p, axis=-1, keepdims=True)
        p_ref[hh] = p.astype(BF16)
        dens.append(den)
        lses.append(mx + jnp.log(den))
    for hh, sl in zip(heads, head_cols):
        v_both = jnp.concatenate([vp_ref[:, sl], vc_ref[:, sl]], axis=0)
        acc = jnp.dot(p_ref[hh], v_both, preferred_element_type=F32)
        o_ref[:, sl] = (acc / dens[hh]).astype(o_ref.dtype)
    lse_ref[...] = jnp.concatenate(lses, axis=-1)


def _attn_group(proj, rel_bias, gi, window, dilation, qcol, kcol, vcol):
    t = proj.shape[0]
    blk = ATTN_BLOCK
    w_steps = window // dilation
    blocks_per_res = t // dilation // blk
    nblk = t // blk
    a = np.arange(blk)[:, None]
    cc = np.arange(2 * blk)[None, :]
    bucket = _t5_bucket(np.maximum(blk + a - cc, 0) * dilation)

    def prev_map(m):
        return jnp.where(m % blocks_per_res > 0, m - 1, m)

    kern = functools.partial(_attn_kernel, gi * HEADS_PER_GROUP, w_steps, blocks_per_res)
    width = A_GROUP_WIDTH
    return pl.pallas_call(
        kern,
        grid=(nblk,),
        in_specs=[
            pl.BlockSpec(memory_space=pltpu.SMEM),
            pl.BlockSpec((blk, 2 * blk), lambda m: (0, 0)),
            pl.BlockSpec((blk, width), lambda m: (m, qcol)),
            pl.BlockSpec((blk, width), lambda m: (prev_map(m), kcol)),
            pl.BlockSpec((blk, width), lambda m: (m, kcol)),
            pl.BlockSpec((blk, width), lambda m: (prev_map(m), vcol)),
            pl.BlockSpec((blk, width), lambda m: (m, vcol)),
        ],
        out_specs=[pl.BlockSpec((blk, width), lambda m: (m, 0)),
                   pl.BlockSpec((blk, HEADS_PER_GROUP), lambda m: (m, 0))],
        out_shape=[jax.ShapeDtypeStruct((t, width), BF16),
                   jax.ShapeDtypeStruct((t, HEADS_PER_GROUP), F32)],
        scratch_shapes=[pltpu.VMEM((HEADS_PER_GROUP, blk, 2 * blk), F32),
                        pltpu.VMEM((blk, 2 * blk), F32),
                        pltpu.VMEM((HEADS_PER_GROUP, blk, 2 * blk), F32),
                        pltpu.VMEM((HEADS_PER_GROUP, blk, 2 * blk), BF16)],
        compiler_params=_params(("arbitrary",)),
        name=f"attn_d{dilation}",
    )(rel_bias, jnp.asarray(bucket), proj, proj, proj, proj, proj)


def _retention_kernel(q_ref, k_ref, v0_ref, v1_ref, g0_ref, g1_ref, dmat_ref, zeta_ref, xi_ref,
                      gch_ref, gn_ref, o_ref, state_ref, s_ref, cross_ref):
    @pl.when(pl.program_id(0) == 0)
    def _():
        state_ref[...] = jnp.zeros_like(state_ref)

    nt = (((1,), (1,)), ((), ()))
    tn = (((0,), (0,)), ((), ()))
    per_half = RET_HEADS // 2

    def head_refs(hh):
        qs = slice(hh * RET_QK_DIM, (hh + 1) * RET_QK_DIM)
        vs = slice(hh * RET_V_DIM, (hh + 1) * RET_V_DIM)
        hs = slice((hh % per_half) * RET_V_DIM, (hh % per_half + 1) * RET_V_DIM)
        v_ref, g_ref = (v0_ref, g0_ref) if hh < per_half else (v1_ref, g1_ref)
        return qs, vs, hs, v_ref, g_ref

    for hh in range(RET_HEADS):
        qs, _, hs, v_ref, _ = head_refs(hh)
        q = q_ref[:, qs]
        k = k_ref[:, qs]
        v = v_ref[:, hs]
        state = state_ref[hh]
        s = lax.dot_general(q, k, nt, preferred_element_type=F32) * dmat_ref[hh]
        s_ref[hh] = s.astype(BF16)
        cross_ref[hh] = jnp.dot(q, state.astype(BF16), preferred_element_type=F32) * xi_ref[hh]
        vz = (v.astype(F32) * zeta_ref[hh]).astype(BF16)
        upd = lax.dot_general(k, vz, tn, preferred_element_type=F32)
        state_ref[hh] = gch_ref[hh] * state + upd
    for hh in range(RET_HEADS):
        _, vs, hs, v_ref, g_ref = head_refs(hh)
        inner = jnp.dot(s_ref[hh], v_ref[:, hs], preferred_element_type=F32)
        ret = inner + cross_ref[hh]
        mu = jnp.mean(ret, axis=-1, keepdims=True)
        cen = ret - mu
        var = jnp.mean(cen * cen, axis=-1, keepdims=True)
        y = cen * lax.rsqrt(var + GN_EPS) * gn_ref[:, vs]
        o_ref[:, vs] = (y * g_ref[:, hs].astype(F32)).astype(o_ref.dtype)


def _retention_tables():
    c = RET_CHUNK
    hh = np.arange(RET_HEADS, dtype=np.float64)
    log_g = np.log1p(-np.exp2(-5.0 - hh))
    idx = np.arange(c, dtype=np.float64)
    diff = idx[:, None] - idx[None, :]
    dmat = np.where(diff >= 0, np.exp(log_g[:, None, None] * np.maximum(diff, 0.0)), 0.0)
    zeta = np.exp(log_g[:, None] * (c - 1 - idx))[:, :, None]
    xi = np.exp(log_g[:, None] * (idx + 1.0))[:, :, None]
    gch = np.exp(log_g * c)
    f = lambda v: jnp.asarray(v.astype(np.float32))
    return f(dmat), f(zeta), f(xi), f(gch)


def _retention(proj, gn_g, qcol, kcol, vcol, gcol):
    t = proj.shape[0]
    c = RET_CHUNK
    qw = RET_HEADS * RET_QK_DIM
    vw = RET_HEADS * RET_V_DIM
    dmat, zeta, xi, gch = _retention_tables()
    full3 = lambda shp: pl.BlockSpec(shp, lambda n: (0, 0, 0))
    return pl.pallas_call(
        _retention_kernel,
        grid=(t // c,),
        in_specs=[
            pl.BlockSpec((c, qw), lambda n: (n, qcol)),
            pl.BlockSpec((c, qw), lambda n: (n, kcol)),
            pl.BlockSpec((c, vw // 2), lambda n: (n, vcol)),
            pl.BlockSpec((c, vw // 2), lambda n: (n, vcol + 1)),
            pl.BlockSpec((c, vw // 2), lambda n: (n, gcol)),
            pl.BlockSpec((c, vw // 2), lambda n: (n, gcol + 1)),
            full3((RET_HEADS, c, c)),
            full3((RET_HEADS, c, 1)),
            full3((RET_HEADS, c, 1)),
            pl.BlockSpec(memory_space=pltpu.SMEM),
            pl.BlockSpec((1, vw), lambda n: (0, 0)),
        ],
        out_specs=pl.BlockSpec((c, vw), lambda n: (n, 0)),
        out_shape=jax.ShapeDtypeStruct((t, vw), BF16),
        scratch_shapes=[pltpu.VMEM((RET_HEADS, RET_QK_DIM, RET_V_DIM), F32),
                        pltpu.VMEM((RET_HEADS, c, c), BF16),
                        pltpu.VMEM((RET_HEADS, c, RET_V_DIM), F32)],
        compiler_params=_params(("arbitrary",)),
        name="retention",
    )(proj, proj, proj, proj, proj, proj, dmat, zeta, xi, gch, gn_g.reshape(1, vw))


MERGE_ROW_CHUNK = PERM_TILE


def _merge_kernel(dils, o1_ref, l1_ref, o2_ref, l2_ref, o3_ref, l3_ref, yb_ref, ga_ref, gb_ref,
                  pa_ref, pb_ref, out_ref):
    tm = out_ref.shape[0]
    width = o1_ref.shape[1]
    perms = [_perm_matrix(d, False) for d in dils]

    def token_order(res_ref, gi, c):
        n = PERM_TILE // dils[gi]
        blk = res_ref[:, c * n:(c + 1) * n, :].reshape(PERM_TILE, width)
        return jnp.dot(perms[gi], blk, preferred_element_type=F32)

    for c in range(tm // MERGE_ROW_CHUNK):
        rows = slice(c * MERGE_ROW_CHUNK, (c + 1) * MERGE_ROW_CHUNK)
        o2 = token_order(o2_ref, 0, c)
        o3 = token_order(o3_ref, 1, c)
        l1 = l1_ref[rows, :]
        l2 = l2_ref[rows, :]
        l3 = l3_ref[rows, :]
        mx = jnp.maximum(jnp.maximum(l1, l2), l3)
        e1 = jnp.exp(l1 - mx)
        e2 = jnp.exp(l2 - mx)
        e3 = jnp.exp(l3 - mx)
        den = e1 + e2 + e3
        a1, a2, a3 = e1 / den, e2 / den, e3 / den
        pieces = []
        for hh in range(HEADS_PER_GROUP):
            sl = slice(hh * HEAD_DIM, (hh + 1) * HEAD_DIM)
            ya = (a1[:, hh:hh + 1] * o1_ref[rows, sl] + a2[:, hh:hh + 1] * o2[:, sl]
                  + a3[:, hh:hh + 1] * o3[:, sl])
            pieces.append(ya.astype(BF16))
        ya = jnp.concatenate(pieces, axis=1)
        za = jnp.dot(ya, pa_ref[...], preferred_element_type=F32)
        zb = jnp.dot(yb_ref[rows, :], pb_ref[...], preferred_element_type=F32)
        out_ref[rows, :] = (ga_ref[rows, :].astype(F32) * za
                            + gb_ref[rows, :].astype(F32) * zb).astype(out_ref.dtype)


def _merge(o1, l1, o2, l2, o3, l3, yb, proj, ga_col, gb_col, pa, pb, dils, tm=512, tn=1024):
    t = o1.shape[0]
    wa = o1.shape[1]
    wb = yb.shape[1]
    n = pa.shape[1]
    hg = HEADS_PER_GROUP
    ratio = tn // COLBLK
    o_spec = lambda: pl.BlockSpec((tm, wa), lambda j, i: (i, 0))
    l_spec = lambda: pl.BlockSpec((tm, hg), lambda j, i: (i, 0))
    res_spec = lambda d: pl.BlockSpec((d, tm // d, wa), lambda j, i: (0, i, 0))
    o2 = o2.reshape(dils[0], t // dils[0], wa)
    o3 = o3.reshape(dils[1], t // dils[1], wa)
    return pl.pallas_call(
        functools.partial(_merge_kernel, dils),
        grid=(n // tn, t // tm),
        in_specs=[
            o_spec(), l_spec(), res_spec(dils[0]), l_spec(), res_spec(dils[1]), l_spec(),
            pl.BlockSpec((tm, wb), lambda j, i: (i, 0)),
            pl.BlockSpec((tm, tn), lambda j, i: (i, ga_col // ratio + j)),
            pl.BlockSpec((tm, tn), lambda j, i: (i, gb_col // ratio + j)),
            pl.BlockSpec((wa, tn), lambda j, i: (0, j)),
            pl.BlockSpec((wb, tn), lambda j, i: (0, j)),
        ],
        out_specs=pl.BlockSpec((tm, tn), lambda j, i: (i, j)),
        out_shape=jax.ShapeDtypeStruct((t, n), BF16),
        compiler_params=_params(("parallel", "parallel")),
        name="merge",
    )(o1, l1, o2, l2, o3, l3, yb, proj, proj, pa, pb)


def _oproj_kernel(x_ref, m_ref, w_ref, g_ref, o_ref):
    z = jnp.dot(m_ref[...], w_ref[...], preferred_element_type=F32)
    o_ref[...] = x_ref[...] + g_ref[...] * z


def _oproj(x, merged, w_bf, mod, gate_blk, tm=512, tn=1024):
    t, d_model = x.shape
    k = merged.shape[1]
    per = d_model // tn
    return pl.pallas_call(
        _oproj_kernel,
        grid=(d_model // tn, t // tm),
        in_specs=[
            pl.BlockSpec((tm, tn), lambda j, i: (i, j)),
            pl.BlockSpec((tm, k), lambda j, i: (i, 0)),
            pl.BlockSpec((k, tn), lambda j, i: (0, j)),
            pl.BlockSpec((1, tn), lambda j, i: (0, gate_blk * per + j)),
        ],
        out_specs=pl.BlockSpec((tm, tn), lambda j, i: (i, j)),
        out_shape=jax.ShapeDtypeStruct((t, d_model), F32),
        compiler_params=_params(("parallel", "parallel")),
        name="oproj",
    )(x, merged, w_bf, mod)


def _pack_pair(lo, hi):
    lo_b = pltpu.bitcast(lo.astype(BF16).astype(F32), U32)
    hi_b = pltpu.bitcast(hi.astype(BF16).astype(F32), U32)
    return (lo_b >> 16) | (hi_b & jnp.uint32(0xFFFF0000))


def _unpack_pair(w):
    lo = pltpu.bitcast(w << 16, F32)
    hi = pltpu.bitcast(w & jnp.uint32(0xFFFF0000), F32)
    return lo, hi


def _route_kernel(x_ref, g_ref, sc_ref, sh_ref, wt_ref, rb_ref, h_ref, hp_ref, idx_ref, rank_ref,
                  wgt_ref, cnt_ref):
    @pl.when(pl.program_id(0) == 0)
    def _():
        cnt_ref[...] = jnp.zeros_like(cnt_ref)

    x = x_ref[...]
    tm, d_model = x.shape
    inv = lax.rsqrt(jnp.mean(x * x, axis=-1, keepdims=True) + RMS_EPS)
    h = (x * inv * g_ref[...]) * (1.0 + sc_ref[...]) + sh_ref[...]
    h_ref[...] = h.astype(h_ref.dtype)
    half = d_model // 2
    hp_ref[...] = _pack_pair(h[:, :half], h[:, half:])

    ne = N_EXPERTS
    per = ne // N_GROUPS
    logits = lax.dot_general(wt_ref[...], h, (((1,), (1,)), ((), ())),
                             precision=lax.Precision.HIGHEST,
                             preferred_element_type=F32)
    scores = jax.nn.sigmoid(logits)
    sel = scores + rb_ref[...]
    eidx = lax.broadcasted_iota(I32, (ne, tm), 0).astype(F32)
    minus_inf = -jnp.inf

    sel3 = sel.reshape(N_GROUPS, per, tm)
    sub = lax.broadcasted_iota(I32, (N_GROUPS, per, tm), 1).astype(F32)
    m1 = jnp.max(sel3, axis=1, keepdims=True)
    first = jnp.min(jnp.where(sel3 == m1, sub, float(per)), axis=1, keepdims=True)
    m2 = jnp.max(jnp.where(sub == first, minus_inf, sel3), axis=1, keepdims=True)
    grp = (m1 + m2).reshape(N_GROUPS, tm)

    gidx = lax.broadcasted_iota(I32, (N_GROUPS, tm), 0).astype(F32)
    gmask = jnp.zeros((N_GROUPS, tm), F32)
    work = grp
    for _ in range(TOPK_GROUPS):
        mx = jnp.max(work, axis=0, keepdims=True)
        pick = jnp.min(jnp.where(work == mx, gidx, float(N_GROUPS)), axis=0, keepdims=True)
        hit = gidx == pick
        gmask = jnp.where(hit, 1.0, gmask)
        work = jnp.where(hit, minus_inf, work)
    emask = jnp.broadcast_to(gmask.reshape(N_GROUPS, 1, tm), (N_GROUPS, per, tm)).reshape(ne, tm)

    work = jnp.where(emask > 0.0, sel, minus_inf)
    onehot = jnp.zeros((ne, tm), F32)
    idx_rows, w_rows = [], []
    for _ in range(TOP_K):
        mx = jnp.max(work, axis=0, keepdims=True)
        pick = jnp.min(jnp.where(work == mx, eidx, float(ne)), axis=0, keepdims=True)
        hit = eidx == pick
        onehot = jnp.where(hit, 1.0, onehot)
        work = jnp.where(hit, minus_inf, work)
        idx_rows.append(pick)
        w_rows.append(jnp.sum(jnp.where(hit, scores, 0.0), axis=0, keepdims=True))
    w_all = jnp.concatenate(w_rows, axis=0)
    wgt_ref[...] = w_all / jnp.sum(w_all, axis=0, keepdims=True) * ROUTED_SCALE
    idx_ref[...] = jnp.concatenate(idx_rows, axis=0).astype(I32)

    ra = lax.broadcasted_iota(I32, (tm, tm), 0)
    rb = lax.broadcasted_iota(I32, (tm, tm), 1)
    tri = jnp.where(ra <= rb, 1.0, 0.0).astype(BF16)
    incl = jnp.dot(onehot.astype(BF16), tri, preferred_element_type=F32)
    before = incl - onehot + cnt_ref[...]
    rank_rows = [jnp.sum(jnp.where(eidx == idx_rows[kk], before, 0.0), axis=0, keepdims=True)
                 for kk in range(TOP_K)]
    rank_ref[...] = jnp.concatenate(rank_rows, axis=0).astype(I32)
    cnt_ref[...] = cnt_ref[...] + jnp.sum(onehot, axis=1, keepdims=True)


def _route(x1, g, mod, sc_blk, sh_blk, router_w, router_bias, tm=256):
    t, d_model = x1.shape
    ne = N_EXPERTS
    vec = lambda k: pl.BlockSpec((1, d_model), lambda i, k=k: (0, k))
    tok = lambda: pl.BlockSpec((TOP_K, tm), lambda i: (0, i))
    return pl.pallas_call(
        _route_kernel,
        grid=(t // tm,),
        in_specs=[pl.BlockSpec((tm, d_model), lambda i: (i, 0)),
                  pl.BlockSpec((1, d_model), lambda i: (0, 0)),
                  vec(sc_blk), vec(sh_blk),
                  pl.BlockSpec((ne, d_model), lambda i: (0, 0)),
                  pl.BlockSpec((ne, 1), lambda i: (0, 0))],
        out_specs=[pl.BlockSpec((tm, d_model), lambda i: (i, 0)),
                   pl.BlockSpec((tm, d_model // 2), lambda i: (i, 0)),
                   tok(), tok(), tok(),
                   pl.BlockSpec((ne, 1), lambda i: (0, 0))],
        out_shape=[jax.ShapeDtypeStruct((t, d_model), BF16),
                   jax.ShapeDtypeStruct((t, d_model // 2), U32),
                   jax.ShapeDtypeStruct((TOP_K, t), I32),
                   jax.ShapeDtypeStruct((TOP_K, t), I32),
                   jax.ShapeDtypeStruct((TOP_K, t), F32),
                   jax.ShapeDtypeStruct((ne, 1), F32)],
        compiler_params=_params(("arbitrary",)),
        name="route",
    )(x1, g.reshape(1, d_model), mod, mod, router_w.T, router_bias.reshape(ne, 1))


SUBLANES = 8


def _pad_chunks(bm):
    sizes, s = [], bm // 2
    while s >= SUBLANES:
        sizes.append(s)
        s //= 2
    return sizes


def _dispatch_kernel(bm, pos_ref, fill_start_ref, fill_len_ref, nv_ref, hp_ref, xs_ref, sem, pad_sem):
    tm = hp_ref.shape[0]

    @pl.when(pl.program_id(0) == 0)
    def _():
        def pad_copies(action):
            def per_expert(e, carry):
                start = fill_start_ref[e]
                n = fill_len_ref[e]
                head = (-start) & (SUBLANES - 1)
                for r in range(SUBLANES - 1):
                    @pl.when(r < head)
                    def _(r=r):
                        action(pltpu.make_async_copy(hp_ref.at[pl.ds(0, 1)],
                                                     xs_ref.at[pl.ds(start + r, 1)], pad_sem))

                start = start + head
                n = n - head
                for size in _pad_chunks(bm):
                    take = (n & size) != 0

                    @pl.when(take)
                    def _(start=start, size=size):
                        dst = pl.multiple_of(start, SUBLANES)
                        action(pltpu.make_async_copy(hp_ref.at[pl.ds(0, size)],
                                                     xs_ref.at[pl.ds(dst, size)], pad_sem))

                    start = start + jnp.where(take, size, 0)
                return carry

            lax.fori_loop(0, N_EXPERTS, per_expert, 0)

            def unused_block(b, carry):
                dst = pl.multiple_of(b * bm, bm)
                action(pltpu.make_async_copy(hp_ref.at[pl.ds(0, bm)],
                                             xs_ref.at[pl.ds(dst, bm)], pad_sem))
                return carry

            lax.fori_loop(nv_ref[0], xs_ref.shape[0] // bm, unused_block, 0)

        pad_copies(lambda cp: cp.start())
        pad_copies(lambda cp: cp.wait())

    def copy_rows(tt, carry):
        for kk in range(TOP_K):
            dst = pos_ref[0, 0, kk * tm + tt]
            pltpu.make_async_copy(hp_ref.at[pl.ds(tt, 1)], xs_ref.at[pl.ds(dst, 1)], sem).start()
        return carry

    lax.fori_loop(0, tm, copy_rows, 0)
    pltpu.make_async_copy(xs_ref.at[pl.ds(0, tm * TOP_K)], xs_ref.at[pl.ds(0, tm * TOP_K)], sem).wait()


def _tile_major(a_t, tm):
    k, t = a_t.shape
    return a_t.reshape(k, t // tm, tm).transpose(1, 0, 2).reshape(t // tm, 1, k * tm)


def _dispatch(hp, pos_t, fill_start, fill_len, n_valid, rows, bm, tm=1024):
    t, width = hp.shape
    assert tm >= bm
    smem = lambda: pl.BlockSpec(memory_space=pltpu.SMEM)
    return pl.pallas_call(
        functools.partial(_dispatch_kernel, bm),
        grid=(t // tm,),
        in_specs=[pl.BlockSpec((1, 1, tm * TOP_K), lambda i: (i, 0, 0), memory_space=pltpu.SMEM),
                  smem(), smem(), smem(),
                  pl.BlockSpec((tm, width), lambda i: (i, 0))],
        out_specs=pl.BlockSpec(memory_space=pl.ANY),
        out_shape=jax.ShapeDtypeStruct((rows, width), U32),
        scratch_shapes=[pltpu.SemaphoreType.DMA(()), pltpu.SemaphoreType.DMA(())],
        compiler_params=_params(("arbitrary",)),
        name="dispatch",
    )(_tile_major(pos_t, tm), fill_start, fill_len, n_valid, hp)


INVERT_UNROLL = 8


def _invert_kernel(bm, n_tok, tm, pos_ref, fill_start_ref, fill_len_ref, nv_ref, inv_ref):
    tile = pl.program_id(0)

    @pl.when(tile == 0)
    def _():
        def mark_block(first_row):
            def mark(i, c):
                for u in range(INVERT_UNROLL):
                    inv_ref[first_row + i * INVERT_UNROLL + u] = -1
                return c

            lax.fori_loop(0, bm // INVERT_UNROLL, mark, 0)

        def mark_padding(e, carry):
            @pl.when(fill_len_ref[e] > 0)
            def _():
                mark_block(fill_start_ref[e] + fill_len_ref[e] - bm)

            return carry

        lax.fori_loop(0, N_EXPERTS, mark_padding, 0)

        def mark_unused(blk, c):
            mark_block(blk * bm)
            return c

        lax.fori_loop(nv_ref[0], inv_ref.shape[0] // bm, mark_unused, 0)

    def body(tt, carry):
        for kk in range(TOP_K):
            inv_ref[pos_ref[0, 0, kk * tm + tt]] = kk * n_tok + tile * tm + tt
        return carry

    lax.fori_loop(0, tm, body, 0, unroll=INVERT_UNROLL)


def _invert(pos_t, fill_start, fill_len, n_valid, rows, bm, tm=1024):
    _, t = pos_t.shape
    smem = lambda: pl.BlockSpec(memory_space=pltpu.SMEM)
    return pl.pallas_call(
        functools.partial(_invert_kernel, bm, t, tm),
        grid=(t // tm,),
        in_specs=[pl.BlockSpec((1, 1, tm * TOP_K), lambda i: (i, 0, 0), memory_space=pltpu.SMEM),
                  smem(), smem(), smem()],
        out_specs=smem(),
        out_shape=jax.ShapeDtypeStruct((rows,), I32),
        compiler_params=_params(("arbitrary",)),
        name="invert",
    )(_tile_major(pos_t, tm), fill_start, fill_len, n_valid)


Y_BUFFERS = 3


def _experts_kernel(bm, n_slot_rows, seg_ref, sege_ref, nv_ref, inv_ref, x_ref, wg_hbm, wu_hbm, wd_hbm,
                    ysl_ref, wg_f32, wu_f32, wd_f32, wg_bf, wu_bf, wd_bf, y0_ref, y1_ref, y2_ref,
                    sems, ysems):
    b = pl.program_id(0)
    nb = seg_ref.shape[0]
    n_valid = nv_ref[0]
    seg = seg_ref[jnp.minimum(b, nb - 1)]
    slot = seg % 2
    first = (b < nb) & ((b == 0) | (seg_ref[jnp.clip(b - 1, 0, nb - 1)] != seg))
    ybufs = (y0_ref, y1_ref, y2_ref)

    def weight_copies(which_seg, which_slot):
        e = sege_ref[which_seg]
        return [pltpu.make_async_copy(src.at[e], dst.at[which_slot], sems.at[which_slot])
                for src, dst in ((wg_hbm, wg_f32), (wu_hbm, wu_f32), (wd_hbm, wd_f32))]

    @pl.when(b == 0)
    def _():
        for cp in weight_copies(0, 0):
            cp.start()

    @pl.when(first)
    def _():
        for cp in weight_copies(seg, slot):
            cp.wait()
        wg_bf[...] = wg_f32[slot].astype(BF16)
        wu_bf[...] = wu_f32[slot].astype(BF16)
        wd_bf[...] = wd_f32[slot].astype(BF16)

        @pl.when(seg + 1 < nv_ref[1])
        def _():
            for cp in weight_copies(seg + 1, 1 - slot):
                cp.start()

    def compute(y_ref):
        lo, hi = _unpack_pair(x_ref[...])
        half = lo.shape[1]
        lo = lo.astype(BF16)
        hi = hi.astype(BF16)
        gate = (jnp.dot(lo, wg_bf[:half, :], preferred_element_type=F32)
                + jnp.dot(hi, wg_bf[half:, :], preferred_element_type=F32))
        up = (jnp.dot(lo, wu_bf[:half, :], preferred_element_type=F32)
              + jnp.dot(hi, wu_bf[half:, :], preferred_element_type=F32))
        act = (_silu(gate) * up).astype(BF16)
        y = jnp.dot(act, wd_bf[...], preferred_element_type=F32)
        y_ref[...] = _pack_pair(y[:, :half], y[:, half:])

    def scatter(block, parity):
        base = block * bm
        spare = n_slot_rows + parity * bm
        for r in range(bm):
            d = inv_ref[base + r]
            d = jnp.where(d < 0, spare + r, d)
            pltpu.make_async_copy(ybufs[parity].at[pl.ds(r, 1)], ysl_ref.at[pl.ds(d, 1)],
                                  ysems.at[parity]).start()

    for p in range(Y_BUFFERS):
        mine = (b % Y_BUFFERS) == p
        before = (p - 1) % Y_BUFFERS

        @pl.when(mine & (b >= Y_BUFFERS) & (b - Y_BUFFERS < n_valid))
        def _(p=p):
            pltpu.make_async_copy(ybufs[p], ysl_ref.at[pl.ds(0, bm)], ysems.at[p]).wait()

        @pl.when(mine & (b >= 1) & (b < n_valid))
        def _(p=p, before=before):
            scatter(b - 1, before)
            compute(ybufs[p])

        @pl.when(mine & (b >= 1) & (b == n_valid))
        def _(before=before):
            scatter(b - 1, before)

    @pl.when(b == 0)
    def _():
        spare_fill = [pltpu.make_async_copy(
            x_ref, ysl_ref.at[pl.ds(n_slot_rows + parity * bm, bm)], ysems.at[parity])
            for parity in range(Y_BUFFERS)]
        for cp in spare_fill:
            cp.start()
        for cp in spare_fill:
            cp.wait()
        compute(ybufs[0])


def _experts(xs, inv, seg_of, seg_e, n_valid, wg, wu, wd, n_tok, bm=EXPERT_ROWS):
    rows, width = xs.shape
    _, d_model, de = wg.shape
    nb = rows // bm
    n_slot_rows = TOP_K * n_tok
    row_map = lambda b, sg, se, nv, iv: (jnp.minimum(b, nv[0] - 1), 0)
    hbm = lambda: pl.BlockSpec(memory_space=pl.ANY)
    grid_spec = pltpu.PrefetchScalarGridSpec(
        num_scalar_prefetch=4,
        grid=(nb + Y_BUFFERS,),
        in_specs=[pl.BlockSpec((bm, width), row_map), hbm(), hbm(), hbm()],
        out_specs=hbm(),
        scratch_shapes=[pltpu.VMEM((2, d_model, de), F32),
                        pltpu.VMEM((2, d_model, de), F32),
                        pltpu.VMEM((2, de, d_model), F32),
                        pltpu.VMEM((d_model, de), BF16),
                        pltpu.VMEM((d_model, de), BF16),
                        pltpu.VMEM((de, d_model), BF16),
                        ]
                       + [pltpu.VMEM((bm, width), U32)] * Y_BUFFERS
                       + [pltpu.SemaphoreType.DMA((2,)),
                          pltpu.SemaphoreType.DMA((Y_BUFFERS,))],
    )
    return pl.pallas_call(
        functools.partial(_experts_kernel, bm, n_slot_rows),
        grid_spec=grid_spec,
        out_shape=jax.ShapeDtypeStruct((n_slot_rows + Y_BUFFERS * bm, width), U32),
        compiler_params=_params(("arbitrary",)),
        name="experts",
    )(seg_of, seg_e, n_valid, inv, xs, wg, wu, wd)


def _combine_kernel(x_ref, h_ref, wt_ref, g_ref, sg_ref, su_ref, sd_ref, *rest):
    y_refs, o_ref = rest[:TOP_K], rest[TOP_K]
    h = h_ref[...]
    act = (_silu(jnp.dot(h, sg_ref[...], preferred_element_type=F32))
           * jnp.dot(h, su_ref[...], preferred_element_type=F32)).astype(BF16)
    shared = jnp.dot(act, sd_ref[...], preferred_element_type=F32)
    half = y_refs[0].shape[1]
    wt = wt_ref[...]
    lo_acc = shared[:, :half]
    hi_acc = shared[:, half:]
    for kk in range(TOP_K):
        lo, hi = _unpack_pair(y_refs[kk][...])
        wk = wt[:, kk:kk + 1]
        lo_acc = lo_acc + wk * lo
        hi_acc = hi_acc + wk * hi
    g = g_ref[...]
    o_ref[:, :half] = x_ref[:, :half] + g[:, :half] * lo_acc
    o_ref[:, half:] = x_ref[:, half:] + g[:, half:] * hi_acc


def _combine(x1, h2, wts, mod, gate_blk, sg, su, sd, ysl, tm=256):
    t, d_model = x1.shape
    ds_ = sg.shape[1]
    width = ysl.shape[1]
    tiles = t // tm
    slot = lambda kk: pl.BlockSpec((tm, width), lambda i, kk=kk: (kk * tiles + i, 0))
    return pl.pallas_call(
        _combine_kernel,
        grid=(tiles,),
        in_specs=[pl.BlockSpec((tm, d_model), lambda i: (i, 0)),
                  pl.BlockSpec((tm, d_model), lambda i: (i, 0)),
                  pl.BlockSpec((tm, TOP_K), lambda i: (i, 0)),
                  pl.BlockSpec((1, d_model), lambda i: (0, gate_blk)),
                  pl.BlockSpec((d_model, ds_), lambda i: (0, 0)),
                  pl.BlockSpec((d_model, ds_), lambda i: (0, 0)),
                  pl.BlockSpec((ds_, d_model), lambda i: (0, 0))]
                 + [slot(kk) for kk in range(TOP_K)],
        out_specs=pl.BlockSpec((tm, d_model), lambda i: (i, 0)),
        out_shape=jax.ShapeDtypeStruct((t, d_model), F32),
        compiler_params=_params(("parallel",)),
        name="combine",
    )(x1, h2, wts, mod, sg, su, sd, *([ysl] * TOP_K))


def _layout_kernel(bm, cnt_ref, idx_ref, rank_ref, pos_ref, seg_ref, sege_ref, nv_ref, fs_ref, fl_ref):
    shift = bm.bit_length() - 1
    pos_ref[...] = rank_ref[...]

    def per_expert(e, carry):
        start, blk, seg = carry
        cnt = cnt_ref[e]
        nblk = (cnt + (bm - 1)) >> shift
        pos_ref[...] = pos_ref[...] + jnp.where(idx_ref[...] == e, start, 0)

        def mark(b, c):
            seg_ref[blk + b] = seg
            return c

        lax.fori_loop(0, nblk, mark, 0)

        @pl.when(nblk > 0)
        def _():
            sege_ref[seg] = e

        fs_ref[e] = start + cnt
        fl_ref[e] = (nblk << shift) - cnt
        return start + (nblk << shift), blk + nblk, seg + jnp.where(nblk > 0, 1, 0)

    zero = jnp.int32(0)
    _, n_valid, n_seg = lax.fori_loop(0, N_EXPERTS, per_expert, (zero, zero, zero))
    nv_ref[0] = n_valid
    nv_ref[1] = n_seg

    def tail_blocks(b, c):
        seg_ref[b] = n_seg - 1
        return c

    lax.fori_loop(n_valid, seg_ref.shape[0], tail_blocks, 0)

    def tail_segs(s, c):
        sege_ref[s] = N_EXPERTS - 1
        return c

    lax.fori_loop(n_seg, N_EXPERTS, tail_segs, 0)


def _layout(counts, idx_t, rank_t, bm, n_blocks):
    assert bm & (bm - 1) == 0
    k, t = idx_t.shape
    smem = lambda: pl.BlockSpec(memory_space=pltpu.SMEM)
    full = lambda: pl.BlockSpec((k, t), lambda: (0, 0))
    return pl.pallas_call(
        functools.partial(_layout_kernel, bm),
        in_specs=[smem(), full(), full()],
        out_specs=[full(), smem(), smem(), smem(), smem(), smem()],
        out_shape=[jax.ShapeDtypeStruct((k, t), I32),
                   jax.ShapeDtypeStruct((n_blocks,), I32),
                   jax.ShapeDtypeStruct((N_EXPERTS,), I32),
                   jax.ShapeDtypeStruct((2,), I32),
                   jax.ShapeDtypeStruct((N_EXPERTS,), I32),
                   jax.ShapeDtypeStruct((N_EXPERTS,), I32)],
        name="layout",
    )(counts.reshape(-1).astype(I32), idx_t, rank_t)


def _layer(x, c, rel_bias, w_ada, b_ada, ln1_g, w_in, q_norm_g, k_norm_g, ret_gn_g, p_a, p_b, w_o,
           ln2_g, router_w, router_bias, w_gate_e, w_up_e, w_down_e, w_gate_s, w_up_s, w_down_s):
    t, d_model = x.shape
    dils = tuple(d for _, d in DILATED_GROUPS)

    mod = _ada(c.reshape(d_model), w_ada, b_ada)
    h_orders = _norm1(x, ln1_g, mod, dils[1:])
    cos_tab, sin_tab = _rotary_tables(t)
    projs = []
    for order, (cols, epis) in enumerate(_inproj_plan(d_model)):
        projs.append(_inproj(h_orders[order], w_in, cols, epis, q_norm_g, k_norm_g, cos_tab, sin_tab,
                             f"inproj_d{dils[order]}"))
    proj = projs[0]

    attn = [_attn_group(projs[gi], rel_bias, gi, win, dil, 0, 1, 2)
            for gi, (win, dil) in enumerate(DILATED_GROUPS)]
    base = 3
    rq = RET_HEADS * RET_QK_DIM // COLBLK
    vw_blk = RET_HEADS * RET_V_DIM // COLBLK
    qcol = base
    kcol = base + rq
    vcol_blk = base + 2 * rq
    gcol_blk = vcol_blk + vw_blk
    ga_blk = gcol_blk + vw_blk
    gb_blk = ga_blk + d_model // COLBLK
    y_b = _retention(proj, ret_gn_g, qcol, kcol, vcol_blk, gcol_blk)
    (o1, l1), (o2, l2), (o3, l3) = attn
    l2 = _from_residue_major(l2, dils[1])
    l3 = _from_residue_major(l3, dils[2])
    merged = _merge(o1, l1, o2, l2, o3, l3, y_b, proj, ga_blk, gb_blk,
                    p_a.astype(BF16), p_b.astype(BF16), dils[1:])
    x1 = _oproj(x, merged, w_o.astype(BF16), mod, 2)

    h2, h2p, idx_t, rank_t, wgt_t, counts = _route(x1, ln2_g, mod, 4, 3, router_w, router_bias)
    bm = EXPERT_ROWS
    n_blocks = (t * TOP_K + N_EXPERTS * (bm - 1) + bm - 1) // bm
    pos_t, seg_of, seg_e, n_valid, fill_start, fill_len = _layout(counts, idx_t, rank_t, bm, n_blocks)
    xs = _dispatch(h2p, pos_t, fill_start, fill_len, n_valid, n_blocks * bm, bm)
    inv = _invert(pos_t, fill_start, fill_len, n_valid, n_blocks * bm, bm)
    ysl = _experts(xs, inv, seg_of, seg_e, n_valid, w_gate_e, w_up_e, w_down_e, t)
    return _combine(x1, h2, wgt_t.T, mod, 5, w_gate_s.astype(BF16), w_up_s.astype(BF16),
                    w_down_s.astype(BF16), ysl)


def kernel(x, c, rel_bias, w_ada, b_ada, ln1_g, w_in, q_norm_g, k_norm_g, ret_gn_g, p_a, p_b, w_o,
           ln2_g, router_w, router_bias, w_gate_e, w_up_e, w_down_e, w_gate_s, w_up_s, w_down_s):
    b, s, d_model = x.shape
    depth = w_ada.shape[0]
    outs = []
    for bi in range(b):
        xb = x[bi]
        for l in range(depth):
            xb = _layer(xb, c[bi], rel_bias, w_ada[l], b_ada[l], ln1_g[l], w_in[l], q_norm_g[l],
                        k_norm_g[l], ret_gn_g[l], p_a[l], p_b[l], w_o[l], ln2_g[l], router_w[l],
                        router_bias[l], w_gate_e[l], w_up_e[l], w_down_e[l], w_gate_s[l],
                        w_up_s[l], w_down_s[l])
        outs.append(xb)
    return jnp.stack(outs, axis=0)
```

```python
import functools

import numpy as np
import jax
import jax.numpy as jnp
from jax import lax
from jax.experimental import pallas as pl
from jax.experimental.pallas import tpu as pltpu

F32 = jnp.float32
BF16 = jnp.bfloat16
U32 = jnp.uint32
I32 = jnp.int32

HEAD_DIM = 128
DILATED_GROUPS = ((128, 1), (512, 4), (2048, 16))
HEADS_PER_GROUP = 8
N_HEADS_A = HEADS_PER_GROUP * len(DILATED_GROUPS)
A_GROUP_WIDTH = HEADS_PER_GROUP * HEAD_DIM
ATTN_BLOCK = 128
NUM_BUCKETS = 32
MAX_DISTANCE = 2048
NEG_INF = -1e30
RET_HEADS = 8
RET_QK_DIM = 128
RET_V_DIM = 256
RET_CHUNK = 128
ROPE_BASE = 10000.0
GN_EPS = 1e-5
N_EXPERTS = 64
N_GROUPS = 8
TOPK_GROUPS = 4
TOP_K = 8
ROUTED_SCALE = 2.5
RMS_EPS = 1e-6

LANE = 128
COLBLK = 1024
VMEM_LIMIT = 56 * 1024 * 1024
EXPERT_ROWS = 256


def _params(sem, vmem=VMEM_LIMIT):
    return pltpu.CompilerParams(dimension_semantics=sem, vmem_limit_bytes=vmem)


def _sigmoid(v):
    return 0.5 * jnp.tanh(0.5 * v) + 0.5


def _silu(v):
    return v * _sigmoid(v)


def _ada_kernel(c_ref, w_ref, b_ref, o_ref):
    sc = _silu(c_ref[...])
    o_ref[...] = jnp.sum(w_ref[...] * sc, axis=0, keepdims=True) + b_ref[...]


def _ada(c, w, b, tn=512):
    d, n = w.shape
    return pl.pallas_call(
        _ada_kernel,
        grid=(n // tn,),
        in_specs=[pl.BlockSpec((d, 1), lambda j: (0, 0)),
                  pl.BlockSpec((d, tn), lambda j: (0, j)),
                  pl.BlockSpec((1, tn), lambda j: (0, j))],
        out_specs=pl.BlockSpec((1, tn), lambda j: (0, j)),
        out_shape=jax.ShapeDtypeStruct((1, n), F32),
        compiler_params=_params(("parallel",)),
        name="ada",
    )(c.reshape(d, 1), w, b.reshape(1, n))


PERM_TILE = 256


def _perm_matrix(d, to_residue):
    tm = PERM_TILE
    n = tm // d
    assert d & (d - 1) == 0 and n & (n - 1) == 0
    ii = lax.broadcasted_iota(I32, (tm, tm), 0)
    jj = lax.broadcasted_iota(I32, (tm, tm), 1)
    if to_residue:
        src = (ii & (n - 1)) * d + (ii >> (n.bit_length() - 1))
    else:
        src = (ii & (d - 1)) * n + (ii >> (d.bit_length() - 1))
    return jnp.where(jj == src, 1.0, 0.0).astype(BF16)


def _norm1_kernel(dils, x_ref, g_ref, sc_ref, sh_ref, o_ref, *res_refs):
    x = x_ref[...]
    inv = lax.rsqrt(jnp.mean(x * x, axis=-1, keepdims=True) + RMS_EPS)
    h = ((x * inv * g_ref[...]) * (1.0 + sc_ref[...]) + sh_ref[...]).astype(o_ref.dtype)
    o_ref[...] = h
    tm, width = h.shape
    for d, r_ref in zip(dils, res_refs):
        perm = _perm_matrix(d, True)
        n = PERM_TILE // d
        for s in range(tm // PERM_TILE):
            sub = h[s * PERM_TILE:(s + 1) * PERM_TILE, :]
            y = jnp.dot(perm, sub, preferred_element_type=F32).astype(r_ref.dtype)
            r_ref[:, s * n:(s + 1) * n, :] = y.reshape(d, n, width)


def _norm1(x, g, mod, dils, tm=512):
    t, d_model = x.shape
    vec = lambda k: pl.BlockSpec((1, d_model), lambda i, k=k: (0, k))
    out_shapes = [jax.ShapeDtypeStruct((t, d_model), BF16)]
    out_specs = [pl.BlockSpec((tm, d_model), lambda i: (i, 0))]
    for d in dils:
        out_shapes.append(jax.ShapeDtypeStruct((d, t // d, d_model), BF16))
        out_specs.append(pl.BlockSpec((d, tm // d, d_model), lambda i: (0, i, 0)))
    outs = pl.pallas_call(
        functools.partial(_norm1_kernel, dils),
        grid=(t // tm,),
        in_specs=[pl.BlockSpec((tm, d_model), lambda i: (i, 0)),
                  pl.BlockSpec((1, d_model), lambda i: (0, 0)),
                  vec(1), vec(0)],
        out_specs=out_specs,
        out_shape=out_shapes,
        compiler_params=_params(("parallel",)),
        name="norm1",
    )(x, g.reshape(1, d_model), mod, mod)
    return [o.reshape(t, d_model) for o in outs]


def _from_residue_major(a, d):
    t, w = a.shape
    return a.reshape(d, t // d, w).transpose(1, 0, 2).reshape(t, w)


EPI_QNORM, EPI_KNORM, EPI_PLAIN, EPI_ROT_Q, EPI_ROT_K, EPI_SILU, EPI_SIGMOID = range(7)
INPROJ_ROW_CHUNK = 256


def _inproj_kernel(epis_present, colblk_ref, epi_ref, h_ref, w_ref, qg_ref, kg_ref, cos_ref, sin_ref,
                   o_ref, wbf_ref):
    del colblk_ref
    epi = epi_ref[pl.program_id(0)]
    tm = h_ref.shape[0]
    nh = o_ref.shape[1] // HEAD_DIM

    @pl.when(pl.program_id(1) == 0)
    def _():
        wbf_ref[...] = w_ref[...].astype(BF16)

    def head_norm(gain, scale):
        def fn(acc, rows):
            for hh in range(nh):
                sl = slice(hh * HEAD_DIM, (hh + 1) * HEAD_DIM)
                a = acc[:, sl]
                inv = lax.rsqrt(jnp.mean(a * a, axis=-1, keepdims=True) + RMS_EPS)
                o_ref[rows, sl] = ((a * inv * gain) * scale).astype(o_ref.dtype)
        return fn

    def rotary(scale):
        def fn(acc, rows):
            cos = cos_ref[rows, :]
            sin = sin_ref[rows, :]
            for hh in range(nh):
                sl = slice(hh * HEAD_DIM, (hh + 1) * HEAD_DIM)
                a = acc[:, sl]
                rot = pltpu.roll(a, HEAD_DIM // 2, 1)
                o_ref[rows, sl] = ((a * cos + rot * sin) * scale).astype(o_ref.dtype)
        return fn

    def elementwise(f):
        def fn(acc, rows):
            o_ref[rows, :] = f(acc).astype(o_ref.dtype)
        return fn

    epilogues = {
        EPI_QNORM: lambda: head_norm(qg_ref[...], HEAD_DIM ** -0.5),
        EPI_KNORM: lambda: head_norm(kg_ref[...], 1.0),
        EPI_PLAIN: lambda: elementwise(lambda a: a),
        EPI_ROT_Q: lambda: rotary(1.0),
        EPI_ROT_K: lambda: rotary(RET_QK_DIM ** -0.5),
        EPI_SILU: lambda: elementwise(_silu),
        EPI_SIGMOID: lambda: elementwise(_sigmoid),
    }
    for code in epis_present:
        @pl.when(epi == code)
        def _(code=code):
            fn = epilogues[code]()
            for c in range(tm // INPROJ_ROW_CHUNK):
                rows = slice(c * INPROJ_ROW_CHUNK, (c + 1) * INPROJ_ROW_CHUNK)
                acc = jnp.dot(h_ref[rows, :], wbf_ref[...], preferred_element_type=F32)
                fn(acc, rows)


def _inproj_plan(d_model):
    a_blocks = N_HEADS_A * HEAD_DIM // COLBLK
    groups = len(DILATED_GROUPS)
    per_group = a_blocks // groups
    rq = RET_HEADS * RET_QK_DIM // COLBLK
    rv = RET_HEADS * RET_V_DIM // COLBLK
    gd = d_model // COLBLK
    seg_epi = ([EPI_QNORM] * a_blocks + [EPI_KNORM] * a_blocks + [EPI_PLAIN] * a_blocks
               + [EPI_ROT_Q] * rq + [EPI_ROT_K] * rq + [EPI_PLAIN] * rv + [EPI_SILU] * rv
               + [EPI_SIGMOID] * (2 * gd))
    order_of = [0] * len(seg_epi)
    for seg in range(3):
        for blk in range(a_blocks):
            order_of[seg * a_blocks + blk] = blk // per_group
    plans = []
    for order in range(groups):
        cols = [cb for cb in range(len(seg_epi)) if order_of[cb] == order]
        plans.append((cols, [seg_epi[cb] for cb in cols]))
    return plans


def _inproj(h, w, cols, epis, qg, kg, cos_tab, sin_tab, name, tm=1024):
    t, d_model = h.shape
    row = lambda width: pl.BlockSpec((tm, width), lambda j, i, cb, ep: (i, 0))
    one = lambda width: pl.BlockSpec((1, width), lambda j, i, cb, ep: (0, 0))
    grid_spec = pltpu.PrefetchScalarGridSpec(
        num_scalar_prefetch=2,
        grid=(len(cols), t // tm),
        in_specs=[
            row(d_model),
            pl.BlockSpec((d_model, COLBLK), lambda j, i, cb, ep: (0, cb[j])),
            one(HEAD_DIM), one(HEAD_DIM), row(HEAD_DIM), row(HEAD_DIM),
        ],
        out_specs=pl.BlockSpec((tm, COLBLK), lambda j, i, cb, ep: (i, j)),
        scratch_shapes=[pltpu.VMEM((d_model, COLBLK), BF16)],
    )
    return pl.pallas_call(
        functools.partial(_inproj_kernel, tuple(sorted(set(epis)))),
        grid_spec=grid_spec,
        out_shape=jax.ShapeDtypeStruct((t, len(cols) * COLBLK), BF16),
        compiler_params=_params(("arbitrary", "arbitrary")),
        name=name,
    )(jnp.asarray(np.array(cols, np.int32)), jnp.asarray(np.array(epis, np.int32)),
      h, w, qg.reshape(1, HEAD_DIM), kg.reshape(1, HEAD_DIM), cos_tab, sin_tab)


def _rotary_tables(t):
    inv = ROPE_BASE ** (-np.arange(0, RET_QK_DIM, 2, dtype=np.float64) / RET_QK_DIM)
    ang = np.arange(t, dtype=np.float64)[:, None] * inv[None, :]
    cos, sin = np.cos(ang), np.sin(ang)
    cos_tab = np.concatenate([cos, cos], axis=1).astype(np.float32)
    sin_tab = np.concatenate([-sin, sin], axis=1).astype(np.float32)
    return jnp.asarray(cos_tab), jnp.asarray(sin_tab)


def _t5_bucket(dist):
    max_exact = NUM_BUCKETS // 2
    safe = np.maximum(dist, 1).astype(np.float32)
    large = max_exact + (np.log(safe / max_exact) / np.log(MAX_DISTANCE / max_exact)
                         * (NUM_BUCKETS - max_exact)).astype(np.int32)
    return np.where(dist < max_exact, dist, np.minimum(large, NUM_BUCKETS - 1)).astype(np.int32)


def _attn_kernel(head0, w_steps, blocks_per_res, tab_ref, bucket_ref, q_ref, kp_ref, kc_ref,
                 vp_ref, vc_ref, o_ref, lse_ref, bias_ref, band_ref, s_ref, p_ref):
    m_idx = pl.program_id(0)
    blk = ATTN_BLOCK

    @pl.when(m_idx == 0)
    def _():
        bucket = bucket_ref[...]
        for hh in range(HEADS_PER_GROUP):
            bias = jnp.zeros(bucket.shape, F32)
            for b in range(NUM_BUCKETS):
                bias = jnp.where(bucket == b, tab_ref[b, head0 + hh], bias)
            bias_ref[hh] = bias
        a = lax.broadcasted_iota(I32, (blk, 2 * blk), 0)
        cc = lax.broadcasted_iota(I32, (blk, 2 * blk), 1)
        delta = blk + a - cc
        band_ref[...] = jnp.where((delta >= 0) & (delta <= w_steps), 1.0, 0.0)

    prev_thr = jnp.where((m_idx % blocks_per_res) > 0, 0.5, 2.0)
    nt = (((1,), (1,)), ((), ()))
    heads = range(HEADS_PER_GROUP)
    head_cols = [slice(hh * HEAD_DIM, (hh + 1) * HEAD_DIM) for hh in heads]
    for hh, sl in zip(heads, head_cols):
        q = q_ref[:, sl]
        s_p = lax.dot_general(q, kp_ref[:, sl], nt, preferred_element_type=F32)
        s_c = lax.dot_general(q, kc_ref[:, sl], nt, preferred_element_type=F32)
        s_ref[hh, :, :blk] = jnp.where(band_ref[:, :blk] > prev_thr,
                                       s_p + bias_ref[hh, :, :blk], NEG_INF)
        s_ref[hh, :, blk:] = jnp.where(band_ref[:, blk:] > 0.5,
                                       s_c + bias_ref[hh, :, blk:], NEG_INF)
    dens, lses = [], []
    for hh in heads:
        s = s_ref[hh]
        mx = jnp.max(s, axis=-1, keepdims=True)
        p = jnp.exp(s - mx)
        den = jnp.sum(p, axis=-1, keepdims=True)
        p_ref[hh] = p.astype(BF16)
        dens.append(den)
        lses.append(mx + jnp.log(den))
    for hh, sl in zip(heads, head_cols):
        v_both = jnp.concatenate([vp_ref[:, sl], vc_ref[:, sl]], axis=0)
        acc = jnp.dot(p_ref[hh], v_both, preferred_element_type=F32)
        o_ref[:, sl] = (acc / dens[hh]).astype(o_ref.dtype)
    lse_ref[...] = jnp.concatenate(lses, axis=-1)


def _attn_group(proj, rel_bias, gi, window, dilation, qcol, kcol, vcol):
    t = proj.shape[0]
    blk = ATTN_BLOCK
    w_steps = window // dilation
    blocks_per_res = t // dilation // blk
    nblk = t // blk
    a = np.arange(blk)[:, None]
    cc = np.arange(2 * blk)[None, :]
    bucket = _t5_bucket(np.maximum(blk + a - cc, 0) * dilation)

    def prev_map(m):
        return jnp.where(m % blocks_per_res > 0, m - 1, m)

    kern = functools.partial(_attn_kernel, gi * HEADS_PER_GROUP, w_steps, blocks_per_res)
    width = A_GROUP_WIDTH
    return pl.pallas_call(
        kern,
        grid=(nblk,),
        in_specs=[
            pl.BlockSpec(memory_space=pltpu.SMEM),
            pl.BlockSpec((blk, 2 * blk), lambda m: (0, 0)),
            pl.BlockSpec((blk, width), lambda m: (m, qcol)),
            pl.BlockSpec((blk, width), lambda m: (prev_map(m), kcol)),
            pl.BlockSpec((blk, width), lambda m: (m, kcol)),
            pl.BlockSpec((blk, width), lambda m: (prev_map(m), vcol)),
            pl.BlockSpec((blk, width), lambda m: (m, vcol)),
        ],
        out_specs=[pl.BlockSpec((blk, width), lambda m: (m, 0)),
                   pl.BlockSpec((blk, HEADS_PER_GROUP), lambda m: (m, 0))],
        out_shape=[jax.ShapeDtypeStruct((t, width), BF16),
                   jax.ShapeDtypeStruct((t, HEADS_PER_GROUP), F32)],
        scratch_shapes=[pltpu.VMEM((HEADS_PER_GROUP, blk, 2 * blk), F32),
                        pltpu.VMEM((blk, 2 * blk), F32),
                        pltpu.VMEM((HEADS_PER_GROUP, blk, 2 * blk), F32),
                        pltpu.VMEM((HEADS_PER_GROUP, blk, 2 * blk), BF16)],
        compiler_params=_params(("arbitrary",)),
        name=f"attn_d{dilation}",
    )(rel_bias, jnp.asarray(bucket), proj, proj, proj, proj, proj)


def _retention_kernel(q_ref, k_ref, v0_ref, v1_ref, g0_ref, g1_ref, dmat_ref, zeta_ref, xi_ref,
                      gch_ref, gn_ref, o_ref, state_ref, s_ref, cross_ref):
    @pl.when(pl.program_id(0) == 0)
    def _():
        state_ref[...] = jnp.zeros_like(state_ref)

    nt = (((1,), (1,)), ((), ()))
    tn = (((0,), (0,)), ((), ()))
    per_half = RET_HEADS // 2

    def head_refs(hh):
        qs = slice(hh * RET_QK_DIM, (hh + 1) * RET_QK_DIM)
        vs = slice(hh * RET_V_DIM, (hh + 1) * RET_V_DIM)
        hs = slice((hh % per_half) * RET_V_DIM, (hh % per_half + 1) * RET_V_DIM)
        v_ref, g_ref = (v0_ref, g0_ref) if hh < per_half else (v1_ref, g1_ref)
        return qs, vs, hs, v_ref, g_ref

    for hh in range(RET_HEADS):
        qs, _, hs, v_ref, _ = head_refs(hh)
        q = q_ref[:, qs]
        k = k_ref[:, qs]
        v = v_ref[:, hs]
        state = state_ref[hh]
        s = lax.dot_general(q, k, nt, preferred_element_type=F32) * dmat_ref[hh]
        s_ref[hh] = s.astype(BF16)
        cross_ref[hh] = jnp.dot(q, state.astype(BF16), preferred_element_type=F32) * xi_ref[hh]
        vz = (v.astype(F32) * zeta_ref[hh]).astype(BF16)
        upd = lax.dot_general(k, vz, tn, preferred_element_type=F32)
        state_ref[hh] = gch_ref[hh] * state + upd
    for hh in range(RET_HEADS):
        _, vs, hs, v_ref, g_ref = head_refs(hh)
        inner = jnp.dot(s_ref[hh], v_ref[:, hs], preferred_element_type=F32)
        ret = inner + cross_ref[hh]
        mu = jnp.mean(ret, axis=-1, keepdims=True)
        cen = ret - mu
        var = jnp.mean(cen * cen, axis=-1, keepdims=True)
        y = cen * lax.rsqrt(var + GN_EPS) * gn_ref[:, vs]
        o_ref[:, vs] = (y * g_ref[:, hs].astype(F32)).astype(o_ref.dtype)


def _retention_tables():
    c = RET_CHUNK
    hh = np.arange(RET_HEADS, dtype=np.float64)
    log_g = np.log1p(-np.exp2(-5.0 - hh))
    idx = np.arange(c, dtype=np.float64)
    diff = idx[:, None] - idx[None, :]
    dmat = np.where(diff >= 0, np.exp(log_g[:, None, None] * np.maximum(diff, 0.0)), 0.0)
    zeta = np.exp(log_g[:, None] * (c - 1 - idx))[:, :, None]
    xi = np.exp(log_g[:, None] * (idx + 1.0))[:, :, None]
    gch = np.exp(log_g * c)
    f = lambda v: jnp.asarray(v.astype(np.float32))
    return f(dmat), f(zeta), f(xi), f(gch)


def _retention(proj, gn_g, qcol, kcol, vcol, gcol):
    t = proj.shape[0]
    c = RET_CHUNK
    qw = RET_HEADS * RET_QK_DIM
    vw = RET_HEADS * RET_V_DIM
    dmat, zeta, xi, gch = _retention_tables()
    full3 = lambda shp: pl.BlockSpec(shp, lambda n: (0, 0, 0))
    return pl.pallas_call(
        _retention_kernel,
        grid=(t // c,),
        in_specs=[
            pl.BlockSpec((c, qw), lambda n: (n, qcol)),
            pl.BlockSpec((c, qw), lambda n: (n, kcol)),
            pl.BlockSpec((c, vw // 2), lambda n: (n, vcol)),
            pl.BlockSpec((c, vw // 2), lambda n: (n, vcol + 1)),
            pl.BlockSpec((c, vw // 2), lambda n: (n, gcol)),
            pl.BlockSpec((c, vw // 2), lambda n: (n, gcol + 1)),
            full3((RET_HEADS, c, c)),
            full3((RET_HEADS, c, 1)),
            full3((RET_HEADS, c, 1)),
            pl.BlockSpec(memory_space=pltpu.SMEM),
            pl.BlockSpec((1, vw), lambda n: (0, 0)),
        ],
        out_specs=pl.BlockSpec((c, vw), lambda n: (n, 0)),
        out_shape=jax.ShapeDtypeStruct((t, vw), BF16),
        scratch_shapes=[pltpu.VMEM((RET_HEADS, RET_QK_DIM, RET_V_DIM), F32),
                        pltpu.VMEM((RET_HEADS, c, c), BF16),
                        pltpu.VMEM((RET_HEADS, c, RET_V_DIM), F32)],
        compiler_params=_params(("arbitrary",)),
        name="retention",
    )(proj, proj, proj, proj, proj, proj, dmat, zeta, xi, gch, gn_g.reshape(1, vw))


MERGE_ROW_CHUNK = PERM_TILE


def _merge_kernel(dils, o1_ref, l1_ref, o2_ref, l2_ref, o3_ref, l3_ref, yb_ref, ga_ref, gb_ref,
                  pa_ref, pb_ref, out_ref, pa_bf, pb_bf):
    tm = out_ref.shape[0]
    width = o1_ref.shape[1]

    @pl.when(pl.program_id(1) == 0)
    def _():
        pa_bf[...] = pa_ref[...].astype(BF16)
        pb_bf[...] = pb_ref[...].astype(BF16)

    perms = [_perm_matrix(d, False) for d in dils]

    def token_order(res_ref, gi, c):
        n = PERM_TILE // dils[gi]
        blk = res_ref[:, c * n:(c + 1) * n, :].reshape(PERM_TILE, width)
        return jnp.dot(perms[gi], blk, preferred_element_type=F32)

    for c in range(tm // MERGE_ROW_CHUNK):
        rows = slice(c * MERGE_ROW_CHUNK, (c + 1) * MERGE_ROW_CHUNK)
        o2 = token_order(o2_ref, 0, c)
        o3 = token_order(o3_ref, 1, c)
        l1 = l1_ref[rows, :]
        l2 = l2_ref[rows, :]
        l3 = l3_ref[rows, :]
        mx = jnp.maximum(jnp.maximum(l1, l2), l3)
        e1 = jnp.exp(l1 - mx)
        e2 = jnp.exp(l2 - mx)
        e3 = jnp.exp(l3 - mx)
        den = e1 + e2 + e3
        a1, a2, a3 = e1 / den, e2 / den, e3 / den
        pieces = []
        for hh in range(HEADS_PER_GROUP):
            sl = slice(hh * HEAD_DIM, (hh + 1) * HEAD_DIM)
            ya = (a1[:, hh:hh + 1] * o1_ref[rows, sl] + a2[:, hh:hh + 1] * o2[:, sl]
                  + a3[:, hh:hh + 1] * o3[:, sl])
            pieces.append(ya.astype(BF16))
        ya = jnp.concatenate(pieces, axis=1)
        za = jnp.dot(ya, pa_bf[...], preferred_element_type=F32)
        zb = jnp.dot(yb_ref[rows, :], pb_bf[...], preferred_element_type=F32)
        out_ref[rows, :] = (ga_ref[rows, :].astype(F32) * za
                            + gb_ref[rows, :].astype(F32) * zb).astype(out_ref.dtype)


def _merge(o1, l1, o2, l2, o3, l3, yb, proj, ga_col, gb_col, pa, pb, dils, tm=512, tn=1024):
    t = o1.shape[0]
    wa = o1.shape[1]
    wb = yb.shape[1]
    n = pa.shape[1]
    hg = HEADS_PER_GROUP
    ratio = tn // COLBLK
    o_spec = lambda: pl.BlockSpec((tm, wa), lambda j, i: (i, 0))
    l_spec = lambda: pl.BlockSpec((tm, hg), lambda j, i: (i, 0))
    res_spec = lambda d: pl.BlockSpec((d, tm // d, wa), lambda j, i: (0, i, 0))
    o2 = o2.reshape(dils[0], t // dils[0], wa)
    o3 = o3.reshape(dils[1], t // dils[1], wa)
    return pl.pallas_call(
        functools.partial(_merge_kernel, dils),
        grid=(n // tn, t // tm),
        in_specs=[
            o_spec(), l_spec(), res_spec(dils[0]), l_spec(), res_spec(dils[1]), l_spec(),
            pl.BlockSpec((tm, wb), lambda j, i: (i, 0)),
            pl.BlockSpec((tm, tn), lambda j, i: (i, ga_col // ratio + j)),
            pl.BlockSpec((tm, tn), lambda j, i: (i, gb_col // ratio + j)),
            pl.BlockSpec((wa, tn), lambda j, i: (0, j)),
            pl.BlockSpec((wb, tn), lambda j, i: (0, j)),
        ],
        out_specs=pl.BlockSpec((tm, tn), lambda j, i: (i, j)),
        out_shape=jax.ShapeDtypeStruct((t, n), BF16),
        scratch_shapes=[pltpu.VMEM((wa, tn), BF16), pltpu.VMEM((wb, tn), BF16)],
        compiler_params=_params(("arbitrary", "arbitrary")),
        name="merge",
    )(o1, l1, o2, l2, o3, l3, yb, proj, proj, pa, pb)


def _oproj_kernel(x_ref, m_ref, w_ref, g_ref, o_ref, w_bf):
    @pl.when(pl.program_id(1) == 0)
    def _():
        w_bf[...] = w_ref[...].astype(BF16)

    z = jnp.dot(m_ref[...], w_bf[...], preferred_element_type=F32)
    o_ref[...] = x_ref[...] + g_ref[...] * z


def _oproj(x, merged, w, mod, gate_blk, tm=512, tn=1024):
    t, d_model = x.shape
    k = merged.shape[1]
    per = d_model // tn
    return pl.pallas_call(
        _oproj_kernel,
        grid=(d_model // tn, t // tm),
        in_specs=[
            pl.BlockSpec((tm, tn), lambda j, i: (i, j)),
            pl.BlockSpec((tm, k), lambda j, i: (i, 0)),
            pl.BlockSpec((k, tn), lambda j, i: (0, j)),
            pl.BlockSpec((1, tn), lambda j, i: (0, gate_blk * per + j)),
        ],
        out_specs=pl.BlockSpec((tm, tn), lambda j, i: (i, j)),
        out_shape=jax.ShapeDtypeStruct((t, d_model), F32),
        scratch_shapes=[pltpu.VMEM((k, tn), BF16)],
        compiler_params=_params(("arbitrary", "arbitrary")),
        name="oproj",
    )(x, merged, w, mod)


def _pack_pair(lo, hi):
    lo_b = pltpu.bitcast(lo.astype(BF16).astype(F32), U32)
    hi_b = pltpu.bitcast(hi.astype(BF16).astype(F32), U32)
    return (lo_b >> 16) | (hi_b & jnp.uint32(0xFFFF0000))


def _unpack_pair(w):
    lo = pltpu.bitcast(w << 16, F32)
    hi = pltpu.bitcast(w & jnp.uint32(0xFFFF0000), F32)
    return lo, hi


def _route_kernel(x_ref, g_ref, sc_ref, sh_ref, wt_ref, rb_ref, h_ref, hp_ref, idx_ref, rank_ref,
                  wgt_ref, cnt_ref):
    @pl.when(pl.program_id(0) == 0)
    def _():
        cnt_ref[...] = jnp.zeros_like(cnt_ref)

    x = x_ref[...]
    tm, d_model = x.shape
    inv = lax.rsqrt(jnp.mean(x * x, axis=-1, keepdims=True) + RMS_EPS)
    h = (x * inv * g_ref[...]) * (1.0 + sc_ref[...]) + sh_ref[...]
    h_ref[...] = h.astype(h_ref.dtype)
    half = d_model // 2
    hp_ref[...] = _pack_pair(h[:, :half], h[:, half:])

    ne = N_EXPERTS
    per = ne // N_GROUPS
    logits = lax.dot_general(wt_ref[...], h, (((1,), (1,)), ((), ())),
                             precision=lax.Precision.HIGHEST,
                             preferred_element_type=F32)
    scores = jax.nn.sigmoid(logits)
    sel = scores + rb_ref[...]
    eidx = lax.broadcasted_iota(I32, (ne, tm), 0).astype(F32)
    minus_inf = -jnp.inf

    sel3 = sel.reshape(N_GROUPS, per, tm)
    sub = lax.broadcasted_iota(I32, (N_GROUPS, per, tm), 1).astype(F32)
    m1 = jnp.max(sel3, axis=1, keepdims=True)
    first = jnp.min(jnp.where(sel3 == m1, sub, float(per)), axis=1, keepdims=True)
    m2 = jnp.max(jnp.where(sub == first, minus_inf, sel3), axis=1, keepdims=True)
    grp = (m1 + m2).reshape(N_GROUPS, tm)

    gidx = lax.broadcasted_iota(I32, (N_GROUPS, tm), 0).astype(F32)
    gmask = jnp.zeros((N_GROUPS, tm), F32)
    work = grp
    for _ in range(TOPK_GROUPS):
        mx = jnp.max(work, axis=0, keepdims=True)
        pick = jnp.min(jnp.where(work == mx, gidx, float(N_GROUPS)), axis=0, keepdims=True)
        hit = gidx == pick
        gmask = jnp.where(hit, 1.0, gmask)
        work = jnp.where(hit, minus_inf, work)
    emask = jnp.broadcast_to(gmask.reshape(N_GROUPS, 1, tm), (N_GROUPS, per, tm)).reshape(ne, tm)

    work = jnp.where(emask > 0.0, sel, minus_inf)
    onehot = jnp.zeros((ne, tm), F32)
    idx_rows, w_rows = [], []
    for _ in range(TOP_K):
        mx = jnp.max(work, axis=0, keepdims=True)
        pick = jnp.min(jnp.where(work == mx, eidx, float(ne)), axis=0, keepdims=True)
        hit = eidx == pick
        onehot = jnp.where(hit, 1.0, onehot)
        work = jnp.where(hit, minus_inf, work)
        idx_rows.append(pick)
        w_rows.append(jnp.sum(jnp.where(hit, scores, 0.0), axis=0, keepdims=True))
    w_all = jnp.concatenate(w_rows, axis=0)
    wgt_ref[...] = w_all / jnp.sum(w_all, axis=0, keepdims=True) * ROUTED_SCALE
    idx_ref[...] = jnp.concatenate(idx_rows, axis=0).astype(I32)

    ra = lax.broadcasted_iota(I32, (tm, tm), 0)
    rb = lax.broadcasted_iota(I32, (tm, tm), 1)
    tri = jnp.where(ra <= rb, 1.0, 0.0).astype(BF16)
    incl = jnp.dot(onehot.astype(BF16), tri, preferred_element_type=F32)
    before = incl - onehot + cnt_ref[...]
    rank_rows = [jnp.sum(jnp.where(eidx == idx_rows[kk], before, 0.0), axis=0, keepdims=True)
                 for kk in range(TOP_K)]
    rank_ref[...] = jnp.concatenate(rank_rows, axis=0).astype(I32)
    cnt_ref[...] = cnt_ref[...] + jnp.sum(onehot, axis=1, keepdims=True)


def _route(x1, g, mod, sc_blk, sh_blk, router_w, router_bias, tm=256):
    t, d_model = x1.shape
    ne = N_EXPERTS
    vec = lambda k: pl.BlockSpec((1, d_model), lambda i, k=k: (0, k))
    tok = lambda: pl.BlockSpec((TOP_K, tm), lambda i: (0, i))
    return pl.pallas_call(
        _route_kernel,
        grid=(t // tm,),
        in_specs=[pl.BlockSpec((tm, d_model), lambda i: (i, 0)),
                  pl.BlockSpec((1, d_model), lambda i: (0, 0)),
                  vec(sc_blk), vec(sh_blk),
                  pl.BlockSpec((ne, d_model), lambda i: (0, 0)),
                  pl.BlockSpec((ne, 1), lambda i: (0, 0))],
        out_specs=[pl.BlockSpec((tm, d_model), lambda i: (i, 0)),
                   pl.BlockSpec((tm, d_model // 2), lambda i: (i, 0)),
                   tok(), tok(), tok(),
                   pl.BlockSpec((ne, 1), lambda i: (0, 0))],
        out_shape=[jax.ShapeDtypeStruct((t, d_model), BF16),
                   jax.ShapeDtypeStruct((t, d_model // 2), U32),
                   jax.ShapeDtypeStruct((TOP_K, t), I32),
                   jax.ShapeDtypeStruct((TOP_K, t), I32),
                   jax.ShapeDtypeStruct((TOP_K, t), F32),
                   jax.ShapeDtypeStruct((ne, 1), F32)],
        compiler_params=_params(("arbitrary",)),
        name="route",
    )(x1, g.reshape(1, d_model), mod, mod, router_w.T, router_bias.reshape(ne, 1))


SUBLANES = 8


def _pad_chunks(bm):
    sizes, s = [], bm // 2
    while s >= SUBLANES:
        sizes.append(s)
        s //= 2
    return sizes


def _dispatch_kernel(bm, pos_ref, fill_start_ref, fill_len_ref, nv_ref, hp_ref, xs_ref, sem, pad_sem):
    tm = hp_ref.shape[0]

    @pl.when(pl.program_id(0) == 0)
    def _():
        def pad_copies(action):
            def per_expert(e, carry):
                start = fill_start_ref[e]
                n = fill_len_ref[e]
                head = (-start) & (SUBLANES - 1)
                for r in range(SUBLANES - 1):
                    @pl.when(r < head)
                    def _(r=r):
                        action(pltpu.make_async_copy(hp_ref.at[pl.ds(0, 1)],
                                                     xs_ref.at[pl.ds(start + r, 1)], pad_sem))

                start = start + head
                n = n - head
                for size in _pad_chunks(bm):
                    take = (n & size) != 0

                    @pl.when(take)
                    def _(start=start, size=size):
                        dst = pl.multiple_of(start, SUBLANES)
                        action(pltpu.make_async_copy(hp_ref.at[pl.ds(0, size)],
                                                     xs_ref.at[pl.ds(dst, size)], pad_sem))

                    start = start + jnp.where(take, size, 0)
                return carry

            lax.fori_loop(0, N_EXPERTS, per_expert, 0)

            def unused_block(b, carry):
                dst = pl.multiple_of(b * bm, bm)
                action(pltpu.make_async_copy(hp_ref.at[pl.ds(0, bm)],
                                             xs_ref.at[pl.ds(dst, bm)], pad_sem))
                return carry

            lax.fori_loop(nv_ref[0], xs_ref.shape[0] // bm, unused_block, 0)

        pad_copies(lambda cp: cp.start())
        pad_copies(lambda cp: cp.wait())

    def copy_rows(tt, carry):
        for kk in range(TOP_K):
            dst = pos_ref[0, 0, kk * tm + tt]
            pltpu.make_async_copy(hp_ref.at[pl.ds(tt, 1)], xs_ref.at[pl.ds(dst, 1)], sem).start()
        return carry

    lax.fori_loop(0, tm, copy_rows, 0)
    pltpu.make_async_copy(xs_ref.at[pl.ds(0, tm * TOP_K)], xs_ref.at[pl.ds(0, tm * TOP_K)], sem).wait()


def _tile_major(a_t, tm):
    k, t = a_t.shape
    return a_t.reshape(k, t // tm, tm).transpose(1, 0, 2).reshape(t // tm, 1, k * tm)


def _dispatch(hp, pos_t, fill_start, fill_len, n_valid, rows, bm, tm=1024):
    t, width = hp.shape
    assert tm >= bm
    smem = lambda: pl.BlockSpec(memory_space=pltpu.SMEM)
    return pl.pallas_call(
        functools.partial(_dispatch_kernel, bm),
        grid=(t // tm,),
        in_specs=[pl.BlockSpec((1, 1, tm * TOP_K), lambda i: (i, 0, 0), memory_space=pltpu.SMEM),
                  smem(), smem(), smem(),
                  pl.BlockSpec((tm, width), lambda i: (i, 0))],
        out_specs=pl.BlockSpec(memory_space=pl.ANY),
        out_shape=jax.ShapeDtypeStruct((rows, width), U32),
        scratch_shapes=[pltpu.SemaphoreType.DMA(()), pltpu.SemaphoreType.DMA(())],
        compiler_params=_params(("arbitrary",)),
        name="dispatch",
    )(_tile_major(pos_t, tm), fill_start, fill_len, n_valid, hp)


INVERT_UNROLL = 8


def _invert_kernel(bm, n_tok, tm, pos_ref, fill_start_ref, fill_len_ref, nv_ref, inv_ref):
    tile = pl.program_id(0)

    @pl.when(tile == 0)
    def _():
        def mark_block(first_row):
            def mark(i, c):
                for u in range(INVERT_UNROLL):
                    inv_ref[first_row + i * INVERT_UNROLL + u] = -1
                return c

            lax.fori_loop(0, bm // INVERT_UNROLL, mark, 0)

        def mark_padding(e, carry):
            @pl.when(fill_len_ref[e] > 0)
            def _():
                mark_block(fill_start_ref[e] + fill_len_ref[e] - bm)

            return carry

        lax.fori_loop(0, N_EXPERTS, mark_padding, 0)

        def mark_unused(blk, c):
            mark_block(blk * bm)
            return c

        lax.fori_loop(nv_ref[0], inv_ref.shape[0] // bm, mark_unused, 0)

    def body(tt, carry):
        for kk in range(TOP_K):
            inv_ref[pos_ref[0, 0, kk * tm + tt]] = kk * n_tok + tile * tm + tt
        return carry

    lax.fori_loop(0, tm, body, 0, unroll=INVERT_UNROLL)


def _invert(pos_t, fill_start, fill_len, n_valid, rows, bm, tm=1024):
    _, t = pos_t.shape
    smem = lambda: pl.BlockSpec(memory_space=pltpu.SMEM)
    return pl.pallas_call(
        functools.partial(_invert_kernel, bm, t, tm),
        grid=(t // tm,),
        in_specs=[pl.BlockSpec((1, 1, tm * TOP_K), lambda i: (i, 0, 0), memory_space=pltpu.SMEM),
                  smem(), smem(), smem()],
        out_specs=smem(),
        out_shape=jax.ShapeDtypeStruct((rows,), I32),
        compiler_params=_params(("arbitrary",)),
        name="invert",
    )(_tile_major(pos_t, tm), fill_start, fill_len, n_valid)


Y_BUFFERS = 3


def _experts_kernel(bm, n_slot_rows, seg_ref, sege_ref, nv_ref, inv_ref, x_ref, wg_hbm, wu_hbm, wd_hbm,
                    ysl_ref, wg_f32, wu_f32, wd_f32, wg_bf, wu_bf, wd_bf, y0_ref, y1_ref, y2_ref,
                    sems, ysems):
    b = pl.program_id(0)
    nb = seg_ref.shape[0]
    n_valid = nv_ref[0]
    seg = seg_ref[jnp.minimum(b, nb - 1)]
    slot = seg % 2
    first = (b < nb) & ((b == 0) | (seg_ref[jnp.clip(b - 1, 0, nb - 1)] != seg))
    ybufs = (y0_ref, y1_ref, y2_ref)

    def weight_copies(which_seg, which_slot):
        e = sege_ref[which_seg]
        return [pltpu.make_async_copy(src.at[e], dst.at[which_slot], sems.at[which_slot])
                for src, dst in ((wg_hbm, wg_f32), (wu_hbm, wu_f32), (wd_hbm, wd_f32))]

    @pl.when(b == 0)
    def _():
        for cp in weight_copies(0, 0):
            cp.start()

    @pl.when(first)
    def _():
        for cp in weight_copies(seg, slot):
            cp.wait()
        wg_bf[...] = wg_f32[slot].astype(BF16)
        wu_bf[...] = wu_f32[slot].astype(BF16)
        wd_bf[...] = wd_f32[slot].astype(BF16)

        @pl.when(seg + 1 < nv_ref[1])
        def _():
            for cp in weight_copies(seg + 1, 1 - slot):
                cp.start()

    def compute(y_ref):
        lo, hi = _unpack_pair(x_ref[...])
        half = lo.shape[1]
        lo = lo.astype(BF16)
        hi = hi.astype(BF16)
        gate = (jnp.dot(lo, wg_bf[:half, :], preferred_element_type=F32)
                + jnp.dot(hi, wg_bf[half:, :], preferred_element_type=F32))
        up = (jnp.dot(lo, wu_bf[:half, :], preferred_element_type=F32)
              + jnp.dot(hi, wu_bf[half:, :], preferred_element_type=F32))
        act = (_silu(gate) * up).astype(BF16)
        y = jnp.dot(act, wd_bf[...], preferred_element_type=F32)
        y_ref[...] = _pack_pair(y[:, :half], y[:, half:])

    def scatter(block, parity):
        base = block * bm
        spare = n_slot_rows + parity * bm
        for r in range(bm):
            d = inv_ref[base + r]
            d = jnp.where(d < 0, spare + r, d)
            pltpu.make_async_copy(ybufs[parity].at[pl.ds(r, 1)], ysl_ref.at[pl.ds(d, 1)],
                                  ysems.at[parity]).start()

    for p in range(Y_BUFFERS):
        mine = (b % Y_BUFFERS) == p
        before = (p - 1) % Y_BUFFERS

        @pl.when(mine & (b >= Y_BUFFERS) & (b - Y_BUFFERS < n_valid))
        def _(p=p):
            pltpu.make_async_copy(ybufs[p], ysl_ref.at[pl.ds(0, bm)], ysems.at[p]).wait()

        @pl.when(mine & (b >= 1) & (b < n_valid))
        def _(p=p, before=before):
            scatter(b - 1, before)
            compute(ybufs[p])

        @pl.when(mine & (b >= 1) & (b == n_valid))
        def _(before=before):
            scatter(b - 1, before)

    @pl.when(b == 0)
    def _():
        spare_fill = [pltpu.make_async_copy(
            x_ref, ysl_ref.at[pl.ds(n_slot_rows + parity * bm, bm)], ysems.at[parity])
            for parity in range(Y_BUFFERS)]
        for cp in spare_fill:
            cp.start()
        for cp in spare_fill:
            cp.wait()
        compute(ybufs[0])


def _experts(xs, inv, seg_of, seg_e, n_valid, wg, wu, wd, n_tok, bm=EXPERT_ROWS):
    rows, width = xs.shape
    _, d_model, de = wg.shape
    nb = rows // bm
    n_slot_rows = TOP_K * n_tok
    row_map = lambda b, sg, se, nv, iv: (jnp.minimum(b, nv[0] - 1), 0)
    hbm = lambda: pl.BlockSpec(memory_space=pl.ANY)
    grid_spec = pltpu.PrefetchScalarGridSpec(
        num_scalar_prefetch=4,
        grid=(nb + Y_BUFFERS,),
        in_specs=[pl.BlockSpec((bm, width), row_map), hbm(), hbm(), hbm()],
        out_specs=hbm(),
        scratch_shapes=[pltpu.VMEM((2, d_model, de), F32),
                        pltpu.VMEM((2, d_model, de), F32),
                        pltpu.VMEM((2, de, d_model), F32),
                        pltpu.VMEM((d_model, de), BF16),
                        pltpu.VMEM((d_model, de), BF16),
                        pltpu.VMEM((de, d_model), BF16),
                        ]
                       + [pltpu.VMEM((bm, width), U32)] * Y_BUFFERS
                       + [pltpu.SemaphoreType.DMA((2,)),
                          pltpu.SemaphoreType.DMA((Y_BUFFERS,))],
    )
    return pl.pallas_call(
        functools.partial(_experts_kernel, bm, n_slot_rows),
        grid_spec=grid_spec,
        out_shape=jax.ShapeDtypeStruct((n_slot_rows + Y_BUFFERS * bm, width), U32),
        compiler_params=_params(("arbitrary",)),
        name="experts",
    )(seg_of, seg_e, n_valid, inv, xs, wg, wu, wd)


def _combine_kernel(x_ref, h_ref, wt_ref, g_ref, sg_ref, su_ref, sd_ref, *rest):
    y_refs, o_ref = rest[:TOP_K], rest[TOP_K]
    h = h_ref[...]
    act = (_silu(jnp.dot(h, sg_ref[...], preferred_element_type=F32))
           * jnp.dot(h, su_ref[...], preferred_element_type=F32)).astype(BF16)
    shared = jnp.dot(act, sd_ref[...], preferred_element_type=F32)
    half = y_refs[0].shape[1]
    wt = wt_ref[...]
    lo_acc = shared[:, :half]
    hi_acc = shared[:, half:]
    for kk in range(TOP_K):
        lo, hi = _unpack_pair(y_refs[kk][...])
        wk = wt[:, kk:kk + 1]
        lo_acc = lo_acc + wk * lo
        hi_acc = hi_acc + wk * hi
    g = g_ref[...]
    o_ref[:, :half] = x_ref[:, :half] + g[:, :half] * lo_acc
    o_ref[:, half:] = x_ref[:, half:] + g[:, half:] * hi_acc


def _combine(x1, h2, wts, mod, gate_blk, sg, su, sd, ysl, tm=256):
    t, d_model = x1.shape
    ds_ = sg.shape[1]
    width = ysl.shape[1]
    tiles = t // tm
    slot = lambda kk: pl.BlockSpec((tm, width), lambda i, kk=kk: (kk * tiles + i, 0))
    return pl.pallas_call(
        _combine_kernel,
        grid=(tiles,),
        in_specs=[pl.BlockSpec((tm, d_model), lambda i: (i, 0)),
                  pl.BlockSpec((tm, d_model), lambda i: (i, 0)),
                  pl.BlockSpec((tm, TOP_K), lambda i: (i, 0)),
                  pl.BlockSpec((1, d_model), lambda i: (0, gate_blk)),
                  pl.BlockSpec((d_model, ds_), lambda i: (0, 0)),
                  pl.BlockSpec((d_model, ds_), lambda i: (0, 0)),
                  pl.BlockSpec((ds_, d_model), lambda i: (0, 0))]
                 + [slot(kk) for kk in range(TOP_K)],
        out_specs=pl.BlockSpec((tm, d_model), lambda i: (i, 0)),
        out_shape=jax.ShapeDtypeStruct((t, d_model), F32),
        compiler_params=_params(("parallel",)),
        name="combine",
    )(x1, h2, wts, mod, sg, su, sd, *([ysl] * TOP_K))


def _layout_kernel(bm, cnt_ref, idx_ref, rank_ref, pos_ref, seg_ref, sege_ref, nv_ref, fs_ref, fl_ref):
    shift = bm.bit_length() - 1
    pos_ref[...] = rank_ref[...]

    def per_expert(e, carry):
        start, blk, seg = carry
        cnt = cnt_ref[e]
        nblk = (cnt + (bm - 1)) >> shift
        pos_ref[...] = pos_ref[...] + jnp.where(idx_ref[...] == e, start, 0)

        def mark(b, c):
            seg_ref[blk + b] = seg
            return c

        lax.fori_loop(0, nblk, mark, 0)

        @pl.when(nblk > 0)
        def _():
            sege_ref[seg] = e

        fs_ref[e] = start + cnt
        fl_ref[e] = (nblk << shift) - cnt
        return start + (nblk << shift), blk + nblk, seg + jnp.where(nblk > 0, 1, 0)

    zero = jnp.int32(0)
    _, n_valid, n_seg = lax.fori_loop(0, N_EXPERTS, per_expert, (zero, zero, zero))
    nv_ref[0] = n_valid
    nv_ref[1] = n_seg

    def tail_blocks(b, c):
        seg_ref[b] = n_seg - 1
        return c

    lax.fori_loop(n_valid, seg_ref.shape[0], tail_blocks, 0)

    def tail_segs(s, c):
        sege_ref[s] = N_EXPERTS - 1
        return c

    lax.fori_loop(n_seg, N_EXPERTS, tail_segs, 0)


def _layout(counts, idx_t, rank_t, bm, n_blocks):
    assert bm & (bm - 1) == 0
    k, t = idx_t.shape
    smem = lambda: pl.BlockSpec(memory_space=pltpu.SMEM)
    full = lambda: pl.BlockSpec((k, t), lambda: (0, 0))
    return pl.pallas_call(
        functools.partial(_layout_kernel, bm),
        in_specs=[smem(), full(), full()],
        out_specs=[full(), smem(), smem(), smem(), smem(), smem()],
        out_shape=[jax.ShapeDtypeStruct((k, t), I32),
                   jax.ShapeDtypeStruct((n_blocks,), I32),
                   jax.ShapeDtypeStruct((N_EXPERTS,), I32),
                   jax.ShapeDtypeStruct((2,), I32),
                   jax.ShapeDtypeStruct((N_EXPERTS,), I32),
                   jax.ShapeDtypeStruct((N_EXPERTS,), I32)],
        name="layout",
    )(counts.reshape(-1).astype(I32), idx_t, rank_t)


def _layer(x, c, rel_bias, w_ada, b_ada, ln1_g, w_in, q_norm_g, k_norm_g, ret_gn_g, p_a, p_b, w_o,
           ln2_g, router_w, router_bias, w_gate_e, w_up_e, w_down_e, w_gate_s, w_up_s, w_down_s):
    t, d_model = x.shape
    dils = tuple(d for _, d in DILATED_GROUPS)

    mod = _ada(c.reshape(d_model), w_ada, b_ada)
    h_orders = _norm1(x, ln1_g, mod, dils[1:])
    cos_tab, sin_tab = _rotary_tables(t)
    projs = []
    for order, (cols, epis) in enumerate(_inproj_plan(d_model)):
        projs.append(_inproj(h_orders[order], w_in, cols, epis, q_norm_g, k_norm_g, cos_tab, sin_tab,
                             f"inproj_d{dils[order]}"))
    proj = projs[0]

    attn = [_attn_group(projs[gi], rel_bias, gi, win, dil, 0, 1, 2)
            for gi, (win, dil) in enumerate(DILATED_GROUPS)]
    base = 3
    rq = RET_HEADS * RET_QK_DIM // COLBLK
    vw_blk = RET_HEADS * RET_V_DIM // COLBLK
    qcol = base
    kcol = base + rq
    vcol_blk = base + 2 * rq
    gcol_blk = vcol_blk + vw_blk
    ga_blk = gcol_blk + vw_blk
    gb_blk = ga_blk + d_model // COLBLK
    y_b = _retention(proj, ret_gn_g, qcol, kcol, vcol_blk, gcol_blk)
    (o1, l1), (o2, l2), (o3, l3) = attn
    l2 = _from_residue_major(l2, dils[1])
    l3 = _from_residue_major(l3, dils[2])
    merged = _merge(o1, l1, o2, l2, o3, l3, y_b, proj, ga_blk, gb_blk,
                    p_a, p_b, dils[1:])
    x1 = _oproj(x, merged, w_o, mod, 2)

    h2, h2p, idx_t, rank_t, wgt_t, counts = _route(x1, ln2_g, mod, 4, 3, router_w, router_bias)
    bm = EXPERT_ROWS
    n_blocks = (t * TOP_K + N_EXPERTS * (bm - 1) + bm - 1) // bm
    pos_t, seg_of, seg_e, n_valid, fill_start, fill_len = _layout(counts, idx_t, rank_t, bm, n_blocks)
    xs = _dispatch(h2p, pos_t, fill_start, fill_len, n_valid, n_blocks * bm, bm)
    inv = _invert(pos_t, fill_start, fill_len, n_valid, n_blocks * bm, bm)
    ysl = _experts(xs, inv, seg_of, seg_e, n_valid, w_gate_e, w_up_e, w_down_e, t)
    return _combine(x1, h2, wgt_t.T, mod, 5, w_gate_s.astype(BF16), w_up_s.astype(BF16),
                    w_down_s.astype(BF16), ysl)


def kernel(x, c, rel_bias, w_ada, b_ada, ln1_g, w_in, q_norm_g, k_norm_g, ret_gn_g, p_a, p_b, w_o,
           ln2_g, router_w, router_bias, w_gate_e, w_up_e, w_down_e, w_gate_s, w_up_s, w_down_s):
    b, s, d_model = x.shape
    depth = w_ada.shape[0]
    outs = []
    for bi in range(b):
        xb = x[bi]
        for l in range(depth):
            xb = _layer(xb, c[bi], rel_bias, w_ada[l], b_ada[l], ln1_g[l], w_in[l], q_norm_g[l],
                        k_norm_g[l], ret_gn_g[l], p_a[l], p_b[l], w_o[l], ln2_g[l], router_w[l],
                        router_bias[l], w_gate_e[l], w_up_e[l], w_down_e[l], w_gate_s[l],
                        w_up_s[l], w_down_s[l])
        outs.append(xb)
    return jnp.stack(outs, axis=0)
```

```python
import functools

import numpy as np
import jax
import jax.numpy as jnp
from jax import lax
from jax.experimental import pallas as pl
from jax.experimental.pallas import tpu as pltpu

F32 = jnp.float32
BF16 = jnp.bfloat16
U32 = jnp.uint32
I32 = jnp.int32

HEAD_DIM = 128
DILATED_GROUPS = ((128, 1), (512, 4), (2048, 16))
HEADS_PER_GROUP = 8
N_HEADS_A = HEADS_PER_GROUP * len(DILATED_GROUPS)
A_GROUP_WIDTH = HEADS_PER_GROUP * HEAD_DIM
ATTN_BLOCK = 128
NUM_BUCKETS = 32
MAX_DISTANCE = 2048
NEG_INF = -1e30
RET_HEADS = 8
RET_QK_DIM = 128
RET_V_DIM = 256
RET_CHUNK = 128
ROPE_BASE = 10000.0
GN_EPS = 1e-5
N_EXPERTS = 64
N_GROUPS = 8
TOPK_GROUPS = 4
TOP_K = 8
ROUTED_SCALE = 2.5
RMS_EPS = 1e-6

LANE = 128
COLBLK = 1024
VMEM_LIMIT = 56 * 1024 * 1024
EXPERT_ROWS = 256


def _params(sem, vmem=VMEM_LIMIT):
    return pltpu.CompilerParams(dimension_semantics=sem, vmem_limit_bytes=vmem)


def _sigmoid(v):
    return 0.5 * jnp.tanh(0.5 * v) + 0.5


def _silu(v):
    return v * _sigmoid(v)


def _ada_kernel(c_ref, w_ref, b_ref, o_ref):
    sc = _silu(c_ref[...])
    o_ref[...] = jnp.sum(w_ref[...] * sc, axis=0, keepdims=True) + b_ref[...]


def _ada(c, w, b, tn=512):
    d, n = w.shape
    return pl.pallas_call(
        _ada_kernel,
        grid=(n // tn,),
        in_specs=[pl.BlockSpec((d, 1), lambda j: (0, 0)),
                  pl.BlockSpec((d, tn), lambda j: (0, j)),
                  pl.BlockSpec((1, tn), lambda j: (0, j))],
        out_specs=pl.BlockSpec((1, tn), lambda j: (0, j)),
        out_shape=jax.ShapeDtypeStruct((1, n), F32),
        compiler_params=_params(("parallel",)),
        name="ada",
    )(c.reshape(d, 1), w, b.reshape(1, n))


PERM_TILE = 256


def _perm_matrix(d, to_residue):
    tm = PERM_TILE
    n = tm // d
    assert d & (d - 1) == 0 and n & (n - 1) == 0
    ii = lax.broadcasted_iota(I32, (tm, tm), 0)
    jj = lax.broadcasted_iota(I32, (tm, tm), 1)
    if to_residue:
        src = (ii & (n - 1)) * d + (ii >> (n.bit_length() - 1))
    else:
        src = (ii & (d - 1)) * n + (ii >> (d.bit_length() - 1))
    return jnp.where(jj == src, 1.0, 0.0).astype(BF16)


def _norm1_kernel(dils, x_ref, g_ref, sc_ref, sh_ref, o_ref, *res_refs):
    x = x_ref[...]
    inv = lax.rsqrt(jnp.mean(x * x, axis=-1, keepdims=True) + RMS_EPS)
    h = ((x * inv * g_ref[...]) * (1.0 + sc_ref[...]) + sh_ref[...]).astype(o_ref.dtype)
    o_ref[...] = h
    tm, width = h.shape
    for d, r_ref in zip(dils, res_refs):
        perm = _perm_matrix(d, True)
        n = PERM_TILE // d
        for s in range(tm // PERM_TILE):
            sub = h[s * PERM_TILE:(s + 1) * PERM_TILE, :]
            y = jnp.dot(perm, sub, preferred_element_type=F32).astype(r_ref.dtype)
            r_ref[:, s * n:(s + 1) * n, :] = y.reshape(d, n, width)


def _norm1(x, g, mod, dils, tm=512):
    t, d_model = x.shape
    vec = lambda k: pl.BlockSpec((1, d_model), lambda i, k=k: (0, k))
    out_shapes = [jax.ShapeDtypeStruct((t, d_model), BF16)]
    out_specs = [pl.BlockSpec((tm, d_model), lambda i: (i, 0))]
    for d in dils:
        out_shapes.append(jax.ShapeDtypeStruct((d, t // d, d_model), BF16))
        out_specs.append(pl.BlockSpec((d, tm // d, d_model), lambda i: (0, i, 0)))
    outs = pl.pallas_call(
        functools.partial(_norm1_kernel, dils),
        grid=(t // tm,),
        in_specs=[pl.BlockSpec((tm, d_model), lambda i: (i, 0)),
                  pl.BlockSpec((1, d_model), lambda i: (0, 0)),
                  vec(1), vec(0)],
        out_specs=out_specs,
        out_shape=out_shapes,
        compiler_params=_params(("parallel",)),
        name="norm1",
    )(x, g.reshape(1, d_model), mod, mod)
    return [o.reshape(t, d_model) for o in outs]


def _from_residue_major(a, d):
    t, w = a.shape
    return a.reshape(d, t // d, w).transpose(1, 0, 2).reshape(t, w)


EPI_QNORM, EPI_KNORM, EPI_PLAIN, EPI_ROT_Q, EPI_ROT_K, EPI_SILU, EPI_SIGMOID = range(7)
INPROJ_ROW_CHUNK = 256


def _inproj_kernel(epis_present, colblk_ref, epi_ref, h_ref, w_ref, qg_ref, kg_ref, cos_ref, sin_ref,
                   o_ref, wbf_ref):
    del colblk_ref
    epi = epi_ref[pl.program_id(0)]
    tm = h_ref.shape[0]
    nh = o_ref.shape[1] // HEAD_DIM

    @pl.when(pl.program_id(1) == 0)
    def _():
        wbf_ref[...] = w_ref[...].astype(BF16)

    def head_norm(gain, scale):
        def fn(acc, rows):
            for hh in range(nh):
                sl = slice(hh * HEAD_DIM, (hh + 1) * HEAD_DIM)
                a = acc[:, sl]
                inv = lax.rsqrt(jnp.mean(a * a, axis=-1, keepdims=True) + RMS_EPS)
                o_ref[rows, sl] = ((a * inv * gain) * scale).astype(o_ref.dtype)
        return fn

    def rotary(scale):
        def fn(acc, rows):
            cos = cos_ref[rows, :]
            sin = sin_ref[rows, :]
            for hh in range(nh):
                sl = slice(hh * HEAD_DIM, (hh + 1) * HEAD_DIM)
                a = acc[:, sl]
                rot = pltpu.roll(a, HEAD_DIM // 2, 1)
                o_ref[rows, sl] = ((a * cos + rot * sin) * scale).astype(o_ref.dtype)
        return fn

    def elementwise(f):
        def fn(acc, rows):
            o_ref[rows, :] = f(acc).astype(o_ref.dtype)
        return fn

    epilogues = {
        EPI_QNORM: lambda: head_norm(qg_ref[...], HEAD_DIM ** -0.5),
        EPI_KNORM: lambda: head_norm(kg_ref[...], 1.0),
        EPI_PLAIN: lambda: elementwise(lambda a: a),
        EPI_ROT_Q: lambda: rotary(1.0),
        EPI_ROT_K: lambda: rotary(RET_QK_DIM ** -0.5),
        EPI_SILU: lambda: elementwise(_silu),
        EPI_SIGMOID: lambda: elementwise(_sigmoid),
    }
    for code in epis_present:
        @pl.when(epi == code)
        def _(code=code):
            fn = epilogues[code]()
            for c in range(tm // INPROJ_ROW_CHUNK):
                rows = slice(c * INPROJ_ROW_CHUNK, (c + 1) * INPROJ_ROW_CHUNK)
                acc = jnp.dot(h_ref[rows, :], wbf_ref[...], preferred_element_type=F32)
                fn(acc, rows)


def _inproj_plan(d_model):
    a_blocks = N_HEADS_A * HEAD_DIM // COLBLK
    groups = len(DILATED_GROUPS)
    per_group = a_blocks // groups
    rq = RET_HEADS * RET_QK_DIM // COLBLK
    rv = RET_HEADS * RET_V_DIM // COLBLK
    gd = d_model // COLBLK
    seg_epi = ([EPI_QNORM] * a_blocks + [EPI_KNORM] * a_blocks + [EPI_PLAIN] * a_blocks
               + [EPI_ROT_Q] * rq + [EPI_ROT_K] * rq + [EPI_PLAIN] * rv + [EPI_SILU] * rv
               + [EPI_SIGMOID] * (2 * gd))
    order_of = [0] * len(seg_epi)
    for seg in range(3):
        for blk in range(a_blocks):
            order_of[seg * a_blocks + blk] = blk // per_group
    plans = []
    for order in range(groups):
        cols = [cb for cb in range(len(seg_epi)) if order_of[cb] == order]
        plans.append((cols, [seg_epi[cb] for cb in cols]))
    return plans


def _inproj(h, w, cols, epis, qg, kg, cos_tab, sin_tab, name, tm=1024):
    t, d_model = h.shape
    row = lambda width: pl.BlockSpec((tm, width), lambda j, i, cb, ep: (i, 0))
    one = lambda width: pl.BlockSpec((1, width), lambda j, i, cb, ep: (0, 0))
    grid_spec = pltpu.PrefetchScalarGridSpec(
        num_scalar_prefetch=2,
        grid=(len(cols), t // tm),
        in_specs=[
            row(d_model),
            pl.BlockSpec((d_model, COLBLK), lambda j, i, cb, ep: (0, cb[j])),
            one(HEAD_DIM), one(HEAD_DIM), row(HEAD_DIM), row(HEAD_DIM),
        ],
        out_specs=pl.BlockSpec((tm, COLBLK), lambda j, i, cb, ep: (i, j)),
        scratch_shapes=[pltpu.VMEM((d_model, COLBLK), BF16)],
    )
    return pl.pallas_call(
        functools.partial(_inproj_kernel, tuple(sorted(set(epis)))),
        grid_spec=grid_spec,
        out_shape=jax.ShapeDtypeStruct((t, len(cols) * COLBLK), BF16),
        compiler_params=_params(("arbitrary", "arbitrary")),
        name=name,
    )(jnp.asarray(np.array(cols, np.int32)), jnp.asarray(np.array(epis, np.int32)),
      h, w, qg.reshape(1, HEAD_DIM), kg.reshape(1, HEAD_DIM), cos_tab, sin_tab)


def _rotary_tables(t):
    inv = ROPE_BASE ** (-np.arange(0, RET_QK_DIM, 2, dtype=np.float64) / RET_QK_DIM)
    ang = np.arange(t, dtype=np.float64)[:, None] * inv[None, :]
    cos, sin = np.cos(ang), np.sin(ang)
    cos_tab = np.concatenate([cos, cos], axis=1).astype(np.float32)
    sin_tab = np.concatenate([-sin, sin], axis=1).astype(np.float32)
    return jnp.asarray(cos_tab), jnp.asarray(sin_tab)


def _t5_bucket(dist):
    max_exact = NUM_BUCKETS // 2
    safe = np.maximum(dist, 1).astype(np.float32)
    large = max_exact + (np.log(safe / max_exact) / np.log(MAX_DISTANCE / max_exact)
                         * (NUM_BUCKETS - max_exact)).astype(np.int32)
    return np.where(dist < max_exact, dist, np.minimum(large, NUM_BUCKETS - 1)).astype(np.int32)


def _attn_kernel(head0, w_steps, blocks_per_res, tab_ref, bucket_ref, q_ref, kp_ref, kc_ref,
                 vp_ref, vc_ref, o_ref, lse_ref, bias_ref, band_ref, s_ref, p_ref):
    m_idx = pl.program_id(0)
    blk = ATTN_BLOCK

    @pl.when(m_idx == 0)
    def _():
        bucket = bucket_ref[...]
        for hh in range(HEADS_PER_GROUP):
            bias = jnp.zeros(bucket.shape, F32)
            for b in range(NUM_BUCKETS):
                bias = jnp.where(bucket == b, tab_ref[b, head0 + hh], bias)
            bias_ref[hh] = bias
        a = lax.broadcasted_iota(I32, (blk, 2 * blk), 0)
        cc = lax.broadcasted_iota(I32, (blk, 2 * blk), 1)
        delta = blk + a - cc
        band_ref[...] = jnp.where((delta >= 0) & (delta <= w_steps), 1.0, 0.0)

    prev_thr = jnp.where((m_idx % blocks_per_res) > 0, 0.5, 2.0)
    nt = (((1,), (1,)), ((), ()))
    heads = range(HEADS_PER_GROUP)
    head_cols = [slice(hh * HEAD_DIM, (hh + 1) * HEAD_DIM) for hh in heads]
    for hh, sl in zip(heads, head_cols):
        q = q_ref[:, sl]
        s_p = lax.dot_general(q, kp_ref[:, sl], nt, preferred_element_type=F32)
        s_c = lax.dot_general(q, kc_ref[:, sl], nt, preferred_element_type=F32)
        s_ref[hh, :, :blk] = jnp.where(band_ref[:, :blk] > prev_thr,
                                       s_p + bias_ref[hh, :, :blk], NEG_INF)
        s_ref[hh, :, blk:] = jnp.where(band_ref[:, blk:] > 0.5,
                                       s_c + bias_ref[hh, :, blk:], NEG_INF)
    dens, lses = [], []
    for hh in heads:
        s = s_ref[hh]
        mx = jnp.max(s, axis=-1, keepdims=True)
        p = jnp.exp(s - mx)
        den = jnp.sum(p, axis=-1, keepdims=True)
        p_ref[hh] = p.astype(BF16)
        dens.append(den)
        lses.append(mx + jnp.log(den))
    for hh, sl in zip(heads, head_cols):
        v_both = jnp.concatenate([vp_ref[:, sl], vc_ref[:, sl]], axis=0)
        acc = jnp.dot(p_ref[hh], v_both, preferred_element_type=F32)
        o_ref[:, sl] = (acc / dens[hh]).astype(o_ref.dtype)
    lse_ref[...] = jnp.concatenate(lses, axis=-1)


def _attn_group(proj, rel_bias, gi, window, dilation, qcol, kcol, vcol):
    t = proj.shape[0]
    blk = ATTN_BLOCK
    w_steps = window // dilation
    blocks_per_res = t // dilation // blk
    nblk = t // blk
    a = np.arange(blk)[:, None]
    cc = np.arange(2 * blk)[None, :]
    bucket = _t5_bucket(np.maximum(blk + a - cc, 0) * dilation)

    def prev_map(m):
        return jnp.where(m % blocks_per_res > 0, m - 1, m)

    kern = functools.partial(_attn_kernel, gi * HEADS_PER_GROUP, w_steps, blocks_per_res)
    width = A_GROUP_WIDTH
    return pl.pallas_call(
        kern,
        grid=(nblk,),
        in_specs=[
            pl.BlockSpec(memory_space=pltpu.SMEM),
            pl.BlockSpec((blk, 2 * blk), lambda m: (0, 0)),
            pl.BlockSpec((blk, width), lambda m: (m, qcol)),
            pl.BlockSpec((blk, width), lambda m: (prev_map(m), kcol)),
            pl.BlockSpec((blk, width), lambda m: (m, kcol)),
            pl.BlockSpec((blk, width), lambda m: (prev_map(m), vcol)),
            pl.BlockSpec((blk, width), lambda m: (m, vcol)),
        ],
        out_specs=[pl.BlockSpec((blk, width), lambda m: (m, 0)),
                   pl.BlockSpec((blk, HEADS_PER_GROUP), lambda m: (m, 0))],
        out_shape=[jax.ShapeDtypeStruct((t, width), BF16),
                   jax.ShapeDtypeStruct((t, HEADS_PER_GROUP), F32)],
        scratch_shapes=[pltpu.VMEM((HEADS_PER_GROUP, blk, 2 * blk), F32),
                        pltpu.VMEM((blk, 2 * blk), F32),
                        pltpu.VMEM((HEADS_PER_GROUP, blk, 2 * blk), F32),
                        pltpu.VMEM((HEADS_PER_GROUP, blk, 2 * blk), BF16)],
        compiler_params=_params(("arbitrary",)),
        name=f"attn_d{dilation}",
    )(rel_bias, jnp.asarray(bucket), proj, proj, proj, proj, proj)


def _retention_kernel(q_ref, k_ref, v0_ref, v1_ref, g0_ref, g1_ref, dmat_ref, zeta_ref, xi_ref,
                      gch_ref, gn_ref, o_ref, state_ref, s_ref, cross_ref):
    @pl.when(pl.program_id(0) == 0)
    def _():
        state_ref[...] = jnp.zeros_like(state_ref)

    nt = (((1,), (1,)), ((), ()))
    tn = (((0,), (0,)), ((), ()))
    per_half = RET_HEADS // 2

    def head_refs(hh):
        qs = slice(hh * RET_QK_DIM, (hh + 1) * RET_QK_DIM)
        vs = slice(hh * RET_V_DIM, (hh + 1) * RET_V_DIM)
        hs = slice((hh % per_half) * RET_V_DIM, (hh % per_half + 1) * RET_V_DIM)
        v_ref, g_ref = (v0_ref, g0_ref) if hh < per_half else (v1_ref, g1_ref)
        return qs, vs, hs, v_ref, g_ref

    for hh in range(RET_HEADS):
        qs, _, hs, v_ref, _ = head_refs(hh)
        q = q_ref[:, qs]
        k = k_ref[:, qs]
        v = v_ref[:, hs]
        state = state_ref[hh]
        s = lax.dot_general(q, k, nt, preferred_element_type=F32) * dmat_ref[hh]
        s_ref[hh] = s.astype(BF16)
        cross_ref[hh] = jnp.dot(q, state.astype(BF16), preferred_element_type=F32) * xi_ref[hh]
        vz = (v.astype(F32) * zeta_ref[hh]).astype(BF16)
        upd = lax.dot_general(k, vz, tn, preferred_element_type=F32)
        state_ref[hh] = gch_ref[hh] * state + upd
    for hh in range(RET_HEADS):
        _, vs, hs, v_ref, g_ref = head_refs(hh)
        inner = jnp.dot(s_ref[hh], v_ref[:, hs], preferred_element_type=F32)
        ret = inner + cross_ref[hh]
        mu = jnp.mean(ret, axis=-1, keepdims=True)
        cen = ret - mu
        var = jnp.mean(cen * cen, axis=-1, keepdims=True)
        y = cen * lax.rsqrt(var + GN_EPS) * gn_ref[:, vs]
        o_ref[:, vs] = (y * g_ref[:, hs].astype(F32)).astype(o_ref.dtype)


def _retention_tables():
    c = RET_CHUNK
    hh = np.arange(RET_HEADS, dtype=np.float64)
    log_g = np.log1p(-np.exp2(-5.0 - hh))
    idx = np.arange(c, dtype=np.float64)
    diff = idx[:, None] - idx[None, :]
    dmat = np.where(diff >= 0, np.exp(log_g[:, None, None] * np.maximum(diff, 0.0)), 0.0)
    zeta = np.exp(log_g[:, None] * (c - 1 - idx))[:, :, None]
    xi = np.exp(log_g[:, None] * (idx + 1.0))[:, :, None]
    gch = np.exp(log_g * c)
    f = lambda v: jnp.asarray(v.astype(np.float32))
    return f(dmat), f(zeta), f(xi), f(gch)


def _retention(proj, gn_g, qcol, kcol, vcol, gcol):
    t = proj.shape[0]
    c = RET_CHUNK
    qw = RET_HEADS * RET_QK_DIM
    vw = RET_HEADS * RET_V_DIM
    dmat, zeta, xi, gch = _retention_tables()
    full3 = lambda shp: pl.BlockSpec(shp, lambda n: (0, 0, 0))
    return pl.pallas_call(
        _retention_kernel,
        grid=(t // c,),
        in_specs=[
            pl.BlockSpec((c, qw), lambda n: (n, qcol)),
            pl.BlockSpec((c, qw), lambda n: (n, kcol)),
            pl.BlockSpec((c, vw // 2), lambda n: (n, vcol)),
            pl.BlockSpec((c, vw // 2), lambda n: (n, vcol + 1)),
            pl.BlockSpec((c, vw // 2), lambda n: (n, gcol)),
            pl.BlockSpec((c, vw // 2), lambda n: (n, gcol + 1)),
            full3((RET_HEADS, c, c)),
            full3((RET_HEADS, c, 1)),
            full3((RET_HEADS, c, 1)),
            pl.BlockSpec(memory_space=pltpu.SMEM),
            pl.BlockSpec((1, vw), lambda n: (0, 0)),
        ],
        out_specs=pl.BlockSpec((c, vw), lambda n: (n, 0)),
        out_shape=jax.ShapeDtypeStruct((t, vw), BF16),
        scratch_shapes=[pltpu.VMEM((RET_HEADS, RET_QK_DIM, RET_V_DIM), F32),
                        pltpu.VMEM((RET_HEADS, c, c), BF16),
                        pltpu.VMEM((RET_HEADS, c, RET_V_DIM), F32)],
        compiler_params=_params(("arbitrary",)),
        name="retention",
    )(proj, proj, proj, proj, proj, proj, dmat, zeta, xi, gch, gn_g.reshape(1, vw))


MERGE_ROW_CHUNK = PERM_TILE


def _merge_kernel(dils, o1_ref, l1_ref, o2_ref, l2_ref, o3_ref, l3_ref, yb_ref, ga_ref, gb_ref,
                  pa_ref, pb_ref, out_ref, pa_bf, pb_bf):
    tm = out_ref.shape[0]
    width = o1_ref.shape[1]

    @pl.when(pl.program_id(1) == 0)
    def _():
        pa_bf[...] = pa_ref[...].astype(BF16)
        pb_bf[...] = pb_ref[...].astype(BF16)

    perms = [_perm_matrix(d, False) for d in dils]

    def token_order(res_ref, gi, c):
        n = PERM_TILE // dils[gi]
        blk = res_ref[:, c * n:(c + 1) * n, :].reshape(PERM_TILE, width)
        return jnp.dot(perms[gi], blk, preferred_element_type=F32)

    for c in range(tm // MERGE_ROW_CHUNK):
        rows = slice(c * MERGE_ROW_CHUNK, (c + 1) * MERGE_ROW_CHUNK)
        o2 = token_order(o2_ref, 0, c)
        o3 = token_order(o3_ref, 1, c)
        l1 = l1_ref[rows, :]
        l2 = l2_ref[rows, :]
        l3 = l3_ref[rows, :]
        mx = jnp.maximum(jnp.maximum(l1, l2), l3)
        e1 = jnp.exp(l1 - mx)
        e2 = jnp.exp(l2 - mx)
        e3 = jnp.exp(l3 - mx)
        den = e1 + e2 + e3
        a1, a2, a3 = e1 / den, e2 / den, e3 / den
        pieces = []
        for hh in range(HEADS_PER_GROUP):
            sl = slice(hh * HEAD_DIM, (hh + 1) * HEAD_DIM)
            ya = (a1[:, hh:hh + 1] * o1_ref[rows, sl] + a2[:, hh:hh + 1] * o2[:, sl]
                  + a3[:, hh:hh + 1] * o3[:, sl])
            pieces.append(ya.astype(BF16))
        ya = jnp.concatenate(pieces, axis=1)
        za = jnp.dot(ya, pa_bf[...], preferred_element_type=F32)
        zb = jnp.dot(yb_ref[rows, :], pb_bf[...], preferred_element_type=F32)
        out_ref[rows, :] = (ga_ref[rows, :].astype(F32) * za
                            + gb_ref[rows, :].astype(F32) * zb).astype(out_ref.dtype)


def _merge(o1, l1, o2, l2, o3, l3, yb, proj, ga_col, gb_col, pa, pb, dils, tm=512, tn=1024):
    t = o1.shape[0]
    wa = o1.shape[1]
    wb = yb.shape[1]
    n = pa.shape[1]
    hg = HEADS_PER_GROUP
    ratio = tn // COLBLK
    o_spec = lambda: pl.BlockSpec((tm, wa), lambda j, i: (i, 0))
    l_spec = lambda: pl.BlockSpec((tm, hg), lambda j, i: (i, 0))
    res_spec = lambda d: pl.BlockSpec((d, tm // d, wa), lambda j, i: (0, i, 0))
    o2 = o2.reshape(dils[0], t // dils[0], wa)
    o3 = o3.reshape(dils[1], t // dils[1], wa)
    return pl.pallas_call(
        functools.partial(_merge_kernel, dils),
        grid=(n // tn, t // tm),
        in_specs=[
            o_spec(), l_spec(), res_spec(dils[0]), l_spec(), res_spec(dils[1]), l_spec(),
            pl.BlockSpec((tm, wb), lambda j, i: (i, 0)),
            pl.BlockSpec((tm, tn), lambda j, i: (i, ga_col // ratio + j)),
            pl.BlockSpec((tm, tn), lambda j, i: (i, gb_col // ratio + j)),
            pl.BlockSpec((wa, tn), lambda j, i: (0, j)),
            pl.BlockSpec((wb, tn), lambda j, i: (0, j)),
        ],
        out_specs=pl.BlockSpec((tm, tn), lambda j, i: (i, j)),
        out_shape=jax.ShapeDtypeStruct((t, n), BF16),
        scratch_shapes=[pltpu.VMEM((wa, tn), BF16), pltpu.VMEM((wb, tn), BF16)],
        compiler_params=_params(("arbitrary", "arbitrary")),
        name="merge",
    )(o1, l1, o2, l2, o3, l3, yb, proj, proj, pa, pb)


def _oproj_kernel(x_ref, m_ref, w_ref, g_ref, o_ref, w_bf):
    @pl.when(pl.program_id(1) == 0)
    def _():
        w_bf[...] = w_ref[...].astype(BF16)

    z = jnp.dot(m_ref[...], w_bf[...], preferred_element_type=F32)
    o_ref[...] = x_ref[...] + g_ref[...] * z


def _oproj(x, merged, w, mod, gate_blk, tm=512, tn=1024):
    t, d_model = x.shape
    k = merged.shape[1]
    per = d_model // tn
    return pl.pallas_call(
        _oproj_kernel,
        grid=(d_model // tn, t // tm),
        in_specs=[
            pl.BlockSpec((tm, tn), lambda j, i: (i, j)),
            pl.BlockSpec((tm, k), lambda j, i: (i, 0)),
            pl.BlockSpec((k, tn), lambda j, i: (0, j)),
            pl.BlockSpec((1, tn), lambda j, i: (0, gate_blk * per + j)),
        ],
        out_specs=pl.BlockSpec((tm, tn), lambda j, i: (i, j)),
        out_shape=jax.ShapeDtypeStruct((t, d_model), F32),
        scratch_shapes=[pltpu.VMEM((k, tn), BF16)],
        compiler_params=_params(("arbitrary", "arbitrary")),
        name="oproj",
    )(x, merged, w, mod)


def _pack_pair(lo, hi):
    lo_b = pltpu.bitcast(lo.astype(BF16).astype(F32), U32)
    hi_b = pltpu.bitcast(hi.astype(BF16).astype(F32), U32)
    return (lo_b >> 16) | (hi_b & jnp.uint32(0xFFFF0000))


def _unpack_pair(w):
    lo = pltpu.bitcast(w << 16, F32)
    hi = pltpu.bitcast(w & jnp.uint32(0xFFFF0000), F32)
    return lo, hi


def _route_kernel(x_ref, g_ref, sc_ref, sh_ref, wt_ref, rb_ref, h_ref, hp_ref, idx_ref, rank_ref,
                  wgt_ref, cnt_ref):
    @pl.when(pl.program_id(0) == 0)
    def _():
        cnt_ref[...] = jnp.zeros_like(cnt_ref)

    x = x_ref[...]
    tm, d_model = x.shape
    inv = lax.rsqrt(jnp.mean(x * x, axis=-1, keepdims=True) + RMS_EPS)
    h = (x * inv * g_ref[...]) * (1.0 + sc_ref[...]) + sh_ref[...]
    h_ref[...] = h.astype(h_ref.dtype)
    half = d_model // 2
    hp_ref[...] = _pack_pair(h[:, :half], h[:, half:])

    ne = N_EXPERTS
    per = ne // N_GROUPS
    logits = lax.dot_general(wt_ref[...], h, (((1,), (1,)), ((), ())),
                             precision=lax.Precision.HIGHEST,
                             preferred_element_type=F32)
    scores = jax.nn.sigmoid(logits)
    sel = scores + rb_ref[...]
    eidx = lax.broadcasted_iota(I32, (ne, tm), 0).astype(F32)
    minus_inf = -jnp.inf

    sel3 = sel.reshape(N_GROUPS, per, tm)
    sub = lax.broadcasted_iota(I32, (N_GROUPS, per, tm), 1).astype(F32)
    m1 = jnp.max(sel3, axis=1, keepdims=True)
    first = jnp.min(jnp.where(sel3 == m1, sub, float(per)), axis=1, keepdims=True)
    m2 = jnp.max(jnp.where(sub == first, minus_inf, sel3), axis=1, keepdims=True)
    grp = (m1 + m2).reshape(N_GROUPS, tm)

    gidx = lax.broadcasted_iota(I32, (N_GROUPS, tm), 0).astype(F32)
    gmask = jnp.zeros((N_GROUPS, tm), F32)
    work = grp
    for _ in range(TOPK_GROUPS):
        mx = jnp.max(work, axis=0, keepdims=True)
        pick = jnp.min(jnp.where(work == mx, gidx, float(N_GROUPS)), axis=0, keepdims=True)
        hit = gidx == pick
        gmask = jnp.where(hit, 1.0, gmask)
        work = jnp.where(hit, minus_inf, work)
    emask = jnp.broadcast_to(gmask.reshape(N_GROUPS, 1, tm), (N_GROUPS, per, tm)).reshape(ne, tm)

    work = jnp.where(emask > 0.0, sel, minus_inf)
    onehot = jnp.zeros((ne, tm), F32)
    idx_rows, w_rows = [], []
    for _ in range(TOP_K):
        mx = jnp.max(work, axis=0, keepdims=True)
        pick = jnp.min(jnp.where(work == mx, eidx, float(ne)), axis=0, keepdims=True)
        hit = eidx == pick
        onehot = jnp.where(hit, 1.0, onehot)
        work = jnp.where(hit, minus_inf, work)
        idx_rows.append(pick)
        w_rows.append(jnp.sum(jnp.where(hit, scores, 0.0), axis=0, keepdims=True))
    w_all = jnp.concatenate(w_rows, axis=0)
    wgt_ref[...] = w_all / jnp.sum(w_all, axis=0, keepdims=True) * ROUTED_SCALE
    idx_ref[...] = jnp.concatenate(idx_rows, axis=0).astype(I32)

    ra = lax.broadcasted_iota(I32, (tm, tm), 0)
    rb = lax.broadcasted_iota(I32, (tm, tm), 1)
    tri = jnp.where(ra <= rb, 1.0, 0.0).astype(BF16)
    incl = jnp.dot(onehot.astype(BF16), tri, preferred_element_type=F32)
    before = incl - onehot + cnt_ref[...]
    rank_rows = [jnp.sum(jnp.where(eidx == idx_rows[kk], before, 0.0), axis=0, keepdims=True)
                 for kk in range(TOP_K)]
    rank_ref[...] = jnp.concatenate(rank_rows, axis=0).astype(I32)
    cnt_ref[...] = cnt_ref[...] + jnp.sum(onehot, axis=1, keepdims=True)


def _route(x1, g, mod, sc_blk, sh_blk, router_w, router_bias, tm=256):
    t, d_model = x1.shape
    ne = N_EXPERTS
    vec = lambda k: pl.BlockSpec((1, d_model), lambda i, k=k: (0, k))
    tok = lambda: pl.BlockSpec((TOP_K, tm), lambda i: (0, i))
    return pl.pallas_call(
        _route_kernel,
        grid=(t // tm,),
        in_specs=[pl.BlockSpec((tm, d_model), lambda i: (i, 0)),
                  pl.BlockSpec((1, d_model), lambda i: (0, 0)),
                  vec(sc_blk), vec(sh_blk),
                  pl.BlockSpec((ne, d_model), lambda i: (0, 0)),
                  pl.BlockSpec((ne, 1), lambda i: (0, 0))],
        out_specs=[pl.BlockSpec((tm, d_model), lambda i: (i, 0)),
                   pl.BlockSpec((tm, d_model // 2), lambda i: (i, 0)),
                   tok(), tok(), tok(),
                   pl.BlockSpec((ne, 1), lambda i: (0, 0))],
        out_shape=[jax.ShapeDtypeStruct((t, d_model), BF16),
                   jax.ShapeDtypeStruct((t, d_model // 2), U32),
                   jax.ShapeDtypeStruct((TOP_K, t), I32),
                   jax.ShapeDtypeStruct((TOP_K, t), I32),
                   jax.ShapeDtypeStruct((TOP_K, t), F32),
                   jax.ShapeDtypeStruct((ne, 1), F32)],
        compiler_params=_params(("arbitrary",)),
        name="route",
    )(x1, g.reshape(1, d_model), mod, mod, router_w.T, router_bias.reshape(ne, 1))


SUBLANES = 8
DMA_PRIORITIES = 2


def _pad_chunks(bm):
    sizes, s = [], bm // 2
    while s >= SUBLANES:
        sizes.append(s)
        s //= 2
    return sizes


def _dispatch_kernel(bm, pos_ref, fill_start_ref, fill_len_ref, nv_ref, hp_ref, xs_ref, sem, pad_sem):
    tm = hp_ref.shape[0]

    @pl.when(pl.program_id(0) == 0)
    def _():
        def pad_copies(action):
            def per_expert(e, carry):
                start = fill_start_ref[e]
                n = fill_len_ref[e]
                head = (-start) & (SUBLANES - 1)
                for r in range(SUBLANES - 1):
                    @pl.when(r < head)
                    def _(r=r):
                        action(pltpu.make_async_copy(hp_ref.at[pl.ds(0, 1)],
                                                     xs_ref.at[pl.ds(start + r, 1)], pad_sem))

                start = start + head
                n = n - head
                for size in _pad_chunks(bm):
                    take = (n & size) != 0

                    @pl.when(take)
                    def _(start=start, size=size):
                        dst = pl.multiple_of(start, SUBLANES)
                        action(pltpu.make_async_copy(hp_ref.at[pl.ds(0, size)],
                                                     xs_ref.at[pl.ds(dst, size)], pad_sem))

                    start = start + jnp.where(take, size, 0)
                return carry

            lax.fori_loop(0, N_EXPERTS, per_expert, 0)

            def unused_block(b, carry):
                dst = pl.multiple_of(b * bm, bm)
                action(pltpu.make_async_copy(hp_ref.at[pl.ds(0, bm)],
                                             xs_ref.at[pl.ds(dst, bm)], pad_sem))
                return carry

            lax.fori_loop(nv_ref[0], xs_ref.shape[0] // bm, unused_block, 0)

        pad_copies(lambda cp: cp.start())
        pad_copies(lambda cp: cp.wait())

    def copy_rows(tt, carry):
        for kk in range(TOP_K):
            dst = pos_ref[0, 0, kk * tm + tt]
            pltpu.make_async_copy(hp_ref.at[pl.ds(tt, 1)], xs_ref.at[pl.ds(dst, 1)],
                                  sem).start(priority=kk % DMA_PRIORITIES)
        return carry

    lax.fori_loop(0, tm, copy_rows, 0)
    pltpu.make_async_copy(xs_ref.at[pl.ds(0, tm * TOP_K)], xs_ref.at[pl.ds(0, tm * TOP_K)], sem).wait()


def _tile_major(a_t, tm):
    k, t = a_t.shape
    return a_t.reshape(k, t // tm, tm).transpose(1, 0, 2).reshape(t // tm, 1, k * tm)


def _dispatch(hp, pos_t, fill_start, fill_len, n_valid, rows, bm, tm=1024):
    t, width = hp.shape
    assert tm >= bm
    smem = lambda: pl.BlockSpec(memory_space=pltpu.SMEM)
    return pl.pallas_call(
        functools.partial(_dispatch_kernel, bm),
        grid=(t // tm,),
        in_specs=[pl.BlockSpec((1, 1, tm * TOP_K), lambda i: (i, 0, 0), memory_space=pltpu.SMEM),
                  smem(), smem(), smem(),
                  pl.BlockSpec((tm, width), lambda i: (i, 0))],
        out_specs=pl.BlockSpec(memory_space=pl.ANY),
        out_shape=jax.ShapeDtypeStruct((rows, width), U32),
        scratch_shapes=[pltpu.SemaphoreType.DMA(()), pltpu.SemaphoreType.DMA(())],
        compiler_params=_params(("arbitrary",)),
        name="dispatch",
    )(_tile_major(pos_t, tm), fill_start, fill_len, n_valid, hp)


INVERT_UNROLL = 8


def _invert_kernel(bm, n_tok, tm, pos_ref, fill_start_ref, fill_len_ref, nv_ref, inv_ref):
    tile = pl.program_id(0)

    @pl.when(tile == 0)
    def _():
        def mark_block(first_row):
            def mark(i, c):
                for u in range(INVERT_UNROLL):
                    inv_ref[first_row + i * INVERT_UNROLL + u] = -1
                return c

            lax.fori_loop(0, bm // INVERT_UNROLL, mark, 0)

        def mark_padding(e, carry):
            @pl.when(fill_len_ref[e] > 0)
            def _():
                mark_block(fill_start_ref[e] + fill_len_ref[e] - bm)

            return carry

        lax.fori_loop(0, N_EXPERTS, mark_padding, 0)

        def mark_unused(blk, c):
            mark_block(blk * bm)
            return c

        lax.fori_loop(nv_ref[0], inv_ref.shape[0] // bm, mark_unused, 0)

    def body(tt, carry):
        for kk in range(TOP_K):
            inv_ref[pos_ref[0, 0, kk * tm + tt]] = kk * n_tok + tile * tm + tt
        return carry

    lax.fori_loop(0, tm, body, 0, unroll=INVERT_UNROLL)


def _invert(pos_t, fill_start, fill_len, n_valid, rows, bm, tm=1024):
    _, t = pos_t.shape
    smem = lambda: pl.BlockSpec(memory_space=pltpu.SMEM)
    return pl.pallas_call(
        functools.partial(_invert_kernel, bm, t, tm),
        grid=(t // tm,),
        in_specs=[pl.BlockSpec((1, 1, tm * TOP_K), lambda i: (i, 0, 0), memory_space=pltpu.SMEM),
                  smem(), smem(), smem()],
        out_specs=smem(),
        out_shape=jax.ShapeDtypeStruct((rows,), I32),
        compiler_params=_params(("arbitrary",)),
        name="invert",
    )(_tile_major(pos_t, tm), fill_start, fill_len, n_valid)


Y_BUFFERS = 3


def _experts_kernel(bm, n_slot_rows, seg_ref, sege_ref, nv_ref, inv_ref, x_ref, wg_hbm, wu_hbm, wd_hbm,
                    ysl_ref, wg_f32, wu_f32, wd_f32, wg_bf, wu_bf, wd_bf, y0_ref, y1_ref, y2_ref,
                    sems, ysems):
    b = pl.program_id(0)
    nb = seg_ref.shape[0]
    n_valid = nv_ref[0]
    seg = seg_ref[jnp.minimum(b, nb - 1)]
    slot = seg % 2
    first = (b < nb) & ((b == 0) | (seg_ref[jnp.clip(b - 1, 0, nb - 1)] != seg))
    ybufs = (y0_ref, y1_ref, y2_ref)

    def weight_copies(which_seg, which_slot):
        e = sege_ref[which_seg]
        return [pltpu.make_async_copy(src.at[e], dst.at[which_slot], sems.at[which_slot])
                for src, dst in ((wg_hbm, wg_f32), (wu_hbm, wu_f32), (wd_hbm, wd_f32))]

    @pl.when(b == 0)
    def _():
        for cp in weight_copies(0, 0):
            cp.start()

    @pl.when(first)
    def _():
        for cp in weight_copies(seg, slot):
            cp.wait()
        wg_bf[...] = wg_f32[slot].astype(BF16)
        wu_bf[...] = wu_f32[slot].astype(BF16)
        wd_bf[...] = wd_f32[slot].astype(BF16)

        @pl.when(seg + 1 < nv_ref[1])
        def _():
            for cp in weight_copies(seg + 1, 1 - slot):
                cp.start()

    def compute(y_ref):
        lo, hi = _unpack_pair(x_ref[...])
        half = lo.shape[1]
        lo = lo.astype(BF16)
        hi = hi.astype(BF16)
        gate = (jnp.dot(lo, wg_bf[:half, :], preferred_element_type=F32)
                + jnp.dot(hi, wg_bf[half:, :], preferred_element_type=F32))
        up = (jnp.dot(lo, wu_bf[:half, :], preferred_element_type=F32)
              + jnp.dot(hi, wu_bf[half:, :], preferred_element_type=F32))
        act = (_silu(gate) * up).astype(BF16)
        y = jnp.dot(act, wd_bf[...], preferred_element_type=F32)
        y_ref[...] = _pack_pair(y[:, :half], y[:, half:])

    def scatter(block, parity):
        base = block * bm
        spare = n_slot_rows + parity * bm
        for r in range(bm):
            d = inv_ref[base + r]
            d = jnp.where(d < 0, spare + r, d)
            pltpu.make_async_copy(ybufs[parity].at[pl.ds(r, 1)], ysl_ref.at[pl.ds(d, 1)],
                                  ysems.at[parity]).start(priority=r % DMA_PRIORITIES)

    for p in range(Y_BUFFERS):
        mine = (b % Y_BUFFERS) == p
        before = (p - 1) % Y_BUFFERS

        @pl.when(mine & (b >= Y_BUFFERS) & (b - Y_BUFFERS < n_valid))
        def _(p=p):
            pltpu.make_async_copy(ybufs[p], ysl_ref.at[pl.ds(0, bm)], ysems.at[p]).wait()

        @pl.when(mine & (b >= 1) & (b < n_valid))
        def _(p=p, before=before):
            scatter(b - 1, before)
            compute(ybufs[p])

        @pl.when(mine & (b >= 1) & (b == n_valid))
        def _(before=before):
            scatter(b - 1, before)

    @pl.when(b == 0)
    def _():
        spare_fill = [pltpu.make_async_copy(
            x_ref, ysl_ref.at[pl.ds(n_slot_rows + parity * bm, bm)], ysems.at[parity])
            for parity in range(Y_BUFFERS)]
        for cp in spare_fill:
            cp.start()
        for cp in spare_fill:
            cp.wait()
        compute(ybufs[0])


def _experts(xs, inv, seg_of, seg_e, n_valid, wg, wu, wd, n_tok, bm=EXPERT_ROWS):
    rows, width = xs.shape
    _, d_model, de = wg.shape
    nb = rows // bm
    n_slot_rows = TOP_K * n_tok
    row_map = lambda b, sg, se, nv, iv: (jnp.minimum(b, nv[0] - 1), 0)
    hbm = lambda: pl.BlockSpec(memory_space=pl.ANY)
    grid_spec = pltpu.PrefetchScalarGridSpec(
        num_scalar_prefetch=4,
        grid=(nb + Y_BUFFERS,),
        in_specs=[pl.BlockSpec((bm, width), row_map), hbm(), hbm(), hbm()],
        out_specs=hbm(),
        scratch_shapes=[pltpu.VMEM((2, d_model, de), F32),
                        pltpu.VMEM((2, d_model, de), F32),
                        pltpu.VMEM((2, de, d_model), F32),
                        pltpu.VMEM((d_model, de), BF16),
                        pltpu.VMEM((d_model, de), BF16),
                        pltpu.VMEM((de, d_model), BF16),
                        ]
                       + [pltpu.VMEM((bm, width), U32)] * Y_BUFFERS
                       + [pltpu.SemaphoreType.DMA((2,)),
                          pltpu.SemaphoreType.DMA((Y_BUFFERS,))],
    )
    return pl.pallas_call(
        functools.partial(_experts_kernel, bm, n_slot_rows),
        grid_spec=grid_spec,
        out_shape=jax.ShapeDtypeStruct((n_slot_rows + Y_BUFFERS * bm, width), U32),
        compiler_params=_params(("arbitrary",)),
        name="experts",
    )(seg_of, seg_e, n_valid, inv, xs, wg, wu, wd)


def _combine_kernel(x_ref, h_ref, wt_ref, g_ref, sg_ref, su_ref, sd_ref, *rest):
    y_refs, o_ref = rest[:TOP_K], rest[TOP_K]
    h = h_ref[...]
    act = (_silu(jnp.dot(h, sg_ref[...], preferred_element_type=F32))
           * jnp.dot(h, su_ref[...], preferred_element_type=F32)).astype(BF16)
    shared = jnp.dot(act, sd_ref[...], preferred_element_type=F32)
    half = y_refs[0].shape[1]
    wt = wt_ref[...]
    lo_acc = shared[:, :half]
    hi_acc = shared[:, half:]
    for kk in range(TOP_K):
        lo, hi = _unpack_pair(y_refs[kk][...])
        wk = wt[:, kk:kk + 1]
        lo_acc = lo_acc + wk * lo
        hi_acc = hi_acc + wk * hi
    g = g_ref[...]
    o_ref[:, :half] = x_ref[:, :half] + g[:, :half] * lo_acc
    o_ref[:, half:] = x_ref[:, half:] + g[:, half:] * hi_acc


def _combine(x1, h2, wts, mod, gate_blk, sg, su, sd, ysl, tm=256):
    t, d_model = x1.shape
    ds_ = sg.shape[1]
    width = ysl.shape[1]
    tiles = t // tm
    slot = lambda kk: pl.BlockSpec((tm, width), lambda i, kk=kk: (kk * tiles + i, 0))
    return pl.pallas_call(
        _combine_kernel,
        grid=(tiles,),
        in_specs=[pl.BlockSpec((tm, d_model), lambda i: (i, 0)),
                  pl.BlockSpec((tm, d_model), lambda i: (i, 0)),
                  pl.BlockSpec((tm, TOP_K), lambda i: (i, 0)),
                  pl.BlockSpec((1, d_model), lambda i: (0, gate_blk)),
                  pl.BlockSpec((d_model, ds_), lambda i: (0, 0)),
                  pl.BlockSpec((d_model, ds_), lambda i: (0, 0)),
                  pl.BlockSpec((ds_, d_model), lambda i: (0, 0))]
                 + [slot(kk) for kk in range(TOP_K)],
        out_specs=pl.BlockSpec((tm, d_model), lambda i: (i, 0)),
        out_shape=jax.ShapeDtypeStruct((t, d_model), F32),
        compiler_params=_params(("parallel",)),
        name="combine",
    )(x1, h2, wts, mod, sg, su, sd, *([ysl] * TOP_K))


def _layout_kernel(bm, cnt_ref, idx_ref, rank_ref, pos_ref, seg_ref, sege_ref, nv_ref, fs_ref, fl_ref):
    shift = bm.bit_length() - 1
    pos_ref[...] = rank_ref[...]

    def per_expert(e, carry):
        start, blk, seg = carry
        cnt = cnt_ref[e]
        nblk = (cnt + (bm - 1)) >> shift
        pos_ref[...] = pos_ref[...] + jnp.where(idx_ref[...] == e, start, 0)

        def mark(b, c):
            seg_ref[blk + b] = seg
            return c

        lax.fori_loop(0, nblk, mark, 0)

        @pl.when(nblk > 0)
        def _():
            sege_ref[seg] = e

        fs_ref[e] = start + cnt
        fl_ref[e] = (nblk << shift) - cnt
        return start + (nblk << shift), blk + nblk, seg + jnp.where(nblk > 0, 1, 0)

    zero = jnp.int32(0)
    _, n_valid, n_seg = lax.fori_loop(0, N_EXPERTS, per_expert, (zero, zero, zero))
    nv_ref[0] = n_valid
    nv_ref[1] = n_seg

    def tail_blocks(b, c):
        seg_ref[b] = n_seg - 1
        return c

    lax.fori_loop(n_valid, seg_ref.shape[0], tail_blocks, 0)

    def tail_segs(s, c):
        sege_ref[s] = N_EXPERTS - 1
        return c

    lax.fori_loop(n_seg, N_EXPERTS, tail_segs, 0)


def _layout(counts, idx_t, rank_t, bm, n_blocks):
    assert bm & (bm - 1) == 0
    k, t = idx_t.shape
    smem = lambda: pl.BlockSpec(memory_space=pltpu.SMEM)
    full = lambda: pl.BlockSpec((k, t), lambda: (0, 0))
    return pl.pallas_call(
        functools.partial(_layout_kernel, bm),
        in_specs=[smem(), full(), full()],
        out_specs=[full(), smem(), smem(), smem(), smem(), smem()],
        out_shape=[jax.ShapeDtypeStruct((k, t), I32),
                   jax.ShapeDtypeStruct((n_blocks,), I32),
                   jax.ShapeDtypeStruct((N_EXPERTS,), I32),
                   jax.ShapeDtypeStruct((2,), I32),
                   jax.ShapeDtypeStruct((N_EXPERTS,), I32),
                   jax.ShapeDtypeStruct((N_EXPERTS,), I32)],
        name="layout",
    )(counts.reshape(-1).astype(I32), idx_t, rank_t)


def _layer(x, c, rel_bias, w_ada, b_ada, ln1_g, w_in, q_norm_g, k_norm_g, ret_gn_g, p_a, p_b, w_o,
           ln2_g, router_w, router_bias, w_gate_e, w_up_e, w_down_e, w_gate_s, w_up_s, w_down_s):
    t, d_model = x.shape
    dils = tuple(d for _, d in DILATED_GROUPS)

    mod = _ada(c.reshape(d_model), w_ada, b_ada)
    h_orders = _norm1(x, ln1_g, mod, dils[1:])
    cos_tab, sin_tab = _rotary_tables(t)
    projs = []
    for order, (cols, epis) in enumerate(_inproj_plan(d_model)):
        projs.append(_inproj(h_orders[order], w_in, cols, epis, q_norm_g, k_norm_g, cos_tab, sin_tab,
                             f"inproj_d{dils[order]}"))
    proj = projs[0]

    attn = [_attn_group(projs[gi], rel_bias, gi, win, dil, 0, 1, 2)
            for gi, (win, dil) in enumerate(DILATED_GROUPS)]
    base = 3
    rq = RET_HEADS * RET_QK_DIM // COLBLK
    vw_blk = RET_HEADS * RET_V_DIM // COLBLK
    qcol = base
    kcol = base + rq
    vcol_blk = base + 2 * rq
    gcol_blk = vcol_blk + vw_blk
    ga_blk = gcol_blk + vw_blk
    gb_blk = ga_blk + d_model // COLBLK
    y_b = _retention(proj, ret_gn_g, qcol, kcol, vcol_blk, gcol_blk)
    (o1, l1), (o2, l2), (o3, l3) = attn
    l2 = _from_residue_major(l2, dils[1])
    l3 = _from_residue_major(l3, dils[2])
    merged = _merge(o1, l1, o2, l2, o3, l3, y_b, proj, ga_blk, gb_blk,
                    p_a, p_b, dils[1:])
    x1 = _oproj(x, merged, w_o, mod, 2)

    h2, h2p, idx_t, rank_t, wgt_t, counts = _route(x1, ln2_g, mod, 4, 3, router_w, router_bias)
    bm = EXPERT_ROWS
    n_blocks = (t * TOP_K + N_EXPERTS * (bm - 1) + bm - 1) // bm
    pos_t, seg_of, seg_e, n_valid, fill_start, fill_len = _layout(counts, idx_t, rank_t, bm, n_blocks)
    xs = _dispatch(h2p, pos_t, fill_start, fill_len, n_valid, n_blocks * bm, bm)
    inv = _invert(pos_t, fill_start, fill_len, n_valid, n_blocks * bm, bm)
    ysl = _experts(xs, inv, seg_of, seg_e, n_valid, w_gate_e, w_up_e, w_down_e, t)
    return _combine(x1, h2, wgt_t.T, mod, 5, w_gate_s.astype(BF16), w_up_s.astype(BF16),
                    w_down_s.astype(BF16), ysl)


def kernel(x, c, rel_bias, w_ada, b_ada, ln1_g, w_in, q_norm_g, k_norm_g, ret_gn_g, p_a, p_b, w_o,
           ln2_g, router_w, router_bias, w_gate_e, w_up_e, w_down_e, w_gate_s, w_up_s, w_down_s):
    b, s, d_model = x.shape
    depth = w_ada.shape[0]
    outs = []
    for bi in range(b):
        xb = x[bi]
        for l in range(depth):
            xb = _layer(xb, c[bi], rel_bias, w_ada[l], b_ada[l], ln1_g[l], w_in[l], q_norm_g[l],
                        k_norm_g[l], ret_gn_g[l], p_a[l], p_b[l], w_o[l], ln2_g[l], router_w[l],
                        router_bias[l], w_gate_e[l], w_up_e[l], w_down_e[l], w_gate_s[l],
                        w_up_s[l], w_down_s[l])
        outs.append(xb)
    return jnp.stack(outs, axis=0)
```
